```python
import math
import jax, jax.numpy as jnp
from jax import lax
import numpy as np

D_MODEL = 1024
BATCH = 16
SEQ = 2048
DEPTH = 4

CHUNK = 64
Q_BLOCK = 128
HEAD_DIM = 64
H_SB = 8
H_DIFF = 4
H_CH = 8
N_PAST_CHUNKS = 8
BAND = (N_PAST_CHUNKS + 1) * CHUNK
REL_CLIP = 128
W_SB = H_SB * HEAD_DIM
W_DIFF = H_DIFF * 2 * HEAD_DIM
W_CH = H_CH * HEAD_DIM
N_BRANCH = 3
D_FF = int(math.ceil(8 * D_MODEL / 3 / 256)) * 256
RMS_EPS = 1e-6
_SIZES = (W_SB, W_SB, W_SB, W_DIFF, W_DIFF, W_DIFF, W_CH, W_CH, W_CH, N_BRANCH * D_MODEL)
N_IN = int(sum(_SIZES))
_SPLIT_POINTS = tuple(int(v) for v in np.cumsum(_SIZES)[:-1])

kernel_name = "hybrid_stickbreak_diff_chunkrel_gated"


def alibi_slopes(n):
    return np.asarray([2.0 ** (-8.0 * (i + 1) / n) for i in range(n)], dtype=np.float32)


def rms_norm(x, g):
    xf = x.astype(jnp.float32)
    y = xf * lax.rsqrt(jnp.mean(xf * xf, axis=-1, keepdims=True) + RMS_EPS)
    return (y * g.astype(jnp.float32)).astype(x.dtype)


def to_heads(t, n_heads):
    b, s, _ = t.shape
    return t.reshape(b, s, n_heads, -1).transpose(0, 2, 1, 3)


def merge_heads(t):
    b, h, s, d = t.shape
    return t.transpose(0, 2, 1, 3).reshape(b, s, h * d)


def stick_breaking_attention(q, k, v):
    b, h, s, d = q.shape
    nqb = s // Q_BLOCK
    scale = d ** -0.5
    kpos = jnp.arange(s)
    qb = q.reshape(b, h, nqb, Q_BLOCK, d).transpose(2, 0, 1, 3, 4)

    def block(args):
        q_blk, i = args
        qpos = i * Q_BLOCK + jnp.arange(Q_BLOCK)
        z = jnp.einsum('bhqd,bhkd->bhqk', q_blk, k, preferred_element_type=jnp.float32) * scale
        strict = kpos[None, :] < qpos[:, None]
        log_beta = jax.nn.log_sigmoid(z)
        log_1m = jnp.where(strict, jax.nn.log_sigmoid(-z), 0.0)
        after = lax.cumsum(log_1m, axis=3, reverse=True) - log_1m
        w = jnp.where(strict, jnp.exp(log_beta + after), 0.0)
        return jnp.einsum('bhqk,bhkd->bhqd', w.astype(v.dtype), v)

    out = lax.map(block, (qb, jnp.arange(nqb)))
    return out.transpose(1, 2, 0, 3, 4).reshape(b, h, s, d)


def diff_attention(q, k, v, lam, slopes):
    b, h, _, s, d = q.shape
    nqb = s // Q_BLOCK
    scale = d ** -0.5
    kpos = jnp.arange(s)
    qb = q.reshape(b, h, 2, nqb, Q_BLOCK, d).transpose(3, 0, 1, 2, 4, 5)

    def block(args):
        q_blk, i = args
        qpos = i * Q_BLOCK + jnp.arange(Q_BLOCK)
        allowed = (kpos[None, :] // CHUNK) <= (qpos[:, None] // CHUNK)
        dist = jnp.abs(qpos[:, None] - kpos[None, :]).astype(jnp.float32)
        bias = -slopes[:, None, None] * dist[None]
        logits = jnp.einsum('bhmqd,bhmkd->bhmqk', q_blk, k, preferred_element_type=jnp.float32) * scale
        logits = jnp.where(allowed[None, None, None], logits + bias[None, :, None], -jnp.inf)
        p = jax.nn.softmax(logits, axis=-1)
        w = p[:, :, 0] - lam * p[:, :, 1]
        return jnp.einsum('bhqk,bhkd->bhqd', w.astype(v.dtype), v)

    out = lax.map(block, (qb, jnp.arange(nqb)))
    return out.transpose(1, 2, 0, 3, 4).reshape(b, h, s, 2 * d)


def chunked_rel_attention(q, k, v, rel_bias):
    b, h, s, d = q.shape
    nc = s // CHUNK
    pad = N_PAST_CHUNKS * CHUNK
    scale = d ** -0.5
    kp = jnp.pad(k, ((0, 0), (0, 0), (pad, 0), (0, 0)))
    vp = jnp.pad(v, ((0, 0), (0, 0), (pad, 0), (0, 0)))
    qc = q.reshape(b, h, nc, CHUNK, d).transpose(2, 0, 1, 3, 4)
    koff = jnp.arange(BAND)
    rel = (pad + jnp.arange(CHUNK))[:, None] - koff[None, :]
    bias = rel_bias[:, jnp.clip(rel, -REL_CLIP, REL_CLIP) + REL_CLIP].astype(jnp.float32)

    def one_chunk(args):
        q_c, c = args
        start = c * CHUNK
        k_band = lax.dynamic_slice_in_dim(kp, start, BAND, axis=2)
        v_band = lax.dynamic_slice_in_dim(vp, start, BAND, axis=2)
        valid = (start + koff) >= pad
        logits = jnp.einsum('bhqd,bhkd->bhqk', q_c, k_band, preferred_element_type=jnp.float32) * scale
        logits = jnp.where(valid[None, None, None, :], logits + bias[None], -jnp.inf)
        p = jax.nn.softmax(logits, axis=-1)
        return jnp.einsum('bhqk,bhkd->bhqd', p.astype(v_band.dtype), v_band)

    out = lax.map(one_chunk, (qc, jnp.arange(nc)))
    return out.transpose(1, 2, 0, 3, 4).reshape(b, h, s, d)


def setup_inputs(seed: int = 0) -> dict:
    key = jax.random.key(seed)
    ks = jax.random.split(key, 17)
    n = jax.random.normal
    f32 = jnp.float32
    return {
        "x": n(ks[0], (BATCH, SEQ, D_MODEL), f32),
        "norm_mix_g": 1.0 + 0.02 * n(ks[1], (DEPTH, D_MODEL), f32),
        "w_in": n(ks[2], (DEPTH, D_MODEL, N_IN), f32) * D_MODEL ** -0.5,
        "b_gate": 0.1 * n(ks[3], (DEPTH, N_BRANCH, D_MODEL), f32),
        "qk_g_diff": 1.0 + 0.02 * n(ks[4], (DEPTH, 2, HEAD_DIM), f32),
        "lambda_qk": 0.1 * n(ks[5], (DEPTH, 4, HEAD_DIM), f32),
        "subln_g": 1.0 + 0.02 * n(ks[6], (DEPTH, 2 * HEAD_DIM), f32),
        "qk_g_ch": 1.0 + 0.02 * n(ks[7], (DEPTH, 2, HEAD_DIM), f32),
        "rel_bias": 0.1 * n(ks[8], (DEPTH, H_CH, 2 * REL_CLIP + 1), f32),
        "w_branch_sb": n(ks[9], (DEPTH, W_SB, D_MODEL), f32) * W_SB ** -0.5,
        "w_branch_diff": n(ks[10], (DEPTH, W_DIFF, D_MODEL), f32) * W_DIFF ** -0.5,
        "w_branch_ch": n(ks[11], (DEPTH, W_CH, D_MODEL), f32) * W_CH ** -0.5,
        "w_out": n(ks[12], (DEPTH, D_MODEL, D_MODEL), f32) * D_MODEL ** -0.5,
        "norm_ffn_g": 1.0 + 0.02 * n(ks[13], (DEPTH, D_MODEL), f32),
        "w_gu": n(ks[14], (DEPTH, D_MODEL, 2 * D_FF), f32) * D_MODEL ** -0.5,
        "w_down": n(ks[15], (DEPTH, D_FF, D_MODEL), f32) * D_FF ** -0.5,
    }


def reference(x, norm_mix_g, w_in, b_gate, qk_g_diff, lambda_qk, subln_g, qk_g_ch, rel_bias,
              w_branch_sb, w_branch_diff, w_branch_ch, w_out, norm_ffn_g, w_gu, w_down):
    b, s, _ = x.shape
    slopes = jnp.asarray(alibi_slopes(H_DIFF))
    for l in range(DEPTH):
        h = rms_norm(x, norm_mix_g[l])
        proj = h @ w_in[l]
        (q_a, k_a, v_a, q_b, k_b, v_b, q_c, k_c, v_c, g_lin) = jnp.split(proj, _SPLIT_POINTS, axis=-1)

        o_a = merge_heads(stick_breaking_attention(to_heads(q_a, H_SB), to_heads(k_a, H_SB), to_heads(v_a, H_SB)))

        qb = rms_norm(q_b.reshape(b, s, H_DIFF, 2, HEAD_DIM).transpose(0, 2, 3, 1, 4), qk_g_diff[l, 0])
        kb = rms_norm(k_b.reshape(b, s, H_DIFF, 2, HEAD_DIM).transpose(0, 2, 3, 1, 4), qk_g_diff[l, 1])
        vb = to_heads(v_b, H_DIFF)
        lam_init = 0.8 - 0.6 * math.exp(-0.3 * l)
        lq = lambda_qk[l].astype(jnp.float32)
        lam = jnp.exp(jnp.sum(lq[0] * lq[1])) - jnp.exp(jnp.sum(lq[2] * lq[3])) + lam_init
        ob = diff_attention(qb, kb, vb, lam, slopes)
        o_b = merge_heads(rms_norm(ob, subln_g[l]) * (1.0 - lam_init))

        qc = rms_norm(to_heads(q_c, H_CH), qk_g_ch[l, 0])
        kc = rms_norm(to_heads(k_c, H_CH), qk_g_ch[l, 1])
        o_c = merge_heads(chunked_rel_attention(qc, kc, to_heads(v_c, H_CH), rel_bias[l]))

        gates = jax.nn.sigmoid(g_lin.reshape(b, s, N_BRANCH, D_MODEL) + b_gate[l])
        merged = (gates[:, :, 0] * (o_a @ w_branch_sb[l])
                  + gates[:, :, 1] * (o_b @ w_branch_diff[l])
                  + gates[:, :, 2] * (o_c @ w_branch_ch[l]))
        x = x + merged @ w_out[l]

        h2 = rms_norm(x, norm_ffn_g[l])
        gate, up = jnp.split(h2 @ w_gu[l], 2, axis=-1)
        x = x + (jax.nn.silu(gate) * up) @ w_down[l]
    return x
```

```python
import functools
import math

import jax
import jax.numpy as jnp
import numpy as np
from jax import lax
from jax.experimental import pallas as pl
from jax.experimental.pallas import tpu as pltpu

F32 = jnp.float32
BF16 = jnp.bfloat16

D_MODEL = 1024
DEPTH = 4
CHUNK = 64
HEAD_DIM = 64
H_DIFF = 4
N_PAST_CHUNKS = 8
REL_CLIP = 128
W_BRANCH = 512
QKV_W = 9 * W_BRANCH
N_BRANCH = 3
D_FF = int(math.ceil(8 * D_MODEL / 3 / 256)) * 256
RMS_EPS = 1e-6
QK_SCALE = HEAD_DIM ** -0.5

LANES = 128
T_Q = 256
T_K = 128
NEG = -1e30
VMEM_LIMIT = 56 * 1024 * 1024

ROW_TILE = 512
FF_CHUNK = 256


def _rms(x, g):
    return x * lax.rsqrt(jnp.mean(x * x, axis=-1, keepdims=True) + RMS_EPS) * g


def _rms_halves(x, g):
    lane = lax.broadcasted_iota(jnp.int32, (1, LANES), 1)
    first = lane < HEAD_DIM
    x2 = x * x
    s0 = jnp.sum(jnp.where(first, x2, 0.0), axis=-1, keepdims=True)
    s1 = jnp.sum(jnp.where(first, 0.0, x2), axis=-1, keepdims=True)
    ms = jnp.where(first, s0, s1) * (1.0 / HEAD_DIM)
    return x * lax.rsqrt(ms + RMS_EPS) * g


def _transpose_to_bf16(x):
    return x.astype(F32).T.astype(BF16)


def _fill_vt(v_ref, vt_ref):
    for j in range(v_ref.shape[0] // T_K):
        vt_ref[j] = _transpose_to_bf16(v_ref[j * T_K:(j + 1) * T_K, :])


def _head_row_masks():
    row = lax.broadcasted_iota(jnp.int32, (LANES, 1), 0)
    return row < HEAD_DIM, row >= HEAD_DIM


def _qkv_kernel(x_ref, g_ref, w_ref, o_ref):
    h = _rms(x_ref[...], g_ref[...]).astype(BF16)
    n = o_ref.shape[1]
    for c in range(0, n, W_BRANCH):
        o_ref[:, c:c + W_BRANCH] = jnp.dot(
            h, w_ref[:, c:c + W_BRANCH], preferred_element_type=F32).astype(BF16)


def _qkv_proj(x, g, w, layer):
    m = x.shape[0]
    return pl.pallas_call(
        _qkv_kernel,
        grid=(m // ROW_TILE,),
        in_specs=[
            pl.BlockSpec((ROW_TILE, D_MODEL), lambda i: (i, 0)),
            pl.BlockSpec((None, 1, D_MODEL), lambda i: (layer, 0, 0)),
            pl.BlockSpec((None, D_MODEL, QKV_W), lambda i: (layer, 0, 0)),
        ],
        out_specs=pl.BlockSpec((ROW_TILE, QKV_W), lambda i: (i, 0)),
        out_shape=jax.ShapeDtypeStruct((m, QKV_W), BF16),
        compiler_params=pltpu.CompilerParams(
            dimension_semantics=("arbitrary",), vmem_limit_bytes=VMEM_LIMIT),
        name="qkv_proj",
    )(x, g, w)


def _merge_kernel(x_ref, g_ref, oa_ref, ob_ref, oc_ref, wg_ref, bg_ref, wbr_ref, wo_ref, out_ref):
    x = x_ref[...]
    h = _rms(x, g_ref[...]).astype(BF16)
    merged = None
    for br, o_ref in enumerate((oa_ref, ob_ref, oc_ref)):
        g_lin = jnp.dot(h, wg_ref[:, br * D_MODEL:(br + 1) * D_MODEL], preferred_element_type=F32)
        gate = 1.0 / (1.0 + jnp.exp(-(g_lin + bg_ref[br:br + 1, :])))
        proj = jnp.dot(o_ref[...], wbr_ref[br * W_BRANCH:(br + 1) * W_BRANCH, :],
                       preferred_element_type=F32)
        term = gate * proj
        merged = term if merged is None else merged + term
    out_ref[...] = x + jnp.dot(merged.astype(BF16), wo_ref[...], preferred_element_type=F32)


def _merge_out(x, g, o_a, o_b, o_c, w_gate, b_gate, w_br, w_out, layer):
    m = x.shape[0]
    row = lambda i: (i, 0)
    lay = lambda i: (layer, 0, 0)
    return pl.pallas_call(
        _merge_kernel,
        grid=(m // ROW_TILE,),
        in_specs=[
            pl.BlockSpec((ROW_TILE, D_MODEL), row),
            pl.BlockSpec((None, 1, D_MODEL), lay),
            pl.BlockSpec((ROW_TILE, W_BRANCH), row),
            pl.BlockSpec((ROW_TILE, W_BRANCH), row),
            pl.BlockSpec((ROW_TILE, W_BRANCH), row),
            pl.BlockSpec((None, D_MODEL, N_BRANCH * D_MODEL), lay),
            pl.BlockSpec((None, N_BRANCH, D_MODEL), lay),
            pl.BlockSpec((None, N_BRANCH * W_BRANCH, D_MODEL), lay),
            pl.BlockSpec((None, D_MODEL, D_MODEL), lay),
        ],
        out_specs=pl.BlockSpec((ROW_TILE, D_MODEL), row),
        out_shape=jax.ShapeDtypeStruct((m, D_MODEL), F32),
        compiler_params=pltpu.CompilerParams(
            dimension_semantics=("arbitrary",), vmem_limit_bytes=VMEM_LIMIT),
        name="merge_out",
    )(x, g, o_a, o_b, o_c, w_gate, b_gate, w_br, w_out)


def _ffn_kernel(x_ref, g_ref, wgu_ref, wd_ref, out_ref, act_ref):
    x = x_ref[...]
    h = _rms(x, g_ref[...]).astype(BF16)
    for c in range(0, D_FF, FF_CHUNK):
        gate = jnp.dot(h, wgu_ref[:, c:c + FF_CHUNK], preferred_element_type=F32)
        up = jnp.dot(h, wgu_ref[:, D_FF + c:D_FF + c + FF_CHUNK], preferred_element_type=F32)
        silu = gate / (1.0 + jnp.exp(-gate))
        act_ref[:, c:c + FF_CHUNK] = (silu * up).astype(BF16)
    out_ref[...] = x + jnp.dot(act_ref[...], wd_ref[...], preferred_element_type=F32)


def _ffn(x, g, w_gu, w_down, layer):
    m = x.shape[0]
    row = lambda i: (i, 0)
    lay = lambda i: (layer, 0, 0)
    return pl.pallas_call(
        _ffn_kernel,
        grid=(m // ROW_TILE,),
        in_specs=[
            pl.BlockSpec((ROW_TILE, D_MODEL), row),
            pl.BlockSpec((None, 1, D_MODEL), lay),
            pl.BlockSpec((None, D_MODEL, 2 * D_FF), lay, pipeline_mode=pl.Buffered(1)),
            pl.BlockSpec((None, D_FF, D_MODEL), lay, pipeline_mode=pl.Buffered(1)),
        ],
        out_specs=pl.BlockSpec((ROW_TILE, D_MODEL), row),
        out_shape=jax.ShapeDtypeStruct((m, D_MODEL), F32),
        scratch_shapes=[pltpu.VMEM((ROW_TILE, D_FF), BF16)],
        compiler_params=pltpu.CompilerParams(
            dimension_semantics=("arbitrary",), vmem_limit_bytes=VMEM_LIMIT),
        name="ffn",
    )(x, g, w_gu, w_down)


def _sb_block(k2, vth, qth, cum, carry, acc, mask):
    z = jnp.dot(k2, qth, preferred_element_type=F32)
    sp = jnp.log(1.0 + jnp.exp(-jnp.abs(z)))
    log_beta = jnp.minimum(z, 0.0) - sp
    log_1m = log_beta - z
    if mask is not None:
        log_1m = jnp.where(mask, log_1m, 0.0)
    hi = log_1m.astype(BF16)
    lo = (log_1m - hi.astype(F32)).astype(BF16)
    after = jnp.dot(cum, jnp.concatenate([hi, lo], axis=0), preferred_element_type=F32)
    total = after[0:1, :] + log_1m[0:1, :]
    w = jnp.exp(log_beta + after + carry)
    if mask is not None:
        w = jnp.where(mask, w, 0.0)
    acc = acc + jnp.dot(vth, w.astype(BF16), preferred_element_type=F32)
    return carry + total, acc


def _sb_kernel(q_ref, k_ref, v_ref, o_ref, vt_ref):
    seq = q_ref.shape[0]
    _fill_vt(v_ref, vt_ref)
    first, second = _head_row_masks()
    kk = lax.broadcasted_iota(jnp.int32, (T_K, T_K), 0)
    kk2 = lax.broadcasted_iota(jnp.int32, (T_K, T_K), 1)
    later = jnp.where(kk2 > kk, 1.0, 0.0).astype(BF16)
    cum = jnp.concatenate([later, later], axis=1)
    d = (lax.broadcasted_iota(jnp.int32, (T_K, T_Q), 0)
         - lax.broadcasted_iota(jnp.int32, (T_K, T_Q), 1))
    diag_masks = (d < -T_K, d < 0)
    ratio = T_Q // T_K

    def q_tile(qi, _):
        q0 = pl.multiple_of(qi * T_Q, T_Q)
        qt = (q_ref[pl.ds(q0, T_Q), :].astype(F32) * QK_SCALE).T
        qts = (jnp.where(first, qt, 0.0).astype(BF16), jnp.where(second, qt, 0.0).astype(BF16))

        def tile(j, state, masks):
            k0 = pl.multiple_of(j * T_K, T_K)
            k2 = k_ref[pl.ds(k0, T_K), :]
            vt = vt_ref[j]
            out = []
            for h in range(2):
                carry, acc = state[h]
                out.append(_sb_block(k2, vt[h * HEAD_DIM:(h + 1) * HEAD_DIM, :], qts[h], cum,
                                     carry, acc, masks))
            return tuple(out)

        zero = (jnp.zeros((1, T_Q), F32), jnp.zeros((HEAD_DIM, T_Q), F32))
        state = (zero, zero)
        state = tile(qi * ratio + 1, state, diag_masks[0])
        state = tile(qi * ratio, state, diag_masks[1])
        state = lax.fori_loop(0, qi * ratio,
                              lambda it, st: tile(qi * ratio - 1 - it, st, None), state)
        out_t = jnp.concatenate([state[0][1], state[1][1]], axis=0)
        o_ref[pl.ds(q0, T_Q), :] = out_t.T.astype(BF16)
        return 0

    lax.fori_loop(0, seq // T_Q, q_tile, 0)


def _sb_attn(qkv):
    b, s, _ = qkv.shape
    blk = lambda col: pl.BlockSpec((None, s, LANES), lambda bi, hp: (bi, 0, col + hp))
    n_hp = W_BRANCH // LANES
    return pl.pallas_call(
        _sb_kernel,
        grid=(b, n_hp),
        in_specs=[blk(0), blk(n_hp), blk(2 * n_hp)],
        out_specs=pl.BlockSpec((None, s, LANES), lambda bi, hp: (bi, 0, hp)),
        out_shape=jax.ShapeDtypeStruct((b, s, W_BRANCH), BF16),
        scratch_shapes=[pltpu.VMEM((s // T_K, LANES, T_K), BF16)],
        compiler_params=pltpu.CompilerParams(
            dimension_semantics=("arbitrary", "arbitrary"), vmem_limit_bytes=VMEM_LIMIT),
        name="sb_attn",
    )(qkv, qkv, qkv)


def _softmax_block(s, vt, m, l, acc):
    m_new = jnp.maximum(m, jnp.max(s, axis=0, keepdims=True))
    alpha = jnp.exp(m - m_new)
    p = jnp.exp(s - m_new)
    l = alpha * l + jnp.sum(p, axis=0, keepdims=True)
    acc = alpha * acc + jnp.dot(vt, p.astype(BF16), preferred_element_type=F32)
    return m_new, l, acc


def _diff_kernel(slope_ref, q_ref, k_ref, v_ref, gq_ref, gk_ref, lam_ref, sg_ref, o_ref,
                 kn_ref, vt_ref, *, lam_init):
    seq = q_ref.shape[0]
    slope = slope_ref[pl.program_id(1)]
    _fill_vt(v_ref, vt_ref)
    for r in range(0, seq, T_Q):
        kn_ref[r:r + T_Q, :] = _rms_halves(k_ref[r:r + T_Q, :].astype(F32), gk_ref[...]).astype(BF16)
    first, second = _head_row_masks()
    lq = lam_ref[...]
    lam = (jnp.exp(jnp.sum(lq[0:1] * lq[1:2], axis=-1, keepdims=True))
           - jnp.exp(jnp.sum(lq[2:3] * lq[3:4], axis=-1, keepdims=True)) + lam_init)
    kpos = lax.broadcasted_iota(jnp.int32, (T_K, T_Q), 0)
    qpos = lax.broadcasted_iota(jnp.int32, (T_K, T_Q), 1)
    rel = (qpos - kpos).astype(F32)
    lin_bias = -slope * rel
    ratio = T_Q // T_K

    def q_tile(qi, _):
        q0 = pl.multiple_of(qi * T_Q, T_Q)
        qn = _rms_halves(q_ref[pl.ds(q0, T_Q), :].astype(F32), gq_ref[...]) * QK_SCALE
        qt = qn.T
        outs = []
        for sel in (first, second):
            qtm = jnp.where(sel, qt, 0.0).astype(BF16)

            def logits(j):
                k0 = pl.multiple_of(j * T_K, T_K)
                return jnp.dot(kn_ref[pl.ds(k0, T_K), :], qtm, preferred_element_type=F32)

            def full_tile(j, state):
                off = (qi * T_Q - j * T_K).astype(F32)
                s = logits(j) + lin_bias - slope * off
                return _softmax_block(s, vt_ref[j], *state)

            def edge_tile(j, state):
                off = qi * T_Q - j * T_K
                allowed = ((kpos + j * T_K) // CHUNK) <= ((qpos + qi * T_Q) // CHUNK)
                bias = -slope * jnp.abs(rel + off.astype(F32))
                s = jnp.where(allowed, logits(j) + bias, NEG)
                return _softmax_block(s, vt_ref[j], *state)

            state = (jnp.full((1, T_Q), NEG, F32), jnp.zeros((1, T_Q), F32),
                     jnp.zeros((LANES, T_Q), F32))
            state = lax.fori_loop(0, qi * ratio, full_tile, state)
            state = edge_tile(qi * ratio, state)
            state = edge_tile(qi * ratio + 1, state)
            outs.append(state[2] / state[1])
        ob = outs[0] - lam * outs[1]
        y = ob * lax.rsqrt(jnp.mean(ob * ob, axis=0, keepdims=True) + RMS_EPS)
        o_ref[pl.ds(q0, T_Q), :] = (y.T * sg_ref[...] * (1.0 - lam_init)).astype(BF16)
        return 0

    lax.fori_loop(0, seq // T_Q, q_tile, 0)


def _diff_attn(qkv, slopes, gq, gk, lam_qk, subln_g, layer, lam_init):
    b, s, _ = qkv.shape
    base = 3 * (W_BRANCH // LANES)
    blk = lambda col: pl.BlockSpec((None, s, LANES), lambda bi, h: (bi, 0, col + h))
    lay = lambda bi, h: (layer, 0, 0)
    return pl.pallas_call(
        functools.partial(_diff_kernel, lam_init=lam_init),
        grid=(b, H_DIFF),
        in_specs=[
            pl.BlockSpec(memory_space=pltpu.SMEM),
            blk(base), blk(base + H_DIFF), blk(base + 2 * H_DIFF),
            pl.BlockSpec((None, 1, LANES), lay),
            pl.BlockSpec((None, 1, LANES), lay),
            pl.BlockSpec((None, 4, HEAD_DIM), lay),
            pl.BlockSpec((None, 1, LANES), lay),
        ],
        out_specs=pl.BlockSpec((None, s, LANES), lambda bi, h: (bi, 0, h)),
        out_shape=jax.ShapeDtypeStruct((b, s, W_BRANCH), BF16),
        scratch_shapes=[pltpu.VMEM((s, LANES), BF16), pltpu.VMEM((s // T_K, LANES, T_K), BF16)],
        compiler_params=pltpu.CompilerParams(
            dimension_semantics=("arbitrary", "arbitrary"), vmem_limit_bytes=VMEM_LIMIT),
        name="diff_attn",
    )(slopes, qkv, qkv, qkv, gq, gk, lam_qk, subln_g)


N_REL_TILES = 6


def _chunk_kernel(q_ref, k_ref, v_ref, gq_ref, gk_ref, tab_ref, o_ref, kn_ref, vt_ref):
    seq = q_ref.shape[0]
    _fill_vt(v_ref, vt_ref)
    for r in range(0, seq, T_Q):
        kn_ref[r:r + T_Q, :] = _rms_halves(k_ref[r:r + T_Q, :].astype(F32), gk_ref[...]).astype(BF16)
    first, second = _head_row_masks()
    kchunk = lax.broadcasted_iota(jnp.int32, (T_K, T_Q), 0) // CHUNK
    qchunk = lax.broadcasted_iota(jnp.int32, (T_K, T_Q), 1) // CHUNK
    dchunk = qchunk - kchunk
    ratio = T_Q // T_K

    def q_tile(qi, _):
        q0 = pl.multiple_of(qi * T_Q, T_Q)
        qn = _rms_halves(q_ref[pl.ds(q0, T_Q), :].astype(F32), gq_ref[...]) * QK_SCALE
        qt = qn.T
        qts = (jnp.where(first, qt, 0.0).astype(BF16), jnp.where(second, qt, 0.0).astype(BF16))

        def tile(j, state):
            k0 = pl.multiple_of(j * T_K, T_K)
            k2 = kn_ref[pl.ds(k0, T_K), :]
            vt = vt_ref[j]
            delta = qi * ratio - j
            dd = dchunk + delta * (T_K // CHUNK)
            allowed = (dd >= 0) & (dd <= N_PAST_CHUNKS)
            out = []
            for h in range(2):
                s = jnp.dot(k2, qts[h], preferred_element_type=F32) + tab_ref[h, delta + 1]
                s = jnp.where(allowed, s, NEG)
                out.append(_softmax_block(s, vt[h * HEAD_DIM:(h + 1) * HEAD_DIM, :], *state[h]))
            return tuple(out)

        init = (jnp.full((1, T_Q), NEG, F32), jnp.zeros((1, T_Q), F32),
                jnp.zeros((HEAD_DIM, T_Q), F32))
        lo = jnp.maximum(qi * ratio - (N_REL_TILES - 2), 0)
        state = lax.fori_loop(lo, qi * ratio + 2, tile, (init, init))
        out_t = jnp.concatenate([state[0][2] / state[0][1], state[1][2] / state[1][1]], axis=0)
        o_ref[pl.ds(q0, T_Q), :] = out_t.T.astype(BF16)
        return 0

    lax.fori_loop(0, seq // T_Q, q_tile, 0)


def _chunk_attn(qkv, gq, gk, tab, layer):
    b, s, _ = qkv.shape
    n_hp = W_BRANCH // LANES
    base = 6 * n_hp
    blk = lambda col: pl.BlockSpec((None, s, LANES), lambda hp, bi: (bi, 0, col + hp))
    lay = lambda hp, bi: (layer, 0, 0)
    return pl.pallas_call(
        _chunk_kernel,
        grid=(n_hp, b),
        in_specs=[
            blk(base), blk(base + n_hp), blk(base + 2 * n_hp),
            pl.BlockSpec((None, 1, LANES), lay),
            pl.BlockSpec((None, 1, LANES), lay),
            pl.BlockSpec((None, 2, N_REL_TILES, T_K, T_Q), lambda hp, bi: (layer, hp, 0, 0, 0)),
        ],
        out_specs=pl.BlockSpec((None, s, LANES), lambda hp, bi: (bi, 0, hp)),
        out_shape=jax.ShapeDtypeStruct((b, s, W_BRANCH), BF16),
        scratch_shapes=[pltpu.VMEM((s, LANES), BF16), pltpu.VMEM((s // T_K, LANES, T_K), BF16)],
        compiler_params=pltpu.CompilerParams(
            dimension_semantics=("arbitrary", "arbitrary"), vmem_limit_bytes=VMEM_LIMIT),
        name="chunk_attn",
    )(qkv, qkv, qkv, gq, gk, tab)


def _rel_bias_tiles(rel_bias):
    r = np.arange(T_K)[:, None]
    c = np.arange(T_Q)[None, :]
    d = np.arange(-1, N_REL_TILES - 1)[:, None, None]
    idx = np.clip(d * T_K + c - r, -REL_CLIP, REL_CLIP) + REL_CLIP
    return rel_bias[:, :, idx]


def kernel(x, norm_mix_g, w_in, b_gate, qk_g_diff, lambda_qk, subln_g, qk_g_ch, rel_bias,
           w_branch_sb, w_branch_diff, w_branch_ch, w_out, norm_ffn_g, w_gu, w_down):
    b, s, d = x.shape
    m = b * s
    w_qkv = w_in[:, :, :QKV_W].astype(BF16)
    w_gate = w_in[:, :, QKV_W:].astype(BF16)
    w_br = jnp.concatenate([w_branch_sb, w_branch_diff, w_branch_ch], axis=1).astype(BF16)
    w_out_b = w_out.astype(BF16)
    w_gu_b = w_gu.astype(BF16)
    w_down_b = w_down.astype(BF16)
    g_mix = norm_mix_g.reshape(DEPTH, 1, d)
    g_ffn = norm_ffn_g.reshape(DEPTH, 1, d)
    gq_diff = jnp.tile(qk_g_diff[:, 0:1, :], (1, 1, 2))
    gk_diff = jnp.tile(qk_g_diff[:, 1:2, :], (1, 1, 2))
    gq_ch = jnp.tile(qk_g_ch[:, 0:1, :], (1, 1, 2))
    gk_ch = jnp.tile(qk_g_ch[:, 1:2, :], (1, 1, 2))
    sg = subln_g.reshape(DEPTH, 1, 2 * HEAD_DIM)
    tab = _rel_bias_tiles(rel_bias)
    slopes = jnp.asarray([2.0 ** (-8.0 * (i + 1) / H_DIFF) for i in range(H_DIFF)], F32)

    xf = x.reshape(m, d)
    for layer in range(DEPTH):
        lam_init = 0.8 - 0.6 * math.exp(-0.3 * layer)
        qkv = _qkv_proj(xf, g_mix, w_qkv, layer).reshape(b, s, QKV_W)
        o_a = _sb_attn(qkv)
        o_b = _diff_attn(qkv, slopes, gq_diff, gk_diff, lambda_qk, sg, layer, lam_init)
        o_c = _chunk_attn(qkv, gq_ch, gk_ch, tab, layer)
        xf = _merge_out(xf, g_mix, o_a.reshape(m, W_BRANCH), o_b.reshape(m, W_BRANCH),
                        o_c.reshape(m, W_BRANCH), w_gate, b_gate, w_br, w_out_b, layer)
        xf = _ffn(xf, g_ffn, w_gu_b, w_down_b, layer)
    return xf.reshape(b, s, d)
```

```python
import functools
import math

import jax
import jax.numpy as jnp
import numpy as np
from jax import lax
from jax.experimental import pallas as pl
from jax.experimental.pallas import tpu as pltpu

F32 = jnp.float32
BF16 = jnp.bfloat16

D_MODEL = 1024
DEPTH = 4
CHUNK = 64
HEAD_DIM = 64
H_DIFF = 4
N_PAST_CHUNKS = 8
REL_CLIP = 128
W_BRANCH = 512
QKV_W = 9 * W_BRANCH
N_BRANCH = 3
D_FF = int(math.ceil(8 * D_MODEL / 3 / 256)) * 256
RMS_EPS = 1e-6
QK_SCALE = HEAD_DIM ** -0.5

LANES = 128
T_Q = 256
T_K = 128
UNROLL = 4
NEG = -1e30
VMEM_LIMIT = 56 * 1024 * 1024

ROW_TILE = 512
FF_CHUNK = 256


def _rms(x, g):
    return x * lax.rsqrt(jnp.mean(x * x, axis=-1, keepdims=True) + RMS_EPS) * g


def _rms_halves(x, g):
    lane = lax.broadcasted_iota(jnp.int32, (1, LANES), 1)
    first = lane < HEAD_DIM
    x2 = x * x
    s0 = jnp.sum(jnp.where(first, x2, 0.0), axis=-1, keepdims=True)
    s1 = jnp.sum(jnp.where(first, 0.0, x2), axis=-1, keepdims=True)
    ms = jnp.where(first, s0, s1) * (1.0 / HEAD_DIM)
    return x * lax.rsqrt(ms + RMS_EPS) * g


def _transpose_to_bf16(x):
    return x.astype(F32).T.astype(BF16)


def _fill_vt(v_ref, vt_ref):
    for j in range(v_ref.shape[0] // T_K):
        vt_ref[j] = _transpose_to_bf16(v_ref[j * T_K:(j + 1) * T_K, :])


def _head_row_masks():
    row = lax.broadcasted_iota(jnp.int32, (LANES, 1), 0)
    return row < HEAD_DIM, row >= HEAD_DIM


def _qkv_kernel(x_ref, g_ref, w_ref, o_ref):
    h = _rms(x_ref[...], g_ref[...]).astype(BF16)
    n = o_ref.shape[1]
    for c in range(0, n, W_BRANCH):
        o_ref[:, c:c + W_BRANCH] = jnp.dot(
            h, w_ref[:, c:c + W_BRANCH], preferred_element_type=F32).astype(BF16)


def _qkv_proj(x, g, w, layer):
    m = x.shape[0]
    return pl.pallas_call(
        _qkv_kernel,
        grid=(m // ROW_TILE,),
        in_specs=[
            pl.BlockSpec((ROW_TILE, D_MODEL), lambda i: (i, 0)),
            pl.BlockSpec((None, 1, D_MODEL), lambda i: (layer, 0, 0)),
            pl.BlockSpec((None, D_MODEL, QKV_W), lambda i: (layer, 0, 0)),
        ],
        out_specs=pl.BlockSpec((ROW_TILE, QKV_W), lambda i: (i, 0)),
        out_shape=jax.ShapeDtypeStruct((m, QKV_W), BF16),
        compiler_params=pltpu.CompilerParams(
            dimension_semantics=("arbitrary",), vmem_limit_bytes=VMEM_LIMIT),
        name="qkv_proj",
    )(x, g, w)


def _merge_kernel(x_ref, g_ref, oa_ref, ob_ref, oc_ref, wg_ref, bg_ref, wbr_ref, wo_ref, out_ref):
    x = x_ref[...]
    h = _rms(x, g_ref[...]).astype(BF16)
    merged = None
    for br, o_ref in enumerate((oa_ref, ob_ref, oc_ref)):
        g_lin = jnp.dot(h, wg_ref[:, br * D_MODEL:(br + 1) * D_MODEL], preferred_element_type=F32)
        gate = 1.0 / (1.0 + jnp.exp(-(g_lin + bg_ref[br:br + 1, :])))
        proj = jnp.dot(o_ref[...], wbr_ref[br * W_BRANCH:(br + 1) * W_BRANCH, :],
                       preferred_element_type=F32)
        term = gate * proj
        merged = term if merged is None else merged + term
    out_ref[...] = x + jnp.dot(merged.astype(BF16), wo_ref[...], preferred_element_type=F32)


def _merge_out(x, g, o_a, o_b, o_c, w_gate, b_gate, w_br, w_out, layer):
    m = x.shape[0]
    row = lambda i: (i, 0)
    lay = lambda i: (layer, 0, 0)
    return pl.pallas_call(
        _merge_kernel,
        grid=(m // ROW_TILE,),
        in_specs=[
            pl.BlockSpec((ROW_TILE, D_MODEL), row),
            pl.BlockSpec((None, 1, D_MODEL), lay),
            pl.BlockSpec((ROW_TILE, W_BRANCH), row),
            pl.BlockSpec((ROW_TILE, W_BRANCH), row),
            pl.BlockSpec((ROW_TILE, W_BRANCH), row),
            pl.BlockSpec((None, D_MODEL, N_BRANCH * D_MODEL), lay),
            pl.BlockSpec((None, N_BRANCH, D_MODEL), lay),
            pl.BlockSpec((None, N_BRANCH * W_BRANCH, D_MODEL), lay),
            pl.BlockSpec((None, D_MODEL, D_MODEL), lay),
        ],
        out_specs=pl.BlockSpec((ROW_TILE, D_MODEL), row),
        out_shape=jax.ShapeDtypeStruct((m, D_MODEL), F32),
        compiler_params=pltpu.CompilerParams(
            dimension_semantics=("arbitrary",), vmem_limit_bytes=VMEM_LIMIT),
        name="merge_out",
    )(x, g, o_a, o_b, o_c, w_gate, b_gate, w_br, w_out)


def _ffn_kernel(x_ref, g_ref, wgu_ref, wd_ref, out_ref, act_ref):
    x = x_ref[...]
    h = _rms(x, g_ref[...]).astype(BF16)
    for c in range(0, D_FF, FF_CHUNK):
        gate = jnp.dot(h, wgu_ref[:, c:c + FF_CHUNK], preferred_element_type=F32)
        up = jnp.dot(h, wgu_ref[:, D_FF + c:D_FF + c + FF_CHUNK], preferred_element_type=F32)
        silu = gate / (1.0 + jnp.exp(-gate))
        act_ref[:, c:c + FF_CHUNK] = (silu * up).astype(BF16)
    out_ref[...] = x + jnp.dot(act_ref[...], wd_ref[...], preferred_element_type=F32)


def _ffn(x, g, w_gu, w_down, layer):
    m = x.shape[0]
    row = lambda i: (i, 0)
    lay = lambda i: (layer, 0, 0)
    return pl.pallas_call(
        _ffn_kernel,
        grid=(m // ROW_TILE,),
        in_specs=[
            pl.BlockSpec((ROW_TILE, D_MODEL), row),
            pl.BlockSpec((None, 1, D_MODEL), lay),
            pl.BlockSpec((None, D_MODEL, 2 * D_FF), lay, pipeline_mode=pl.Buffered(1)),
            pl.BlockSpec((None, D_FF, D_MODEL), lay, pipeline_mode=pl.Buffered(1)),
        ],
        out_specs=pl.BlockSpec((ROW_TILE, D_MODEL), row),
        out_shape=jax.ShapeDtypeStruct((m, D_MODEL), F32),
        scratch_shapes=[pltpu.VMEM((ROW_TILE, D_FF), BF16)],
        compiler_params=pltpu.CompilerParams(
            dimension_semantics=("arbitrary",), vmem_limit_bytes=VMEM_LIMIT),
        name="ffn",
    )(x, g, w_gu, w_down)


def _sb_block(k2, vth, qth, cum, carry, acc, mask):
    z = jnp.dot(k2, qth, preferred_element_type=F32)
    sp = jnp.log(1.0 + jnp.exp(-jnp.abs(z)))
    log_beta = jnp.minimum(z, 0.0) - sp
    log_1m = log_beta - z
    if mask is not None:
        log_1m = jnp.where(mask, log_1m, 0.0)
    hi = log_1m.astype(BF16)
    lo = (log_1m - hi.astype(F32)).astype(BF16)
    after = jnp.dot(cum, jnp.concatenate([hi, lo], axis=0), preferred_element_type=F32)
    total = after[0:1, :] + log_1m[0:1, :]
    w = jnp.exp(log_beta + after + carry)
    if mask is not None:
        w = jnp.where(mask, w, 0.0)
    acc = acc + jnp.dot(vth, w.astype(BF16), preferred_element_type=F32)
    return carry + total, acc


def _sb_kernel(q_ref, k_ref, v_ref, o_ref, vt_ref):
    seq = q_ref.shape[0]
    _fill_vt(v_ref, vt_ref)
    first, second = _head_row_masks()
    kk = lax.broadcasted_iota(jnp.int32, (T_K, T_K), 0)
    kk2 = lax.broadcasted_iota(jnp.int32, (T_K, T_K), 1)
    later = jnp.where(kk2 > kk, 1.0, 0.0).astype(BF16)
    cum = jnp.concatenate([later, later], axis=1)
    d = (lax.broadcasted_iota(jnp.int32, (T_K, T_Q), 0)
         - lax.broadcasted_iota(jnp.int32, (T_K, T_Q), 1))
    diag_masks = (d < -T_K, d < 0)

    def q_tile(qp, parity):
        qi = 2 * qp + parity
        q0 = pl.multiple_of(qi * T_Q, T_Q)
        qt = (q_ref[pl.ds(q0, T_Q), :].astype(F32) * QK_SCALE).T
        qts = (jnp.where(first, qt, 0.0).astype(BF16), jnp.where(second, qt, 0.0).astype(BF16))

        def tile(j, state, masks):
            k0 = pl.multiple_of(j * T_K, T_K)
            k2 = k_ref[pl.ds(k0, T_K), :]
            vt = vt_ref[j]
            out = []
            for h in range(2):
                carry, acc = state[h]
                out.append(_sb_block(k2, vt[h * HEAD_DIM:(h + 1) * HEAD_DIM, :], qts[h], cum,
                                     carry, acc, masks))
            return tuple(out)

        zero = (jnp.zeros((1, T_Q), F32), jnp.zeros((HEAD_DIM, T_Q), F32))
        state = (zero, zero)
        top = 2 * qi + 1
        state = tile(top, state, diag_masks[0])
        state = tile(top - 1, state, diag_masks[1])
        if parity:
            state = tile(top - 2, state, None)
            state = tile(top - 3, state, None)
        def group(it, st):
            for u in range(UNROLL):
                st = tile(4 * qp - 1 - UNROLL * it - u, st, None)
            return st
        state = lax.fori_loop(0, qp * (4 // UNROLL), group, state)
        out_t = jnp.concatenate([state[0][1], state[1][1]], axis=0)
        o_ref[pl.ds(q0, T_Q), :] = out_t.T.astype(BF16)

    def q_pair(qp, _):
        q_tile(qp, 0)
        q_tile(qp, 1)
        return 0

    lax.fori_loop(0, seq // (2 * T_Q), q_pair, 0)


def _sb_attn(qkv):
    b, s, _ = qkv.shape
    blk = lambda col: pl.BlockSpec((None, s, LANES), lambda bi, hp: (bi, 0, col + hp))
    n_hp = W_BRANCH // LANES
    return pl.pallas_call(
        _sb_kernel,
        grid=(b, n_hp),
        in_specs=[blk(0), blk(n_hp), blk(2 * n_hp)],
        out_specs=pl.BlockSpec((None, s, LANES), lambda bi, hp: (bi, 0, hp)),
        out_shape=jax.ShapeDtypeStruct((b, s, W_BRANCH), BF16),
        scratch_shapes=[pltpu.VMEM((s // T_K, LANES, T_K), BF16)],
        compiler_params=pltpu.CompilerParams(
            dimension_semantics=("arbitrary", "arbitrary"), vmem_limit_bytes=VMEM_LIMIT),
        name="sb_attn",
    )(qkv, qkv, qkv)


def _softmax_block(s, vt, m, l, acc):
    m_new = jnp.maximum(m, jnp.max(s, axis=0, keepdims=True))
    alpha = jnp.exp(m - m_new)
    p = jnp.exp(s - m_new)
    l = alpha * l + jnp.sum(p, axis=0, keepdims=True)
    acc = alpha * acc + jnp.dot(vt, p.astype(BF16), preferred_element_type=F32)
    return m_new, l, acc


def _diff_kernel(slope_ref, q_ref, k_ref, v_ref, gq_ref, gk_ref, lam_ref, sg_ref, o_ref,
                 kn_ref, vt_ref, *, lam_init):
    seq = q_ref.shape[0]
    slope = slope_ref[pl.program_id(1)]
    _fill_vt(v_ref, vt_ref)
    for r in range(0, seq, T_Q):
        kn_ref[r:r + T_Q, :] = _rms_halves(k_ref[r:r + T_Q, :].astype(F32), gk_ref[...]).astype(BF16)
    first, second = _head_row_masks()
    lq = lam_ref[...]
    lam = (jnp.exp(jnp.sum(lq[0:1] * lq[1:2], axis=-1, keepdims=True))
           - jnp.exp(jnp.sum(lq[2:3] * lq[3:4], axis=-1, keepdims=True)) + lam_init)
    kpos = lax.broadcasted_iota(jnp.int32, (T_K, T_Q), 0)
    qpos = lax.broadcasted_iota(jnp.int32, (T_K, T_Q), 1)
    rel = (qpos - kpos).astype(F32)
    lin_bias = -slope * rel
    edge_allowed = [((kpos + e * T_K) // CHUNK) <= (qpos // CHUNK) for e in range(2)]
    edge_bias = [-slope * jnp.abs(rel - float(e * T_K)) for e in range(2)]

    def q_tile(qp, parity):
        qi = 2 * qp + parity
        q0 = pl.multiple_of(qi * T_Q, T_Q)
        qn = _rms_halves(q_ref[pl.ds(q0, T_Q), :].astype(F32), gq_ref[...]) * QK_SCALE
        qt = qn.T
        qtm = (jnp.where(first, qt, 0.0).astype(BF16), jnp.where(second, qt, 0.0).astype(BF16))

        def tile(j, state, edge):
            k0 = pl.multiple_of(j * T_K, T_K)
            kn2 = kn_ref[pl.ds(k0, T_K), :]
            vt = vt_ref[j]
            out = []
            for mp in range(2):
                z = jnp.dot(kn2, qtm[mp], preferred_element_type=F32)
                if edge is None:
                    s = z + lin_bias
                    c = -slope * (qi * T_Q - j * T_K).astype(F32)
                else:
                    s = jnp.where(edge_allowed[edge], z + edge_bias[edge], NEG)
                    c = 0.0
                m, l, acc = state[mp]
                m_new = jnp.maximum(m, jnp.max(s, axis=0, keepdims=True) + c)
                alpha = jnp.exp(m - m_new)
                p = jnp.exp(s + (c - m_new))
                l = alpha * l + jnp.sum(p, axis=0, keepdims=True)
                acc = alpha * acc + jnp.dot(vt, p.astype(BF16), preferred_element_type=F32)
                out.append((m_new, l, acc))
            return tuple(out)

        init = (jnp.full((1, T_Q), NEG, F32), jnp.zeros((1, T_Q), F32), jnp.zeros((LANES, T_Q), F32))

        def group(it, st):
            for u in range(UNROLL):
                st = tile(UNROLL * it + u, st, None)
            return st
        state = lax.fori_loop(0, qp * (4 // UNROLL), group, (init, init))
        if parity:
            state = tile(4 * qp, state, None)
            state = tile(4 * qp + 1, state, None)
        state = tile(2 * qi, state, 0)
        state = tile(2 * qi + 1, state, 1)
        ob = state[0][2] / state[0][1] - lam * (state[1][2] / state[1][1])
        y = ob * lax.rsqrt(jnp.mean(ob * ob, axis=0, keepdims=True) + RMS_EPS)
        o_ref[pl.ds(q0, T_Q), :] = (y.T * sg_ref[...] * (1.0 - lam_init)).astype(BF16)

    def q_pair(qp, _):
        q_tile(qp, 0)
        q_tile(qp, 1)
        return 0

    lax.fori_loop(0, seq // (2 * T_Q), q_pair, 0)


def _diff_attn(qkv, slopes, gq, gk, lam_qk, subln_g, layer, lam_init):
    b, s, _ = qkv.shape
    base = 3 * (W_BRANCH // LANES)
    blk = lambda col: pl.BlockSpec((None, s, LANES), lambda bi, h: (bi, 0, col + h))
    lay = lambda bi, h: (layer, 0, 0)
    return pl.pallas_call(
        functools.partial(_diff_kernel, lam_init=lam_init),
        grid=(b, H_DIFF),
        in_specs=[
            pl.BlockSpec(memory_space=pltpu.SMEM),
            blk(base), blk(base + H_DIFF), blk(base + 2 * H_DIFF),
            pl.BlockSpec((None, 1, LANES), lay),
            pl.BlockSpec((None, 1, LANES), lay),
            pl.BlockSpec((None, 4, HEAD_DIM), lay),
            pl.BlockSpec((None, 1, LANES), lay),
        ],
        out_specs=pl.BlockSpec((None, s, LANES), lambda bi, h: (bi, 0, h)),
        out_shape=jax.ShapeDtypeStruct((b, s, W_BRANCH), BF16),
        scratch_shapes=[pltpu.VMEM((s, LANES), BF16), pltpu.VMEM((s // T_K, LANES, T_K), BF16)],
        compiler_params=pltpu.CompilerParams(
            dimension_semantics=("arbitrary", "arbitrary"), vmem_limit_bytes=VMEM_LIMIT),
        name="diff_attn",
    )(slopes, qkv, qkv, qkv, gq, gk, lam_qk, subln_g)


N_REL_TILES = 6


def _chunk_kernel(q_ref, k_ref, v_ref, gq_ref, gk_ref, tab_ref, o_ref, kn_ref, vt_ref):
    seq = q_ref.shape[0]
    _fill_vt(v_ref, vt_ref)
    for r in range(0, seq, T_Q):
        kn_ref[r:r + T_Q, :] = _rms_halves(k_ref[r:r + T_Q, :].astype(F32), gk_ref[...]).astype(BF16)
    first, second = _head_row_masks()
    kchunk = lax.broadcasted_iota(jnp.int32, (T_K, T_Q), 0) // CHUNK
    qchunk = lax.broadcasted_iota(jnp.int32, (T_K, T_Q), 1) // CHUNK
    dchunk = qchunk - kchunk
    chunks_per_tile = T_K // CHUNK
    deltas = tuple(range(N_REL_TILES - 2, -2, -1))
    allowed = {}
    for dl in deltas:
        lo_dd, hi_dd = dl * chunks_per_tile - 1, dl * chunks_per_tile + T_Q // CHUNK - 1
        if lo_dd < 0 or hi_dd > N_PAST_CHUNKS:
            dd = dchunk + dl * chunks_per_tile
            allowed[dl] = (dd >= 0) & (dd <= N_PAST_CHUNKS)

    def q_tile(qi, tile_deltas):
        q0 = pl.multiple_of(qi * T_Q, T_Q)
        qn = _rms_halves(q_ref[pl.ds(q0, T_Q), :].astype(F32), gq_ref[...]) * QK_SCALE
        qt = qn.T
        qts = (jnp.where(first, qt, 0.0).astype(BF16), jnp.where(second, qt, 0.0).astype(BF16))
        init = (jnp.full((1, T_Q), NEG, F32), jnp.zeros((1, T_Q), F32),
                jnp.zeros((HEAD_DIM, T_Q), F32))
        state = [init, init]
        for dl in tile_deltas:
            j = 2 * qi - dl
            k0 = pl.multiple_of(j * T_K, T_K)
            k2 = kn_ref[pl.ds(k0, T_K), :]
            vt = vt_ref[j]
            for h in range(2):
                s = jnp.dot(k2, qts[h], preferred_element_type=F32) + tab_ref[h, dl + 1]
                if dl in allowed:
                    s = jnp.where(allowed[dl], s, NEG)
                state[h] = _softmax_block(s, vt[h * HEAD_DIM:(h + 1) * HEAD_DIM, :], *state[h])
        out_t = jnp.concatenate([state[0][2] / state[0][1], state[1][2] / state[1][1]], axis=0)
        o_ref[pl.ds(q0, T_Q), :] = out_t.T.astype(BF16)

    first_full = (N_REL_TILES - 2) // 2
    for qi in range(first_full):
        q_tile(qi, tuple(dl for dl in deltas if 2 * qi - dl >= 0))

    def body(qi, _):
        q_tile(qi, deltas)
        return 0

    lax.fori_loop(first_full, seq // T_Q, body, 0)


def _chunk_attn(qkv, gq, gk, tab, layer):
    b, s, _ = qkv.shape
    n_hp = W_BRANCH // LANES
    base = 6 * n_hp
    blk = lambda col: pl.BlockSpec((None, s, LANES), lambda hp, bi: (bi, 0, col + hp))
    lay = lambda hp, bi: (layer, 0, 0)
    return pl.pallas_call(
        _chunk_kernel,
        grid=(n_hp, b),
        in_specs=[
            blk(base), blk(base + n_hp), blk(base + 2 * n_hp),
            pl.BlockSpec((None, 1, LANES), lay),
            pl.BlockSpec((None, 1, LANES), lay),
            pl.BlockSpec((None, 2, N_REL_TILES, T_K, T_Q), lambda hp, bi: (layer, hp, 0, 0, 0)),
        ],
        out_specs=pl.BlockSpec((None, s, LANES), lambda hp, bi: (bi, 0, hp)),
        out_shape=jax.ShapeDtypeStruct((b, s, W_BRANCH), BF16),
        scratch_shapes=[pltpu.VMEM((s, LANES), BF16), pltpu.VMEM((s // T_K, LANES, T_K), BF16)],
        compiler_params=pltpu.CompilerParams(
            dimension_semantics=("arbitrary", "arbitrary"), vmem_limit_bytes=VMEM_LIMIT),
        name="chunk_attn",
    )(qkv, qkv, qkv, gq, gk, tab)


def _rel_bias_tiles(rel_bias):
    lead = rel_bias.shape[:-1]
    period = (N_REL_TILES + 2) * T_K
    span = N_REL_TILES * T_K + T_Q - T_K
    pos = jnp.broadcast_to(rel_bias[..., -1:], lead + (span - rel_bias.shape[-1],))
    neg = jnp.broadcast_to(rel_bias[..., :1], lead + (period - span,))
    row = jnp.concatenate([rel_bias, pos, neg], axis=-1)
    flat = jnp.tile(row, (1,) * len(lead) + (T_K,))[..., :T_K * (period - 1)]
    toep = flat.reshape(lead + (T_K, period - 1))
    return jnp.stack([toep[..., d * T_K:d * T_K + T_Q] for d in range(N_REL_TILES)], axis=-3)


def kernel(x, norm_mix_g, w_in, b_gate, qk_g_diff, lambda_qk, subln_g, qk_g_ch, rel_bias,
           w_branch_sb, w_branch_diff, w_branch_ch, w_out, norm_ffn_g, w_gu, w_down):
    b, s, d = x.shape
    m = b * s
    w_qkv = w_in[:, :, :QKV_W].astype(BF16)
    w_gate = w_in[:, :, QKV_W:].astype(BF16)
    w_br = jnp.concatenate([w_branch_sb, w_branch_diff, w_branch_ch], axis=1).astype(BF16)
    w_out_b = w_out.astype(BF16)
    w_gu_b = w_gu.astype(BF16)
    w_down_b = w_down.astype(BF16)
    g_mix = norm_mix_g.reshape(DEPTH, 1, d)
    g_ffn = norm_ffn_g.reshape(DEPTH, 1, d)
    gq_diff = jnp.tile(qk_g_diff[:, 0:1, :], (1, 1, 2))
    gk_diff = jnp.tile(qk_g_diff[:, 1:2, :], (1, 1, 2))
    gq_ch = jnp.tile(qk_g_ch[:, 0:1, :], (1, 1, 2))
    gk_ch = jnp.tile(qk_g_ch[:, 1:2, :], (1, 1, 2))
    sg = subln_g.reshape(DEPTH, 1, 2 * HEAD_DIM)
    tab = _rel_bias_tiles(rel_bias)
    slopes = jnp.asarray([2.0 ** (-8.0 * (i + 1) / H_DIFF) for i in range(H_DIFF)], F32)

    xf = x.reshape(m, d)
    for layer in range(DEPTH):
        lam_init = 0.8 - 0.6 * math.exp(-0.3 * layer)
        qkv = _qkv_proj(xf, g_mix, w_qkv, layer).reshape(b, s, QKV_W)
        o_a = _sb_attn(qkv)
        o_b = _diff_attn(qkv, slopes, gq_diff, gk_diff, lambda_qk, sg, layer, lam_init)
        o_c = _chunk_attn(qkv, gq_ch, gk_ch, tab, layer)
        xf = _merge_out(xf, g_mix, o_a.reshape(m, W_BRANCH), o_b.reshape(m, W_BRANCH),
                        o_c.reshape(m, W_BRANCH), w_gate, b_gate, w_br, w_out_b, layer)
        xf = _ffn(xf, g_ffn, w_gu_b, w_down_b, layer)
    return xf.reshape(b, s, d)
```

```python
import functools
import math

import jax
import jax.numpy as jnp
import numpy as np
from jax import lax
from jax.experimental import pallas as pl
from jax.experimental.pallas import tpu as pltpu

F32 = jnp.float32
BF16 = jnp.bfloat16

D_MODEL = 1024
DEPTH = 4
CHUNK = 64
HEAD_DIM = 64
H_DIFF = 4
N_PAST_CHUNKS = 8
REL_CLIP = 128
W_BRANCH = 512
QKV_W = 9 * W_BRANCH
N_BRANCH = 3
D_FF = int(math.ceil(8 * D_MODEL / 3 / 256)) * 256
RMS_EPS = 1e-6
QK_SCALE = HEAD_DIM ** -0.5

LANES = 128
T_Q = 256
T_K = 128
T_G = 256
NEG = -1e30
VMEM_LIMIT = 56 * 1024 * 1024

ROW_TILE = 512
FF_CHUNK = 256


def _rms(x, g):
    return x * lax.rsqrt(jnp.mean(x * x, axis=-1, keepdims=True) + RMS_EPS) * g


def _rms_halves(x, g):
    lane = lax.broadcasted_iota(jnp.int32, (1, LANES), 1)
    first = lane < HEAD_DIM
    x2 = x * x
    s0 = jnp.sum(jnp.where(first, x2, 0.0), axis=-1, keepdims=True)
    s1 = jnp.sum(jnp.where(first, 0.0, x2), axis=-1, keepdims=True)
    ms = jnp.where(first, s0, s1) * (1.0 / HEAD_DIM)
    return x * lax.rsqrt(ms + RMS_EPS) * g


def _transpose_to_bf16(x):
    return x.astype(F32).T.astype(BF16)


def _fill_vt(v_ref, vt_ref):
    width = vt_ref.shape[-1]
    for j in range(v_ref.shape[0] // width):
        vt_ref[j] = _transpose_to_bf16(v_ref[j * width:(j + 1) * width, :])


def _head_row_masks():
    row = lax.broadcasted_iota(jnp.int32, (LANES, 1), 0)
    return row < HEAD_DIM, row >= HEAD_DIM


def _qkv_kernel(x_ref, g_ref, w_ref, o_ref):
    h = _rms(x_ref[...], g_ref[...]).astype(BF16)
    n = o_ref.shape[1]
    for c in range(0, n, W_BRANCH):
        o_ref[:, c:c + W_BRANCH] = jnp.dot(
            h, w_ref[:, c:c + W_BRANCH], preferred_element_type=F32).astype(BF16)


def _qkv_proj(x, g, w, layer):
    m = x.shape[0]
    return pl.pallas_call(
        _qkv_kernel,
        grid=(m // ROW_TILE,),
        in_specs=[
            pl.BlockSpec((ROW_TILE, D_MODEL), lambda i: (i, 0)),
            pl.BlockSpec((None, 1, D_MODEL), lambda i: (layer, 0, 0)),
            pl.BlockSpec((None, D_MODEL, QKV_W), lambda i: (layer, 0, 0)),
        ],
        out_specs=pl.BlockSpec((ROW_TILE, QKV_W), lambda i: (i, 0)),
        out_shape=jax.ShapeDtypeStruct((m, QKV_W), BF16),
        compiler_params=pltpu.CompilerParams(
            dimension_semantics=("arbitrary",), vmem_limit_bytes=VMEM_LIMIT),
        name="qkv_proj",
    )(x, g, w)


def _merge_kernel(x_ref, g_ref, oa_ref, ob_ref, oc_ref, wg_ref, bg_ref, wbr_ref, wo_ref, out_ref):
    x = x_ref[...]
    h = _rms(x, g_ref[...]).astype(BF16)
    merged = None
    for br, o_ref in enumerate((oa_ref, ob_ref, oc_ref)):
        g_lin = jnp.dot(h, wg_ref[:, br * D_MODEL:(br + 1) * D_MODEL], preferred_element_type=F32)
        gate = 1.0 / (1.0 + jnp.exp(-(g_lin + bg_ref[br:br + 1, :])))
        proj = jnp.dot(o_ref[...], wbr_ref[br * W_BRANCH:(br + 1) * W_BRANCH, :],
                       preferred_element_type=F32)
        term = gate * proj
        merged = term if merged is None else merged + term
    out_ref[...] = x + jnp.dot(merged.astype(BF16), wo_ref[...], preferred_element_type=F32)


def _merge_out(x, g, o_a, o_b, o_c, w_gate, b_gate, w_br, w_out, layer):
    m = x.shape[0]
    row = lambda i: (i, 0)
    lay = lambda i: (layer, 0, 0)
    return pl.pallas_call(
        _merge_kernel,
        grid=(m // ROW_TILE,),
        in_specs=[
            pl.BlockSpec((ROW_TILE, D_MODEL), row),
            pl.BlockSpec((None, 1, D_MODEL), lay),
            pl.BlockSpec((ROW_TILE, W_BRANCH), row),
            pl.BlockSpec((ROW_TILE, W_BRANCH), row),
            pl.BlockSpec((ROW_TILE, W_BRANCH), row),
            pl.BlockSpec((None, D_MODEL, N_BRANCH * D_MODEL), lay),
            pl.BlockSpec((None, N_BRANCH, D_MODEL), lay),
            pl.BlockSpec((None, N_BRANCH * W_BRANCH, D_MODEL), lay),
            pl.BlockSpec((None, D_MODEL, D_MODEL), lay),
        ],
        out_specs=pl.BlockSpec((ROW_TILE, D_MODEL), row),
        out_shape=jax.ShapeDtypeStruct((m, D_MODEL), F32),
        compiler_params=pltpu.CompilerParams(
            dimension_semantics=("arbitrary",), vmem_limit_bytes=VMEM_LIMIT),
        name="merge_out",
    )(x, g, o_a, o_b, o_c, w_gate, b_gate, w_br, w_out)


def _ffn_kernel(x_ref, g_ref, wgu_ref, wd_ref, out_ref, act_ref):
    x = x_ref[...]
    h = _rms(x, g_ref[...]).astype(BF16)
    for c in range(0, D_FF, FF_CHUNK):
        gate = jnp.dot(h, wgu_ref[:, c:c + FF_CHUNK], preferred_element_type=F32)
        up = jnp.dot(h, wgu_ref[:, D_FF + c:D_FF + c + FF_CHUNK], preferred_element_type=F32)
        silu = gate / (1.0 + jnp.exp(-gate))
        act_ref[:, c:c + FF_CHUNK] = (silu * up).astype(BF16)
    out_ref[...] = x + jnp.dot(act_ref[...], wd_ref[...], preferred_element_type=F32)


def _ffn(x, g, w_gu, w_down, layer):
    m = x.shape[0]
    row = lambda i: (i, 0)
    lay = lambda i: (layer, 0, 0)
    return pl.pallas_call(
        _ffn_kernel,
        grid=(m // ROW_TILE,),
        in_specs=[
            pl.BlockSpec((ROW_TILE, D_MODEL), row),
            pl.BlockSpec((None, 1, D_MODEL), lay),
            pl.BlockSpec((None, D_MODEL, 2 * D_FF), lay, pipeline_mode=pl.Buffered(1)),
            pl.BlockSpec((None, D_FF, D_MODEL), lay, pipeline_mode=pl.Buffered(1)),
        ],
        out_specs=pl.BlockSpec((ROW_TILE, D_MODEL), row),
        out_shape=jax.ShapeDtypeStruct((m, D_MODEL), F32),
        scratch_shapes=[pltpu.VMEM((ROW_TILE, D_FF), BF16)],
        compiler_params=pltpu.CompilerParams(
            dimension_semantics=("arbitrary",), vmem_limit_bytes=VMEM_LIMIT),
        name="ffn",
    )(x, g, w_gu, w_down)


def _sb_weights(z, cum, carry, mask):
    sp = jnp.log(1.0 + jnp.exp(-jnp.abs(z)))
    log_beta = jnp.minimum(z, 0.0) - sp
    log_1m = log_beta - z
    if mask is not None:
        log_1m = jnp.where(mask, log_1m, 0.0)
    hi = log_1m.astype(BF16)
    lo = (log_1m - hi.astype(F32)).astype(BF16)
    afters = []
    for u in reversed(range(z.shape[0] // T_K)):
        r0, r1 = u * T_K, (u + 1) * T_K
        within = jnp.dot(cum, jnp.concatenate([hi[r0:r1], lo[r0:r1]], axis=0),
                         preferred_element_type=F32)
        afters.append(within + carry)
        carry = carry + within[0:1, :] + log_1m[r0:r0 + 1, :]
    after = afters[0] if len(afters) == 1 else jnp.concatenate(afters[::-1], axis=0)
    w = jnp.exp(log_beta + after)
    if mask is not None:
        w = jnp.where(mask, w, 0.0)
    return w.astype(BF16), carry


def _sb_kernel(q_ref, k_ref, v_ref, o_ref, vt_ref):
    seq = q_ref.shape[0]
    _fill_vt(v_ref, vt_ref)
    first, second = _head_row_masks()
    kk = lax.broadcasted_iota(jnp.int32, (T_K, T_K), 0)
    kk2 = lax.broadcasted_iota(jnp.int32, (T_K, T_K), 1)
    later = jnp.where(kk2 > kk, 1.0, 0.0).astype(BF16)
    cum = jnp.concatenate([later, later], axis=1)
    strict = (lax.broadcasted_iota(jnp.int32, (T_G, T_Q), 0)
              < lax.broadcasted_iota(jnp.int32, (T_G, T_Q), 1))
    heads = range(2)

    def q_tile(qp, parity):
        qi = 2 * qp + parity
        q0 = pl.multiple_of(qi * T_Q, T_Q)
        qt = (q_ref[pl.ds(q0, T_Q), :].astype(F32) * QK_SCALE).T
        qts = (jnp.where(first, qt, 0.0).astype(BF16), jnp.where(second, qt, 0.0).astype(BF16))

        def logits(g):
            k2 = k_ref[pl.ds(pl.multiple_of(g * T_G, T_G), T_G), :]
            return [jnp.dot(k2, qts[h], preferred_element_type=F32) for h in heads]

        def add_pv(g, ws, accs):
            vt = vt_ref[g]
            return [accs[h] + jnp.dot(vt[h * HEAD_DIM:(h + 1) * HEAD_DIM, :], ws[h],
                                      preferred_element_type=F32) for h in heads]

        carries = [jnp.zeros((1, T_Q), F32) for _ in heads]
        accs = [jnp.zeros((HEAD_DIM, T_Q), F32) for _ in heads]

        def whole_group(g, mask, carries, accs):
            zs = logits(g)
            ws = [None, None]
            for h in heads:
                ws[h], carries[h] = _sb_weights(zs[h], cum, carries[h], mask)
            return carries, add_pv(g, ws, accs)

        carries, accs = whole_group(qi, strict, carries, accs)
        if parity:
            carries, accs = whole_group(qi - 1, None, carries, accs)

        def trip_groups(it):
            return 2 * qp - 1 - 2 * it, 2 * qp - 2 - 2 * it

        def trip(it, st):
            carries, accs, zs_cur, ws_prev = st
            carries, accs = list(carries), list(accs)
            g_prev = trip_groups(jnp.maximum(it - 1, 0))
            g_next = trip_groups(jnp.minimum(it + 1, qp - 1))
            for u in range(2):
                accs = add_pv(g_prev[u], ws_prev[u], accs)
            zs_next = tuple(tuple(logits(g_next[u])) for u in range(2))
            ws_cur = []
            for u in range(2):
                ws = [None, None]
                for h in heads:
                    ws[h], carries[h] = _sb_weights(zs_cur[u][h], cum, carries[h], None)
                ws_cur.append(tuple(ws))
            return tuple(carries), tuple(accs), zs_next, tuple(ws_cur)

        zero_w = tuple(tuple(jnp.zeros((T_G, T_Q), BF16) for _ in heads) for _ in range(2))
        g_first = trip_groups(0)
        zs0 = tuple(tuple(logits(jnp.maximum(g_first[u], 0))) for u in range(2))
        carries, accs, _, ws_last = lax.fori_loop(
            0, qp, trip, (tuple(carries), tuple(accs), zs0, zero_w))
        g_last = trip_groups(jnp.maximum(qp - 1, 0))
        accs = list(accs)
        for u in range(2):
            accs = add_pv(jnp.maximum(g_last[u], 0), ws_last[u], accs)
        out_t = jnp.concatenate(accs, axis=0)
        o_ref[pl.ds(q0, T_Q), :] = out_t.T.astype(BF16)

    def q_pair(qp, _):
        q_tile(qp, 0)
        q_tile(qp, 1)
        return 0

    lax.fori_loop(0, seq // (2 * T_Q), q_pair, 0)


def _sb_attn(qkv):
    b, s, _ = qkv.shape
    blk = lambda col: pl.BlockSpec((None, s, LANES), lambda bi, hp: (bi, 0, col + hp))
    n_hp = W_BRANCH // LANES
    return pl.pallas_call(
        _sb_kernel,
        grid=(b, n_hp),
        in_specs=[blk(0), blk(n_hp), blk(2 * n_hp)],
        out_specs=pl.BlockSpec((None, s, LANES), lambda bi, hp: (bi, 0, hp)),
        out_shape=jax.ShapeDtypeStruct((b, s, W_BRANCH), BF16),
        scratch_shapes=[pltpu.VMEM((s // T_G, LANES, T_G), BF16)],
        compiler_params=pltpu.CompilerParams(
            dimension_semantics=("arbitrary", "arbitrary"), vmem_limit_bytes=VMEM_LIMIT),
        name="sb_attn",
    )(qkv, qkv, qkv)


def _softmax_update(scores, offsets, vts, state):
    m, l, acc = state
    m_new = m
    for s, c in zip(scores, offsets):
        m_new = jnp.maximum(m_new, jnp.max(s, axis=0, keepdims=True) + c)
    alpha = jnp.exp(m - m_new)
    l = alpha * l
    acc = alpha * acc
    for s, c, vt in zip(scores, offsets, vts):
        p = jnp.exp(s + (c - m_new))
        l = l + jnp.sum(p, axis=0, keepdims=True)
        acc = acc + jnp.dot(vt, p.astype(BF16), preferred_element_type=F32)
    return m_new, l, acc


def _diff_kernel(slope_ref, q_ref, k_ref, v_ref, gq_ref, gk_ref, lam_ref, sg_ref, o_ref,
                 kn_ref, vt_ref, *, lam_init):
    seq = q_ref.shape[0]
    slope = slope_ref[pl.program_id(1)]
    _fill_vt(v_ref, vt_ref)
    for r in range(0, seq, T_Q):
        kn_ref[r:r + T_Q, :] = _rms_halves(k_ref[r:r + T_Q, :].astype(F32), gk_ref[...]).astype(BF16)
    first, second = _head_row_masks()
    lq = lam_ref[...]
    lam = (jnp.exp(jnp.sum(lq[0:1] * lq[1:2], axis=-1, keepdims=True))
           - jnp.exp(jnp.sum(lq[2:3] * lq[3:4], axis=-1, keepdims=True)) + lam_init)
    kpos = lax.broadcasted_iota(jnp.int32, (T_G, T_Q), 0)
    qpos = lax.broadcasted_iota(jnp.int32, (T_G, T_Q), 1)
    rel = (qpos - kpos).astype(F32)
    lin_bias = -slope * rel
    diag_allowed = (kpos // CHUNK) <= (qpos // CHUNK)
    diag_bias = -slope * jnp.abs(rel)
    chains = [(par, mp) for par in range(2) for mp in range(2)]

    def q_pair(qp, _):
        qtm = {}
        for par in range(2):
            q0 = pl.multiple_of((2 * qp + par) * T_Q, T_Q)
            qn = _rms_halves(q_ref[pl.ds(q0, T_Q), :].astype(F32), gq_ref[...]) * QK_SCALE
            qt = qn.T
            qtm[par, 0] = jnp.where(first, qt, 0.0).astype(BF16)
            qtm[par, 1] = jnp.where(second, qt, 0.0).astype(BF16)

        def keys(g):
            return kn_ref[pl.ds(pl.multiple_of(g * T_G, T_G), T_G), :]

        def full_scores(kn2, g, chain):
            par, _ = chain
            off = jnp.asarray((2 * qp + par) * T_Q - g * T_G, F32)
            return jnp.dot(kn2, qtm[chain], preferred_element_type=F32) + lin_bias, -slope * off

        def diag_scores(kn2, chain):
            z = jnp.dot(kn2, qtm[chain], preferred_element_type=F32)
            return jnp.where(diag_allowed, z + diag_bias, NEG), 0.0

        def trip(it, states):
            gs = (2 * it, 2 * it + 1)
            kns = [keys(g) for g in gs]
            vts = [vt_ref[g] for g in gs]
            out = []
            scs = [[full_scores(kn2, g, chain) for kn2, g in zip(kns, gs)] for chain in chains]
            for sc, st in zip(scs, states):
                out.append(_softmax_update([s for s, _ in sc], [c for _, c in sc], vts, st))
            return tuple(out)

        init = (jnp.full((1, T_Q), NEG, F32), jnp.zeros((1, T_Q), F32), jnp.zeros((LANES, T_Q), F32))
        states = list(lax.fori_loop(0, qp, trip, (init,) * len(chains)))
        g = 2 * qp
        kns, vts = [keys(g), keys(g + 1)], [vt_ref[g], vt_ref[g + 1]]
        scs = []
        for chain in chains:
            if chain[0] == 0:
                scs.append([diag_scores(kns[0], chain)])
            else:
                scs.append([full_scores(kns[0], g, chain), diag_scores(kns[1], chain)])
        for i, sc in enumerate(scs):
            states[i] = _softmax_update([s for s, _ in sc], [c for _, c in sc], vts[:len(sc)], states[i])
        for par in range(2):
            (_, l0, a0), (_, l1, a1) = states[2 * par], states[2 * par + 1]
            ob = a0 / l0 - lam * (a1 / l1)
            y = ob * lax.rsqrt(jnp.mean(ob * ob, axis=0, keepdims=True) + RMS_EPS)
            q0 = pl.multiple_of((2 * qp + par) * T_Q, T_Q)
            o_ref[pl.ds(q0, T_Q), :] = (y.T * sg_ref[...] * (1.0 - lam_init)).astype(BF16)
        return 0

    lax.fori_loop(0, seq // (2 * T_Q), q_pair, 0)


def _diff_attn(qkv, slopes, gq, gk, lam_qk, subln_g, layer, lam_init):
    b, s, _ = qkv.shape
    base = 3 * (W_BRANCH // LANES)
    blk = lambda col: pl.BlockSpec((None, s, LANES), lambda bi, h: (bi, 0, col + h))
    lay = lambda bi, h: (layer, 0, 0)
    return pl.pallas_call(
        functools.partial(_diff_kernel, lam_init=lam_init),
        grid=(b, H_DIFF),
        in_specs=[
            pl.BlockSpec(memory_space=pltpu.SMEM),
            blk(base), blk(base + H_DIFF), blk(base + 2 * H_DIFF),
            pl.BlockSpec((None, 1, LANES), lay),
            pl.BlockSpec((None, 1, LANES), lay),
            pl.BlockSpec((None, 4, HEAD_DIM), lay),
            pl.BlockSpec((None, 1, LANES), lay),
        ],
        out_specs=pl.BlockSpec((None, s, LANES), lambda bi, h: (bi, 0, h)),
        out_shape=jax.ShapeDtypeStruct((b, s, W_BRANCH), BF16),
        scratch_shapes=[pltpu.VMEM((s, LANES), BF16), pltpu.VMEM((s // T_G, LANES, T_G), BF16)],
        compiler_params=pltpu.CompilerParams(
            dimension_semantics=("arbitrary", "arbitrary"), vmem_limit_bytes=VMEM_LIMIT),
        name="diff_attn",
    )(slopes, qkv, qkv, qkv, gq, gk, lam_qk, subln_g)


N_REL_GROUPS = (N_PAST_CHUNKS * CHUNK + T_Q - 1) // T_G + 1


def _chunk_kernel(q_ref, k_ref, v_ref, gq_ref, gk_ref, tab_ref, o_ref, kn_ref, vt_ref):
    seq = q_ref.shape[0]
    _fill_vt(v_ref, vt_ref)
    for r in range(0, seq, T_Q):
        kn_ref[r:r + T_Q, :] = _rms_halves(k_ref[r:r + T_Q, :].astype(F32), gk_ref[...]).astype(BF16)
    first, second = _head_row_masks()
    kchunk = lax.broadcasted_iota(jnp.int32, (T_G, T_Q), 0) // CHUNK
    qchunk = lax.broadcasted_iota(jnp.int32, (T_G, T_Q), 1) // CHUNK
    dchunk = qchunk - kchunk
    chunks_per_group = T_G // CHUNK
    deltas = tuple(range(N_REL_GROUPS - 1, -1, -1))
    allowed = {}
    for dl in deltas:
        lo_dd = dl * chunks_per_group - (chunks_per_group - 1)
        hi_dd = dl * chunks_per_group + T_Q // CHUNK - 1
        if lo_dd < 0 or hi_dd > N_PAST_CHUNKS:
            dd = dchunk + dl * chunks_per_group
            allowed[dl] = (dd >= 0) & (dd <= N_PAST_CHUNKS)

    def q_tile(qi, tile_deltas):
        q0 = pl.multiple_of(qi * T_Q, T_Q)
        qn = _rms_halves(q_ref[pl.ds(q0, T_Q), :].astype(F32), gq_ref[...]) * QK_SCALE
        qt = qn.T
        qts = (jnp.where(first, qt, 0.0).astype(BF16), jnp.where(second, qt, 0.0).astype(BF16))
        scores = [[], []]
        for dl in tile_deltas:
            k2 = kn_ref[pl.ds(pl.multiple_of((qi - dl) * T_G, T_G), T_G), :]
            for h in range(2):
                s = jnp.dot(k2, qts[h], preferred_element_type=F32) + tab_ref[h, dl]
                if dl in allowed:
                    s = jnp.where(allowed[dl], s, NEG)
                scores[h].append(s)
        outs = []
        for h in range(2):
            m = functools.reduce(jnp.maximum, [jnp.max(s, axis=0, keepdims=True) for s in scores[h]])
            ps = [jnp.exp(s - m) for s in scores[h]]
            l = functools.reduce(jnp.add, [jnp.sum(p, axis=0, keepdims=True) for p in ps])
            acc = None
            for dl, p in zip(tile_deltas, ps):
                vth = vt_ref[qi - dl][h * HEAD_DIM:(h + 1) * HEAD_DIM, :]
                pv = jnp.dot(vth, p.astype(BF16), preferred_element_type=F32)
                acc = pv if acc is None else acc + pv
            outs.append(acc / l)
        o_ref[pl.ds(q0, T_Q), :] = jnp.concatenate(outs, axis=0).T.astype(BF16)

    for qi in range(N_REL_GROUPS - 1):
        q_tile(qi, tuple(dl for dl in deltas if qi - dl >= 0))

    def body(qi, _):
        q_tile(qi, deltas)
        return 0

    lax.fori_loop(N_REL_GROUPS - 1, seq // T_Q, body, 0)


def _chunk_attn(qkv, gq, gk, tab, layer):
    b, s, _ = qkv.shape
    n_hp = W_BRANCH // LANES
    base = 6 * n_hp
    blk = lambda col: pl.BlockSpec((None, s, LANES), lambda hp, bi: (bi, 0, col + hp))
    lay = lambda hp, bi: (layer, 0, 0)
    return pl.pallas_call(
        _chunk_kernel,
        grid=(n_hp, b),
        in_specs=[
            blk(base), blk(base + n_hp), blk(base + 2 * n_hp),
            pl.BlockSpec((None, 1, LANES), lay),
            pl.BlockSpec((None, 1, LANES), lay),
            pl.BlockSpec((None, 2, N_REL_GROUPS, T_G, T_Q), lambda hp, bi: (layer, hp, 0, 0, 0)),
        ],
        out_specs=pl.BlockSpec((None, s, LANES), lambda hp, bi: (bi, 0, hp)),
        out_shape=jax.ShapeDtypeStruct((b, s, W_BRANCH), BF16),
        scratch_shapes=[pltpu.VMEM((s, LANES), BF16), pltpu.VMEM((s // T_G, LANES, T_G), BF16)],
        compiler_params=pltpu.CompilerParams(
            dimension_semantics=("arbitrary", "arbitrary"), vmem_limit_bytes=VMEM_LIMIT),
        name="chunk_attn",
    )(qkv, qkv, qkv, gq, gk, tab)


def _rel_bias_tiles(rel_bias):
    lead = rel_bias.shape[:-1]
    span = (N_REL_GROUPS - 1) * T_G + T_Q
    period = span + T_G
    edge_lo = jnp.broadcast_to(rel_bias[..., :1], lead + (T_G - REL_CLIP,))
    edge_hi = jnp.broadcast_to(rel_bias[..., -1:], lead + (span - REL_CLIP - 1,))
    row = jnp.concatenate([rel_bias[..., REL_CLIP:], edge_hi, edge_lo, rel_bias[..., :REL_CLIP]], axis=-1)
    flat = jnp.tile(row, (1,) * len(lead) + (T_G,))[..., :T_G * (period - 1)]
    toep = flat.reshape(lead + (T_G, period - 1))
    return jnp.stack([toep[..., d * T_G:d * T_G + T_Q] for d in range(N_REL_GROUPS)], axis=-3)


def kernel(x, norm_mix_g, w_in, b_gate, qk_g_diff, lambda_qk, subln_g, qk_g_ch, rel_bias,
           w_branch_sb, w_branch_diff, w_branch_ch, w_out, norm_ffn_g, w_gu, w_down):
    b, s, d = x.shape
    m = b * s
    w_qkv = w_in[:, :, :QKV_W].astype(BF16)
    w_gate = w_in[:, :, QKV_W:].astype(BF16)
    w_br = jnp.concatenate([w_branch_sb, w_branch_diff, w_branch_ch], axis=1).astype(BF16)
    w_out_b = w_out.astype(BF16)
    w_gu_b = w_gu.astype(BF16)
    w_down_b = w_down.astype(BF16)
    g_mix = norm_mix_g.reshape(DEPTH, 1, d)
    g_ffn = norm_ffn_g.reshape(DEPTH, 1, d)
    gq_diff = jnp.tile(qk_g_diff[:, 0:1, :], (1, 1, 2))
    gk_diff = jnp.tile(qk_g_diff[:, 1:2, :], (1, 1, 2))
    gq_ch = jnp.tile(qk_g_ch[:, 0:1, :], (1, 1, 2))
    gk_ch = jnp.tile(qk_g_ch[:, 1:2, :], (1, 1, 2))
    sg = subln_g.reshape(DEPTH, 1, 2 * HEAD_DIM)
    tab = _rel_bias_tiles(rel_bias)
    slopes = jnp.asarray([2.0 ** (-8.0 * (i + 1) / H_DIFF) for i in range(H_DIFF)], F32)

    xf = x.reshape(m, d)
    for layer in range(DEPTH):
        lam_init = 0.8 - 0.6 * math.exp(-0.3 * layer)
        qkv = _qkv_proj(xf, g_mix, w_qkv, layer).reshape(b, s, QKV_W)
        o_a = _sb_attn(qkv)
        o_b = _diff_attn(qkv, slopes, gq_diff, gk_diff, lambda_qk, sg, layer, lam_init)
        o_c = _chunk_attn(qkv, gq_ch, gk_ch, tab, layer)
        xf = _merge_out(xf, g_mix, o_a.reshape(m, W_BRANCH), o_b.reshape(m, W_BRANCH),
                        o_c.reshape(m, W_BRANCH), w_gate, b_gate, w_br, w_out_b, layer)
        xf = _ffn(xf, g_ffn, w_gu_b, w_down_b, layer)
    return xf.reshape(b, s, d)
```

```python
import functools
import math

import jax
import jax.numpy as jnp
import numpy as np
from jax import lax
from jax.experimental import pallas as pl
from jax.experimental.pallas import tpu as pltpu

F32 = jnp.float32
BF16 = jnp.bfloat16

D_MODEL = 1024
DEPTH = 4
CHUNK = 64
HEAD_DIM = 64
H_DIFF = 4
N_PAST_CHUNKS = 8
REL_CLIP = 128
W_BRANCH = 512
QKV_W = 9 * W_BRANCH
N_BRANCH = 3
D_FF = int(math.ceil(8 * D_MODEL / 3 / 256)) * 256
RMS_EPS = 1e-6
QK_SCALE = HEAD_DIM ** -0.5

LANES = 128
T_Q = 256
T_K = 128
T_G = 256
NEG = -1e30
SB_DEAD = -104.0
VMEM_LIMIT = 56 * 1024 * 1024

ROW_TILE = 512
FF_CHUNK = 256


def _rms(x, g):
    return x * lax.rsqrt(jnp.mean(x * x, axis=-1, keepdims=True) + RMS_EPS) * g


def _rms_halves(x, g):
    lane = lax.broadcasted_iota(jnp.int32, (1, LANES), 1)
    first = lane < HEAD_DIM
    x2 = x * x
    s0 = jnp.sum(jnp.where(first, x2, 0.0), axis=-1, keepdims=True)
    s1 = jnp.sum(jnp.where(first, 0.0, x2), axis=-1, keepdims=True)
    ms = jnp.where(first, s0, s1) * (1.0 / HEAD_DIM)
    return x * lax.rsqrt(ms + RMS_EPS) * g


def _transpose_to_bf16(x):
    return x.astype(F32).T.astype(BF16)


def _fill_vt(v_ref, vt_ref):
    width = vt_ref.shape[-1]
    for j in range(v_ref.shape[0] // width):
        vt_ref[j] = _transpose_to_bf16(v_ref[j * width:(j + 1) * width, :])


def _head_row_masks():
    row = lax.broadcasted_iota(jnp.int32, (LANES, 1), 0)
    return row < HEAD_DIM, row >= HEAD_DIM


def _qkv_kernel(x_ref, g_ref, w_ref, o_ref):
    h = _rms(x_ref[...], g_ref[...]).astype(BF16)
    n = o_ref.shape[1]
    for c in range(0, n, W_BRANCH):
        o_ref[:, c:c + W_BRANCH] = jnp.dot(
            h, w_ref[:, c:c + W_BRANCH], preferred_element_type=F32).astype(BF16)


def _qkv_proj(x, g, w, layer):
    m = x.shape[0]
    return pl.pallas_call(
        _qkv_kernel,
        grid=(m // ROW_TILE,),
        in_specs=[
            pl.BlockSpec((ROW_TILE, D_MODEL), lambda i: (i, 0)),
            pl.BlockSpec((None, 1, D_MODEL), lambda i: (layer, 0, 0)),
            pl.BlockSpec((None, D_MODEL, QKV_W), lambda i: (layer, 0, 0)),
        ],
        out_specs=pl.BlockSpec((ROW_TILE, QKV_W), lambda i: (i, 0)),
        out_shape=jax.ShapeDtypeStruct((m, QKV_W), BF16),
        compiler_params=pltpu.CompilerParams(
            dimension_semantics=("arbitrary",), vmem_limit_bytes=VMEM_LIMIT),
        name="qkv_proj",
    )(x, g, w)


def _merge_kernel(x_ref, g_ref, oa_ref, ob_ref, oc_ref, wg_ref, bg_ref, wbr_ref, wo_ref, out_ref):
    x = x_ref[...]
    h = _rms(x, g_ref[...]).astype(BF16)
    merged = None
    for br, o_ref in enumerate((oa_ref, ob_ref, oc_ref)):
        g_lin = jnp.dot(h, wg_ref[:, br * D_MODEL:(br + 1) * D_MODEL], preferred_element_type=F32)
        gate = 1.0 / (1.0 + jnp.exp(-(g_lin + bg_ref[br:br + 1, :])))
        proj = jnp.dot(o_ref[...], wbr_ref[br * W_BRANCH:(br + 1) * W_BRANCH, :],
                       preferred_element_type=F32)
        term = gate * proj
        merged = term if merged is None else merged + term
    out_ref[...] = x + jnp.dot(merged.astype(BF16), wo_ref[...], preferred_element_type=F32)


def _merge_out(x, g, o_a, o_b, o_c, w_gate, b_gate, w_br, w_out, layer):
    m = x.shape[0]
    row = lambda i: (i, 0)
    lay = lambda i: (layer, 0, 0)
    return pl.pallas_call(
        _merge_kernel,
        grid=(m // ROW_TILE,),
        in_specs=[
            pl.BlockSpec((ROW_TILE, D_MODEL), row),
            pl.BlockSpec((None, 1, D_MODEL), lay),
            pl.BlockSpec((ROW_TILE, W_BRANCH), row),
            pl.BlockSpec((ROW_TILE, W_BRANCH), row),
            pl.BlockSpec((ROW_TILE, W_BRANCH), row),
            pl.BlockSpec((None, D_MODEL, N_BRANCH * D_MODEL), lay),
            pl.BlockSpec((None, N_BRANCH, D_MODEL), lay),
            pl.BlockSpec((None, N_BRANCH * W_BRANCH, D_MODEL), lay),
            pl.BlockSpec((None, D_MODEL, D_MODEL), lay),
        ],
        out_specs=pl.BlockSpec((ROW_TILE, D_MODEL), row),
        out_shape=jax.ShapeDtypeStruct((m, D_MODEL), F32),
        compiler_params=pltpu.CompilerParams(
            dimension_semantics=("arbitrary",), vmem_limit_bytes=VMEM_LIMIT),
        name="merge_out",
    )(x, g, o_a, o_b, o_c, w_gate, b_gate, w_br, w_out)


def _ffn_kernel(x_ref, g_ref, wgu_ref, wd_ref, out_ref, act_ref):
    x = x_ref[...]
    h = _rms(x, g_ref[...]).astype(BF16)
    for c in range(0, D_FF, FF_CHUNK):
        gate = jnp.dot(h, wgu_ref[:, c:c + FF_CHUNK], preferred_element_type=F32)
        up = jnp.dot(h, wgu_ref[:, D_FF + c:D_FF + c + FF_CHUNK], preferred_element_type=F32)
        silu = gate / (1.0 + jnp.exp(-gate))
        act_ref[:, c:c + FF_CHUNK] = (silu * up).astype(BF16)
    out_ref[...] = x + jnp.dot(act_ref[...], wd_ref[...], preferred_element_type=F32)


def _ffn(x, g, w_gu, w_down, layer):
    m = x.shape[0]
    row = lambda i: (i, 0)
    lay = lambda i: (layer, 0, 0)
    return pl.pallas_call(
        _ffn_kernel,
        grid=(m // ROW_TILE,),
        in_specs=[
            pl.BlockSpec((ROW_TILE, D_MODEL), row),
            pl.BlockSpec((None, 1, D_MODEL), lay),
            pl.BlockSpec((None, D_MODEL, 2 * D_FF), lay, pipeline_mode=pl.Buffered(1)),
            pl.BlockSpec((None, D_FF, D_MODEL), lay, pipeline_mode=pl.Buffered(1)),
        ],
        out_specs=pl.BlockSpec((ROW_TILE, D_MODEL), row),
        out_shape=jax.ShapeDtypeStruct((m, D_MODEL), F32),
        scratch_shapes=[pltpu.VMEM((ROW_TILE, D_FF), BF16)],
        compiler_params=pltpu.CompilerParams(
            dimension_semantics=("arbitrary",), vmem_limit_bytes=VMEM_LIMIT),
        name="ffn",
    )(x, g, w_gu, w_down)


def _sb_weights(z, cum, carry, mask):
    sp = jnp.log(1.0 + jnp.exp(-jnp.abs(z)))
    log_beta = jnp.minimum(z, 0.0) - sp
    log_1m = log_beta - z
    if mask is not None:
        log_1m = jnp.where(mask, log_1m, 0.0)
    hi = log_1m.astype(BF16)
    lo = (log_1m - hi.astype(F32)).astype(BF16)
    afters = []
    for u in reversed(range(z.shape[0] // T_K)):
        r0, r1 = u * T_K, (u + 1) * T_K
        within = jnp.dot(cum, jnp.concatenate([hi[r0:r1], lo[r0:r1]], axis=0),
                         preferred_element_type=F32)
        afters.append(within + carry)
        carry = carry + within[0:1, :] + log_1m[r0:r0 + 1, :]
    after = afters[0] if len(afters) == 1 else jnp.concatenate(afters[::-1], axis=0)
    w = jnp.exp(log_beta + after)
    if mask is not None:
        w = jnp.where(mask, w, 0.0)
    return w.astype(BF16), carry


def _sb_kernel(q_ref, k_ref, v_ref, o_ref, vt_ref):
    seq = q_ref.shape[0]
    _fill_vt(v_ref, vt_ref)
    first, second = _head_row_masks()
    kk = lax.broadcasted_iota(jnp.int32, (T_K, T_K), 0)
    kk2 = lax.broadcasted_iota(jnp.int32, (T_K, T_K), 1)
    later = jnp.where(kk2 > kk, 1.0, 0.0).astype(BF16)
    cum = jnp.concatenate([later, later], axis=1)
    strict = (lax.broadcasted_iota(jnp.int32, (T_G, T_Q), 0)
              < lax.broadcasted_iota(jnp.int32, (T_G, T_Q), 1))
    heads = range(2)

    def load_q(qi):
        q0 = pl.multiple_of(qi * T_Q, T_Q)
        qt = (q_ref[pl.ds(q0, T_Q), :].astype(F32) * QK_SCALE).T
        return (jnp.where(first, qt, 0.0).astype(BF16), jnp.where(second, qt, 0.0).astype(BF16))

    def logits(g, qts):
        k2 = k_ref[pl.ds(pl.multiple_of(g * T_G, T_G), T_G), :]
        return [jnp.dot(k2, qts[h], preferred_element_type=F32) for h in heads]

    def weights(zs, mask, carries):
        out = [_sb_weights(zs[h], cum, carries[h], mask) for h in heads]
        return [w for w, _ in out], [c for _, c in out]

    def add_pv(g, ws, accs):
        vt = vt_ref[g]
        return [accs[h] + jnp.dot(vt[h * HEAD_DIM:(h + 1) * HEAD_DIM, :], ws[h],
                                  preferred_element_type=F32) for h in heads]

    def tail(qts, g_start, carries, accs):
        def live(st):
            g, carries, _ = st
            return (g >= 0) & (jnp.max(jnp.maximum(carries[0], carries[1])) >= SB_DEAD)

        def body(st):
            g, carries, accs = st
            ws, carries = weights(logits(g, qts), None, list(carries))
            return g - 1, tuple(carries), tuple(add_pv(g, ws, list(accs)))

        _, _, accs = lax.while_loop(live, body, (g_start, tuple(carries), tuple(accs)))
        return list(accs)

    def store(qi, accs):
        q0 = pl.multiple_of(qi * T_Q, T_Q)
        o_ref[pl.ds(q0, T_Q), :] = jnp.concatenate(accs, axis=0).T.astype(BF16)

    def q_pair(qp, has_earlier):
        qa, qb = 2 * qp, 2 * qp + 1
        qts_a, qts_b = load_q(qa), load_q(qb)
        z_a0, z_b0, z_b1 = logits(qa, qts_a), logits(qb, qts_b), logits(qa, qts_b)
        if has_earlier:
            z_a1 = logits(qa - 1, qts_a)
        zeros = [jnp.zeros((1, T_Q), F32) for _ in heads]
        acc0 = [jnp.zeros((HEAD_DIM, T_Q), F32) for _ in heads]
        w_a0, car_a = weights(z_a0, strict, zeros)
        w_b0, car_b = weights(z_b0, strict, zeros)
        acc_a = add_pv(qa, w_a0, acc0)
        acc_b = add_pv(qb, w_b0, acc0)
        w_b1, car_b = weights(z_b1, None, car_b)
        acc_b = add_pv(qa, w_b1, acc_b)
        if has_earlier:
            w_a1, car_a = weights(z_a1, None, car_a)
            acc_a = add_pv(qa - 1, w_a1, acc_a)
            acc_a = tail(qts_a, qa - 2, car_a, acc_a)
            acc_b = tail(qts_b, qa - 1, car_b, acc_b)
        store(qa, acc_a)
        store(qb, acc_b)

    q_pair(0, False)

    def body(qp, _):
        q_pair(qp, True)
        return 0

    lax.fori_loop(1, seq // (2 * T_Q), body, 0)


def _sb_attn(qkv):
    b, s, _ = qkv.shape
    blk = lambda col: pl.BlockSpec((None, s, LANES), lambda bi, hp: (bi, 0, col + hp))
    n_hp = W_BRANCH // LANES
    return pl.pallas_call(
        _sb_kernel,
        grid=(b, n_hp),
        in_specs=[blk(0), blk(n_hp), blk(2 * n_hp)],
        out_specs=pl.BlockSpec((None, s, LANES), lambda bi, hp: (bi, 0, hp)),
        out_shape=jax.ShapeDtypeStruct((b, s, W_BRANCH), BF16),
        scratch_shapes=[pltpu.VMEM((s // T_G, LANES, T_G), BF16)],
        compiler_params=pltpu.CompilerParams(
            dimension_semantics=("arbitrary", "arbitrary"), vmem_limit_bytes=VMEM_LIMIT),
        name="sb_attn",
    )(qkv, qkv, qkv)


def _softmax_update(scores, offsets, vts, state):
    m, l, acc = state
    m_new = m
    for s, c in zip(scores, offsets):
        m_new = jnp.maximum(m_new, jnp.max(s, axis=0, keepdims=True) + c)
    alpha = jnp.exp(m - m_new)
    l = alpha * l
    acc = alpha * acc
    for s, c, vt in zip(scores, offsets, vts):
        p = jnp.exp(s + (c - m_new))
        l = l + jnp.sum(p, axis=0, keepdims=True)
        acc = acc + jnp.dot(vt, p.astype(BF16), preferred_element_type=F32)
    return m_new, l, acc


def _diff_kernel(slope_ref, q_ref, k_ref, v_ref, gq_ref, gk_ref, lam_ref, sg_ref, o_ref,
                 kn_ref, vt_ref, *, lam_init):
    seq = q_ref.shape[0]
    slope = slope_ref[pl.program_id(1)]
    _fill_vt(v_ref, vt_ref)
    for r in range(0, seq, T_Q):
        kn_ref[r:r + T_Q, :] = _rms_halves(k_ref[r:r + T_Q, :].astype(F32), gk_ref[...]).astype(BF16)
    first, second = _head_row_masks()
    lq = lam_ref[...]
    lam = (jnp.exp(jnp.sum(lq[0:1] * lq[1:2], axis=-1, keepdims=True))
           - jnp.exp(jnp.sum(lq[2:3] * lq[3:4], axis=-1, keepdims=True)) + lam_init)
    kpos = lax.broadcasted_iota(jnp.int32, (T_G, T_Q), 0)
    qpos = lax.broadcasted_iota(jnp.int32, (T_G, T_Q), 1)
    rel = (qpos - kpos).astype(F32)
    lin_bias = -slope * rel
    diag_allowed = (kpos // CHUNK) <= (qpos // CHUNK)
    diag_bias = -slope * jnp.abs(rel)
    chains = [(par, mp) for par in range(2) for mp in range(2)]

    def q_pair(qp, _):
        qtm = {}
        for par in range(2):
            q0 = pl.multiple_of((2 * qp + par) * T_Q, T_Q)
            qn = _rms_halves(q_ref[pl.ds(q0, T_Q), :].astype(F32), gq_ref[...]) * QK_SCALE
            qt = qn.T
            qtm[par, 0] = jnp.where(first, qt, 0.0).astype(BF16)
            qtm[par, 1] = jnp.where(second, qt, 0.0).astype(BF16)

        def keys(g):
            return kn_ref[pl.ds(pl.multiple_of(g * T_G, T_G), T_G), :]

        def full_scores(kn2, g, chain):
            par, _ = chain
            off = jnp.asarray((2 * qp + par) * T_Q - g * T_G, F32)
            return jnp.dot(kn2, qtm[chain], preferred_element_type=F32) + lin_bias, -slope * off

        def diag_scores(kn2, chain):
            z = jnp.dot(kn2, qtm[chain], preferred_element_type=F32)
            return jnp.where(diag_allowed, z + diag_bias, NEG), 0.0

        def trip(it, states):
            gs = (2 * it, 2 * it + 1)
            kns = [keys(g) for g in gs]
            vts = [vt_ref[g] for g in gs]
            out = []
            scs = [[full_scores(kn2, g, chain) for kn2, g in zip(kns, gs)] for chain in chains]
            for sc, st in zip(scs, states):
                out.append(_softmax_update([s for s, _ in sc], [c for _, c in sc], vts, st))
            return tuple(out)

        init = (jnp.full((1, T_Q), NEG, F32), jnp.zeros((1, T_Q), F32), jnp.zeros((LANES, T_Q), F32))
        states = list(lax.fori_loop(0, qp, trip, (init,) * len(chains)))
        g = 2 * qp
        kns, vts = [keys(g), keys(g + 1)], [vt_ref[g], vt_ref[g + 1]]
        scs = []
        for chain in chains:
            if chain[0] == 0:
                scs.append([diag_scores(kns[0], chain)])
            else:
                scs.append([full_scores(kns[0], g, chain), diag_scores(kns[1], chain)])
        for i, sc in enumerate(scs):
            states[i] = _softmax_update([s for s, _ in sc], [c for _, c in sc], vts[:len(sc)], states[i])
        for par in range(2):
            (_, l0, a0), (_, l1, a1) = states[2 * par], states[2 * par + 1]
            ob = a0 / l0 - lam * (a1 / l1)
            y = ob * lax.rsqrt(jnp.mean(ob * ob, axis=0, keepdims=True) + RMS_EPS)
            q0 = pl.multiple_of((2 * qp + par) * T_Q, T_Q)
            o_ref[pl.ds(q0, T_Q), :] = (y.T * sg_ref[...] * (1.0 - lam_init)).astype(BF16)
        return 0

    lax.fori_loop(0, seq // (2 * T_Q), q_pair, 0)


def _diff_attn(qkv, slopes, gq, gk, lam_qk, subln_g, layer, lam_init):
    b, s, _ = qkv.shape
    base = 3 * (W_BRANCH // LANES)
    blk = lambda col: pl.BlockSpec((None, s, LANES), lambda bi, h: (bi, 0, col + h))
    lay = lambda bi, h: (layer, 0, 0)
    return pl.pallas_call(
        functools.partial(_diff_kernel, lam_init=lam_init),
        grid=(b, H_DIFF),
        in_specs=[
            pl.BlockSpec(memory_space=pltpu.SMEM),
            blk(base), blk(base + H_DIFF), blk(base + 2 * H_DIFF),
            pl.BlockSpec((None, 1, LANES), lay),
            pl.BlockSpec((None, 1, LANES), lay),
            pl.BlockSpec((None, 4, HEAD_DIM), lay),
            pl.BlockSpec((None, 1, LANES), lay),
        ],
        out_specs=pl.BlockSpec((None, s, LANES), lambda bi, h: (bi, 0, h)),
        out_shape=jax.ShapeDtypeStruct((b, s, W_BRANCH), BF16),
        scratch_shapes=[pltpu.VMEM((s, LANES), BF16), pltpu.VMEM((s // T_G, LANES, T_G), BF16)],
        compiler_params=pltpu.CompilerParams(
            dimension_semantics=("arbitrary", "arbitrary"), vmem_limit_bytes=VMEM_LIMIT),
        name="diff_attn",
    )(slopes, qkv, qkv, qkv, gq, gk, lam_qk, subln_g)


N_REL_GROUPS = (N_PAST_CHUNKS * CHUNK + T_Q - 1) // T_G + 1


def _chunk_kernel(q_ref, k_ref, v_ref, gq_ref, gk_ref, tab_ref, o_ref, kn_ref, vt_ref):
    seq = q_ref.shape[0]
    _fill_vt(v_ref, vt_ref)
    for r in range(0, seq, T_Q):
        kn_ref[r:r + T_Q, :] = _rms_halves(k_ref[r:r + T_Q, :].astype(F32), gk_ref[...]).astype(BF16)
    first, second = _head_row_masks()
    kchunk = lax.broadcasted_iota(jnp.int32, (T_G, T_Q), 0) // CHUNK
    qchunk = lax.broadcasted_iota(jnp.int32, (T_G, T_Q), 1) // CHUNK
    dchunk = qchunk - kchunk
    chunks_per_group = T_G // CHUNK
    deltas = tuple(range(N_REL_GROUPS - 1, -1, -1))
    allowed = {}
    for dl in deltas:
        lo_dd = dl * chunks_per_group - (chunks_per_group - 1)
        hi_dd = dl * chunks_per_group + T_Q // CHUNK - 1
        if lo_dd < 0 or hi_dd > N_PAST_CHUNKS:
            dd = dchunk + dl * chunks_per_group
            allowed[dl] = (dd >= 0) & (dd <= N_PAST_CHUNKS)

    def q_tile(qi, tile_deltas):
        q0 = pl.multiple_of(qi * T_Q, T_Q)
        qn = _rms_halves(q_ref[pl.ds(q0, T_Q), :].astype(F32), gq_ref[...]) * QK_SCALE
        qt = qn.T
        qts = (jnp.where(first, qt, 0.0).astype(BF16), jnp.where(second, qt, 0.0).astype(BF16))
        scores = [[], []]
        for dl in tile_deltas:
            k2 = kn_ref[pl.ds(pl.multiple_of((qi - dl) * T_G, T_G), T_G), :]
            for h in range(2):
                s = jnp.dot(k2, qts[h], preferred_element_type=F32) + tab_ref[h, dl]
                if dl in allowed:
                    s = jnp.where(allowed[dl], s, NEG)
                scores[h].append(s)
        outs = []
        for h in range(2):
            m = functools.reduce(jnp.maximum, [jnp.max(s, axis=0, keepdims=True) for s in scores[h]])
            ps = [jnp.exp(s - m) for s in scores[h]]
            l = functools.reduce(jnp.add, [jnp.sum(p, axis=0, keepdims=True) for p in ps])
            acc = None
            for dl, p in zip(tile_deltas, ps):
                vth = vt_ref[qi - dl][h * HEAD_DIM:(h + 1) * HEAD_DIM, :]
                pv = jnp.dot(vth, p.astype(BF16), preferred_element_type=F32)
                acc = pv if acc is None else acc + pv
            outs.append(acc / l)
        o_ref[pl.ds(q0, T_Q), :] = jnp.concatenate(outs, axis=0).T.astype(BF16)

    for qi in range(N_REL_GROUPS - 1):
        q_tile(qi, tuple(dl for dl in deltas if qi - dl >= 0))

    def body(qi, _):
        q_tile(qi, deltas)
        return 0

    lax.fori_loop(N_REL_GROUPS - 1, seq // T_Q, body, 0)


def _chunk_attn(qkv, gq, gk, tab, layer):
    b, s, _ = qkv.shape
    n_hp = W_BRANCH // LANES
    base = 6 * n_hp
    blk = lambda col: pl.BlockSpec((None, s, LANES), lambda hp, bi: (bi, 0, col + hp))
    lay = lambda hp, bi: (layer, 0, 0)
    return pl.pallas_call(
        _chunk_kernel,
        grid=(n_hp, b),
        in_specs=[
            blk(base), blk(base + n_hp), blk(base + 2 * n_hp),
            pl.BlockSpec((None, 1, LANES), lay),
            pl.BlockSpec((None, 1, LANES), lay),
            pl.BlockSpec((None, 2, N_REL_GROUPS, T_G, T_Q), lambda hp, bi: (layer, hp, 0, 0, 0)),
        ],
        out_specs=pl.BlockSpec((None, s, LANES), lambda hp, bi: (bi, 0, hp)),
        out_shape=jax.ShapeDtypeStruct((b, s, W_BRANCH), BF16),
        scratch_shapes=[pltpu.VMEM((s, LANES), BF16), pltpu.VMEM((s // T_G, LANES, T_G), BF16)],
        compiler_params=pltpu.CompilerParams(
            dimension_semantics=("arbitrary", "arbitrary"), vmem_limit_bytes=VMEM_LIMIT),
        name="chunk_attn",
    )(qkv, qkv, qkv, gq, gk, tab)


def _rel_bias_tiles(rel_bias):
    lead = rel_bias.shape[:-1]
    span = (N_REL_GROUPS - 1) * T_G + T_Q
    period = span + T_G
    edge_lo = jnp.broadcast_to(rel_bias[..., :1], lead + (T_G - REL_CLIP,))
    edge_hi = jnp.broadcast_to(rel_bias[..., -1:], lead + (span - REL_CLIP - 1,))
    row = jnp.concatenate([rel_bias[..., REL_CLIP:], edge_hi, edge_lo, rel_bias[..., :REL_CLIP]], axis=-1)
    flat = jnp.tile(row, (1,) * len(lead) + (T_G,))[..., :T_G * (period - 1)]
    toep = flat.reshape(lead + (T_G, period - 1))
    return jnp.stack([toep[..., d * T_G:d * T_G + T_Q] for d in range(N_REL_GROUPS)], axis=-3)


def kernel(x, norm_mix_g, w_in, b_gate, qk_g_diff, lambda_qk, subln_g, qk_g_ch, rel_bias,
           w_branch_sb, w_branch_diff, w_branch_ch, w_out, norm_ffn_g, w_gu, w_down):
    b, s, d = x.shape
    m = b * s
    w_qkv = w_in[:, :, :QKV_W].astype(BF16)
    w_gate = w_in[:, :, QKV_W:].astype(BF16)
    w_br = jnp.concatenate([w_branch_sb, w_branch_diff, w_branch_ch], axis=1).astype(BF16)
    w_out_b = w_out.astype(BF16)
    w_gu_b = w_gu.astype(BF16)
    w_down_b = w_down.astype(BF16)
    g_mix = norm_mix_g.reshape(DEPTH, 1, d)
    g_ffn = norm_ffn_g.reshape(DEPTH, 1, d)
    gq_diff = jnp.tile(qk_g_diff[:, 0:1, :], (1, 1, 2))
    gk_diff = jnp.tile(qk_g_diff[:, 1:2, :], (1, 1, 2))
    gq_ch = jnp.tile(qk_g_ch[:, 0:1, :], (1, 1, 2))
    gk_ch = jnp.tile(qk_g_ch[:, 1:2, :], (1, 1, 2))
    sg = subln_g.reshape(DEPTH, 1, 2 * HEAD_DIM)
    tab = _rel_bias_tiles(rel_bias)
    slopes = jnp.asarray([2.0 ** (-8.0 * (i + 1) / H_DIFF) for i in range(H_DIFF)], F32)

    xf = x.reshape(m, d)
    for layer in range(DEPTH):
        lam_init = 0.8 - 0.6 * math.exp(-0.3 * layer)
        qkv = _qkv_proj(xf, g_mix, w_qkv, layer).reshape(b, s, QKV_W)
        o_a = _sb_attn(qkv)
        o_b = _diff_attn(qkv, slopes, gq_diff, gk_diff, lambda_qk, sg, layer, lam_init)
        o_c = _chunk_attn(qkv, gq_ch, gk_ch, tab, layer)
        xf = _merge_out(xf, g_mix, o_a.reshape(m, W_BRANCH), o_b.reshape(m, W_BRANCH),
                        o_c.reshape(m, W_BRANCH), w_gate, b_gate, w_br, w_out_b, layer)
        xf = _ffn(xf, g_ffn, w_gu_b, w_down_b, layer)
    return xf.reshape(b, s, d)
```

```python
import functools
import math

import jax
import jax.numpy as jnp
import numpy as np
from jax import lax
from jax.experimental import pallas as pl
from jax.experimental.pallas import tpu as pltpu

F32 = jnp.float32
BF16 = jnp.bfloat16

D_MODEL = 1024
DEPTH = 4
CHUNK = 64
HEAD_DIM = 64
H_DIFF = 4
N_PAST_CHUNKS = 8
REL_CLIP = 128
W_BRANCH = 512
QKV_W = 9 * W_BRANCH
N_BRANCH = 3
D_FF = int(math.ceil(8 * D_MODEL / 3 / 256)) * 256
RMS_EPS = 1e-6
QK_SCALE = HEAD_DIM ** -0.5

LANES = 128
T_Q = 256
T_K = 128
T_G = 256
NEG = -1e30
SB_DEAD = -104.0
LOG2E = 1.4426950408889634
SAFE_LOG2 = 60.0
VMEM_LIMIT = 56 * 1024 * 1024

ROW_TILE = 512
FF_CHUNK = 256


def _rms(x, g):
    return x * lax.rsqrt(jnp.mean(x * x, axis=-1, keepdims=True) + RMS_EPS) * g


def _rms_halves(x, g):
    lane = lax.broadcasted_iota(jnp.int32, (1, LANES), 1)
    first = lane < HEAD_DIM
    x2 = x * x
    s0 = jnp.sum(jnp.where(first, x2, 0.0), axis=-1, keepdims=True)
    s1 = jnp.sum(jnp.where(first, 0.0, x2), axis=-1, keepdims=True)
    ms = jnp.where(first, s0, s1) * (1.0 / HEAD_DIM)
    return x * lax.rsqrt(ms + RMS_EPS) * g


def _qk_logit_bound(gq, gk):
    return (1.02 * LOG2E * QK_SCALE * HEAD_DIM) * jnp.max(jnp.abs(gq)) * jnp.max(jnp.abs(gk))


def _transpose_to_bf16(x):
    return x.astype(F32).T.astype(BF16)


def _fill_vt(v_ref, vt_ref):
    width = vt_ref.shape[-1]
    for j in range(v_ref.shape[0] // width):
        vt_ref[j] = _transpose_to_bf16(v_ref[j * width:(j + 1) * width, :])


def _head_row_masks():
    row = lax.broadcasted_iota(jnp.int32, (LANES, 1), 0)
    return row < HEAD_DIM, row >= HEAD_DIM


def _qkv_kernel(x_ref, g_ref, w_ref, o_ref):
    h = _rms(x_ref[...], g_ref[...]).astype(BF16)
    n = o_ref.shape[1]
    for c in range(0, n, W_BRANCH):
        o_ref[:, c:c + W_BRANCH] = jnp.dot(
            h, w_ref[:, c:c + W_BRANCH], preferred_element_type=F32).astype(BF16)


def _qkv_proj(x, g, w, layer):
    m = x.shape[0]
    return pl.pallas_call(
        _qkv_kernel,
        grid=(m // ROW_TILE,),
        in_specs=[
            pl.BlockSpec((ROW_TILE, D_MODEL), lambda i: (i, 0)),
            pl.BlockSpec((None, 1, D_MODEL), lambda i: (layer, 0, 0)),
            pl.BlockSpec((None, D_MODEL, QKV_W), lambda i: (layer, 0, 0)),
        ],
        out_specs=pl.BlockSpec((ROW_TILE, QKV_W), lambda i: (i, 0)),
        out_shape=jax.ShapeDtypeStruct((m, QKV_W), BF16),
        compiler_params=pltpu.CompilerParams(
            dimension_semantics=("arbitrary",), vmem_limit_bytes=VMEM_LIMIT),
        name="qkv_proj",
    )(x, g, w)


def _merge_kernel(x_ref, g_ref, oa_ref, ob_ref, oc_ref, wg_ref, bg_ref, wbr_ref, wo_ref, out_ref):
    x = x_ref[...]
    h = _rms(x, g_ref[...]).astype(BF16)
    merged = None
    for br, o_ref in enumerate((oa_ref, ob_ref, oc_ref)):
        g_lin = jnp.dot(h, wg_ref[:, br * D_MODEL:(br + 1) * D_MODEL], preferred_element_type=F32)
        gate = 1.0 / (1.0 + jnp.exp(-(g_lin + bg_ref[br:br + 1, :])))
        proj = jnp.dot(o_ref[...], wbr_ref[br * W_BRANCH:(br + 1) * W_BRANCH, :],
                       preferred_element_type=F32)
        term = gate * proj
        merged = term if merged is None else merged + term
    out_ref[...] = x + jnp.dot(merged.astype(BF16), wo_ref[...], preferred_element_type=F32)


def _merge_out(x, g, o_a, o_b, o_c, w_gate, b_gate, w_br, w_out, layer):
    m = x.shape[0]
    row = lambda i: (i, 0)
    lay = lambda i: (layer, 0, 0)
    return pl.pallas_call(
        _merge_kernel,
        grid=(m // ROW_TILE,),
        in_specs=[
            pl.BlockSpec((ROW_TILE, D_MODEL), row),
            pl.BlockSpec((None, 1, D_MODEL), lay),
            pl.BlockSpec((ROW_TILE, W_BRANCH), row),
            pl.BlockSpec((ROW_TILE, W_BRANCH), row),
            pl.BlockSpec((ROW_TILE, W_BRANCH), row),
            pl.BlockSpec((None, D_MODEL, N_BRANCH * D_MODEL), lay),
            pl.BlockSpec((None, N_BRANCH, D_MODEL), lay),
            pl.BlockSpec((None, N_BRANCH * W_BRANCH, D_MODEL), lay),
            pl.BlockSpec((None, D_MODEL, D_MODEL), lay),
        ],
        out_specs=pl.BlockSpec((ROW_TILE, D_MODEL), row),
        out_shape=jax.ShapeDtypeStruct((m, D_MODEL), F32),
        compiler_params=pltpu.CompilerParams(
            dimension_semantics=("arbitrary",), vmem_limit_bytes=VMEM_LIMIT),
        name="merge_out",
    )(x, g, o_a, o_b, o_c, w_gate, b_gate, w_br, w_out)


def _ffn_kernel(x_ref, g_ref, wgu_ref, wd_ref, out_ref, act_ref):
    x = x_ref[...]
    h = _rms(x, g_ref[...]).astype(BF16)
    for c in range(0, D_FF, FF_CHUNK):
        gate = jnp.dot(h, wgu_ref[:, c:c + FF_CHUNK], preferred_element_type=F32)
        up = jnp.dot(h, wgu_ref[:, D_FF + c:D_FF + c + FF_CHUNK], preferred_element_type=F32)
        silu = gate / (1.0 + jnp.exp(-gate))
        act_ref[:, c:c + FF_CHUNK] = (silu * up).astype(BF16)
    out_ref[...] = x + jnp.dot(act_ref[...], wd_ref[...], preferred_element_type=F32)


def _ffn(x, g, w_gu, w_down, layer):
    m = x.shape[0]
    row = lambda i: (i, 0)
    lay = lambda i: (layer, 0, 0)
    return pl.pallas_call(
        _ffn_kernel,
        grid=(m // ROW_TILE,),
        in_specs=[
            pl.BlockSpec((ROW_TILE, D_MODEL), row),
            pl.BlockSpec((None, 1, D_MODEL), lay),
            pl.BlockSpec((None, D_MODEL, 2 * D_FF), lay, pipeline_mode=pl.Buffered(1)),
            pl.BlockSpec((None, D_FF, D_MODEL), lay, pipeline_mode=pl.Buffered(1)),
        ],
        out_specs=pl.BlockSpec((ROW_TILE, D_MODEL), row),
        out_shape=jax.ShapeDtypeStruct((m, D_MODEL), F32),
        scratch_shapes=[pltpu.VMEM((ROW_TILE, D_FF), BF16)],
        compiler_params=pltpu.CompilerParams(
            dimension_semantics=("arbitrary",), vmem_limit_bytes=VMEM_LIMIT),
        name="ffn",
    )(x, g, w_gu, w_down)


def _sb_weights(z, cum, carry, mask):
    sp = jnp.log(1.0 + jnp.exp(-jnp.abs(z)))
    log_beta = jnp.minimum(z, 0.0) - sp
    log_1m = log_beta - z
    if mask is not None:
        log_1m = jnp.where(mask, log_1m, 0.0)
    hi = log_1m.astype(BF16)
    lo = (log_1m - hi.astype(F32)).astype(BF16)
    afters = []
    for u in reversed(range(z.shape[0] // T_K)):
        r0, r1 = u * T_K, (u + 1) * T_K
        within = jnp.dot(cum, jnp.concatenate([hi[r0:r1], lo[r0:r1]], axis=0),
                         preferred_element_type=F32)
        afters.append(within + carry)
        carry = carry + within[0:1, :] + log_1m[r0:r0 + 1, :]
    after = afters[0] if len(afters) == 1 else jnp.concatenate(afters[::-1], axis=0)
    w = jnp.exp(log_beta + after)
    if mask is not None:
        w = jnp.where(mask, w, 0.0)
    return w.astype(BF16), carry


def _sb_kernel(q_ref, k_ref, v_ref, o_ref, vt_ref):
    seq = q_ref.shape[0]
    _fill_vt(v_ref, vt_ref)
    first, second = _head_row_masks()
    kk = lax.broadcasted_iota(jnp.int32, (T_K, T_K), 0)
    kk2 = lax.broadcasted_iota(jnp.int32, (T_K, T_K), 1)
    later = jnp.where(kk2 > kk, 1.0, 0.0).astype(BF16)
    cum = jnp.concatenate([later, later], axis=1)
    strict = (lax.broadcasted_iota(jnp.int32, (T_G, T_Q), 0)
              < lax.broadcasted_iota(jnp.int32, (T_G, T_Q), 1))
    heads = range(2)

    def load_q(qi):
        q0 = pl.multiple_of(qi * T_Q, T_Q)
        qt = (q_ref[pl.ds(q0, T_Q), :].astype(F32) * QK_SCALE).T
        return (jnp.where(first, qt, 0.0).astype(BF16), jnp.where(second, qt, 0.0).astype(BF16))

    def logits(g, qts):
        k2 = k_ref[pl.ds(pl.multiple_of(g * T_G, T_G), T_G), :]
        return [jnp.dot(k2, qts[h], preferred_element_type=F32) for h in heads]

    def weights(zs, mask, carries):
        out = [_sb_weights(zs[h], cum, carries[h], mask) for h in heads]
        return [w for w, _ in out], [c for _, c in out]

    def add_pv(g, ws, accs):
        vt = vt_ref[g]
        return [accs[h] + jnp.dot(vt[h * HEAD_DIM:(h + 1) * HEAD_DIM, :], ws[h],
                                  preferred_element_type=F32) for h in heads]

    def tail(qts, g_start, carries, accs):
        def live(st):
            g, carries, _ = st
            return (g >= 0) & (jnp.max(jnp.maximum(carries[0], carries[1])) >= SB_DEAD)

        def body(st):
            g, carries, accs = st
            ws, carries = weights(logits(g, qts), None, list(carries))
            return g - 1, tuple(carries), tuple(add_pv(g, ws, list(accs)))

        _, _, accs = lax.while_loop(live, body, (g_start, tuple(carries), tuple(accs)))
        return list(accs)

    def store(qi, accs):
        q0 = pl.multiple_of(qi * T_Q, T_Q)
        o_ref[pl.ds(q0, T_Q), :] = jnp.concatenate(accs, axis=0).T.astype(BF16)

    def q_pair(qp, has_earlier):
        qa, qb = 2 * qp, 2 * qp + 1
        qts_a, qts_b = load_q(qa), load_q(qb)
        z_a0, z_b0, z_b1 = logits(qa, qts_a), logits(qb, qts_b), logits(qa, qts_b)
        if has_earlier:
            z_a1 = logits(qa - 1, qts_a)
        zeros = [jnp.zeros((1, T_Q), F32) for _ in heads]
        acc0 = [jnp.zeros((HEAD_DIM, T_Q), F32) for _ in heads]
        w_a0, car_a = weights(z_a0, strict, zeros)
        w_b0, car_b = weights(z_b0, strict, zeros)
        acc_a = add_pv(qa, w_a0, acc0)
        acc_b = add_pv(qb, w_b0, acc0)
        w_b1, car_b = weights(z_b1, None, car_b)
        acc_b = add_pv(qa, w_b1, acc_b)
        if has_earlier:
            w_a1, car_a = weights(z_a1, None, car_a)
            acc_a = add_pv(qa - 1, w_a1, acc_a)
            acc_a = tail(qts_a, qa - 2, car_a, acc_a)
            acc_b = tail(qts_b, qa - 1, car_b, acc_b)
        store(qa, acc_a)
        store(qb, acc_b)

    q_pair(0, False)

    def body(qp, _):
        q_pair(qp, True)
        return 0

    lax.fori_loop(1, seq // (2 * T_Q), body, 0)


def _sb_attn(qkv):
    b, s, _ = qkv.shape
    blk = lambda col: pl.BlockSpec((None, s, LANES), lambda bi, hp: (bi, 0, col + hp))
    n_hp = W_BRANCH // LANES
    return pl.pallas_call(
        _sb_kernel,
        grid=(b, n_hp),
        in_specs=[blk(0), blk(n_hp), blk(2 * n_hp)],
        out_specs=pl.BlockSpec((None, s, LANES), lambda bi, hp: (bi, 0, hp)),
        out_shape=jax.ShapeDtypeStruct((b, s, W_BRANCH), BF16),
        scratch_shapes=[pltpu.VMEM((s // T_G, LANES, T_G), BF16)],
        compiler_params=pltpu.CompilerParams(
            dimension_semantics=("arbitrary", "arbitrary"), vmem_limit_bytes=VMEM_LIMIT),
        name="sb_attn",
    )(qkv, qkv, qkv)


def _softmax_update(scores, offsets, vts, state):
    m, l, acc = state
    m_new = m
    for s, c in zip(scores, offsets):
        m_new = jnp.maximum(m_new, jnp.max(s, axis=0, keepdims=True) + c)
    alpha = jnp.exp(m - m_new)
    l = alpha * l
    acc = alpha * acc
    for s, c, vt in zip(scores, offsets, vts):
        p = jnp.exp(s + (c - m_new))
        l = l + jnp.sum(p, axis=0, keepdims=True)
        acc = acc + jnp.dot(vt, p.astype(BF16), preferred_element_type=F32)
    return m_new, l, acc


def _diff_kernel(slope_ref, q_ref, k_ref, v_ref, gq_ref, gk_ref, lam_ref, sg_ref, o_ref,
                 kn_ref, vt_ref, *, lam_init):
    seq = q_ref.shape[0]
    slope = slope_ref[pl.program_id(1)]
    _fill_vt(v_ref, vt_ref)
    for r in range(0, seq, T_Q):
        kn_ref[r:r + T_Q, :] = _rms_halves(k_ref[r:r + T_Q, :].astype(F32), gk_ref[...]).astype(BF16)
    first, second = _head_row_masks()
    lq = lam_ref[...]
    lam = (jnp.exp(jnp.sum(lq[0:1] * lq[1:2], axis=-1, keepdims=True))
           - jnp.exp(jnp.sum(lq[2:3] * lq[3:4], axis=-1, keepdims=True)) + lam_init)
    kpos = lax.broadcasted_iota(jnp.int32, (T_G, T_Q), 0)
    qpos = lax.broadcasted_iota(jnp.int32, (T_G, T_Q), 1)
    rel = (qpos - kpos).astype(F32)
    lin_bias = -slope * rel
    diag_allowed = (kpos // CHUNK) <= (qpos // CHUNK)
    diag_bias = -slope * jnp.abs(rel)
    chains = [(par, mp) for par in range(2) for mp in range(2)]

    def q_pair(qp, _):
        qtm = {}
        for par in range(2):
            q0 = pl.multiple_of((2 * qp + par) * T_Q, T_Q)
            qn = _rms_halves(q_ref[pl.ds(q0, T_Q), :].astype(F32), gq_ref[...]) * QK_SCALE
            qt = qn.T
            qtm[par, 0] = jnp.where(first, qt, 0.0).astype(BF16)
            qtm[par, 1] = jnp.where(second, qt, 0.0).astype(BF16)

        def keys(g):
            return kn_ref[pl.ds(pl.multiple_of(g * T_G, T_G), T_G), :]

        def full_scores(kn2, g, chain):
            par, _ = chain
            off = jnp.asarray((2 * qp + par) * T_Q - g * T_G, F32)
            return jnp.dot(kn2, qtm[chain], preferred_element_type=F32) + lin_bias, -slope * off

        def diag_scores(kn2, chain):
            z = jnp.dot(kn2, qtm[chain], preferred_element_type=F32)
            return jnp.where(diag_allowed, z + diag_bias, NEG), 0.0

        def trip(it, states):
            gs = (2 * it, 2 * it + 1)
            kns = [keys(g) for g in gs]
            vts = [vt_ref[g] for g in gs]
            out = []
            scs = [[full_scores(kn2, g, chain) for kn2, g in zip(kns, gs)] for chain in chains]
            for sc, st in zip(scs, states):
                out.append(_softmax_update([s for s, _ in sc], [c for _, c in sc], vts, st))
            return tuple(out)

        init = (jnp.full((1, T_Q), NEG, F32), jnp.zeros((1, T_Q), F32), jnp.zeros((LANES, T_Q), F32))
        states = list(lax.fori_loop(0, qp, trip, (init,) * len(chains)))
        g = 2 * qp
        kns, vts = [keys(g), keys(g + 1)], [vt_ref[g], vt_ref[g + 1]]
        scs = []
        for chain in chains:
            if chain[0] == 0:
                scs.append([diag_scores(kns[0], chain)])
            else:
                scs.append([full_scores(kns[0], g, chain), diag_scores(kns[1], chain)])
        for i, sc in enumerate(scs):
            states[i] = _softmax_update([s for s, _ in sc], [c for _, c in sc], vts[:len(sc)], states[i])
        for par in range(2):
            (_, l0, a0), (_, l1, a1) = states[2 * par], states[2 * par + 1]
            ob = a0 / l0 - lam * (a1 / l1)
            y = ob * lax.rsqrt(jnp.mean(ob * ob, axis=0, keepdims=True) + RMS_EPS)
            q0 = pl.multiple_of((2 * qp + par) * T_Q, T_Q)
            o_ref[pl.ds(q0, T_Q), :] = (y.T * sg_ref[...] * (1.0 - lam_init)).astype(BF16)
        return 0

    lax.fori_loop(0, seq // (2 * T_Q), q_pair, 0)


def _diff_attn(qkv, slopes, gq, gk, lam_qk, subln_g, layer, lam_init):
    b, s, _ = qkv.shape
    base = 3 * (W_BRANCH // LANES)
    blk = lambda col: pl.BlockSpec((None, s, LANES), lambda bi, h: (bi, 0, col + h))
    lay = lambda bi, h: (layer, 0, 0)
    return pl.pallas_call(
        functools.partial(_diff_kernel, lam_init=lam_init),
        grid=(b, H_DIFF),
        in_specs=[
            pl.BlockSpec(memory_space=pltpu.SMEM),
            blk(base), blk(base + H_DIFF), blk(base + 2 * H_DIFF),
            pl.BlockSpec((None, 1, LANES), lay),
            pl.BlockSpec((None, 1, LANES), lay),
            pl.BlockSpec((None, 4, HEAD_DIM), lay),
            pl.BlockSpec((None, 1, LANES), lay),
        ],
        out_specs=pl.BlockSpec((None, s, LANES), lambda bi, h: (bi, 0, h)),
        out_shape=jax.ShapeDtypeStruct((b, s, W_BRANCH), BF16),
        scratch_shapes=[pltpu.VMEM((s, LANES), BF16), pltpu.VMEM((s // T_G, LANES, T_G), BF16)],
        compiler_params=pltpu.CompilerParams(
            dimension_semantics=("arbitrary", "arbitrary"), vmem_limit_bytes=VMEM_LIMIT),
        name="diff_attn",
    )(slopes, qkv, qkv, qkv, gq, gk, lam_qk, subln_g)


N_REL_GROUPS = (N_PAST_CHUNKS * CHUNK + T_Q - 1) // T_G + 1


def _chunk_kernel(q_ref, k_ref, v_ref, gq_ref, gk_ref, tab_ref, o_ref, kn_ref, vt_ref, sc_ref):
    seq = q_ref.shape[0]
    _fill_vt(v_ref, vt_ref)
    for r in range(0, seq, T_Q):
        kn_ref[r:r + T_Q, :] = _rms_halves(k_ref[r:r + T_Q, :].astype(F32), gk_ref[...]).astype(BF16)
    first, second = _head_row_masks()
    deltas = tuple(range(N_REL_GROUPS - 1, -1, -1))

    def score_stage(qi, tile_deltas, slot):
        q0 = pl.multiple_of(qi * T_Q, T_Q)
        qn = _rms_halves(q_ref[pl.ds(q0, T_Q), :].astype(F32), gq_ref[...]) * (QK_SCALE * LOG2E)
        qt = qn.T
        qts = (jnp.where(first, qt, 0.0).astype(BF16), jnp.where(second, qt, 0.0).astype(BF16))
        for dl in tile_deltas:
            k2 = kn_ref[pl.ds(pl.multiple_of((qi - dl) * T_G, T_G), T_G), :]
            for h in range(2):
                sc_ref[slot, h, dl] = (jnp.dot(k2, qts[h], preferred_element_type=F32)
                                       + tab_ref[h, dl])

    def softmax_stage(qi, tile_deltas, slot, fixed_shift):
        outs = []
        for h in range(2):
            scores = [sc_ref[slot, h, dl] for dl in tile_deltas]
            if fixed_shift:
                ps = [jnp.exp2(s) for s in scores]
            else:
                m = functools.reduce(jnp.maximum, [jnp.max(s, axis=0, keepdims=True) for s in scores])
                ps = [jnp.exp2(s - m) for s in scores]
            l = functools.reduce(jnp.add, [jnp.sum(p, axis=0, keepdims=True) for p in ps])
            acc = None
            for dl, p in zip(tile_deltas, ps):
                vth = vt_ref[qi - dl][h * HEAD_DIM:(h + 1) * HEAD_DIM, :]
                pv = jnp.dot(vth, p.astype(BF16), preferred_element_type=F32)
                acc = pv if acc is None else acc + pv
            outs.append(acc / l)
        q0 = pl.multiple_of(qi * T_Q, T_Q)
        o_ref[pl.ds(q0, T_Q), :] = jnp.concatenate(outs, axis=0).T.astype(BF16)

    def tile_deltas(qi):
        return tuple(dl for dl in deltas if qi - dl >= 0)

    n_q = seq // T_Q
    n_static = N_REL_GROUPS - 1 + (n_q - (N_REL_GROUPS - 1)) % 2

    def run(fixed_shift):
        score_stage(0, tile_deltas(0), 0)
        for qi in range(n_static):
            score_stage(qi + 1, tile_deltas(qi + 1), (qi + 1) % 2)
            softmax_stage(qi, tile_deltas(qi), qi % 2, fixed_shift)

        def pair(it, _):
            qi = n_static + 2 * it
            for u in range(2):
                score_stage(jnp.minimum(qi + u + 1, n_q - 1), deltas, (n_static + u + 1) % 2)
                softmax_stage(qi + u, deltas, (n_static + u) % 2, fixed_shift)
            return 0

        lax.fori_loop(0, (n_q - n_static) // 2, pair, 0)

    tabs = tab_ref[...]
    bound = (_qk_logit_bound(gq_ref[...], gk_ref[...])
             + jnp.max(jnp.where(tabs > 0.5 * NEG, jnp.abs(tabs), 0.0)))
    lax.cond(bound <= SAFE_LOG2, lambda: run(True), lambda: run(False))


def _chunk_attn(qkv, gq, gk, tab, layer):
    b, s, _ = qkv.shape
    n_hp = W_BRANCH // LANES
    base = 6 * n_hp
    blk = lambda col: pl.BlockSpec((None, s, LANES), lambda hp, bi: (bi, 0, col + hp))
    lay = lambda hp, bi: (layer, 0, 0)
    return pl.pallas_call(
        _chunk_kernel,
        grid=(n_hp, b),
        in_specs=[
            blk(base), blk(base + n_hp), blk(base + 2 * n_hp),
            pl.BlockSpec((None, 1, LANES), lay),
            pl.BlockSpec((None, 1, LANES), lay),
            pl.BlockSpec((None, 2, N_REL_GROUPS, T_G, T_Q), lambda hp, bi: (layer, hp, 0, 0, 0)),
        ],
        out_specs=pl.BlockSpec((None, s, LANES), lambda hp, bi: (bi, 0, hp)),
        out_shape=jax.ShapeDtypeStruct((b, s, W_BRANCH), BF16),
        scratch_shapes=[pltpu.VMEM((s, LANES), BF16), pltpu.VMEM((s // T_G, LANES, T_G), BF16),
                        pltpu.VMEM((2, 2, N_REL_GROUPS, T_G, T_Q), F32)],
        compiler_params=pltpu.CompilerParams(
            dimension_semantics=("arbitrary", "arbitrary"), vmem_limit_bytes=VMEM_LIMIT),
        name="chunk_attn",
    )(qkv, qkv, qkv, gq, gk, tab)


def _rel_bias_tiles(rel_bias):
    lead = rel_bias.shape[:-1]
    span = (N_REL_GROUPS - 1) * T_G + T_Q
    period = span + T_G
    edge_lo = jnp.broadcast_to(rel_bias[..., :1], lead + (T_G - REL_CLIP,))
    edge_hi = jnp.broadcast_to(rel_bias[..., -1:], lead + (span - REL_CLIP - 1,))
    row = jnp.concatenate([rel_bias[..., REL_CLIP:], edge_hi, edge_lo, rel_bias[..., :REL_CLIP]], axis=-1)
    flat = jnp.tile(row, (1,) * len(lead) + (T_G,))[..., :T_G * (period - 1)]
    toep = flat.reshape(lead + (T_G, period - 1))
    tiles = jnp.stack([toep[..., d * T_G:d * T_G + T_Q] for d in range(N_REL_GROUPS)], axis=-3)
    kchunk = np.arange(T_G)[:, None] // CHUNK
    qchunk = np.arange(T_Q)[None, :] // CHUNK
    dd = np.stack([qchunk - kchunk + d * (T_G // CHUNK) for d in range(N_REL_GROUPS)])
    return jnp.where((dd >= 0) & (dd <= N_PAST_CHUNKS), tiles * LOG2E, NEG)


def kernel(x, norm_mix_g, w_in, b_gate, qk_g_diff, lambda_qk, subln_g, qk_g_ch, rel_bias,
           w_branch_sb, w_branch_diff, w_branch_ch, w_out, norm_ffn_g, w_gu, w_down):
    b, s, d = x.shape
    m = b * s
    w_qkv = w_in[:, :, :QKV_W].astype(BF16)
    w_gate = w_in[:, :, QKV_W:].astype(BF16)
    w_br = jnp.concatenate([w_branch_sb, w_branch_diff, w_branch_ch], axis=1).astype(BF16)
    w_out_b = w_out.astype(BF16)
    w_gu_b = w_gu.astype(BF16)
    w_down_b = w_down.astype(BF16)
    g_mix = norm_mix_g.reshape(DEPTH, 1, d)
    g_ffn = norm_ffn_g.reshape(DEPTH, 1, d)
    gq_diff = jnp.tile(qk_g_diff[:, 0:1, :], (1, 1, 2))
    gk_diff = jnp.tile(qk_g_diff[:, 1:2, :], (1, 1, 2))
    gq_ch = jnp.tile(qk_g_ch[:, 0:1, :], (1, 1, 2))
    gk_ch = jnp.tile(qk_g_ch[:, 1:2, :], (1, 1, 2))
    sg = subln_g.reshape(DEPTH, 1, 2 * HEAD_DIM)
    tab = _rel_bias_tiles(rel_bias)
    slopes = jnp.asarray([2.0 ** (-8.0 * (i + 1) / H_DIFF) for i in range(H_DIFF)], F32)

    xf = x.reshape(m, d)
    for layer in range(DEPTH):
        lam_init = 0.8 - 0.6 * math.exp(-0.3 * layer)
        qkv = _qkv_proj(xf, g_mix, w_qkv, layer).reshape(b, s, QKV_W)
        o_a = _sb_attn(qkv)
        o_b = _diff_attn(qkv, slopes, gq_diff, gk_diff, lambda_qk, sg, layer, lam_init)
        o_c = _chunk_attn(qkv, gq_ch, gk_ch, tab, layer)
        xf = _merge_out(xf, g_mix, o_a.reshape(m, W_BRANCH), o_b.reshape(m, W_BRANCH),
                        o_c.reshape(m, W_BRANCH), w_gate, b_gate, w_br, w_out_b, layer)
        xf = _ffn(xf, g_ffn, w_gu_b, w_down_b, layer)
    return xf.reshape(b, s, d)
```

```python
import functools
import math

import jax
import jax.numpy as jnp
import numpy as np
from jax import lax
from jax.experimental import pallas as pl
from jax.experimental.pallas import tpu as pltpu

F32 = jnp.float32
BF16 = jnp.bfloat16

D_MODEL = 1024
DEPTH = 4
CHUNK = 64
HEAD_DIM = 64
H_DIFF = 4
N_PAST_CHUNKS = 8
REL_CLIP = 128
W_BRANCH = 512
QKV_W = 9 * W_BRANCH
N_BRANCH = 3
D_FF = int(math.ceil(8 * D_MODEL / 3 / 256)) * 256
RMS_EPS = 1e-6
QK_SCALE = HEAD_DIM ** -0.5

LANES = 128
T_Q = 256
T_K = 128
T_G = 256
NEG = -1e30
SB_DEAD = -104.0
LOG2E = 1.4426950408889634
SAFE_LOG2 = 60.0
VMEM_LIMIT = 56 * 1024 * 1024

ROW_TILE = 512
FF_CHUNK = 256


def _rms(x, g):
    return x * lax.rsqrt(jnp.mean(x * x, axis=-1, keepdims=True) + RMS_EPS) * g


def _rms_halves(x, g):
    lane = lax.broadcasted_iota(jnp.int32, (1, LANES), 1)
    first = lane < HEAD_DIM
    x2 = x * x
    s0 = jnp.sum(jnp.where(first, x2, 0.0), axis=-1, keepdims=True)
    s1 = jnp.sum(jnp.where(first, 0.0, x2), axis=-1, keepdims=True)
    ms = jnp.where(first, s0, s1) * (1.0 / HEAD_DIM)
    return x * lax.rsqrt(ms + RMS_EPS) * g


def _qk_logit_bound(gq, gk):
    return (1.02 * LOG2E * QK_SCALE * HEAD_DIM) * jnp.max(jnp.abs(gq)) * jnp.max(jnp.abs(gk))


def _transpose_to_bf16(x):
    return x.astype(F32).T.astype(BF16)


def _fill_vt(v_ref, vt_ref):
    width = vt_ref.shape[-1]
    for j in range(v_ref.shape[0] // width):
        vt_ref[j] = _transpose_to_bf16(v_ref[j * width:(j + 1) * width, :])


def _head_row_masks():
    row = lax.broadcasted_iota(jnp.int32, (LANES, 1), 0)
    return row < HEAD_DIM, row >= HEAD_DIM


def _qkv_kernel(x_ref, g_ref, w_ref, o_ref):
    h = _rms(x_ref[...], g_ref[...]).astype(BF16)
    n = o_ref.shape[1]
    for c in range(0, n, W_BRANCH):
        o_ref[:, c:c + W_BRANCH] = jnp.dot(
            h, w_ref[:, c:c + W_BRANCH], preferred_element_type=F32).astype(BF16)


def _qkv_proj(x, g, w, layer):
    m = x.shape[0]
    return pl.pallas_call(
        _qkv_kernel,
        grid=(m // ROW_TILE,),
        in_specs=[
            pl.BlockSpec((ROW_TILE, D_MODEL), lambda i: (i, 0)),
            pl.BlockSpec((None, 1, D_MODEL), lambda i: (layer, 0, 0)),
            pl.BlockSpec((None, D_MODEL, QKV_W), lambda i: (layer, 0, 0)),
        ],
        out_specs=pl.BlockSpec((ROW_TILE, QKV_W), lambda i: (i, 0)),
        out_shape=jax.ShapeDtypeStruct((m, QKV_W), BF16),
        compiler_params=pltpu.CompilerParams(
            dimension_semantics=("arbitrary",), vmem_limit_bytes=VMEM_LIMIT),
        name="qkv_proj",
    )(x, g, w)


def _merge_kernel(x_ref, g_ref, oa_ref, ob_ref, oc_ref, wg_ref, bg_ref, wbr_ref, wo_ref, out_ref):
    x = x_ref[...]
    h = _rms(x, g_ref[...]).astype(BF16)
    merged = None
    for br, o_ref in enumerate((oa_ref, ob_ref, oc_ref)):
        g_lin = jnp.dot(h, wg_ref[:, br * D_MODEL:(br + 1) * D_MODEL], preferred_element_type=F32)
        gate = 1.0 / (1.0 + jnp.exp(-(g_lin + bg_ref[br:br + 1, :])))
        proj = jnp.dot(o_ref[...], wbr_ref[br * W_BRANCH:(br + 1) * W_BRANCH, :],
                       preferred_element_type=F32)
        term = gate * proj
        merged = term if merged is None else merged + term
    out_ref[...] = x + jnp.dot(merged.astype(BF16), wo_ref[...], preferred_element_type=F32)


def _merge_out(x, g, o_a, o_b, o_c, w_gate, b_gate, w_br, w_out, layer):
    m = x.shape[0]
    row = lambda i: (i, 0)
    lay = lambda i: (layer, 0, 0)
    return pl.pallas_call(
        _merge_kernel,
        grid=(m // ROW_TILE,),
        in_specs=[
            pl.BlockSpec((ROW_TILE, D_MODEL), row),
            pl.BlockSpec((None, 1, D_MODEL), lay),
            pl.BlockSpec((ROW_TILE, W_BRANCH), row),
            pl.BlockSpec((ROW_TILE, W_BRANCH), row),
            pl.BlockSpec((ROW_TILE, W_BRANCH), row),
            pl.BlockSpec((None, D_MODEL, N_BRANCH * D_MODEL), lay),
            pl.BlockSpec((None, N_BRANCH, D_MODEL), lay),
            pl.BlockSpec((None, N_BRANCH * W_BRANCH, D_MODEL), lay),
            pl.BlockSpec((None, D_MODEL, D_MODEL), lay),
        ],
        out_specs=pl.BlockSpec((ROW_TILE, D_MODEL), row),
        out_shape=jax.ShapeDtypeStruct((m, D_MODEL), F32),
        compiler_params=pltpu.CompilerParams(
            dimension_semantics=("arbitrary",), vmem_limit_bytes=VMEM_LIMIT),
        name="merge_out",
    )(x, g, o_a, o_b, o_c, w_gate, b_gate, w_br, w_out)


def _ffn_kernel(x_ref, g_ref, wgu_ref, wd_ref, out_ref, act_ref):
    x = x_ref[...]
    h = _rms(x, g_ref[...]).astype(BF16)
    for c in range(0, D_FF, FF_CHUNK):
        gate = jnp.dot(h, wgu_ref[:, c:c + FF_CHUNK], preferred_element_type=F32)
        up = jnp.dot(h, wgu_ref[:, D_FF + c:D_FF + c + FF_CHUNK], preferred_element_type=F32)
        silu = gate / (1.0 + jnp.exp(-gate))
        act_ref[:, c:c + FF_CHUNK] = (silu * up).astype(BF16)
    out_ref[...] = x + jnp.dot(act_ref[...], wd_ref[...], preferred_element_type=F32)


def _ffn(x, g, w_gu, w_down, layer):
    m = x.shape[0]
    row = lambda i: (i, 0)
    lay = lambda i: (layer, 0, 0)
    return pl.pallas_call(
        _ffn_kernel,
        grid=(m // ROW_TILE,),
        in_specs=[
            pl.BlockSpec((ROW_TILE, D_MODEL), row),
            pl.BlockSpec((None, 1, D_MODEL), lay),
            pl.BlockSpec((None, D_MODEL, 2 * D_FF), lay, pipeline_mode=pl.Buffered(1)),
            pl.BlockSpec((None, D_FF, D_MODEL), lay, pipeline_mode=pl.Buffered(1)),
        ],
        out_specs=pl.BlockSpec((ROW_TILE, D_MODEL), row),
        out_shape=jax.ShapeDtypeStruct((m, D_MODEL), F32),
        scratch_shapes=[pltpu.VMEM((ROW_TILE, D_FF), BF16)],
        compiler_params=pltpu.CompilerParams(
            dimension_semantics=("arbitrary",), vmem_limit_bytes=VMEM_LIMIT),
        name="ffn",
    )(x, g, w_gu, w_down)


def _sb_weights(z, cum, carry, mask):
    sp = jnp.log(1.0 + jnp.exp(-jnp.abs(z)))
    log_beta = jnp.minimum(z, 0.0) - sp
    log_1m = log_beta - z
    if mask is not None:
        log_1m = jnp.where(mask, log_1m, 0.0)
    hi = log_1m.astype(BF16)
    lo = (log_1m - hi.astype(F32)).astype(BF16)
    afters = []
    for u in reversed(range(z.shape[0] // T_K)):
        r0, r1 = u * T_K, (u + 1) * T_K
        within = jnp.dot(cum, jnp.concatenate([hi[r0:r1], lo[r0:r1]], axis=0),
                         preferred_element_type=F32)
        afters.append(within + carry)
        carry = carry + within[0:1, :] + log_1m[r0:r0 + 1, :]
    after = afters[0] if len(afters) == 1 else jnp.concatenate(afters[::-1], axis=0)
    w = jnp.exp(log_beta + after)
    if mask is not None:
        w = jnp.where(mask, w, 0.0)
    return w.astype(BF16), carry


def _sb_kernel(q_ref, k_ref, v_ref, o_ref, vt_ref):
    seq = q_ref.shape[0]
    _fill_vt(v_ref, vt_ref)
    first, second = _head_row_masks()
    kk = lax.broadcasted_iota(jnp.int32, (T_K, T_K), 0)
    kk2 = lax.broadcasted_iota(jnp.int32, (T_K, T_K), 1)
    later = jnp.where(kk2 > kk, 1.0, 0.0).astype(BF16)
    cum = jnp.concatenate([later, later], axis=1)
    strict = (lax.broadcasted_iota(jnp.int32, (T_G, T_Q), 0)
              < lax.broadcasted_iota(jnp.int32, (T_G, T_Q), 1))
    heads = range(2)

    def load_q(qi):
        q0 = pl.multiple_of(qi * T_Q, T_Q)
        qt = (q_ref[pl.ds(q0, T_Q), :].astype(F32) * QK_SCALE).T
        return (jnp.where(first, qt, 0.0).astype(BF16), jnp.where(second, qt, 0.0).astype(BF16))

    def logits(g, qts):
        k2 = k_ref[pl.ds(pl.multiple_of(g * T_G, T_G), T_G), :]
        return [jnp.dot(k2, qts[h], preferred_element_type=F32) for h in heads]

    def weights(zs, mask, carries):
        out = [_sb_weights(zs[h], cum, carries[h], mask) for h in heads]
        return [w for w, _ in out], [c for _, c in out]

    def add_pv(g, ws, accs):
        vt = vt_ref[g]
        return [accs[h] + jnp.dot(vt[h * HEAD_DIM:(h + 1) * HEAD_DIM, :], ws[h],
                                  preferred_element_type=F32) for h in heads]

    def tail(qts, g_start, carries, accs):
        def live(st):
            g, carries, _ = st
            return (g >= 0) & (jnp.max(jnp.maximum(carries[0], carries[1])) >= SB_DEAD)

        def body(st):
            g, carries, accs = st
            ws, carries = weights(logits(g, qts), None, list(carries))
            return g - 1, tuple(carries), tuple(add_pv(g, ws, list(accs)))

        _, _, accs = lax.while_loop(live, body, (g_start, tuple(carries), tuple(accs)))
        return list(accs)

    def store(qi, accs):
        q0 = pl.multiple_of(qi * T_Q, T_Q)
        o_ref[pl.ds(q0, T_Q), :] = jnp.concatenate(accs, axis=0).T.astype(BF16)

    def q_pair(qp, has_earlier):
        qa, qb = 2 * qp, 2 * qp + 1
        qts_a, qts_b = load_q(qa), load_q(qb)
        z_a0, z_b0, z_b1 = logits(qa, qts_a), logits(qb, qts_b), logits(qa, qts_b)
        if has_earlier:
            z_a1 = logits(qa - 1, qts_a)
        zeros = [jnp.zeros((1, T_Q), F32) for _ in heads]
        acc0 = [jnp.zeros((HEAD_DIM, T_Q), F32) for _ in heads]
        w_a0, car_a = weights(z_a0, strict, zeros)
        w_b0, car_b = weights(z_b0, strict, zeros)
        acc_a = add_pv(qa, w_a0, acc0)
        acc_b = add_pv(qb, w_b0, acc0)
        w_b1, car_b = weights(z_b1, None, car_b)
        acc_b = add_pv(qa, w_b1, acc_b)
        if has_earlier:
            w_a1, car_a = weights(z_a1, None, car_a)
            acc_a = add_pv(qa - 1, w_a1, acc_a)
            acc_a = tail(qts_a, qa - 2, car_a, acc_a)
            acc_b = tail(qts_b, qa - 1, car_b, acc_b)
        store(qa, acc_a)
        store(qb, acc_b)

    q_pair(0, False)

    def body(qp, _):
        q_pair(qp, True)
        return 0

    lax.fori_loop(1, seq // (2 * T_Q), body, 0)


def _sb_attn(qkv):
    b, s, _ = qkv.shape
    blk = lambda col: pl.BlockSpec((None, s, LANES), lambda bi, hp: (bi, 0, col + hp))
    n_hp = W_BRANCH // LANES
    return pl.pallas_call(
        _sb_kernel,
        grid=(b, n_hp),
        in_specs=[blk(0), blk(n_hp), blk(2 * n_hp)],
        out_specs=pl.BlockSpec((None, s, LANES), lambda bi, hp: (bi, 0, hp)),
        out_shape=jax.ShapeDtypeStruct((b, s, W_BRANCH), BF16),
        scratch_shapes=[pltpu.VMEM((s // T_G, LANES, T_G), BF16)],
        compiler_params=pltpu.CompilerParams(
            dimension_semantics=("arbitrary", "arbitrary"), vmem_limit_bytes=VMEM_LIMIT),
        name="sb_attn",
    )(qkv, qkv, qkv)


def _softmax_update(scores, offsets, vts, state):
    m, l, acc = state
    m_new = m
    for s, c in zip(scores, offsets):
        m_new = jnp.maximum(m_new, jnp.max(s, axis=0, keepdims=True) + c)
    alpha = jnp.exp(m - m_new)
    l = alpha * l
    acc = alpha * acc
    for s, c, vt in zip(scores, offsets, vts):
        p = jnp.exp(s + (c - m_new))
        l = l + jnp.sum(p, axis=0, keepdims=True)
        acc = acc + jnp.dot(vt, p.astype(BF16), preferred_element_type=F32)
    return m_new, l, acc


def _diff_kernel(slope_ref, q_ref, k_ref, v_ref, gq_ref, gk_ref, lam_ref, sg_ref, o_ref,
                 kn_ref, vt_ref, qt_ref, bt_ref, raw_ref, sum_ref, *, lam_init):
    seq = q_ref.shape[0]
    slope = slope_ref[pl.program_id(1)]
    _fill_vt(v_ref, vt_ref)
    for r in range(0, seq, T_Q):
        kn_ref[r:r + T_Q, :] = _rms_halves(k_ref[r:r + T_Q, :].astype(F32), gk_ref[...]).astype(BF16)
    first, second = _head_row_masks()
    lq = lam_ref[...]
    lam = (jnp.exp(jnp.sum(lq[0:1] * lq[1:2], axis=-1, keepdims=True))
           - jnp.exp(jnp.sum(lq[2:3] * lq[3:4], axis=-1, keepdims=True)) + lam_init)
    kpos = lax.broadcasted_iota(jnp.int32, (T_G, T_Q), 0)
    qpos = lax.broadcasted_iota(jnp.int32, (T_G, T_Q), 1)
    rel = (qpos - kpos).astype(F32)
    lin_bias = -slope * rel
    diag_allowed = (kpos // CHUNK) <= (qpos // CHUNK)
    diag_bias = -slope * jnp.abs(rel)
    chains = [(par, mp) for par in range(2) for mp in range(2)]

    def q_pair(qp, _):
        qtm = {}
        for par in range(2):
            q0 = pl.multiple_of((2 * qp + par) * T_Q, T_Q)
            qn = _rms_halves(q_ref[pl.ds(q0, T_Q), :].astype(F32), gq_ref[...]) * QK_SCALE
            qt = qn.T
            qtm[par, 0] = jnp.where(first, qt, 0.0).astype(BF16)
            qtm[par, 1] = jnp.where(second, qt, 0.0).astype(BF16)

        def keys(g):
            return kn_ref[pl.ds(pl.multiple_of(g * T_G, T_G), T_G), :]

        def full_scores(kn2, g, chain):
            par, _ = chain
            off = jnp.asarray((2 * qp + par) * T_Q - g * T_G, F32)
            return jnp.dot(kn2, qtm[chain], preferred_element_type=F32) + lin_bias, -slope * off

        def diag_scores(kn2, chain):
            z = jnp.dot(kn2, qtm[chain], preferred_element_type=F32)
            return jnp.where(diag_allowed, z + diag_bias, NEG), 0.0

        def trip(it, states):
            gs = (2 * it, 2 * it + 1)
            kns = [keys(g) for g in gs]
            vts = [vt_ref[g] for g in gs]
            out = []
            scs = [[full_scores(kn2, g, chain) for kn2, g in zip(kns, gs)] for chain in chains]
            for sc, st in zip(scs, states):
                out.append(_softmax_update([s for s, _ in sc], [c for _, c in sc], vts, st))
            return tuple(out)

        init = (jnp.full((1, T_Q), NEG, F32), jnp.zeros((1, T_Q), F32), jnp.zeros((LANES, T_Q), F32))
        states = list(lax.fori_loop(0, qp, trip, (init,) * len(chains)))
        g = 2 * qp
        kns, vts = [keys(g), keys(g + 1)], [vt_ref[g], vt_ref[g + 1]]
        scs = []
        for chain in chains:
            if chain[0] == 0:
                scs.append([diag_scores(kns[0], chain)])
            else:
                scs.append([full_scores(kns[0], g, chain), diag_scores(kns[1], chain)])
        for i, sc in enumerate(scs):
            states[i] = _softmax_update([s for s, _ in sc], [c for _, c in sc], vts[:len(sc)], states[i])
        for par in range(2):
            (_, l0, a0), (_, l1, a1) = states[2 * par], states[2 * par + 1]
            ob = a0 / l0 - lam * (a1 / l1)
            y = ob * lax.rsqrt(jnp.mean(ob * ob, axis=0, keepdims=True) + RMS_EPS)
            q0 = pl.multiple_of((2 * qp + par) * T_Q, T_Q)
            o_ref[pl.ds(q0, T_Q), :] = (y.T * sg_ref[...] * (1.0 - lam_init)).astype(BF16)
        return 0

    n_q = seq // T_Q

    def bounded():
        slope2 = slope * LOG2E
        bt_ref[0] = jnp.where(diag_allowed, -slope2 * jnp.abs(rel), NEG)
        for d in range(1, n_q):
            bt_ref[d] = -slope2 * (rel + float(d * T_G))
        for qi in range(n_q):
            qn = _rms_halves(q_ref[qi * T_Q:(qi + 1) * T_Q, :].astype(F32), gq_ref[...])
            qt = (qn * (QK_SCALE * LOG2E)).T
            qt_ref[qi, 0] = jnp.where(first, qt, 0.0).astype(BF16)
            qt_ref[qi, 1] = jnp.where(second, qt, 0.0).astype(BF16)
        n_steps = n_q + 1

        def balanced_pair(i, _):
            hi = n_q - 1 - i

            def step_ids(k):
                if k <= n_q // 2:
                    return hi, k, None
                is_hi = k <= hi
                return jnp.where(is_hi, hi, i), jnp.where(is_hi, k, n_q - k), k == hi + 1

            def scores(k):
                tile, g, _ = step_ids(k)
                kn2 = kn_ref[pl.ds(pl.multiple_of(g * T_G, T_G), T_G), :]
                bias = bt_ref[tile - g]
                return [jnp.dot(kn2, qt_ref[tile, mp], preferred_element_type=F32) + bias
                        for mp in range(2)]

            accs = [jnp.zeros((LANES, T_Q), F32) for _ in range(2)]
            sums = [jnp.zeros((1, T_Q), F32) for _ in range(2)]
            z_next = scores(0)
            for k in range(n_steps):
                z = z_next
                if k + 1 < n_steps:
                    z_next = scores(k + 1)
                tile, g, restart = step_ids(k)
                vt = vt_ref[g]
                for mp in range(2):
                    p = jnp.exp2(z[mp])
                    psum = jnp.sum(p, axis=0, keepdims=True)
                    pv = jnp.dot(vt, p.astype(BF16), preferred_element_type=F32)
                    if restart is None:
                        sums[mp], accs[mp] = sums[mp] + psum, accs[mp] + pv
                    else:
                        keep = jnp.where(restart, 0.0, 1.0)
                        sums[mp], accs[mp] = sums[mp] * keep + psum, accs[mp] * keep + pv
                    if k >= n_q // 2:
                        raw_ref[tile, mp] = accs[mp]
                        sum_ref[tile, mp] = sums[mp]
            return 0

        lax.fori_loop(0, n_q // 2, balanced_pair, 0)
        for qi in range(n_q):
            ob = (raw_ref[qi, 0] * (1.0 / sum_ref[qi, 0])
                  - lam * (raw_ref[qi, 1] * (1.0 / sum_ref[qi, 1])))
            y = ob * lax.rsqrt(jnp.mean(ob * ob, axis=0, keepdims=True) + RMS_EPS)
            o_ref[qi * T_Q:(qi + 1) * T_Q, :] = (y.T * sg_ref[...] * (1.0 - lam_init)).astype(BF16)

    def general():
        lax.fori_loop(0, seq // (2 * T_Q), q_pair, 0)

    lax.cond(_qk_logit_bound(gq_ref[...], gk_ref[...]) <= SAFE_LOG2, bounded, general)


def _diff_attn(qkv, slopes, gq, gk, lam_qk, subln_g, layer, lam_init):
    b, s, _ = qkv.shape
    base = 3 * (W_BRANCH // LANES)
    blk = lambda col: pl.BlockSpec((None, s, LANES), lambda bi, h: (bi, 0, col + h))
    lay = lambda bi, h: (layer, 0, 0)
    return pl.pallas_call(
        functools.partial(_diff_kernel, lam_init=lam_init),
        grid=(b, H_DIFF),
        in_specs=[
            pl.BlockSpec(memory_space=pltpu.SMEM),
            blk(base), blk(base + H_DIFF), blk(base + 2 * H_DIFF),
            pl.BlockSpec((None, 1, LANES), lay),
            pl.BlockSpec((None, 1, LANES), lay),
            pl.BlockSpec((None, 4, HEAD_DIM), lay),
            pl.BlockSpec((None, 1, LANES), lay),
        ],
        out_specs=pl.BlockSpec((None, s, LANES), lambda bi, h: (bi, 0, h)),
        out_shape=jax.ShapeDtypeStruct((b, s, W_BRANCH), BF16),
        scratch_shapes=[
            pltpu.VMEM((s, LANES), BF16),
            pltpu.VMEM((s // T_G, LANES, T_G), BF16),
            pltpu.VMEM((s // T_Q, 2, LANES, T_Q), BF16),
            pltpu.VMEM((s // T_Q, T_G, T_Q), F32),
            pltpu.VMEM((s // T_Q, 2, LANES, T_Q), F32),
            pltpu.VMEM((s // T_Q, 2, 1, T_Q), F32),
        ],
        compiler_params=pltpu.CompilerParams(
            dimension_semantics=("arbitrary", "arbitrary"), vmem_limit_bytes=VMEM_LIMIT),
        name="diff_attn",
    )(slopes, qkv, qkv, qkv, gq, gk, lam_qk, subln_g)


N_REL_GROUPS = (N_PAST_CHUNKS * CHUNK + T_Q - 1) // T_G + 1


def _chunk_kernel(q_ref, k_ref, v_ref, gq_ref, gk_ref, tab_ref, o_ref, kn_ref, vt_ref, sc_ref):
    seq = q_ref.shape[0]
    _fill_vt(v_ref, vt_ref)
    for r in range(0, seq, T_Q):
        kn_ref[r:r + T_Q, :] = _rms_halves(k_ref[r:r + T_Q, :].astype(F32), gk_ref[...]).astype(BF16)
    first, second = _head_row_masks()
    deltas = tuple(range(N_REL_GROUPS - 1, -1, -1))

    def score_stage(qi, tile_deltas, slot):
        q0 = pl.multiple_of(qi * T_Q, T_Q)
        qn = _rms_halves(q_ref[pl.ds(q0, T_Q), :].astype(F32), gq_ref[...]) * (QK_SCALE * LOG2E)
        qt = qn.T
        qts = (jnp.where(first, qt, 0.0).astype(BF16), jnp.where(second, qt, 0.0).astype(BF16))
        for dl in tile_deltas:
            k2 = kn_ref[pl.ds(pl.multiple_of((qi - dl) * T_G, T_G), T_G), :]
            for h in range(2):
                sc_ref[slot, h, dl] = (jnp.dot(k2, qts[h], preferred_element_type=F32)
                                       + tab_ref[h, dl])

    def softmax_stage(qi, tile_deltas, slot, fixed_shift):
        outs = []
        for h in range(2):
            scores = [sc_ref[slot, h, dl] for dl in tile_deltas]
            if fixed_shift:
                ps = [jnp.exp2(s) for s in scores]
            else:
                m = functools.reduce(jnp.maximum, [jnp.max(s, axis=0, keepdims=True) for s in scores])
                ps = [jnp.exp2(s - m) for s in scores]
            l = functools.reduce(jnp.add, [jnp.sum(p, axis=0, keepdims=True) for p in ps])
            acc = None
            for dl, p in zip(tile_deltas, ps):
                vth = vt_ref[qi - dl][h * HEAD_DIM:(h + 1) * HEAD_DIM, :]
                pv = jnp.dot(vth, p.astype(BF16), preferred_element_type=F32)
                acc = pv if acc is None else acc + pv
            outs.append(acc / l)
        q0 = pl.multiple_of(qi * T_Q, T_Q)
        o_ref[pl.ds(q0, T_Q), :] = jnp.concatenate(outs, axis=0).T.astype(BF16)

    def tile_deltas(qi):
        return tuple(dl for dl in deltas if qi - dl >= 0)

    n_q = seq // T_Q
    n_static = N_REL_GROUPS - 1 + (n_q - (N_REL_GROUPS - 1)) % 2

    def run(fixed_shift):
        score_stage(0, tile_deltas(0), 0)
        for qi in range(n_static):
            score_stage(qi + 1, tile_deltas(qi + 1), (qi + 1) % 2)
            softmax_stage(qi, tile_deltas(qi), qi % 2, fixed_shift)

        def pair(it, _):
            qi = n_static + 2 * it
            for u in range(2):
                score_stage(jnp.minimum(qi + u + 1, n_q - 1), deltas, (n_static + u + 1) % 2)
                softmax_stage(qi + u, deltas, (n_static + u) % 2, fixed_shift)
            return 0

        lax.fori_loop(0, (n_q - n_static) // 2, pair, 0)

    tabs = tab_ref[...]
    bound = (_qk_logit_bound(gq_ref[...], gk_ref[...])
             + jnp.max(jnp.where(tabs > 0.5 * NEG, jnp.abs(tabs), 0.0)))
    lax.cond(bound <= SAFE_LOG2, lambda: run(True), lambda: run(False))


def _chunk_attn(qkv, gq, gk, tab, layer):
    b, s, _ = qkv.shape
    n_hp = W_BRANCH // LANES
    base = 6 * n_hp
    blk = lambda col: pl.BlockSpec((None, s, LANES), lambda hp, bi: (bi, 0, col + hp))
    lay = lambda hp, bi: (layer, 0, 0)
    return pl.pallas_call(
        _chunk_kernel,
        grid=(n_hp, b),
        in_specs=[
            blk(base), blk(base + n_hp), blk(base + 2 * n_hp),
            pl.BlockSpec((None, 1, LANES), lay),
            pl.BlockSpec((None, 1, LANES), lay),
            pl.BlockSpec((None, 2, N_REL_GROUPS, T_G, T_Q), lambda hp, bi: (layer, hp, 0, 0, 0)),
        ],
        out_specs=pl.BlockSpec((None, s, LANES), lambda hp, bi: (bi, 0, hp)),
        out_shape=jax.ShapeDtypeStruct((b, s, W_BRANCH), BF16),
        scratch_shapes=[pltpu.VMEM((s, LANES), BF16), pltpu.VMEM((s // T_G, LANES, T_G), BF16),
                        pltpu.VMEM((2, 2, N_REL_GROUPS, T_G, T_Q), F32)],
        compiler_params=pltpu.CompilerParams(
            dimension_semantics=("arbitrary", "arbitrary"), vmem_limit_bytes=VMEM_LIMIT),
        name="chunk_attn",
    )(qkv, qkv, qkv, gq, gk, tab)


def _rel_bias_tiles(rel_bias):
    lead = rel_bias.shape[:-1]
    span = (N_REL_GROUPS - 1) * T_G + T_Q
    period = span + T_G
    edge_lo = jnp.broadcast_to(rel_bias[..., :1], lead + (T_G - REL_CLIP,))
    edge_hi = jnp.broadcast_to(rel_bias[..., -1:], lead + (span - REL_CLIP - 1,))
    row = jnp.concatenate([rel_bias[..., REL_CLIP:], edge_hi, edge_lo, rel_bias[..., :REL_CLIP]], axis=-1)
    flat = jnp.tile(row, (1,) * len(lead) + (T_G,))[..., :T_G * (period - 1)]
    toep = flat.reshape(lead + (T_G, period - 1))
    tiles = jnp.stack([toep[..., d * T_G:d * T_G + T_Q] for d in range(N_REL_GROUPS)], axis=-3)
    kchunk = np.arange(T_G)[:, None] // CHUNK
    qchunk = np.arange(T_Q)[None, :] // CHUNK
    dd = np.stack([qchunk - kchunk + d * (T_G // CHUNK) for d in range(N_REL_GROUPS)])
    return jnp.where((dd >= 0) & (dd <= N_PAST_CHUNKS), tiles * LOG2E, NEG)


def kernel(x, norm_mix_g, w_in, b_gate, qk_g_diff, lambda_qk, subln_g, qk_g_ch, rel_bias,
           w_branch_sb, w_branch_diff, w_branch_ch, w_out, norm_ffn_g, w_gu, w_down):
    b, s, d = x.shape
    m = b * s
    w_qkv = w_in[:, :, :QKV_W].astype(BF16)
    w_gate = w_in[:, :, QKV_W:].astype(BF16)
    w_br = jnp.concatenate([w_branch_sb, w_branch_diff, w_branch_ch], axis=1).astype(BF16)
    w_out_b = w_out.astype(BF16)
    w_gu_b = w_gu.astype(BF16)
    w_down_b = w_down.astype(BF16)
    g_mix = norm_mix_g.reshape(DEPTH, 1, d)
    g_ffn = norm_ffn_g.reshape(DEPTH, 1, d)
    gq_diff = jnp.tile(qk_g_diff[:, 0:1, :], (1, 1, 2))
    gk_diff = jnp.tile(qk_g_diff[:, 1:2, :], (1, 1, 2))
    gq_ch = jnp.tile(qk_g_ch[:, 0:1, :], (1, 1, 2))
    gk_ch = jnp.tile(qk_g_ch[:, 1:2, :], (1, 1, 2))
    sg = subln_g.reshape(DEPTH, 1, 2 * HEAD_DIM)
    tab = _rel_bias_tiles(rel_bias)
    slopes = jnp.asarray([2.0 ** (-8.0 * (i + 1) / H_DIFF) for i in range(H_DIFF)], F32)

    xf = x.reshape(m, d)
    for layer in range(DEPTH):
        lam_init = 0.8 - 0.6 * math.exp(-0.3 * layer)
        qkv = _qkv_proj(xf, g_mix, w_qkv, layer).reshape(b, s, QKV_W)
        o_a = _sb_attn(qkv)
        o_b = _diff_attn(qkv, slopes, gq_diff, gk_diff, lambda_qk, sg, layer, lam_init)
        o_c = _chunk_attn(qkv, gq_ch, gk_ch, tab, layer)
        xf = _merge_out(xf, g_mix, o_a.reshape(m, W_BRANCH), o_b.reshape(m, W_BRANCH),
                        o_c.reshape(m, W_BRANCH), w_gate, b_gate, w_br, w_out_b, layer)
        xf = _ffn(xf, g_ffn, w_gu_b, w_down_b, layer)
    return xf.reshape(b, s, d)
```

```python
import functools
import math

import jax
import jax.numpy as jnp
import numpy as np
from jax import lax
from jax.experimental import pallas as pl
from jax.experimental.pallas import tpu as pltpu

F32 = jnp.float32
BF16 = jnp.bfloat16

D_MODEL = 1024
DEPTH = 4
CHUNK = 64
HEAD_DIM = 64
H_DIFF = 4
N_PAST_CHUNKS = 8
REL_CLIP = 128
W_BRANCH = 512
QKV_W = 9 * W_BRANCH
N_BRANCH = 3
D_FF = int(math.ceil(8 * D_MODEL / 3 / 256)) * 256
RMS_EPS = 1e-6
QK_SCALE = HEAD_DIM ** -0.5

LANES = 128
T_Q = 256
T_K = 128
T_G = 256
NEG = -1e30
SB_DEAD = -104.0
LOG2E = 1.4426950408889634
SAFE_LOG2 = 60.0
VMEM_LIMIT = 56 * 1024 * 1024

ROW_TILE = 512
FF_CHUNK = 256


def _rms(x, g):
    return x * lax.rsqrt(jnp.mean(x * x, axis=-1, keepdims=True) + RMS_EPS) * g


def _rms_halves(x, g):
    lane = lax.broadcasted_iota(jnp.int32, (1, LANES), 1)
    first = lane < HEAD_DIM
    x2 = x * x
    s0 = jnp.sum(jnp.where(first, x2, 0.0), axis=-1, keepdims=True)
    s1 = jnp.sum(jnp.where(first, 0.0, x2), axis=-1, keepdims=True)
    ms = jnp.where(first, s0, s1) * (1.0 / HEAD_DIM)
    return x * lax.rsqrt(ms + RMS_EPS) * g


def _qk_logit_bound(gq, gk):
    return (1.02 * LOG2E * QK_SCALE * HEAD_DIM) * jnp.max(jnp.abs(gq)) * jnp.max(jnp.abs(gk))


def _head_row_masks():
    row = lax.broadcasted_iota(jnp.int32, (LANES, 1), 0)
    return row < HEAD_DIM, row >= HEAD_DIM


def _split_heads(qt, first, second):
    zero = jnp.zeros_like(qt)
    return jnp.where(first, qt, zero), jnp.where(second, qt, zero)


N_SECTIONS = 9


def _qkv_kernel(x_ref, g_ref, w_ref, qkg_ref, k_out, qt_out, vt_out):
    h = _rms(x_ref[...], g_ref[...]).astype(BF16)

    def project(sec):
        return jnp.dot(h, w_ref[:, sec * W_BRANCH:(sec + 1) * W_BRANCH], preferred_element_type=F32)

    z_next = project(0)
    for sec in range(N_SECTIONS):
        z = z_next
        if sec + 1 < N_SECTIONS:
            z_next = project(sec + 1)
        branch, role = divmod(sec, 3)
        for cb in range(W_BRANCH // LANES):
            blk = z[:, cb * LANES:(cb + 1) * LANES]
            col = sec * W_BRANCH + cb * LANES
            if branch > 0 and role < 2:
                blk = _rms_halves(blk, qkg_ref[:, col:col + LANES])
            elif role == 0:
                blk = blk * QK_SCALE
            out_col = branch * W_BRANCH + cb * LANES
            if role == 1:
                k_out[:, out_col:out_col + LANES] = blk.astype(BF16)
            else:
                out = qt_out if role == 0 else vt_out
                for r in range(ROW_TILE // T_Q):
                    out[r, out_col:out_col + LANES, :] = blk[r * T_Q:(r + 1) * T_Q, :].T.astype(BF16)


def _qkv_proj(x, g, w, qk_gain, layer):
    m = x.shape[0]
    width = N_BRANCH * W_BRANCH
    tiles = ROW_TILE // T_Q
    lay = lambda i: (layer, 0, 0)
    transposed = jax.ShapeDtypeStruct((m // T_Q, width, T_Q), BF16)
    return pl.pallas_call(
        _qkv_kernel,
        grid=(m // ROW_TILE,),
        in_specs=[
            pl.BlockSpec((ROW_TILE, D_MODEL), lambda i: (i, 0)),
            pl.BlockSpec((None, 1, D_MODEL), lay),
            pl.BlockSpec((None, D_MODEL, QKV_W), lay),
            pl.BlockSpec((None, 1, QKV_W), lay),
        ],
        out_specs=[
            pl.BlockSpec((ROW_TILE, width), lambda i: (i, 0)),
            pl.BlockSpec((tiles, width, T_Q), lambda i: (i, 0, 0)),
            pl.BlockSpec((tiles, width, T_Q), lambda i: (i, 0, 0)),
        ],
        out_shape=[jax.ShapeDtypeStruct((m, width), BF16), transposed, transposed],
        compiler_params=pltpu.CompilerParams(
            dimension_semantics=("arbitrary",), vmem_limit_bytes=VMEM_LIMIT),
        name="qkv_proj",
    )(x, g, w, qk_gain)


def _merge_kernel(x_ref, g_ref, oa_ref, ob_ref, oc_ref, wg_ref, bg_ref, wbr_ref, wo_ref, out_ref):
    x = x_ref[...]
    h = _rms(x, g_ref[...]).astype(BF16)
    merged = None
    for br, o_ref in enumerate((oa_ref, ob_ref, oc_ref)):
        g_lin = jnp.dot(h, wg_ref[:, br * D_MODEL:(br + 1) * D_MODEL], preferred_element_type=F32)
        gate = 1.0 / (1.0 + jnp.exp(-(g_lin + bg_ref[br:br + 1, :])))
        proj = jnp.dot(o_ref[...], wbr_ref[br * W_BRANCH:(br + 1) * W_BRANCH, :],
                       preferred_element_type=F32)
        term = gate * proj
        merged = term if merged is None else merged + term
    out_ref[...] = x + jnp.dot(merged.astype(BF16), wo_ref[...], preferred_element_type=F32)


def _merge_out(x, g, o_a, o_b, o_c, w_gate, b_gate, w_br, w_out, layer):
    m = x.shape[0]
    row = lambda i: (i, 0)
    lay = lambda i: (layer, 0, 0)
    return pl.pallas_call(
        _merge_kernel,
        grid=(m // ROW_TILE,),
        in_specs=[
            pl.BlockSpec((ROW_TILE, D_MODEL), row),
            pl.BlockSpec((None, 1, D_MODEL), lay),
            pl.BlockSpec((ROW_TILE, W_BRANCH), row),
            pl.BlockSpec((ROW_TILE, W_BRANCH), row),
            pl.BlockSpec((ROW_TILE, W_BRANCH), row),
            pl.BlockSpec((None, D_MODEL, N_BRANCH * D_MODEL), lay),
            pl.BlockSpec((None, N_BRANCH, D_MODEL), lay),
            pl.BlockSpec((None, N_BRANCH * W_BRANCH, D_MODEL), lay),
            pl.BlockSpec((None, D_MODEL, D_MODEL), lay),
        ],
        out_specs=pl.BlockSpec((ROW_TILE, D_MODEL), row),
        out_shape=jax.ShapeDtypeStruct((m, D_MODEL), F32),
        compiler_params=pltpu.CompilerParams(
            dimension_semantics=("arbitrary",), vmem_limit_bytes=VMEM_LIMIT),
        name="merge_out",
    )(x, g, o_a, o_b, o_c, w_gate, b_gate, w_br, w_out)


def _ffn_kernel(x_ref, g_ref, wgu_ref, wd_ref, out_ref, act_ref):
    x = x_ref[...]
    h = _rms(x, g_ref[...]).astype(BF16)
    for c in range(0, D_FF, FF_CHUNK):
        gate = jnp.dot(h, wgu_ref[:, c:c + FF_CHUNK], preferred_element_type=F32)
        up = jnp.dot(h, wgu_ref[:, D_FF + c:D_FF + c + FF_CHUNK], preferred_element_type=F32)
        silu = gate / (1.0 + jnp.exp(-gate))
        act_ref[:, c:c + FF_CHUNK] = (silu * up).astype(BF16)
    out_ref[...] = x + jnp.dot(act_ref[...], wd_ref[...], preferred_element_type=F32)


def _ffn(x, g, w_gu, w_down, layer):
    m = x.shape[0]
    row = lambda i: (i, 0)
    lay = lambda i: (layer, 0, 0)
    return pl.pallas_call(
        _ffn_kernel,
        grid=(m // ROW_TILE,),
        in_specs=[
            pl.BlockSpec((ROW_TILE, D_MODEL), row),
            pl.BlockSpec((None, 1, D_MODEL), lay),
            pl.BlockSpec((None, D_MODEL, 2 * D_FF), lay, pipeline_mode=pl.Buffered(1)),
            pl.BlockSpec((None, D_FF, D_MODEL), lay, pipeline_mode=pl.Buffered(1)),
        ],
        out_specs=pl.BlockSpec((ROW_TILE, D_MODEL), row),
        out_shape=jax.ShapeDtypeStruct((m, D_MODEL), F32),
        scratch_shapes=[pltpu.VMEM((ROW_TILE, D_FF), BF16)],
        compiler_params=pltpu.CompilerParams(
            dimension_semantics=("arbitrary",), vmem_limit_bytes=VMEM_LIMIT),
        name="ffn",
    )(x, g, w_gu, w_down)


def _sb_weights(z, cum, carry, mask):
    sp = jnp.log(1.0 + jnp.exp(-jnp.abs(z)))
    log_beta = jnp.minimum(z, 0.0) - sp
    log_1m = log_beta - z
    if mask is not None:
        log_1m = jnp.where(mask, log_1m, 0.0)
    hi = log_1m.astype(BF16)
    lo = (log_1m - hi.astype(F32)).astype(BF16)
    afters = []
    for u in reversed(range(z.shape[0] // T_K)):
        r0, r1 = u * T_K, (u + 1) * T_K
        within = jnp.dot(cum, jnp.concatenate([hi[r0:r1], lo[r0:r1]], axis=0),
                         preferred_element_type=F32)
        afters.append(within + carry)
        carry = carry + within[0:1, :] + log_1m[r0:r0 + 1, :]
    after = afters[0] if len(afters) == 1 else jnp.concatenate(afters[::-1], axis=0)
    w = jnp.exp(log_beta + after)
    if mask is not None:
        w = jnp.where(mask, w, 0.0)
    return w.astype(BF16), carry


def _sb_kernel(qt_ref, k_ref, vt_ref, o_ref):
    seq = k_ref.shape[0]
    first, second = _head_row_masks()
    kk = lax.broadcasted_iota(jnp.int32, (T_K, T_K), 0)
    kk2 = lax.broadcasted_iota(jnp.int32, (T_K, T_K), 1)
    later = jnp.where(kk2 > kk, 1.0, 0.0).astype(BF16)
    cum = jnp.concatenate([later, later], axis=1)
    strict = (lax.broadcasted_iota(jnp.int32, (T_G, T_Q), 0)
              < lax.broadcasted_iota(jnp.int32, (T_G, T_Q), 1))
    heads = range(2)

    def load_q(qi):
        return _split_heads(qt_ref[qi], first, second)

    def logits(g, qts):
        k2 = k_ref[pl.ds(pl.multiple_of(g * T_G, T_G), T_G), :]
        return [jnp.dot(k2, qts[h], preferred_element_type=F32) for h in heads]

    def weights(zs, mask, carries):
        out = [_sb_weights(zs[h], cum, carries[h], mask) for h in heads]
        return [w for w, _ in out], [c for _, c in out]

    def add_pv(g, ws, accs):
        vt = vt_ref[g]
        return [accs[h] + jnp.dot(vt[h * HEAD_DIM:(h + 1) * HEAD_DIM, :], ws[h],
                                  preferred_element_type=F32) for h in heads]

    def tail(qts, g_start, carries, accs):
        def live(st):
            g, carries, _ = st
            return (g >= 0) & (jnp.max(jnp.maximum(carries[0], carries[1])) >= SB_DEAD)

        def body(st):
            g, carries, accs = st
            ws, carries = weights(logits(g, qts), None, list(carries))
            return g - 1, tuple(carries), tuple(add_pv(g, ws, list(accs)))

        _, _, accs = lax.while_loop(live, body, (g_start, tuple(carries), tuple(accs)))
        return list(accs)

    def store(qi, accs):
        q0 = pl.multiple_of(qi * T_Q, T_Q)
        o_ref[pl.ds(q0, T_Q), :] = jnp.concatenate(accs, axis=0).T.astype(BF16)

    def q_pair(qp, has_earlier):
        qa, qb = 2 * qp, 2 * qp + 1
        qts_a, qts_b = load_q(qa), load_q(qb)
        z_a0, z_b0, z_b1 = logits(qa, qts_a), logits(qb, qts_b), logits(qa, qts_b)
        if has_earlier:
            z_a1 = logits(qa - 1, qts_a)
        zeros = [jnp.zeros((1, T_Q), F32) for _ in heads]
        acc0 = [jnp.zeros((HEAD_DIM, T_Q), F32) for _ in heads]
        w_a0, car_a = weights(z_a0, strict, zeros)
        w_b0, car_b = weights(z_b0, strict, zeros)
        acc_a = add_pv(qa, w_a0, acc0)
        acc_b = add_pv(qb, w_b0, acc0)
        w_b1, car_b = weights(z_b1, None, car_b)
        acc_b = add_pv(qa, w_b1, acc_b)
        if has_earlier:
            w_a1, car_a = weights(z_a1, None, car_a)
            acc_a = add_pv(qa - 1, w_a1, acc_a)
            acc_a = tail(qts_a, qa - 2, car_a, acc_a)
            acc_b = tail(qts_b, qa - 1, car_b, acc_b)
        store(qa, acc_a)
        store(qb, acc_b)

    q_pair(0, False)

    def body(qp, _):
        q_pair(qp, True)
        return 0

    lax.fori_loop(1, seq // (2 * T_Q), body, 0)


def _head_block_specs(branch, seq, index):
    def transposed(*ids):
        bi, hb = index(*ids)
        return bi, 0, branch * (W_BRANCH // LANES) + hb, 0

    def rows(*ids):
        bi, hb = index(*ids)
        return bi, 0, branch * (W_BRANCH // LANES) + hb

    t_spec = pl.BlockSpec((None, seq // T_Q, LANES, T_Q), transposed)
    return [t_spec, pl.BlockSpec((None, seq, LANES), rows), t_spec]


def _sb_attn(qt, k, vt):
    b, s, _ = k.shape
    n_hp = W_BRANCH // LANES
    return pl.pallas_call(
        _sb_kernel,
        grid=(b, n_hp),
        in_specs=_head_block_specs(0, s, lambda bi, hp: (bi, hp)),
        out_specs=pl.BlockSpec((None, s, LANES), lambda bi, hp: (bi, 0, hp)),
        out_shape=jax.ShapeDtypeStruct((b, s, W_BRANCH), BF16),
        compiler_params=pltpu.CompilerParams(
            dimension_semantics=("arbitrary", "arbitrary"), vmem_limit_bytes=VMEM_LIMIT),
        name="sb_attn",
    )(qt, k, vt)


def _softmax_update(scores, offsets, vts, state):
    m, l, acc = state
    m_new = m
    for s, c in zip(scores, offsets):
        m_new = jnp.maximum(m_new, jnp.max(s, axis=0, keepdims=True) + c)
    alpha = jnp.exp2(m - m_new)
    l = alpha * l
    acc = alpha * acc
    for s, c, vt in zip(scores, offsets, vts):
        p = jnp.exp2(s + (c - m_new))
        l = l + jnp.sum(p, axis=0, keepdims=True)
        acc = acc + jnp.dot(vt, p.astype(BF16), preferred_element_type=F32)
    return m_new, l, acc


def _diff_kernel(slope_ref, qt_ref, kn_ref, vt_ref, gq_ref, gk_ref, lam_ref, sg_ref, o_ref,
                 bt_ref, raw_ref, sum_ref, *, lam_init):
    seq = kn_ref.shape[0]
    slope = slope_ref[pl.program_id(1)] * LOG2E
    first, second = _head_row_masks()
    lq = lam_ref[...]
    lam = (jnp.exp(jnp.sum(lq[0:1] * lq[1:2], axis=-1, keepdims=True))
           - jnp.exp(jnp.sum(lq[2:3] * lq[3:4], axis=-1, keepdims=True)) + lam_init)
    kpos = lax.broadcasted_iota(jnp.int32, (T_G, T_Q), 0)
    qpos = lax.broadcasted_iota(jnp.int32, (T_G, T_Q), 1)
    rel = (qpos - kpos).astype(F32)
    lin_bias = -slope * rel
    diag_allowed = (kpos // CHUNK) <= (qpos // CHUNK)
    diag_bias = -slope * jnp.abs(rel)
    chains = [(par, mp) for par in range(2) for mp in range(2)]

    def q_pair(qp, _):
        qtm = {}
        for par in range(2):
            qtm[par, 0], qtm[par, 1] = _split_heads(qt_ref[2 * qp + par], first, second)

        def keys(g):
            return kn_ref[pl.ds(pl.multiple_of(g * T_G, T_G), T_G), :]

        def full_scores(kn2, g, chain):
            par, _ = chain
            off = jnp.asarray((2 * qp + par) * T_Q - g * T_G, F32)
            return jnp.dot(kn2, qtm[chain], preferred_element_type=F32) + lin_bias, -slope * off

        def diag_scores(kn2, chain):
            z = jnp.dot(kn2, qtm[chain], preferred_element_type=F32)
            return jnp.where(diag_allowed, z + diag_bias, NEG), 0.0

        def trip(it, states):
            gs = (2 * it, 2 * it + 1)
            kns = [keys(g) for g in gs]
            vts = [vt_ref[g] for g in gs]
            out = []
            scs = [[full_scores(kn2, g, chain) for kn2, g in zip(kns, gs)] for chain in chains]
            for sc, st in zip(scs, states):
                out.append(_softmax_update([s for s, _ in sc], [c for _, c in sc], vts, st))
            return tuple(out)

        init = (jnp.full((1, T_Q), NEG, F32), jnp.zeros((1, T_Q), F32), jnp.zeros((LANES, T_Q), F32))
        states = list(lax.fori_loop(0, qp, trip, (init,) * len(chains)))
        g = 2 * qp
        kns, vts = [keys(g), keys(g + 1)], [vt_ref[g], vt_ref[g + 1]]
        scs = []
        for chain in chains:
            if chain[0] == 0:
                scs.append([diag_scores(kns[0], chain)])
            else:
                scs.append([full_scores(kns[0], g, chain), diag_scores(kns[1], chain)])
        for i, sc in enumerate(scs):
            states[i] = _softmax_update([s for s, _ in sc], [c for _, c in sc], vts[:len(sc)], states[i])
        for par in range(2):
            (_, l0, a0), (_, l1, a1) = states[2 * par], states[2 * par + 1]
            ob = a0 / l0 - lam * (a1 / l1)
            y = ob * lax.rsqrt(jnp.mean(ob * ob, axis=0, keepdims=True) + RMS_EPS)
            q0 = pl.multiple_of((2 * qp + par) * T_Q, T_Q)
            o_ref[pl.ds(q0, T_Q), :] = (y.T * sg_ref[...] * (1.0 - lam_init)).astype(BF16)
        return 0

    n_q = seq // T_Q

    def bounded():
        bt_ref[0] = jnp.where(diag_allowed, diag_bias, NEG)
        for d in range(1, n_q):
            bt_ref[d] = lin_bias - slope * float(d * T_G)
        n_steps = n_q + 1

        def balanced_pair(i, _):
            hi = n_q - 1 - i

            def step_ids(k):
                if k <= n_q // 2:
                    return hi, k, None
                is_hi = k <= hi
                return jnp.where(is_hi, hi, i), jnp.where(is_hi, k, n_q - k), k == hi + 1

            def scores(k):
                tile, g, _ = step_ids(k)
                kn2 = kn_ref[pl.ds(pl.multiple_of(g * T_G, T_G), T_G), :]
                bias = bt_ref[tile - g]
                return [jnp.dot(kn2, qtm, preferred_element_type=F32) + bias
                        for qtm in _split_heads(qt_ref[tile], first, second)]

            accs = [jnp.zeros((LANES, T_Q), F32) for _ in range(2)]
            sums = [jnp.zeros((1, T_Q), F32) for _ in range(2)]
            z_next = scores(0)
            for k in range(n_steps):
                z = z_next
                if k + 1 < n_steps:
                    z_next = scores(k + 1)
                tile, g, restart = step_ids(k)
                vt = vt_ref[g]
                for mp in range(2):
                    p = jnp.exp2(z[mp])
                    psum = jnp.sum(p, axis=0, keepdims=True)
                    pv = jnp.dot(vt, p.astype(BF16), preferred_element_type=F32)
                    if restart is None:
                        sums[mp], accs[mp] = sums[mp] + psum, accs[mp] + pv
                    else:
                        keep = jnp.where(restart, 0.0, 1.0)
                        sums[mp], accs[mp] = sums[mp] * keep + psum, accs[mp] * keep + pv
                    if k >= n_q // 2:
                        raw_ref[tile, mp] = accs[mp]
                        sum_ref[tile, mp] = sums[mp]
            return 0

        lax.fori_loop(0, n_q // 2, balanced_pair, 0)
        for qi in range(n_q):
            ob = (raw_ref[qi, 0] * (1.0 / sum_ref[qi, 0])
                  - lam * (raw_ref[qi, 1] * (1.0 / sum_ref[qi, 1])))
            y = ob * lax.rsqrt(jnp.mean(ob * ob, axis=0, keepdims=True) + RMS_EPS)
            o_ref[qi * T_Q:(qi + 1) * T_Q, :] = (y.T * sg_ref[...] * (1.0 - lam_init)).astype(BF16)

    def general():
        lax.fori_loop(0, seq // (2 * T_Q), q_pair, 0)

    lax.cond(_qk_logit_bound(gq_ref[...], gk_ref[...]) <= SAFE_LOG2, bounded, general)


def _diff_attn(qt, k, vt, slopes, gq, gk, lam_qk, subln_g, layer, lam_init):
    b, s, _ = k.shape
    lay = lambda bi, h: (layer, 0, 0)
    return pl.pallas_call(
        functools.partial(_diff_kernel, lam_init=lam_init),
        grid=(b, H_DIFF),
        in_specs=[
            pl.BlockSpec(memory_space=pltpu.SMEM),
            *_head_block_specs(1, s, lambda bi, h: (bi, h)),
            pl.BlockSpec((None, 1, LANES), lay),
            pl.BlockSpec((None, 1, LANES), lay),
            pl.BlockSpec((None, 4, HEAD_DIM), lay),
            pl.BlockSpec((None, 1, LANES), lay),
        ],
        out_specs=pl.BlockSpec((None, s, LANES), lambda bi, h: (bi, 0, h)),
        out_shape=jax.ShapeDtypeStruct((b, s, W_BRANCH), BF16),
        scratch_shapes=[
            pltpu.VMEM((s // T_Q, T_G, T_Q), F32),
            pltpu.VMEM((s // T_Q, 2, LANES, T_Q), F32),
            pltpu.VMEM((s // T_Q, 2, 1, T_Q), F32),
        ],
        compiler_params=pltpu.CompilerParams(
            dimension_semantics=("arbitrary", "arbitrary"), vmem_limit_bytes=VMEM_LIMIT),
        name="diff_attn",
    )(slopes, qt, k, vt, gq, gk, lam_qk, subln_g)


N_REL_GROUPS = (N_PAST_CHUNKS * CHUNK + T_Q - 1) // T_G + 1


def _chunk_kernel(qt_ref, kn_ref, vt_ref, gq_ref, gk_ref, tab_ref, o_ref, sc_ref):
    seq = kn_ref.shape[0]
    first, second = _head_row_masks()
    deltas = tuple(range(N_REL_GROUPS - 1, -1, -1))

    def score_stage(qi, tile_deltas, slot):
        qts = _split_heads(qt_ref[qi], first, second)
        for dl in tile_deltas:
            k2 = kn_ref[pl.ds(pl.multiple_of((qi - dl) * T_G, T_G), T_G), :]
            for h in range(2):
                sc_ref[slot, h, dl] = (jnp.dot(k2, qts[h], preferred_element_type=F32)
                                       + tab_ref[h, dl])

    def softmax_stage(qi, tile_deltas, slot, fixed_shift):
        outs = []
        for h in range(2):
            scores = [sc_ref[slot, h, dl] for dl in tile_deltas]
            if fixed_shift:
                ps = [jnp.exp2(s) for s in scores]
            else:
                m = functools.reduce(jnp.maximum, [jnp.max(s, axis=0, keepdims=True) for s in scores])
                ps = [jnp.exp2(s - m) for s in scores]
            l = functools.reduce(jnp.add, [jnp.sum(p, axis=0, keepdims=True) for p in ps])
            acc = None
            for dl, p in zip(tile_deltas, ps):
                vth = vt_ref[qi - dl][h * HEAD_DIM:(h + 1) * HEAD_DIM, :]
                pv = jnp.dot(vth, p.astype(BF16), preferred_element_type=F32)
                acc = pv if acc is None else acc + pv
            outs.append(acc / l)
        q0 = pl.multiple_of(qi * T_Q, T_Q)
        o_ref[pl.ds(q0, T_Q), :] = jnp.concatenate(outs, axis=0).T.astype(BF16)

    def tile_deltas(qi):
        return tuple(dl for dl in deltas if qi - dl >= 0)

    n_q = seq // T_Q
    n_static = N_REL_GROUPS - 1 + (n_q - (N_REL_GROUPS - 1)) % 2

    def run(fixed_shift):
        score_stage(0, tile_deltas(0), 0)
        for qi in range(n_static):
            score_stage(qi + 1, tile_deltas(qi + 1), (qi + 1) % 2)
            softmax_stage(qi, tile_deltas(qi), qi % 2, fixed_shift)

        def pair(it, _):
            qi = n_static + 2 * it
            for u in range(2):
                score_stage(jnp.minimum(qi + u + 1, n_q - 1), deltas, (n_static + u + 1) % 2)
                softmax_stage(qi + u, deltas, (n_static + u) % 2, fixed_shift)
            return 0

        lax.fori_loop(0, (n_q - n_static) // 2, pair, 0)

    tabs = tab_ref[...]
    bound = (_qk_logit_bound(gq_ref[...], gk_ref[...])
             + jnp.max(jnp.where(tabs > 0.5 * NEG, jnp.abs(tabs), 0.0)))
    lax.cond(bound <= SAFE_LOG2, lambda: run(True), lambda: run(False))


def _chunk_attn(qt, k, vt, gq, gk, tab, layer):
    b, s, _ = k.shape
    n_hp = W_BRANCH // LANES
    lay = lambda hp, bi: (layer, 0, 0)
    return pl.pallas_call(
        _chunk_kernel,
        grid=(n_hp, b),
        in_specs=[
            *_head_block_specs(2, s, lambda hp, bi: (bi, hp)),
            pl.BlockSpec((None, 1, LANES), lay),
            pl.BlockSpec((None, 1, LANES), lay),
            pl.BlockSpec((None, 2, N_REL_GROUPS, T_G, T_Q), lambda hp, bi: (layer, hp, 0, 0, 0)),
        ],
        out_specs=pl.BlockSpec((None, s, LANES), lambda hp, bi: (bi, 0, hp)),
        out_shape=jax.ShapeDtypeStruct((b, s, W_BRANCH), BF16),
        scratch_shapes=[pltpu.VMEM((2, 2, N_REL_GROUPS, T_G, T_Q), F32)],
        compiler_params=pltpu.CompilerParams(
            dimension_semantics=("arbitrary", "arbitrary"), vmem_limit_bytes=VMEM_LIMIT),
        name="chunk_attn",
    )(qt, k, vt, gq, gk, tab)


def _rel_bias_tiles(rel_bias):
    lead = rel_bias.shape[:-1]
    span = (N_REL_GROUPS - 1) * T_G + T_Q
    period = span + T_G
    edge_lo = jnp.broadcast_to(rel_bias[..., :1], lead + (T_G - REL_CLIP,))
    edge_hi = jnp.broadcast_to(rel_bias[..., -1:], lead + (span - REL_CLIP - 1,))
    row = jnp.concatenate([rel_bias[..., REL_CLIP:], edge_hi, edge_lo, rel_bias[..., :REL_CLIP]], axis=-1)
    flat = jnp.tile(row, (1,) * len(lead) + (T_G,))[..., :T_G * (period - 1)]
    toep = flat.reshape(lead + (T_G, period - 1))
    tiles = jnp.stack([toep[..., d * T_G:d * T_G + T_Q] for d in range(N_REL_GROUPS)], axis=-3)
    kchunk = np.arange(T_G)[:, None] // CHUNK
    qchunk = np.arange(T_Q)[None, :] // CHUNK
    dd = np.stack([qchunk - kchunk + d * (T_G // CHUNK) for d in range(N_REL_GROUPS)])
    return jnp.where((dd >= 0) & (dd <= N_PAST_CHUNKS), tiles * LOG2E, NEG)


def kernel(x, norm_mix_g, w_in, b_gate, qk_g_diff, lambda_qk, subln_g, qk_g_ch, rel_bias,
           w_branch_sb, w_branch_diff, w_branch_ch, w_out, norm_ffn_g, w_gu, w_down):
    b, s, d = x.shape
    m = b * s
    w_qkv = w_in[:, :, :QKV_W].astype(BF16)
    w_gate = w_in[:, :, QKV_W:].astype(BF16)
    w_br = jnp.concatenate([w_branch_sb, w_branch_diff, w_branch_ch], axis=1).astype(BF16)
    w_out_b = w_out.astype(BF16)
    w_gu_b = w_gu.astype(BF16)
    w_down_b = w_down.astype(BF16)
    g_mix = norm_mix_g.reshape(DEPTH, 1, d)
    g_ffn = norm_ffn_g.reshape(DEPTH, 1, d)
    gq_diff = jnp.tile(qk_g_diff[:, 0:1, :], (1, 1, 2))
    gk_diff = jnp.tile(qk_g_diff[:, 1:2, :], (1, 1, 2))
    gq_ch = jnp.tile(qk_g_ch[:, 0:1, :], (1, 1, 2))
    gk_ch = jnp.tile(qk_g_ch[:, 1:2, :], (1, 1, 2))
    sg = subln_g.reshape(DEPTH, 1, 2 * HEAD_DIM)
    tab = _rel_bias_tiles(rel_bias)
    slopes = jnp.asarray([2.0 ** (-8.0 * (i + 1) / H_DIFF) for i in range(H_DIFF)], F32)
    ones = jnp.ones((DEPTH, 1, W_BRANCH), F32)
    widen = lambda g: jnp.tile(g, (1, 1, W_BRANCH // LANES))
    q_scale = QK_SCALE * LOG2E
    qk_gain = jnp.concatenate([ones, ones, ones, widen(gq_diff) * q_scale, widen(gk_diff), ones,
                               widen(gq_ch) * q_scale, widen(gk_ch), ones], axis=-1)

    xf = x.reshape(m, d)
    for layer in range(DEPTH):
        lam_init = 0.8 - 0.6 * math.exp(-0.3 * layer)
        k, qt, vt = _qkv_proj(xf, g_mix, w_qkv, qk_gain, layer)
        k = k.reshape(b, s, N_BRANCH * W_BRANCH)
        qt = qt.reshape(b, s // T_Q, N_BRANCH * W_BRANCH, T_Q)
        vt = vt.reshape(b, s // T_Q, N_BRANCH * W_BRANCH, T_Q)
        o_a = _sb_attn(qt, k, vt)
        o_b = _diff_attn(qt, k, vt, slopes, gq_diff, gk_diff, lambda_qk, sg, layer, lam_init)
        o_c = _chunk_attn(qt, k, vt, gq_ch, gk_ch, tab, layer)
        xf = _merge_out(xf, g_mix, o_a.reshape(m, W_BRANCH), o_b.reshape(m, W_BRANCH),
                        o_c.reshape(m, W_BRANCH), w_gate, b_gate, w_br, w_out_b, layer)
        xf = _ffn(xf, g_ffn, w_gu_b, w_down_b, layer)
    return xf.reshape(b, s, d)
```

```python
import functools
import math

import jax
import jax.numpy as jnp
import numpy as np
from jax import lax
from jax.experimental import pallas as pl
from jax.experimental.pallas import tpu as pltpu

F32 = jnp.float32
BF16 = jnp.bfloat16

D_MODEL = 1024
DEPTH = 4
CHUNK = 64
HEAD_DIM = 64
H_DIFF = 4
N_PAST_CHUNKS = 8
REL_CLIP = 128
W_BRANCH = 512
QKV_W = 9 * W_BRANCH
N_BRANCH = 3
D_FF = int(math.ceil(8 * D_MODEL / 3 / 256)) * 256
RMS_EPS = 1e-6
QK_SCALE = HEAD_DIM ** -0.5

LANES = 128
T_Q = 256
T_K = 128
T_G = 256
NEG = -1e30
SB_DEAD = -150.0
SB_TILES_PER_BLOCK = 4
LOG2E = 1.4426950408889634
SAFE_LOG2 = 60.0
VMEM_LIMIT = 56 * 1024 * 1024

ROW_TILE = 512
FF_CHUNK = 256


def _rms(x, g):
    return x * lax.rsqrt(jnp.mean(x * x, axis=-1, keepdims=True) + RMS_EPS) * g


def _rms_halves(x, g):
    lane = lax.broadcasted_iota(jnp.int32, (1, LANES), 1)
    first = lane < HEAD_DIM
    x2 = x * x
    s0 = jnp.sum(jnp.where(first, x2, 0.0), axis=-1, keepdims=True)
    s1 = jnp.sum(jnp.where(first, 0.0, x2), axis=-1, keepdims=True)
    ms = jnp.where(first, s0, s1) * (1.0 / HEAD_DIM)
    return x * lax.rsqrt(ms + RMS_EPS) * g


def _qk_logit_bound(gq, gk):
    return (1.02 * LOG2E * QK_SCALE * HEAD_DIM) * jnp.max(jnp.abs(gq)) * jnp.max(jnp.abs(gk))


def _head_row_masks():
    row = lax.broadcasted_iota(jnp.int32, (LANES, 1), 0)
    return row < HEAD_DIM, row >= HEAD_DIM


def _split_heads(qt, first, second):
    zero = jnp.zeros_like(qt)
    return jnp.where(first, qt, zero), jnp.where(second, qt, zero)


N_SECTIONS = 9


def _qkv_kernel(x_ref, g_ref, w_ref, qkg_ref, k_out, qt_out, vt_out):
    h = _rms(x_ref[...], g_ref[...]).astype(BF16)

    def project(sec):
        return jnp.dot(h, w_ref[:, sec * W_BRANCH:(sec + 1) * W_BRANCH], preferred_element_type=F32)

    z_next = project(0)
    for sec in range(N_SECTIONS):
        z = z_next
        if sec + 1 < N_SECTIONS:
            z_next = project(sec + 1)
        branch, role = divmod(sec, 3)
        for cb in range(W_BRANCH // LANES):
            blk = z[:, cb * LANES:(cb + 1) * LANES]
            col = sec * W_BRANCH + cb * LANES
            if branch > 0 and role < 2:
                blk = _rms_halves(blk, qkg_ref[:, col:col + LANES])
            elif role == 0:
                blk = blk * (QK_SCALE * LOG2E)
            out_col = branch * W_BRANCH + cb * LANES
            if role == 1:
                k_out[:, out_col:out_col + LANES] = blk.astype(BF16)
            else:
                out = qt_out if role == 0 else vt_out
                for r in range(ROW_TILE // T_Q):
                    out[r, out_col:out_col + LANES, :] = blk[r * T_Q:(r + 1) * T_Q, :].T.astype(BF16)


def _qkv_proj(x, g, w, qk_gain, layer):
    m = x.shape[0]
    width = N_BRANCH * W_BRANCH
    tiles = ROW_TILE // T_Q
    lay = lambda i: (layer, 0, 0)
    transposed = jax.ShapeDtypeStruct((m // T_Q, width, T_Q), BF16)
    return pl.pallas_call(
        _qkv_kernel,
        grid=(m // ROW_TILE,),
        in_specs=[
            pl.BlockSpec((ROW_TILE, D_MODEL), lambda i: (i, 0)),
            pl.BlockSpec((None, 1, D_MODEL), lay),
            pl.BlockSpec((None, D_MODEL, QKV_W), lay),
            pl.BlockSpec((None, 1, QKV_W), lay),
        ],
        out_specs=[
            pl.BlockSpec((ROW_TILE, width), lambda i: (i, 0)),
            pl.BlockSpec((tiles, width, T_Q), lambda i: (i, 0, 0)),
            pl.BlockSpec((tiles, width, T_Q), lambda i: (i, 0, 0)),
        ],
        out_shape=[jax.ShapeDtypeStruct((m, width), BF16), transposed, transposed],
        compiler_params=pltpu.CompilerParams(
            dimension_semantics=("arbitrary",), vmem_limit_bytes=VMEM_LIMIT),
        name="qkv_proj",
    )(x, g, w, qk_gain)


def _merge_kernel(x_ref, g_ref, oa_ref, ob_ref, oc_ref, wg_ref, bg_ref, wbr_ref, wo_ref, out_ref):
    x = x_ref[...]
    h = _rms(x, g_ref[...]).astype(BF16)
    merged = None
    for br, o_ref in enumerate((oa_ref, ob_ref, oc_ref)):
        g_lin = jnp.dot(h, wg_ref[:, br * D_MODEL:(br + 1) * D_MODEL], preferred_element_type=F32)
        gate = 1.0 / (1.0 + jnp.exp(-(g_lin + bg_ref[br:br + 1, :])))
        proj = jnp.dot(o_ref[...], wbr_ref[br * W_BRANCH:(br + 1) * W_BRANCH, :],
                       preferred_element_type=F32)
        term = gate * proj
        merged = term if merged is None else merged + term
    out_ref[...] = x + jnp.dot(merged.astype(BF16), wo_ref[...], preferred_element_type=F32)


def _merge_out(x, g, o_a, o_b, o_c, w_gate, b_gate, w_br, w_out, layer):
    m = x.shape[0]
    row = lambda i: (i, 0)
    lay = lambda i: (layer, 0, 0)
    return pl.pallas_call(
        _merge_kernel,
        grid=(m // ROW_TILE,),
        in_specs=[
            pl.BlockSpec((ROW_TILE, D_MODEL), row),
            pl.BlockSpec((None, 1, D_MODEL), lay),
            pl.BlockSpec((ROW_TILE, W_BRANCH), row),
            pl.BlockSpec((ROW_TILE, W_BRANCH), row),
            pl.BlockSpec((ROW_TILE, W_BRANCH), row),
            pl.BlockSpec((None, D_MODEL, N_BRANCH * D_MODEL), lay),
            pl.BlockSpec((None, N_BRANCH, D_MODEL), lay),
            pl.BlockSpec((None, N_BRANCH * W_BRANCH, D_MODEL), lay),
            pl.BlockSpec((None, D_MODEL, D_MODEL), lay),
        ],
        out_specs=pl.BlockSpec((ROW_TILE, D_MODEL), row),
        out_shape=jax.ShapeDtypeStruct((m, D_MODEL), F32),
        compiler_params=pltpu.CompilerParams(
            dimension_semantics=("arbitrary",), vmem_limit_bytes=VMEM_LIMIT),
        name="merge_out",
    )(x, g, o_a, o_b, o_c, w_gate, b_gate, w_br, w_out)


def _ffn_kernel(x_ref, g_ref, wgu_ref, wd_ref, out_ref, act_ref):
    x = x_ref[...]
    h = _rms(x, g_ref[...]).astype(BF16)
    for c in range(0, D_FF, FF_CHUNK):
        gate = jnp.dot(h, wgu_ref[:, c:c + FF_CHUNK], preferred_element_type=F32)
        up = jnp.dot(h, wgu_ref[:, D_FF + c:D_FF + c + FF_CHUNK], preferred_element_type=F32)
        silu = gate / (1.0 + jnp.exp(-gate))
        act_ref[:, c:c + FF_CHUNK] = (silu * up).astype(BF16)
    out_ref[...] = x + jnp.dot(act_ref[...], wd_ref[...], preferred_element_type=F32)


def _ffn(x, g, w_gu, w_down, layer):
    m = x.shape[0]
    row = lambda i: (i, 0)
    lay = lambda i: (layer, 0, 0)
    return pl.pallas_call(
        _ffn_kernel,
        grid=(m // ROW_TILE,),
        in_specs=[
            pl.BlockSpec((ROW_TILE, D_MODEL), row),
            pl.BlockSpec((None, 1, D_MODEL), lay),
            pl.BlockSpec((None, D_MODEL, 2 * D_FF), lay, pipeline_mode=pl.Buffered(1)),
            pl.BlockSpec((None, D_FF, D_MODEL), lay, pipeline_mode=pl.Buffered(1)),
        ],
        out_specs=pl.BlockSpec((ROW_TILE, D_MODEL), row),
        out_shape=jax.ShapeDtypeStruct((m, D_MODEL), F32),
        scratch_shapes=[pltpu.VMEM((ROW_TILE, D_FF), BF16)],
        compiler_params=pltpu.CompilerParams(
            dimension_semantics=("arbitrary",), vmem_limit_bytes=VMEM_LIMIT),
        name="ffn",
    )(x, g, w_gu, w_down)


def _sb_weights(z, cum, carry, mask):
    neg_abs = lax.bitcast_convert_type(
        lax.bitcast_convert_type(z, jnp.uint32) | jnp.uint32(0x80000000), F32)
    sp = jnp.log2(1.0 + jnp.exp2(neg_abs))
    log_beta = jnp.minimum(z, 0.0) - sp
    log_1m = log_beta - z
    if mask is not None:
        log_1m = jnp.where(mask, log_1m, 0.0)
    hi = log_1m.astype(BF16)
    lo = (log_1m - hi.astype(F32)).astype(BF16)
    afters = []
    for u in reversed(range(z.shape[0] // T_K)):
        r0, r1 = u * T_K, (u + 1) * T_K
        within = jnp.dot(cum, jnp.concatenate([hi[r0:r1], lo[r0:r1]], axis=0),
                         preferred_element_type=F32)
        afters.append(within + carry)
        carry = carry + within[0:1, :] + log_1m[r0:r0 + 1, :]
    after = afters[0] if len(afters) == 1 else jnp.concatenate(afters[::-1], axis=0)
    w = jnp.exp2(log_beta + after)
    if mask is not None:
        w = jnp.where(mask, w, 0.0)
    return w.astype(BF16), carry


def _sb_kernel(qt_ref, k_ref, vt_ref, o_ref):
    seq = k_ref.shape[0]
    first, second = _head_row_masks()
    kk = lax.broadcasted_iota(jnp.int32, (T_K, T_K), 0)
    kk2 = lax.broadcasted_iota(jnp.int32, (T_K, T_K), 1)
    later = jnp.where(kk2 > kk, 1.0, 0.0).astype(BF16)
    cum = jnp.concatenate([later, later], axis=1)
    strict = (lax.broadcasted_iota(jnp.int32, (T_G, T_Q), 0)
              < lax.broadcasted_iota(jnp.int32, (T_G, T_Q), 1))
    heads = range(2)

    def load_q(qi):
        return _split_heads(qt_ref[qi], first, second)

    def logits(g, qts):
        k2 = k_ref[pl.ds(pl.multiple_of(g * T_G, T_G), T_G), :]
        return [jnp.dot(k2, qts[h], preferred_element_type=F32) for h in heads]

    def weights(zs, mask, carries):
        out = [_sb_weights(zs[h], cum, carries[h], mask) for h in heads]
        return [w for w, _ in out], [c for _, c in out]

    def add_pv(g, ws, accs):
        vt = vt_ref[g]
        return [accs[h] + jnp.dot(vt[h * HEAD_DIM:(h + 1) * HEAD_DIM, :], ws[h],
                                  preferred_element_type=F32) for h in heads]

    def tail(qts, g_start, carries, accs):
        def live(st):
            g, carries, _ = st
            return (g >= 0) & (jnp.max(jnp.maximum(carries[0], carries[1])) >= SB_DEAD)

        def body(st):
            g, carries, accs = st
            ws, carries = weights(logits(g, qts), None, list(carries))
            return g - 1, tuple(carries), tuple(add_pv(g, ws, list(accs)))

        _, _, accs = lax.while_loop(live, body, (g_start, tuple(carries), tuple(accs)))
        return list(accs)

    def store(qi, accs):
        q0 = pl.multiple_of(qi * T_Q, T_Q)
        o_ref[pl.ds(q0, T_Q), :] = jnp.concatenate(accs, axis=0).T.astype(BF16)

    def q_tiles(tiles):
        qts = {qi: load_q(qi) for qi in tiles}
        groups = {qi: [g for g in (qi, qi - 1) if g >= 0] for qi in tiles}
        zs = {(qi, g): logits(g, qts[qi]) for qi in tiles for g in groups[qi]}
        state = {}
        for qi in tiles:
            carries = [jnp.zeros((1, T_Q), F32) for _ in heads]
            accs = [jnp.zeros((HEAD_DIM, T_Q), F32) for _ in heads]
            for g in groups[qi]:
                ws, carries = weights(zs[qi, g], strict if g == qi else None, carries)
                accs = add_pv(g, ws, accs)
            state[qi] = carries, accs
        for qi in tiles:
            carries, accs = state[qi]
            if qi >= 2:
                accs = tail(qts[qi], qi - 2, carries, accs)
            store(qi, accs)

    n_q = seq // T_Q
    for first_tile in range(0, n_q, SB_TILES_PER_BLOCK):
        q_tiles(range(first_tile, min(first_tile + SB_TILES_PER_BLOCK, n_q)))


def _head_block_specs(branch, seq, index):
    def transposed(*ids):
        bi, hb = index(*ids)
        return bi, 0, branch * (W_BRANCH // LANES) + hb, 0

    def rows(*ids):
        bi, hb = index(*ids)
        return bi, 0, branch * (W_BRANCH // LANES) + hb

    t_spec = pl.BlockSpec((None, seq // T_Q, LANES, T_Q), transposed)
    return [t_spec, pl.BlockSpec((None, seq, LANES), rows), t_spec]


def _sb_attn(qt, k, vt):
    b, s, _ = k.shape
    n_hp = W_BRANCH // LANES
    return pl.pallas_call(
        _sb_kernel,
        grid=(b, n_hp),
        in_specs=_head_block_specs(0, s, lambda bi, hp: (bi, hp)),
        out_specs=pl.BlockSpec((None, s, LANES), lambda bi, hp: (bi, 0, hp)),
        out_shape=jax.ShapeDtypeStruct((b, s, W_BRANCH), BF16),
        compiler_params=pltpu.CompilerParams(
            dimension_semantics=("arbitrary", "arbitrary"), vmem_limit_bytes=VMEM_LIMIT),
        name="sb_attn",
    )(qt, k, vt)


def _softmax_update(scores, offsets, vts, state):
    m, l, acc = state
    m_new = m
    for s, c in zip(scores, offsets):
        m_new = jnp.maximum(m_new, jnp.max(s, axis=0, keepdims=True) + c)
    alpha = jnp.exp2(m - m_new)
    l = alpha * l
    acc = alpha * acc
    for s, c, vt in zip(scores, offsets, vts):
        p = jnp.exp2(s + (c - m_new))
        l = l + jnp.sum(p, axis=0, keepdims=True)
        acc = acc + jnp.dot(vt, p.astype(BF16), preferred_element_type=F32)
    return m_new, l, acc


def _diff_kernel(slope_ref, qt_ref, kn_ref, vt_ref, gq_ref, gk_ref, lam_ref, sg_ref, o_ref,
                 bt_ref, raw_ref, sum_ref, *, lam_init):
    seq = kn_ref.shape[0]
    slope = slope_ref[pl.program_id(1)] * LOG2E
    first, second = _head_row_masks()
    lq = lam_ref[...]
    lam = (jnp.exp(jnp.sum(lq[0:1] * lq[1:2], axis=-1, keepdims=True))
           - jnp.exp(jnp.sum(lq[2:3] * lq[3:4], axis=-1, keepdims=True)) + lam_init)
    kpos = lax.broadcasted_iota(jnp.int32, (T_G, T_Q), 0)
    qpos = lax.broadcasted_iota(jnp.int32, (T_G, T_Q), 1)
    rel = (qpos - kpos).astype(F32)
    lin_bias = -slope * rel
    diag_allowed = (kpos // CHUNK) <= (qpos // CHUNK)
    diag_bias = -slope * jnp.abs(rel)
    chains = [(par, mp) for par in range(2) for mp in range(2)]

    def q_pair(qp, _):
        qtm = {}
        for par in range(2):
            qtm[par, 0], qtm[par, 1] = _split_heads(qt_ref[2 * qp + par], first, second)

        def keys(g):
            return kn_ref[pl.ds(pl.multiple_of(g * T_G, T_G), T_G), :]

        def full_scores(kn2, g, chain):
            par, _ = chain
            off = jnp.asarray((2 * qp + par) * T_Q - g * T_G, F32)
            return jnp.dot(kn2, qtm[chain], preferred_element_type=F32) + lin_bias, -slope * off

        def diag_scores(kn2, chain):
            z = jnp.dot(kn2, qtm[chain], preferred_element_type=F32)
            return jnp.where(diag_allowed, z + diag_bias, NEG), 0.0

        def trip(it, states):
            gs = (2 * it, 2 * it + 1)
            kns = [keys(g) for g in gs]
            vts = [vt_ref[g] for g in gs]
            out = []
            scs = [[full_scores(kn2, g, chain) for kn2, g in zip(kns, gs)] for chain in chains]
            for sc, st in zip(scs, states):
                out.append(_softmax_update([s for s, _ in sc], [c for _, c in sc], vts, st))
            return tuple(out)

        init = (jnp.full((1, T_Q), NEG, F32), jnp.zeros((1, T_Q), F32), jnp.zeros((LANES, T_Q), F32))
        states = list(lax.fori_loop(0, qp, trip, (init,) * len(chains)))
        g = 2 * qp
        kns, vts = [keys(g), keys(g + 1)], [vt_ref[g], vt_ref[g + 1]]
        scs = []
        for chain in chains:
            if chain[0] == 0:
                scs.append([diag_scores(kns[0], chain)])
            else:
                scs.append([full_scores(kns[0], g, chain), diag_scores(kns[1], chain)])
        for i, sc in enumerate(scs):
            states[i] = _softmax_update([s for s, _ in sc], [c for _, c in sc], vts[:len(sc)], states[i])
        for par in range(2):
            (_, l0, a0), (_, l1, a1) = states[2 * par], states[2 * par + 1]
            ob = a0 / l0 - lam * (a1 / l1)
            y = ob * lax.rsqrt(jnp.mean(ob * ob, axis=0, keepdims=True) + RMS_EPS)
            q0 = pl.multiple_of((2 * qp + par) * T_Q, T_Q)
            o_ref[pl.ds(q0, T_Q), :] = (y.T * sg_ref[...] * (1.0 - lam_init)).astype(BF16)
        return 0

    n_q = seq // T_Q

    def bounded():
        bt_ref[0] = jnp.where(diag_allowed, diag_bias, NEG)
        for d in range(1, n_q):
            bt_ref[d] = lin_bias - slope * float(d * T_G)
        n_steps = n_q + 1

        def balanced_pair(i, _):
            hi = n_q - 1 - i

            def step_ids(k):
                if k <= n_q // 2:
                    return hi, k, None
                is_hi = k <= hi
                return jnp.where(is_hi, hi, i), jnp.where(is_hi, k, n_q - k), k == hi + 1

            def scores(k):
                tile, g, _ = step_ids(k)
                kn2 = kn_ref[pl.ds(pl.multiple_of(g * T_G, T_G), T_G), :]
                bias = bt_ref[tile - g]
                return [jnp.dot(kn2, qtm, preferred_element_type=F32) + bias
                        for qtm in _split_heads(qt_ref[tile], first, second)]

            accs = [jnp.zeros((LANES, T_Q), F32) for _ in range(2)]
            sums = [jnp.zeros((1, T_Q), F32) for _ in range(2)]
            z_next = scores(0)
            for k in range(n_steps):
                z = z_next
                if k + 1 < n_steps:
                    z_next = scores(k + 1)
                tile, g, restart = step_ids(k)
                vt = vt_ref[g]
                for mp in range(2):
                    p = jnp.exp2(z[mp])
                    psum = jnp.sum(p, axis=0, keepdims=True)
                    pv = jnp.dot(vt, p.astype(BF16), preferred_element_type=F32)
                    if restart is None:
                        sums[mp], accs[mp] = sums[mp] + psum, accs[mp] + pv
                    else:
                        keep = jnp.where(restart, 0.0, 1.0)
                        sums[mp], accs[mp] = sums[mp] * keep + psum, accs[mp] * keep + pv
                    if k >= n_q // 2:
                        raw_ref[tile, mp] = accs[mp]
                        sum_ref[tile, mp] = sums[mp]
            return 0

        lax.fori_loop(0, n_q // 2, balanced_pair, 0)
        for qi in range(n_q):
            ob = (raw_ref[qi, 0] * (1.0 / sum_ref[qi, 0])
                  - lam * (raw_ref[qi, 1] * (1.0 / sum_ref[qi, 1])))
            y = ob * lax.rsqrt(jnp.mean(ob * ob, axis=0, keepdims=True) + RMS_EPS)
            o_ref[qi * T_Q:(qi + 1) * T_Q, :] = (y.T * sg_ref[...] * (1.0 - lam_init)).astype(BF16)

    def general():
        lax.fori_loop(0, seq // (2 * T_Q), q_pair, 0)

    lax.cond(_qk_logit_bound(gq_ref[...], gk_ref[...]) <= SAFE_LOG2, bounded, general)


def _diff_attn(qt, k, vt, slopes, gq, gk, lam_qk, subln_g, layer, lam_init):
    b, s, _ = k.shape
    lay = lambda bi, h: (layer, 0, 0)
    return pl.pallas_call(
        functools.partial(_diff_kernel, lam_init=lam_init),
        grid=(b, H_DIFF),
        in_specs=[
            pl.BlockSpec(memory_space=pltpu.SMEM),
            *_head_block_specs(1, s, lambda bi, h: (bi, h)),
            pl.BlockSpec((None, 1, LANES), lay),
            pl.BlockSpec((None, 1, LANES), lay),
            pl.BlockSpec((None, 4, HEAD_DIM), lay),
            pl.BlockSpec((None, 1, LANES), lay),
        ],
        out_specs=pl.BlockSpec((None, s, LANES), lambda bi, h: (bi, 0, h)),
        out_shape=jax.ShapeDtypeStruct((b, s, W_BRANCH), BF16),
        scratch_shapes=[
            pltpu.VMEM((s // T_Q, T_G, T_Q), F32),
            pltpu.VMEM((s // T_Q, 2, LANES, T_Q), F32),
            pltpu.VMEM((s // T_Q, 2, 1, T_Q), F32),
        ],
        compiler_params=pltpu.CompilerParams(
            dimension_semantics=("arbitrary", "arbitrary"), vmem_limit_bytes=VMEM_LIMIT),
        name="diff_attn",
    )(slopes, qt, k, vt, gq, gk, lam_qk, subln_g)


N_REL_GROUPS = (N_PAST_CHUNKS * CHUNK + T_Q - 1) // T_G + 1


def _chunk_kernel(qt_ref, kn_ref, vt_ref, gq_ref, gk_ref, tab_ref, o_ref, sc_ref):
    seq = kn_ref.shape[0]
    first, second = _head_row_masks()
    deltas = tuple(range(N_REL_GROUPS - 1, -1, -1))

    def score_stage(qi, tile_deltas, slot):
        qts = _split_heads(qt_ref[qi], first, second)
        for dl in tile_deltas:
            k2 = kn_ref[pl.ds(pl.multiple_of((qi - dl) * T_G, T_G), T_G), :]
            for h in range(2):
                sc_ref[slot, h, dl] = (jnp.dot(k2, qts[h], preferred_element_type=F32)
                                       + tab_ref[h, dl])

    def softmax_stage(qi, tile_deltas, slot, fixed_shift):
        outs = []
        for h in range(2):
            scores = [sc_ref[slot, h, dl] for dl in tile_deltas]
            if fixed_shift:
                ps = [jnp.exp2(s) for s in scores]
            else:
                m = functools.reduce(jnp.maximum, [jnp.max(s, axis=0, keepdims=True) for s in scores])
                ps = [jnp.exp2(s - m) for s in scores]
            l = functools.reduce(jnp.add, [jnp.sum(p, axis=0, keepdims=True) for p in ps])
            acc = None
            for dl, p in zip(tile_deltas, ps):
                vth = vt_ref[qi - dl][h * HEAD_DIM:(h + 1) * HEAD_DIM, :]
                pv = jnp.dot(vth, p.astype(BF16), preferred_element_type=F32)
                acc = pv if acc is None else acc + pv
            outs.append(acc / l)
        q0 = pl.multiple_of(qi * T_Q, T_Q)
        o_ref[pl.ds(q0, T_Q), :] = jnp.concatenate(outs, axis=0).T.astype(BF16)

    def tile_deltas(qi):
        return tuple(dl for dl in deltas if qi - dl >= 0)

    n_q = seq // T_Q
    n_static = N_REL_GROUPS - 1 + (n_q - (N_REL_GROUPS - 1)) % 2

    def run(fixed_shift):
        score_stage(0, tile_deltas(0), 0)
        for qi in range(n_static):
            score_stage(qi + 1, tile_deltas(qi + 1), (qi + 1) % 2)
            softmax_stage(qi, tile_deltas(qi), qi % 2, fixed_shift)

        def pair(it, _):
            qi = n_static + 2 * it
            for u in range(2):
                score_stage(jnp.minimum(qi + u + 1, n_q - 1), deltas, (n_static + u + 1) % 2)
                softmax_stage(qi + u, deltas, (n_static + u) % 2, fixed_shift)
            return 0

        lax.fori_loop(0, (n_q - n_static) // 2, pair, 0)

    tabs = tab_ref[...]
    bound = (_qk_logit_bound(gq_ref[...], gk_ref[...])
             + jnp.max(jnp.where(tabs > 0.5 * NEG, jnp.abs(tabs), 0.0)))
    lax.cond(bound <= SAFE_LOG2, lambda: run(True), lambda: run(False))


def _chunk_attn(qt, k, vt, gq, gk, tab, layer):
    b, s, _ = k.shape
    n_hp = W_BRANCH // LANES
    lay = lambda hp, bi: (layer, 0, 0)
    return pl.pallas_call(
        _chunk_kernel,
        grid=(n_hp, b),
        in_specs=[
            *_head_block_specs(2, s, lambda hp, bi: (bi, hp)),
            pl.BlockSpec((None, 1, LANES), lay),
            pl.BlockSpec((None, 1, LANES), lay),
            pl.BlockSpec((None, 2, N_REL_GROUPS, T_G, T_Q), lambda hp, bi: (layer, hp, 0, 0, 0)),
        ],
        out_specs=pl.BlockSpec((None, s, LANES), lambda hp, bi: (bi, 0, hp)),
        out_shape=jax.ShapeDtypeStruct((b, s, W_BRANCH), BF16),
        scratch_shapes=[pltpu.VMEM((2, 2, N_REL_GROUPS, T_G, T_Q), F32)],
        compiler_params=pltpu.CompilerParams(
            dimension_semantics=("arbitrary", "arbitrary"), vmem_limit_bytes=VMEM_LIMIT),
        name="chunk_attn",
    )(qt, k, vt, gq, gk, tab)


def _rel_bias_tiles(rel_bias):
    lead = rel_bias.shape[:-1]
    span = (N_REL_GROUPS - 1) * T_G + T_Q
    period = span + T_G
    edge_lo = jnp.broadcast_to(rel_bias[..., :1], lead + (T_G - REL_CLIP,))
    edge_hi = jnp.broadcast_to(rel_bias[..., -1:], lead + (span - REL_CLIP - 1,))
    row = jnp.concatenate([rel_bias[..., REL_CLIP:], edge_hi, edge_lo, rel_bias[..., :REL_CLIP]], axis=-1)
    flat = jnp.tile(row, (1,) * len(lead) + (T_G,))[..., :T_G * (period - 1)]
    toep = flat.reshape(lead + (T_G, period - 1))
    tiles = jnp.stack([toep[..., d * T_G:d * T_G + T_Q] for d in range(N_REL_GROUPS)], axis=-3)
    kchunk = np.arange(T_G)[:, None] // CHUNK
    qchunk = np.arange(T_Q)[None, :] // CHUNK
    dd = np.stack([qchunk - kchunk + d * (T_G // CHUNK) for d in range(N_REL_GROUPS)])
    return jnp.where((dd >= 0) & (dd <= N_PAST_CHUNKS), tiles * LOG2E, NEG)


def kernel(x, norm_mix_g, w_in, b_gate, qk_g_diff, lambda_qk, subln_g, qk_g_ch, rel_bias,
           w_branch_sb, w_branch_diff, w_branch_ch, w_out, norm_ffn_g, w_gu, w_down):
    b, s, d = x.shape
    m = b * s
    w_qkv = w_in[:, :, :QKV_W].astype(BF16)
    w_gate = w_in[:, :, QKV_W:].astype(BF16)
    w_br = jnp.concatenate([w_branch_sb, w_branch_diff, w_branch_ch], axis=1).astype(BF16)
    w_out_b = w_out.astype(BF16)
    w_gu_b = w_gu.astype(BF16)
    w_down_b = w_down.astype(BF16)
    g_mix = norm_mix_g.reshape(DEPTH, 1, d)
    g_ffn = norm_ffn_g.reshape(DEPTH, 1, d)
    gq_diff = jnp.tile(qk_g_diff[:, 0:1, :], (1, 1, 2))
    gk_diff = jnp.tile(qk_g_diff[:, 1:2, :], (1, 1, 2))
    gq_ch = jnp.tile(qk_g_ch[:, 0:1, :], (1, 1, 2))
    gk_ch = jnp.tile(qk_g_ch[:, 1:2, :], (1, 1, 2))
    sg = subln_g.reshape(DEPTH, 1, 2 * HEAD_DIM)
    tab = _rel_bias_tiles(rel_bias)
    slopes = jnp.asarray([2.0 ** (-8.0 * (i + 1) / H_DIFF) for i in range(H_DIFF)], F32)
    ones = jnp.ones((DEPTH, 1, W_BRANCH), F32)
    widen = lambda g: jnp.tile(g, (1, 1, W_BRANCH // LANES))
    q_scale = QK_SCALE * LOG2E
    qk_gain = jnp.concatenate([ones, ones, ones, widen(gq_diff) * q_scale, widen(gk_diff), ones,
                               widen(gq_ch) * q_scale, widen(gk_ch), ones], axis=-1)

    xf = x.reshape(m, d)
    for layer in range(DEPTH):
        lam_init = 0.8 - 0.6 * math.exp(-0.3 * layer)
        k, qt, vt = _qkv_proj(xf, g_mix, w_qkv, qk_gain, layer)
        k = k.reshape(b, s, N_BRANCH * W_BRANCH)
        qt = qt.reshape(b, s // T_Q, N_BRANCH * W_BRANCH, T_Q)
        vt = vt.reshape(b, s // T_Q, N_BRANCH * W_BRANCH, T_Q)
        o_a = _sb_attn(qt, k, vt)
        o_b = _diff_attn(qt, k, vt, slopes, gq_diff, gk_diff, lambda_qk, sg, layer, lam_init)
        o_c = _chunk_attn(qt, k, vt, gq_ch, gk_ch, tab, layer)
        xf = _merge_out(xf, g_mix, o_a.reshape(m, W_BRANCH), o_b.reshape(m, W_BRANCH),
                        o_c.reshape(m, W_BRANCH), w_gate, b_gate, w_br, w_out_b, layer)
        xf = _ffn(xf, g_ffn, w_gu_b, w_down_b, layer)
    return xf.reshape(b, s, d)
```

```python
import functools
import math

import jax
import jax.numpy as jnp
import numpy as np
from jax import lax
from jax.experimental import pallas as pl
from jax.experimental.pallas import tpu as pltpu

F32 = jnp.float32
BF16 = jnp.bfloat16

D_MODEL = 1024
DEPTH = 4
CHUNK = 64
HEAD_DIM = 64
H_DIFF = 4
N_PAST_CHUNKS = 8
REL_CLIP = 128
W_BRANCH = 512
QKV_W = 9 * W_BRANCH
N_BRANCH = 3
D_FF = int(math.ceil(8 * D_MODEL / 3 / 256)) * 256
RMS_EPS = 1e-6
QK_SCALE = HEAD_DIM ** -0.5

LANES = 128
T_Q = 256
T_K = 128
T_G = 256
NEG = -1e30
SB_DEAD = -150.0
SB_TILES_PER_BLOCK = 4
LOG2E = 1.4426950408889634
SAFE_LOG2 = 60.0
VMEM_LIMIT = 56 * 1024 * 1024

ROW_TILE = 1024
FF_CHUNK = 256


def _rms(x, g):
    return x * lax.rsqrt(jnp.mean(x * x, axis=-1, keepdims=True) + RMS_EPS) * g


def _rms_halves(x, g):
    lane = lax.broadcasted_iota(jnp.int32, (1, LANES), 1)
    first = lane < HEAD_DIM
    x2 = x * x
    s0 = jnp.sum(jnp.where(first, x2, 0.0), axis=-1, keepdims=True)
    s1 = jnp.sum(jnp.where(first, 0.0, x2), axis=-1, keepdims=True)
    ms = jnp.where(first, s0, s1) * (1.0 / HEAD_DIM)
    return x * lax.rsqrt(ms + RMS_EPS) * g


def _qk_logit_bound(gq, gk):
    return (1.02 * LOG2E * QK_SCALE * HEAD_DIM) * jnp.max(jnp.abs(gq)) * jnp.max(jnp.abs(gk))


def _head_row_masks():
    row = lax.broadcasted_iota(jnp.int32, (LANES, 1), 0)
    return row < HEAD_DIM, row >= HEAD_DIM


def _split_heads(qt, first, second):
    zero = jnp.zeros_like(qt)
    return jnp.where(first, qt, zero), jnp.where(second, qt, zero)


N_SECTIONS = 9


def _qkv_kernel(x_ref, g_ref, w_ref, qkg_ref, k_out, qt_out, vt_out):
    h = _rms(x_ref[...], g_ref[...]).astype(BF16)

    def project(sec):
        return jnp.dot(h, w_ref[:, sec * W_BRANCH:(sec + 1) * W_BRANCH], preferred_element_type=F32)

    z_next = project(0)
    for sec in range(N_SECTIONS):
        z = z_next
        if sec + 1 < N_SECTIONS:
            z_next = project(sec + 1)
        branch, role = divmod(sec, 3)
        for cb in range(W_BRANCH // LANES):
            blk = z[:, cb * LANES:(cb + 1) * LANES]
            col = sec * W_BRANCH + cb * LANES
            if branch > 0 and role < 2:
                blk = _rms_halves(blk, qkg_ref[:, col:col + LANES])
            elif role == 0:
                blk = blk * (QK_SCALE * LOG2E)
            out_col = branch * W_BRANCH + cb * LANES
            if role == 1:
                k_out[:, out_col:out_col + LANES] = blk.astype(BF16)
            else:
                out = qt_out if role == 0 else vt_out
                for r in range(ROW_TILE // T_Q):
                    out[r, out_col:out_col + LANES, :] = blk[r * T_Q:(r + 1) * T_Q, :].T.astype(BF16)


def _qkv_proj(x, g, w, qk_gain, layer):
    m = x.shape[0]
    width = N_BRANCH * W_BRANCH
    tiles = ROW_TILE // T_Q
    lay = lambda i: (layer, 0, 0)
    transposed = jax.ShapeDtypeStruct((m // T_Q, width, T_Q), BF16)
    return pl.pallas_call(
        _qkv_kernel,
        grid=(m // ROW_TILE,),
        in_specs=[
            pl.BlockSpec((ROW_TILE, D_MODEL), lambda i: (i, 0)),
            pl.BlockSpec((None, 1, D_MODEL), lay),
            pl.BlockSpec((None, D_MODEL, QKV_W), lay, pipeline_mode=pl.Buffered(1)),
            pl.BlockSpec((None, 1, QKV_W), lay),
        ],
        out_specs=[
            pl.BlockSpec((ROW_TILE, width), lambda i: (i, 0)),
            pl.BlockSpec((tiles, width, T_Q), lambda i: (i, 0, 0)),
            pl.BlockSpec((tiles, width, T_Q), lambda i: (i, 0, 0)),
        ],
        out_shape=[jax.ShapeDtypeStruct((m, width), BF16), transposed, transposed],
        compiler_params=pltpu.CompilerParams(
            dimension_semantics=("arbitrary",), vmem_limit_bytes=VMEM_LIMIT),
        name="qkv_proj",
    )(x, g, w, qk_gain)


def _merge_kernel(x_ref, g_ref, oa_ref, ob_ref, oc_ref, wg_ref, bg_ref, wbr_ref, wo_ref, out_ref):
    x = x_ref[...]
    h = _rms(x, g_ref[...]).astype(BF16)
    merged = None
    for br, o_ref in enumerate((oa_ref, ob_ref, oc_ref)):
        g_lin = jnp.dot(h, wg_ref[:, br * D_MODEL:(br + 1) * D_MODEL], preferred_element_type=F32)
        gate = 1.0 / (1.0 + jnp.exp(-(g_lin + bg_ref[br:br + 1, :])))
        proj = jnp.dot(o_ref[...], wbr_ref[br * W_BRANCH:(br + 1) * W_BRANCH, :],
                       preferred_element_type=F32)
        term = gate * proj
        merged = term if merged is None else merged + term
    out_ref[...] = x + jnp.dot(merged.astype(BF16), wo_ref[...], preferred_element_type=F32)


def _merge_out(x, g, o_a, o_b, o_c, w_gate, b_gate, w_br, w_out, layer):
    m = x.shape[0]
    row = lambda i: (i, 0)
    lay = lambda i: (layer, 0, 0)
    return pl.pallas_call(
        _merge_kernel,
        grid=(m // ROW_TILE,),
        in_specs=[
            pl.BlockSpec((ROW_TILE, D_MODEL), row),
            pl.BlockSpec((None, 1, D_MODEL), lay),
            pl.BlockSpec((ROW_TILE, W_BRANCH), row),
            pl.BlockSpec((ROW_TILE, W_BRANCH), row),
            pl.BlockSpec((ROW_TILE, W_BRANCH), row),
            pl.BlockSpec((None, D_MODEL, N_BRANCH * D_MODEL), lay, pipeline_mode=pl.Buffered(1)),
            pl.BlockSpec((None, N_BRANCH, D_MODEL), lay),
            pl.BlockSpec((None, N_BRANCH * W_BRANCH, D_MODEL), lay, pipeline_mode=pl.Buffered(1)),
            pl.BlockSpec((None, D_MODEL, D_MODEL), lay, pipeline_mode=pl.Buffered(1)),
        ],
        out_specs=pl.BlockSpec((ROW_TILE, D_MODEL), row),
        out_shape=jax.ShapeDtypeStruct((m, D_MODEL), F32),
        compiler_params=pltpu.CompilerParams(
            dimension_semantics=("arbitrary",), vmem_limit_bytes=VMEM_LIMIT),
        name="merge_out",
    )(x, g, o_a, o_b, o_c, w_gate, b_gate, w_br, w_out)


def _ffn_kernel(x_ref, g_ref, wgu_ref, wd_ref, out_ref, act_ref):
    x = x_ref[...]
    h = _rms(x, g_ref[...]).astype(BF16)
    for c in range(0, D_FF, FF_CHUNK):
        gate = jnp.dot(h, wgu_ref[:, c:c + FF_CHUNK], preferred_element_type=F32)
        up = jnp.dot(h, wgu_ref[:, D_FF + c:D_FF + c + FF_CHUNK], preferred_element_type=F32)
        silu = gate / (1.0 + jnp.exp(-gate))
        act_ref[:, c:c + FF_CHUNK] = (silu * up).astype(BF16)
    out_ref[...] = x + jnp.dot(act_ref[...], wd_ref[...], preferred_element_type=F32)


def _ffn(x, g, w_gu, w_down, layer):
    m = x.shape[0]
    row = lambda i: (i, 0)
    lay = lambda i: (layer, 0, 0)
    return pl.pallas_call(
        _ffn_kernel,
        grid=(m // ROW_TILE,),
        in_specs=[
            pl.BlockSpec((ROW_TILE, D_MODEL), row),
            pl.BlockSpec((None, 1, D_MODEL), lay),
            pl.BlockSpec((None, D_MODEL, 2 * D_FF), lay, pipeline_mode=pl.Buffered(1)),
            pl.BlockSpec((None, D_FF, D_MODEL), lay, pipeline_mode=pl.Buffered(1)),
        ],
        out_specs=pl.BlockSpec((ROW_TILE, D_MODEL), row),
        out_shape=jax.ShapeDtypeStruct((m, D_MODEL), F32),
        scratch_shapes=[pltpu.VMEM((ROW_TILE, D_FF), BF16)],
        compiler_params=pltpu.CompilerParams(
            dimension_semantics=("arbitrary",), vmem_limit_bytes=VMEM_LIMIT),
        name="ffn",
    )(x, g, w_gu, w_down)


def _sb_weights(z, cum, carry, mask):
    neg_abs = lax.bitcast_convert_type(
        lax.bitcast_convert_type(z, jnp.uint32) | jnp.uint32(0x80000000), F32)
    sp = jnp.log2(1.0 + jnp.exp2(neg_abs))
    log_beta = jnp.minimum(z, 0.0) - sp
    log_1m = log_beta - z
    if mask is not None:
        log_1m = jnp.where(mask, log_1m, 0.0)
    hi = log_1m.astype(BF16)
    lo = (log_1m - hi.astype(F32)).astype(BF16)
    afters = []
    for u in reversed(range(z.shape[0] // T_K)):
        r0, r1 = u * T_K, (u + 1) * T_K
        within = jnp.dot(cum, jnp.concatenate([hi[r0:r1], lo[r0:r1]], axis=0),
                         preferred_element_type=F32)
        afters.append(within + carry)
        carry = carry + within[0:1, :] + log_1m[r0:r0 + 1, :]
    after = afters[0] if len(afters) == 1 else jnp.concatenate(afters[::-1], axis=0)
    w = jnp.exp2(log_beta + after)
    if mask is not None:
        w = jnp.where(mask, w, 0.0)
    return w.astype(BF16), carry


def _sb_kernel(qt_ref, k_ref, vt_ref, o_ref):
    seq = k_ref.shape[0]
    first, second = _head_row_masks()
    kk = lax.broadcasted_iota(jnp.int32, (T_K, T_K), 0)
    kk2 = lax.broadcasted_iota(jnp.int32, (T_K, T_K), 1)
    later = jnp.where(kk2 > kk, 1.0, 0.0).astype(BF16)
    cum = jnp.concatenate([later, later], axis=1)
    strict = (lax.broadcasted_iota(jnp.int32, (T_G, T_Q), 0)
              < lax.broadcasted_iota(jnp.int32, (T_G, T_Q), 1))
    heads = range(2)

    def load_q(qi):
        return _split_heads(qt_ref[qi], first, second)

    def logits(g, qts):
        k2 = k_ref[pl.ds(pl.multiple_of(g * T_G, T_G), T_G), :]
        return [jnp.dot(k2, qts[h], preferred_element_type=F32) for h in heads]

    def weights(zs, mask, carries):
        out = [_sb_weights(zs[h], cum, carries[h], mask) for h in heads]
        return [w for w, _ in out], [c for _, c in out]

    def add_pv(g, ws, accs):
        vt = vt_ref[g]
        return [accs[h] + jnp.dot(vt[h * HEAD_DIM:(h + 1) * HEAD_DIM, :], ws[h],
                                  preferred_element_type=F32) for h in heads]

    def tail(qts, g_start, carries, accs):
        def live(st):
            g, carries, _ = st
            return (g >= 0) & (jnp.max(jnp.maximum(carries[0], carries[1])) >= SB_DEAD)

        def body(st):
            g, carries, accs = st
            ws, carries = weights(logits(g, qts), None, list(carries))
            return g - 1, tuple(carries), tuple(add_pv(g, ws, list(accs)))

        _, _, accs = lax.while_loop(live, body, (g_start, tuple(carries), tuple(accs)))
        return list(accs)

    def store(qi, accs):
        q0 = pl.multiple_of(qi * T_Q, T_Q)
        o_ref[pl.ds(q0, T_Q), :] = jnp.concatenate(accs, axis=0).T.astype(BF16)

    def q_tiles(tiles):
        qts = {qi: load_q(qi) for qi in tiles}
        groups = {qi: [g for g in (qi, qi - 1) if g >= 0] for qi in tiles}
        zs = {(qi, g): logits(g, qts[qi]) for qi in tiles for g in groups[qi]}
        state = {}
        for qi in tiles:
            carries = [jnp.zeros((1, T_Q), F32) for _ in heads]
            accs = [jnp.zeros((HEAD_DIM, T_Q), F32) for _ in heads]
            for g in groups[qi]:
                ws, carries = weights(zs[qi, g], strict if g == qi else None, carries)
                accs = add_pv(g, ws, accs)
            state[qi] = carries, accs
        for qi in tiles:
            carries, accs = state[qi]
            if qi >= 2:
                accs = tail(qts[qi], qi - 2, carries, accs)
            store(qi, accs)

    n_q = seq // T_Q
    for first_tile in range(0, n_q, SB_TILES_PER_BLOCK):
        q_tiles(range(first_tile, min(first_tile + SB_TILES_PER_BLOCK, n_q)))


def _head_block_specs(branch, seq, index):
    def transposed(*ids):
        bi, hb = index(*ids)
        return bi, 0, branch * (W_BRANCH // LANES) + hb, 0

    def rows(*ids):
        bi, hb = index(*ids)
        return bi, 0, branch * (W_BRANCH // LANES) + hb

    t_spec = pl.BlockSpec((None, seq // T_Q, LANES, T_Q), transposed)
    return [t_spec, pl.BlockSpec((None, seq, LANES), rows), t_spec]


def _sb_attn(qt, k, vt):
    b, s, _ = k.shape
    n_hp = W_BRANCH // LANES
    return pl.pallas_call(
        _sb_kernel,
        grid=(b, n_hp),
        in_specs=_head_block_specs(0, s, lambda bi, hp: (bi, hp)),
        out_specs=pl.BlockSpec((None, s, LANES), lambda bi, hp: (bi, 0, hp)),
        out_shape=jax.ShapeDtypeStruct((b, s, W_BRANCH), BF16),
        compiler_params=pltpu.CompilerParams(
            dimension_semantics=("arbitrary", "arbitrary"), vmem_limit_bytes=VMEM_LIMIT),
        name="sb_attn",
    )(qt, k, vt)


def _softmax_update(scores, vts, state):
    m, l, acc = state
    m_new = m
    for s in scores:
        m_new = jnp.maximum(m_new, jnp.max(s, axis=0, keepdims=True))
    alpha = jnp.exp2(m - m_new)
    l = alpha * l
    acc = alpha * acc
    for s, vt in zip(scores, vts):
        p = jnp.exp2(s - m_new)
        l = l + jnp.sum(p, axis=0, keepdims=True)
        acc = acc + jnp.dot(vt, p.astype(BF16), preferred_element_type=F32)
    return m_new, l, acc


def _diff_kernel(qt_ref, kn_ref, vt_ref, bias_ref, gq_ref, gk_ref, lam_ref, sg_ref, o_ref, *, lam_init):
    seq = kn_ref.shape[0]
    n_q = seq // T_Q
    first, second = _head_row_masks()
    lq = lam_ref[...]
    lam = (jnp.exp(jnp.sum(lq[0:1] * lq[1:2], axis=-1, keepdims=True))
           - jnp.exp(jnp.sum(lq[2:3] * lq[3:4], axis=-1, keepdims=True)) + lam_init)

    def keys(g):
        return kn_ref[pl.ds(pl.multiple_of(g * T_G, T_G), T_G), :]

    def scores(qi, g):
        kn2, bias = keys(g), bias_ref[qi - g]
        return [jnp.dot(kn2, qtm, preferred_element_type=F32) + bias
                for qtm in _split_heads(qt_ref[qi], first, second)]

    def finish(qi, accs, sums):
        ob = accs[0] * (1.0 / sums[0]) - lam * (accs[1] * (1.0 / sums[1]))
        y = ob * lax.rsqrt(jnp.mean(ob * ob, axis=0, keepdims=True) + RMS_EPS)
        q0 = pl.multiple_of(qi * T_Q, T_Q)
        o_ref[pl.ds(q0, T_Q), :] = (y.T * sg_ref[...] * (1.0 - lam_init)).astype(BF16)

    def q_pair(qp, _):
        tiles = (2 * qp, 2 * qp + 1)

        def trip(it, states):
            gs = (2 * it, 2 * it + 1)
            vts = [vt_ref[g] for g in gs]
            scs = [[scores(qi, g) for g in gs] for qi in tiles]
            states = [list(st) for st in states]
            for i in range(2):
                for mp in range(2):
                    states[i][mp] = _softmax_update([sc[mp] for sc in scs[i]], vts, states[i][mp])
            return tuple(tuple(st) for st in states)

        init = (jnp.full((1, T_Q), NEG, F32), jnp.zeros((1, T_Q), F32), jnp.zeros((LANES, T_Q), F32))
        states = lax.fori_loop(0, qp, trip, ((init, init), (init, init)))
        gs = (2 * qp, 2 * qp + 1)
        vts = [vt_ref[g] for g in gs]
        scs = [[scores(tiles[0], gs[0])], [scores(tiles[1], g) for g in gs]]
        for i in range(2):
            done = [_softmax_update([sc[mp] for sc in scs[i]], vts, states[i][mp]) for mp in range(2)]
            finish(tiles[i], [st[2] for st in done], [st[1] for st in done])
        return 0

    def bounded():
        work = [(qi, g) for qi in range(n_q) for g in range(qi + 1)]
        z_next = scores(*work[0])
        for step, (qi, g) in enumerate(work):
            z = z_next
            if step + 1 < len(work):
                z_next = scores(*work[step + 1])
            vt = vt_ref[g]
            ps = [jnp.exp2(z[mp]) for mp in range(2)]
            psums = [jnp.sum(p, axis=0, keepdims=True) for p in ps]
            pvs = [jnp.dot(vt, p.astype(BF16), preferred_element_type=F32) for p in ps]
            if g == 0:
                accs, sums = pvs, psums
            else:
                accs = [a + pv for a, pv in zip(accs, pvs)]
                sums = [s + ps_ for s, ps_ in zip(sums, psums)]
            if g == qi:
                finish(qi, accs, sums)

    def general():
        lax.fori_loop(0, n_q // 2, q_pair, 0)

    lax.cond(_qk_logit_bound(gq_ref[...], gk_ref[...]) <= SAFE_LOG2, bounded, general)


def _alibi_tiles(n_q):
    slopes = np.asarray([2.0 ** (-8.0 * (i + 1) / H_DIFF) for i in range(H_DIFF)], np.float32)
    slope2 = jnp.asarray(slopes * LOG2E, F32)[:, None, None, None]
    kpos = lax.broadcasted_iota(jnp.int32, (T_G, T_Q), 0)
    qpos = lax.broadcasted_iota(jnp.int32, (T_G, T_Q), 1)
    tiles_back = lax.broadcasted_iota(jnp.int32, (n_q, 1, 1), 0)
    dist = (qpos - kpos)[None] + T_G * tiles_back
    bias = -slope2 * jnp.abs(dist).astype(F32)[None]
    visible = (tiles_back > 0) | ((kpos // CHUNK) <= (qpos // CHUNK))[None]
    return jnp.where(visible[None], bias, NEG)


def _diff_attn(qt, k, vt, alibi, gq, gk, lam_qk, subln_g, layer, lam_init):
    b, s, _ = k.shape
    lay = lambda h, bi: (layer, 0, 0)
    return pl.pallas_call(
        functools.partial(_diff_kernel, lam_init=lam_init),
        grid=(H_DIFF, b),
        in_specs=[
            *_head_block_specs(1, s, lambda h, bi: (bi, h)),
            pl.BlockSpec((None, s // T_Q, T_G, T_Q), lambda h, bi: (h, 0, 0, 0)),
            pl.BlockSpec((None, 1, LANES), lay),
            pl.BlockSpec((None, 1, LANES), lay),
            pl.BlockSpec((None, 4, HEAD_DIM), lay),
            pl.BlockSpec((None, 1, LANES), lay),
        ],
        out_specs=pl.BlockSpec((None, s, LANES), lambda h, bi: (bi, 0, h)),
        out_shape=jax.ShapeDtypeStruct((b, s, W_BRANCH), BF16),
        compiler_params=pltpu.CompilerParams(
            dimension_semantics=("arbitrary", "arbitrary"), vmem_limit_bytes=VMEM_LIMIT),
        name="diff_attn",
    )(qt, k, vt, alibi, gq, gk, lam_qk, subln_g)


N_REL_GROUPS = (N_PAST_CHUNKS * CHUNK + T_Q - 1) // T_G + 1


def _chunk_kernel(qt_ref, kn_ref, vt_ref, gq_ref, gk_ref, tab_ref, o_ref, sc_ref):
    seq = kn_ref.shape[0]
    first, second = _head_row_masks()
    deltas = tuple(range(N_REL_GROUPS - 1, -1, -1))

    def score_stage(qi, tile_deltas, slot):
        qts = _split_heads(qt_ref[qi], first, second)
        for dl in tile_deltas:
            k2 = kn_ref[pl.ds(pl.multiple_of((qi - dl) * T_G, T_G), T_G), :]
            for h in range(2):
                sc_ref[slot, h, dl] = (jnp.dot(k2, qts[h], preferred_element_type=F32)
                                       + tab_ref[h, dl])

    def softmax_stage(qi, tile_deltas, slot, fixed_shift):
        outs = []
        for h in range(2):
            scores = [sc_ref[slot, h, dl] for dl in tile_deltas]
            if fixed_shift:
                ps = [jnp.exp2(s) for s in scores]
            else:
                m = functools.reduce(jnp.maximum, [jnp.max(s, axis=0, keepdims=True) for s in scores])
                ps = [jnp.exp2(s - m) for s in scores]
            l = functools.reduce(jnp.add, [jnp.sum(p, axis=0, keepdims=True) for p in ps])
            acc = None
            for dl, p in zip(tile_deltas, ps):
                vth = vt_ref[qi - dl][h * HEAD_DIM:(h + 1) * HEAD_DIM, :]
                pv = jnp.dot(vth, p.astype(BF16), preferred_element_type=F32)
                acc = pv if acc is None else acc + pv
            outs.append(acc / l)
        q0 = pl.multiple_of(qi * T_Q, T_Q)
        o_ref[pl.ds(q0, T_Q), :] = jnp.concatenate(outs, axis=0).T.astype(BF16)

    def tile_deltas(qi):
        return tuple(dl for dl in deltas if qi - dl >= 0)

    n_q = seq // T_Q

    def run(fixed_shift):
        score_stage(0, tile_deltas(0), 0)
        for qi in range(n_q):
            if qi + 1 < n_q:
                score_stage(qi + 1, tile_deltas(qi + 1), (qi + 1) % 2)
            softmax_stage(qi, tile_deltas(qi), qi % 2, fixed_shift)

    tabs = tab_ref[...]
    bound = (_qk_logit_bound(gq_ref[...], gk_ref[...])
             + jnp.max(jnp.where(tabs > 0.5 * NEG, jnp.abs(tabs), 0.0)))
    lax.cond(bound <= SAFE_LOG2, lambda: run(True), lambda: run(False))


def _chunk_attn(qt, k, vt, gq, gk, tab, layer):
    b, s, _ = k.shape
    n_hp = W_BRANCH // LANES
    lay = lambda hp, bi: (layer, 0, 0)
    return pl.pallas_call(
        _chunk_kernel,
        grid=(n_hp, b),
        in_specs=[
            *_head_block_specs(2, s, lambda hp, bi: (bi, hp)),
            pl.BlockSpec((None, 1, LANES), lay),
            pl.BlockSpec((None, 1, LANES), lay),
            pl.BlockSpec((None, 2, N_REL_GROUPS, T_G, T_Q), lambda hp, bi: (layer, hp, 0, 0, 0)),
        ],
        out_specs=pl.BlockSpec((None, s, LANES), lambda hp, bi: (bi, 0, hp)),
        out_shape=jax.ShapeDtypeStruct((b, s, W_BRANCH), BF16),
        scratch_shapes=[pltpu.VMEM((2, 2, N_REL_GROUPS, T_G, T_Q), F32)],
        compiler_params=pltpu.CompilerParams(
            dimension_semantics=("arbitrary", "arbitrary"), vmem_limit_bytes=VMEM_LIMIT),
        name="chunk_attn",
    )(qt, k, vt, gq, gk, tab)


def _rel_bias_tiles(rel_bias):
    lead = rel_bias.shape[:-1]
    span = (N_REL_GROUPS - 1) * T_G + T_Q
    period = span + T_G
    edge_lo = jnp.broadcast_to(rel_bias[..., :1], lead + (T_G - REL_CLIP,))
    edge_hi = jnp.broadcast_to(rel_bias[..., -1:], lead + (span - REL_CLIP - 1,))
    row = jnp.concatenate([rel_bias[..., REL_CLIP:], edge_hi, edge_lo, rel_bias[..., :REL_CLIP]], axis=-1)
    flat = jnp.tile(row, (1,) * len(lead) + (T_G,))[..., :T_G * (period - 1)]
    toep = flat.reshape(lead + (T_G, period - 1))
    tiles = jnp.stack([toep[..., d * T_G:d * T_G + T_Q] for d in range(N_REL_GROUPS)], axis=-3)
    kchunk = np.arange(T_G)[:, None] // CHUNK
    qchunk = np.arange(T_Q)[None, :] // CHUNK
    dd = np.stack([qchunk - kchunk + d * (T_G // CHUNK) for d in range(N_REL_GROUPS)])
    return jnp.where((dd >= 0) & (dd <= N_PAST_CHUNKS), tiles * LOG2E, NEG)


def kernel(x, norm_mix_g, w_in, b_gate, qk_g_diff, lambda_qk, subln_g, qk_g_ch, rel_bias,
           w_branch_sb, w_branch_diff, w_branch_ch, w_out, norm_ffn_g, w_gu, w_down):
    b, s, d = x.shape
    m = b * s
    w_qkv = w_in[:, :, :QKV_W].astype(BF16)
    w_gate = w_in[:, :, QKV_W:].astype(BF16)
    w_br = jnp.concatenate([w_branch_sb, w_branch_diff, w_branch_ch], axis=1).astype(BF16)
    w_out_b = w_out.astype(BF16)
    w_gu_b = w_gu.astype(BF16)
    w_down_b = w_down.astype(BF16)
    g_mix = norm_mix_g.reshape(DEPTH, 1, d)
    g_ffn = norm_ffn_g.reshape(DEPTH, 1, d)
    gq_diff = jnp.tile(qk_g_diff[:, 0:1, :], (1, 1, 2))
    gk_diff = jnp.tile(qk_g_diff[:, 1:2, :], (1, 1, 2))
    gq_ch = jnp.tile(qk_g_ch[:, 0:1, :], (1, 1, 2))
    gk_ch = jnp.tile(qk_g_ch[:, 1:2, :], (1, 1, 2))
    sg = subln_g.reshape(DEPTH, 1, 2 * HEAD_DIM)
    tab = _rel_bias_tiles(rel_bias)
    alibi = _alibi_tiles(s // T_Q)
    ones = jnp.ones((DEPTH, 1, W_BRANCH), F32)
    widen = lambda g: jnp.tile(g, (1, 1, W_BRANCH // LANES))
    q_scale = QK_SCALE * LOG2E
    qk_gain = jnp.concatenate([ones, ones, ones, widen(gq_diff) * q_scale, widen(gk_diff), ones,
                               widen(gq_ch) * q_scale, widen(gk_ch), ones], axis=-1)

    xf = x.reshape(m, d)
    for layer in range(DEPTH):
        lam_init = 0.8 - 0.6 * math.exp(-0.3 * layer)
        k, qt, vt = _qkv_proj(xf, g_mix, w_qkv, qk_gain, layer)
        k = k.reshape(b, s, N_BRANCH * W_BRANCH)
        qt = qt.reshape(b, s // T_Q, N_BRANCH * W_BRANCH, T_Q)
        vt = vt.reshape(b, s // T_Q, N_BRANCH * W_BRANCH, T_Q)
        o_a = _sb_attn(qt, k, vt)
        o_b = _diff_attn(qt, k, vt, alibi, gq_diff, gk_diff, lambda_qk, sg, layer, lam_init)
        o_c = _chunk_attn(qt, k, vt, gq_ch, gk_ch, tab, layer)
        xf = _merge_out(xf, g_mix, o_a.reshape(m, W_BRANCH), o_b.reshape(m, W_BRANCH),
                        o_c.reshape(m, W_BRANCH), w_gate, b_gate, w_br, w_out_b, layer)
        xf = _ffn(xf, g_ffn, w_gu_b, w_down_b, layer)
    return xf.reshape(b, s, d)
```

```python
import functools
import math

import jax
import jax.numpy as jnp
import numpy as np
from jax import lax
from jax.experimental import pallas as pl
from jax.experimental.pallas import tpu as pltpu

F32 = jnp.float32
BF16 = jnp.bfloat16

D_MODEL = 1024
DEPTH = 4
CHUNK = 64
HEAD_DIM = 64
H_DIFF = 4
N_PAST_CHUNKS = 8
REL_CLIP = 128
W_BRANCH = 512
QKV_W = 9 * W_BRANCH
N_BRANCH = 3
D_FF = int(math.ceil(8 * D_MODEL / 3 / 256)) * 256
RMS_EPS = 1e-6
QK_SCALE = HEAD_DIM ** -0.5

LANES = 128
T_Q = 256
T_K = 128
T_G = 256
NEG = -1e30
SB_DEAD = -150.0
SB_TILES_PER_BLOCK = 4
LOG2E = 1.4426950408889634
SAFE_LOG2 = 60.0
VMEM_LIMIT = 56 * 1024 * 1024

ROW_TILE = 1024
FF_CHUNK = 256


def _rms(x, g):
    return x * lax.rsqrt(jnp.mean(x * x, axis=-1, keepdims=True) + RMS_EPS) * g


def _rms_halves(x, g):
    lane = lax.broadcasted_iota(jnp.int32, (1, LANES), 1)
    first = lane < HEAD_DIM
    x2 = x * x
    s0 = jnp.sum(jnp.where(first, x2, 0.0), axis=-1, keepdims=True)
    s1 = jnp.sum(jnp.where(first, 0.0, x2), axis=-1, keepdims=True)
    ms = jnp.where(first, s0, s1) * (1.0 / HEAD_DIM)
    return x * lax.rsqrt(ms + RMS_EPS) * g


def _qk_logit_bound(gq, gk):
    return (1.02 * LOG2E * QK_SCALE * HEAD_DIM) * jnp.max(jnp.abs(gq)) * jnp.max(jnp.abs(gk))


def _head_row_masks():
    row = lax.broadcasted_iota(jnp.int32, (LANES, 1), 0)
    return row < HEAD_DIM, row >= HEAD_DIM


def _split_heads(qt, first, second):
    zero = jnp.zeros_like(qt)
    return jnp.where(first, qt, zero), jnp.where(second, qt, zero)


N_SECTIONS = 9


def _qkv_kernel(x_ref, g_ref, w_ref, qkg_ref, k_out, qt_out, vt_out):
    h = _rms(x_ref[...], g_ref[...]).astype(BF16)

    def project(sec):
        return jnp.dot(h, w_ref[:, sec * W_BRANCH:(sec + 1) * W_BRANCH], preferred_element_type=F32)

    z_next = project(0)
    for sec in range(N_SECTIONS):
        z = z_next
        if sec + 1 < N_SECTIONS:
            z_next = project(sec + 1)
        branch, role = divmod(sec, 3)
        for cb in range(W_BRANCH // LANES):
            blk = z[:, cb * LANES:(cb + 1) * LANES]
            col = sec * W_BRANCH + cb * LANES
            if branch > 0 and role < 2:
                blk = _rms_halves(blk, qkg_ref[:, col:col + LANES])
            elif role == 0:
                blk = blk * (QK_SCALE * LOG2E)
            out_col = branch * W_BRANCH + cb * LANES
            if role == 1:
                k_out[:, out_col:out_col + LANES] = blk.astype(BF16)
            else:
                out = qt_out if role == 0 else vt_out
                for r in range(ROW_TILE // T_Q):
                    out[r, out_col:out_col + LANES, :] = blk[r * T_Q:(r + 1) * T_Q, :].T.astype(BF16)


def _qkv_proj(x, g, w, qk_gain, layer):
    m = x.shape[0]
    width = N_BRANCH * W_BRANCH
    tiles = ROW_TILE // T_Q
    lay = lambda i: (layer, 0, 0)
    transposed = jax.ShapeDtypeStruct((m // T_Q, width, T_Q), BF16)
    return pl.pallas_call(
        _qkv_kernel,
        grid=(m // ROW_TILE,),
        in_specs=[
            pl.BlockSpec((ROW_TILE, D_MODEL), lambda i: (i, 0)),
            pl.BlockSpec((None, 1, D_MODEL), lay),
            pl.BlockSpec((None, D_MODEL, QKV_W), lay, pipeline_mode=pl.Buffered(1)),
            pl.BlockSpec((None, 1, QKV_W), lay),
        ],
        out_specs=[
            pl.BlockSpec((ROW_TILE, width), lambda i: (i, 0)),
            pl.BlockSpec((tiles, width, T_Q), lambda i: (i, 0, 0)),
            pl.BlockSpec((tiles, width, T_Q), lambda i: (i, 0, 0)),
        ],
        out_shape=[jax.ShapeDtypeStruct((m, width), BF16), transposed, transposed],
        compiler_params=pltpu.CompilerParams(
            dimension_semantics=("arbitrary",), vmem_limit_bytes=VMEM_LIMIT),
        name="qkv_proj",
    )(x, g, w, qk_gain)


def _merge_kernel(x_ref, g_ref, oa_ref, ob_ref, oc_ref, wg_ref, bg_ref, wbr_ref, wo_ref, out_ref):
    x = x_ref[...]
    h = _rms(x, g_ref[...]).astype(BF16)
    merged = None
    for br, o_ref in enumerate((oa_ref, ob_ref, oc_ref)):
        g_lin = jnp.dot(h, wg_ref[:, br * D_MODEL:(br + 1) * D_MODEL], preferred_element_type=F32)
        gate = 1.0 / (1.0 + jnp.exp(-(g_lin + bg_ref[br:br + 1, :])))
        proj = jnp.dot(o_ref[...], wbr_ref[br * W_BRANCH:(br + 1) * W_BRANCH, :],
                       preferred_element_type=F32)
        term = gate * proj
        merged = term if merged is None else merged + term
    out_ref[...] = x + jnp.dot(merged.astype(BF16), wo_ref[...], preferred_element_type=F32)


def _merge_out(x, g, o_a, o_b, o_c, w_gate, b_gate, w_br, w_out, layer):
    m = x.shape[0]
    row = lambda i: (i, 0)
    lay = lambda i: (layer, 0, 0)
    return pl.pallas_call(
        _merge_kernel,
        grid=(m // ROW_TILE,),
        in_specs=[
            pl.BlockSpec((ROW_TILE, D_MODEL), row),
            pl.BlockSpec((None, 1, D_MODEL), lay),
            pl.BlockSpec((ROW_TILE, W_BRANCH), row),
            pl.BlockSpec((ROW_TILE, W_BRANCH), row),
            pl.BlockSpec((ROW_TILE, W_BRANCH), row),
            pl.BlockSpec((None, D_MODEL, N_BRANCH * D_MODEL), lay, pipeline_mode=pl.Buffered(1)),
            pl.BlockSpec((None, N_BRANCH, D_MODEL), lay),
            pl.BlockSpec((None, N_BRANCH * W_BRANCH, D_MODEL), lay, pipeline_mode=pl.Buffered(1)),
            pl.BlockSpec((None, D_MODEL, D_MODEL), lay, pipeline_mode=pl.Buffered(1)),
        ],
        out_specs=pl.BlockSpec((ROW_TILE, D_MODEL), row),
        out_shape=jax.ShapeDtypeStruct((m, D_MODEL), F32),
        compiler_params=pltpu.CompilerParams(
            dimension_semantics=("arbitrary",), vmem_limit_bytes=VMEM_LIMIT),
        name="merge_out",
    )(x, g, o_a, o_b, o_c, w_gate, b_gate, w_br, w_out)


def _ffn_kernel(x_ref, g_ref, wgu_ref, wd_ref, out_ref, act_ref):
    x = x_ref[...]
    h = _rms(x, g_ref[...]).astype(BF16)
    for c in range(0, D_FF, FF_CHUNK):
        gate = jnp.dot(h, wgu_ref[:, c:c + FF_CHUNK], preferred_element_type=F32)
        up = jnp.dot(h, wgu_ref[:, D_FF + c:D_FF + c + FF_CHUNK], preferred_element_type=F32)
        silu = gate / (1.0 + jnp.exp(-gate))
        act_ref[:, c:c + FF_CHUNK] = (silu * up).astype(BF16)
    out_ref[...] = x + jnp.dot(act_ref[...], wd_ref[...], preferred_element_type=F32)


def _ffn(x, g, w_gu, w_down, layer):
    m = x.shape[0]
    row = lambda i: (i, 0)
    lay = lambda i: (layer, 0, 0)
    return pl.pallas_call(
        _ffn_kernel,
        grid=(m // ROW_TILE,),
        in_specs=[
            pl.BlockSpec((ROW_TILE, D_MODEL), row),
            pl.BlockSpec((None, 1, D_MODEL), lay),
            pl.BlockSpec((None, D_MODEL, 2 * D_FF), lay, pipeline_mode=pl.Buffered(1)),
            pl.BlockSpec((None, D_FF, D_MODEL), lay, pipeline_mode=pl.Buffered(1)),
        ],
        out_specs=pl.BlockSpec((ROW_TILE, D_MODEL), row),
        out_shape=jax.ShapeDtypeStruct((m, D_MODEL), F32),
        scratch_shapes=[pltpu.VMEM((ROW_TILE, D_FF), BF16)],
        compiler_params=pltpu.CompilerParams(
            dimension_semantics=("arbitrary",), vmem_limit_bytes=VMEM_LIMIT),
        name="ffn",
    )(x, g, w_gu, w_down)


def _sb_weights(z, cum, carry, mask):
    neg_abs = lax.bitcast_convert_type(
        lax.bitcast_convert_type(z, jnp.uint32) | jnp.uint32(0x80000000), F32)
    sp = jnp.log2(1.0 + jnp.exp2(neg_abs))
    log_beta = jnp.minimum(z, 0.0) - sp
    log_1m = log_beta - z
    if mask is not None:
        log_1m = jnp.where(mask, log_1m, 0.0)
    hi = log_1m.astype(BF16)
    lo = (log_1m - hi.astype(F32)).astype(BF16)
    afters = []
    for u in reversed(range(z.shape[0] // T_K)):
        r0, r1 = u * T_K, (u + 1) * T_K
        within = jnp.dot(cum, jnp.concatenate([hi[r0:r1], lo[r0:r1]], axis=0),
                         preferred_element_type=F32)
        afters.append(within + carry)
        carry = carry + within[0:1, :] + log_1m[r0:r0 + 1, :]
    after = afters[0] if len(afters) == 1 else jnp.concatenate(afters[::-1], axis=0)
    w = jnp.exp2(log_beta + after)
    if mask is not None:
        w = jnp.where(mask, w, 0.0)
    return w.astype(BF16), carry


class _StickBreaking:
    HEADS = range(2)
    GROUPS_IN_MAIN = 2

    def __init__(self, qt_ref, k_ref, vt_ref, o_ref, acc_ref, carry_ref):
        self.qt_ref, self.k_ref, self.vt_ref, self.o_ref = qt_ref, k_ref, vt_ref, o_ref
        self.acc_ref, self.carry_ref = acc_ref, carry_ref
        self.n_q = k_ref.shape[0] // T_Q
        self.masks = _head_row_masks()
        kk = lax.broadcasted_iota(jnp.int32, (T_K, T_K), 0)
        kk2 = lax.broadcasted_iota(jnp.int32, (T_K, T_K), 1)
        later = jnp.where(kk2 > kk, 1.0, 0.0).astype(BF16)
        self.cum = jnp.concatenate([later, later], axis=1)
        self.strict = (lax.broadcasted_iota(jnp.int32, (T_G, T_Q), 0)
                       < lax.broadcasted_iota(jnp.int32, (T_G, T_Q), 1))

    def main_groups(self, qi):
        return [g for g in range(qi, qi - self.GROUPS_IN_MAIN, -1) if g >= 0]

    def n_main_items(self):
        return sum(len(self.main_groups(qi)) for qi in range(self.n_q)) * len(self.HEADS)

    def load_q(self, qi):
        return _split_heads(self.qt_ref[qi], *self.masks)

    def logits(self, g, qts):
        k2 = self.k_ref[pl.ds(pl.multiple_of(g * T_G, T_G), T_G), :]
        return [jnp.dot(k2, qts[h], preferred_element_type=F32) for h in self.HEADS]

    def add_group(self, g, h, z, mask, carry, acc):
        w, carry = _sb_weights(z, self.cum, carry, mask)
        vth = self.vt_ref[g][h * HEAD_DIM:(h + 1) * HEAD_DIM, :]
        return carry, acc + jnp.dot(vth, w, preferred_element_type=F32)

    def store(self, qi, accs):
        q0 = pl.multiple_of(qi * T_Q, T_Q)
        self.o_ref[pl.ds(q0, T_Q), :] = jnp.concatenate(accs, axis=0).T.astype(BF16)

    def main(self):
        for first_tile in range(0, self.n_q, SB_TILES_PER_BLOCK):
            tiles = range(first_tile, min(first_tile + SB_TILES_PER_BLOCK, self.n_q))
            qts = {qi: self.load_q(qi) for qi in tiles}
            zs = {(qi, g): self.logits(g, qts[qi]) for qi in tiles for g in self.main_groups(qi)}
            for qi in tiles:
                carries = [jnp.zeros((1, T_Q), F32) for _ in self.HEADS]
                accs = [jnp.zeros((HEAD_DIM, T_Q), F32) for _ in self.HEADS]
                for g in self.main_groups(qi):
                    for h in self.HEADS:
                        carries[h], accs[h] = self.add_group(
                            g, h, zs[qi, g][h], self.strict if g == qi else None, carries[h], accs[h])
                        yield
                self.store(qi, accs)
                for h in self.HEADS:
                    self.acc_ref[qi, h] = accs[h]
                    self.carry_ref[qi, h] = carries[h]

    def _live(self, carries):
        return jnp.max(functools.reduce(jnp.maximum, carries)) >= SB_DEAD

    def tails(self):
        tiles = [qi for qi in range(self.n_q) if qi - self.GROUPS_IN_MAIN >= 0]

        def tile_tail(qi):
            qts = self.load_q(qi)

            def live(st):
                g, carries, _ = st
                return (g >= 0) & self._live(carries)

            def body(st):
                g, carries, accs = st
                carries, accs = list(carries), list(accs)
                zs = self.logits(g, qts)
                for h in self.HEADS:
                    carries[h], accs[h] = self.add_group(g, h, zs[h], None, carries[h], accs[h])
                return g - 1, tuple(carries), tuple(accs)

            init = (qi - self.GROUPS_IN_MAIN,
                    tuple(self.carry_ref[qi, h] for h in self.HEADS),
                    tuple(self.acc_ref[qi, h] for h in self.HEADS))
            _, _, accs = lax.while_loop(live, body, init)
            self.store(qi, list(accs))

        def all_tails():
            for qi in tiles:
                tile_tail(qi)

        any_live = self._live([self.carry_ref[qi, h] for qi in tiles for h in self.HEADS])
        lax.cond(any_live, all_tails, lambda: None)


def _exhaust(items):
    for _ in items:
        pass


def _sb_kernel(qt_ref, k_ref, vt_ref, o_ref, acc_ref, carry_ref):
    sb = _StickBreaking(qt_ref, k_ref, vt_ref, o_ref, acc_ref, carry_ref)
    _exhaust(sb.main())
    sb.tails()


def _sb_scratch(seq):
    n_q = seq // T_Q
    return [pltpu.VMEM((n_q, 2, HEAD_DIM, T_Q), F32), pltpu.VMEM((n_q, 2, 1, T_Q), F32)]


def _head_block_specs(branch, seq, index):
    def transposed(*ids):
        bi, hb = index(*ids)
        return bi, 0, branch * (W_BRANCH // LANES) + hb, 0

    def rows(*ids):
        bi, hb = index(*ids)
        return bi, 0, branch * (W_BRANCH // LANES) + hb

    t_spec = pl.BlockSpec((None, seq // T_Q, LANES, T_Q), transposed)
    return [t_spec, pl.BlockSpec((None, seq, LANES), rows), t_spec]


def _sb_attn(qt, k, vt):
    b, s, _ = k.shape
    n_hp = W_BRANCH // LANES
    return pl.pallas_call(
        _sb_kernel,
        grid=(b, n_hp),
        in_specs=_head_block_specs(0, s, lambda bi, hp: (bi, hp)),
        out_specs=pl.BlockSpec((None, s, LANES), lambda bi, hp: (bi, 0, hp)),
        out_shape=jax.ShapeDtypeStruct((b, s, W_BRANCH), BF16),
        scratch_shapes=_sb_scratch(s),
        compiler_params=pltpu.CompilerParams(
            dimension_semantics=("arbitrary", "arbitrary"), vmem_limit_bytes=VMEM_LIMIT),
        name="sb_attn",
    )(qt, k, vt)


def _softmax_update(scores, vts, state):
    m, l, acc = state
    m_new = m
    for s in scores:
        m_new = jnp.maximum(m_new, jnp.max(s, axis=0, keepdims=True))
    alpha = jnp.exp2(m - m_new)
    l = alpha * l
    acc = alpha * acc
    for s, vt in zip(scores, vts):
        p = jnp.exp2(s - m_new)
        l = l + jnp.sum(p, axis=0, keepdims=True)
        acc = acc + jnp.dot(vt, p.astype(BF16), preferred_element_type=F32)
    return m_new, l, acc


def _diff_paths(qt_ref, kn_ref, vt_ref, bias_ref, lam_ref, sg_ref, o_ref, lam_init):
    seq = kn_ref.shape[0]
    n_q = seq // T_Q
    first, second = _head_row_masks()
    lq = lam_ref[...]
    lam = (jnp.exp(jnp.sum(lq[0:1] * lq[1:2], axis=-1, keepdims=True))
           - jnp.exp(jnp.sum(lq[2:3] * lq[3:4], axis=-1, keepdims=True)) + lam_init)

    def keys(g):
        return kn_ref[pl.ds(pl.multiple_of(g * T_G, T_G), T_G), :]

    def scores(qi, g):
        kn2, bias = keys(g), bias_ref[qi - g]
        return [jnp.dot(kn2, qtm, preferred_element_type=F32) + bias
                for qtm in _split_heads(qt_ref[qi], first, second)]

    def finish(qi, accs, sums):
        ob = accs[0] * (1.0 / sums[0]) - lam * (accs[1] * (1.0 / sums[1]))
        y = ob * lax.rsqrt(jnp.mean(ob * ob, axis=0, keepdims=True) + RMS_EPS)
        q0 = pl.multiple_of(qi * T_Q, T_Q)
        o_ref[pl.ds(q0, T_Q), :] = (y.T * sg_ref[...] * (1.0 - lam_init)).astype(BF16)

    def q_pair(qp, _):
        tiles = (2 * qp, 2 * qp + 1)

        def trip(it, states):
            gs = (2 * it, 2 * it + 1)
            vts = [vt_ref[g] for g in gs]
            scs = [[scores(qi, g) for g in gs] for qi in tiles]
            states = [list(st) for st in states]
            for i in range(2):
                for mp in range(2):
                    states[i][mp] = _softmax_update([sc[mp] for sc in scs[i]], vts, states[i][mp])
            return tuple(tuple(st) for st in states)

        init = (jnp.full((1, T_Q), NEG, F32), jnp.zeros((1, T_Q), F32), jnp.zeros((LANES, T_Q), F32))
        states = lax.fori_loop(0, qp, trip, ((init, init), (init, init)))
        gs = (2 * qp, 2 * qp + 1)
        vts = [vt_ref[g] for g in gs]
        scs = [[scores(tiles[0], gs[0])], [scores(tiles[1], g) for g in gs]]
        for i in range(2):
            done = [_softmax_update([sc[mp] for sc in scs[i]], vts, states[i][mp]) for mp in range(2)]
            finish(tiles[i], [st[2] for st in done], [st[1] for st in done])
        return 0

    work = [(qi, g) for qi in range(n_q) for g in range(qi + 1)]

    def bounded():
        z_next = scores(*work[0])
        for step, (qi, g) in enumerate(work):
            z = z_next
            if step + 1 < len(work):
                z_next = scores(*work[step + 1])
            vt = vt_ref[g]
            ps = [jnp.exp2(z[mp]) for mp in range(2)]
            psums = [jnp.sum(p, axis=0, keepdims=True) for p in ps]
            pvs = [jnp.dot(vt, p.astype(BF16), preferred_element_type=F32) for p in ps]
            if g == 0:
                accs, sums = pvs, psums
            else:
                accs = [a + pv for a, pv in zip(accs, pvs)]
                sums = [s + ps_ for s, ps_ in zip(sums, psums)]
            if g == qi:
                finish(qi, accs, sums)
            yield

    def general():
        lax.fori_loop(0, n_q // 2, q_pair, 0)

    return bounded, len(work), general


def _diff_kernel(qt_ref, kn_ref, vt_ref, bias_ref, gq_ref, gk_ref, lam_ref, sg_ref, o_ref, *, lam_init):
    bounded, _, general = _diff_paths(qt_ref, kn_ref, vt_ref, bias_ref, lam_ref, sg_ref, o_ref, lam_init)
    lax.cond(_qk_logit_bound(gq_ref[...], gk_ref[...]) <= SAFE_LOG2,
             lambda: _exhaust(bounded()), general)


def _alibi_tiles(n_q):
    slopes = np.asarray([2.0 ** (-8.0 * (i + 1) / H_DIFF) for i in range(H_DIFF)], np.float32)
    slope2 = jnp.asarray(slopes * LOG2E, F32)[:, None, None, None]
    kpos = lax.broadcasted_iota(jnp.int32, (T_G, T_Q), 0)
    qpos = lax.broadcasted_iota(jnp.int32, (T_G, T_Q), 1)
    tiles_back = lax.broadcasted_iota(jnp.int32, (n_q, 1, 1), 0)
    dist = (qpos - kpos)[None] + T_G * tiles_back
    bias = -slope2 * jnp.abs(dist).astype(F32)[None]
    visible = (tiles_back > 0) | ((kpos // CHUNK) <= (qpos // CHUNK))[None]
    return jnp.where(visible[None], bias, NEG)


def _diff_attn(qt, k, vt, alibi, gq, gk, lam_qk, subln_g, layer, lam_init):
    b, s, _ = k.shape
    lay = lambda h, bi: (layer, 0, 0)
    return pl.pallas_call(
        functools.partial(_diff_kernel, lam_init=lam_init),
        grid=(H_DIFF, b),
        in_specs=[
            *_head_block_specs(1, s, lambda h, bi: (bi, h)),
            pl.BlockSpec((None, s // T_Q, T_G, T_Q), lambda h, bi: (h, 0, 0, 0)),
            pl.BlockSpec((None, 1, LANES), lay),
            pl.BlockSpec((None, 1, LANES), lay),
            pl.BlockSpec((None, 4, HEAD_DIM), lay),
            pl.BlockSpec((None, 1, LANES), lay),
        ],
        out_specs=pl.BlockSpec((None, s, LANES), lambda h, bi: (bi, 0, h)),
        out_shape=jax.ShapeDtypeStruct((b, s, W_BRANCH), BF16),
        compiler_params=pltpu.CompilerParams(
            dimension_semantics=("arbitrary", "arbitrary"), vmem_limit_bytes=VMEM_LIMIT),
        name="diff_attn",
    )(qt, k, vt, alibi, gq, gk, lam_qk, subln_g)


N_REL_GROUPS = (N_PAST_CHUNKS * CHUNK + T_Q - 1) // T_G + 1


def _chunk_items(qt_ref, kn_ref, vt_ref, tab_ref, o_ref, sc_ref):
    seq = kn_ref.shape[0]
    first, second = _head_row_masks()
    deltas = tuple(range(N_REL_GROUPS - 1, -1, -1))

    def score_stage(qi, tile_deltas, slot):
        qts = _split_heads(qt_ref[qi], first, second)
        for dl in tile_deltas:
            k2 = kn_ref[pl.ds(pl.multiple_of((qi - dl) * T_G, T_G), T_G), :]
            for h in range(2):
                sc_ref[slot, h, dl] = (jnp.dot(k2, qts[h], preferred_element_type=F32)
                                       + tab_ref[h, dl])

    def softmax_stage(qi, tile_deltas, slot, fixed_shift):
        outs = []
        for h in range(2):
            scores = [sc_ref[slot, h, dl] for dl in tile_deltas]
            if fixed_shift:
                ps = [jnp.exp2(s) for s in scores]
            else:
                m = functools.reduce(jnp.maximum, [jnp.max(s, axis=0, keepdims=True) for s in scores])
                ps = [jnp.exp2(s - m) for s in scores]
            l = functools.reduce(jnp.add, [jnp.sum(p, axis=0, keepdims=True) for p in ps])
            acc = None
            for dl, p in zip(tile_deltas, ps):
                vth = vt_ref[qi - dl][h * HEAD_DIM:(h + 1) * HEAD_DIM, :]
                pv = jnp.dot(vth, p.astype(BF16), preferred_element_type=F32)
                acc = pv if acc is None else acc + pv
            outs.append(acc / l)
        q0 = pl.multiple_of(qi * T_Q, T_Q)
        o_ref[pl.ds(q0, T_Q), :] = jnp.concatenate(outs, axis=0).T.astype(BF16)

    def tile_deltas(qi):
        return tuple(dl for dl in deltas if qi - dl >= 0)

    n_q = seq // T_Q

    def run(fixed_shift):
        score_stage(0, tile_deltas(0), 0)
        for qi in range(n_q):
            if qi + 1 < n_q:
                score_stage(qi + 1, tile_deltas(qi + 1), (qi + 1) % 2)
            yield
            softmax_stage(qi, tile_deltas(qi), qi % 2, fixed_shift)
            yield

    return run, 2 * n_q


def _chunk_logit_bound(gq, gk, tabs):
    return _qk_logit_bound(gq, gk) + jnp.max(jnp.where(tabs > 0.5 * NEG, jnp.abs(tabs), 0.0))


def _chunk_kernel(qt_ref, kn_ref, vt_ref, gq_ref, gk_ref, tab_ref, o_ref, sc_ref):
    run, _ = _chunk_items(qt_ref, kn_ref, vt_ref, tab_ref, o_ref, sc_ref)
    bound = _chunk_logit_bound(gq_ref[...], gk_ref[...], tab_ref[...])
    lax.cond(bound <= SAFE_LOG2, lambda: _exhaust(run(True)), lambda: _exhaust(run(False)))


def _chunk_attn(qt, k, vt, gq, gk, tab, layer):
    b, s, _ = k.shape
    n_hp = W_BRANCH // LANES
    lay = lambda hp, bi: (layer, 0, 0)
    return pl.pallas_call(
        _chunk_kernel,
        grid=(n_hp, b),
        in_specs=[
            *_head_block_specs(2, s, lambda hp, bi: (bi, hp)),
            pl.BlockSpec((None, 1, LANES), lay),
            pl.BlockSpec((None, 1, LANES), lay),
            pl.BlockSpec((None, 2, N_REL_GROUPS, T_G, T_Q), lambda hp, bi: (layer, hp, 0, 0, 0)),
        ],
        out_specs=pl.BlockSpec((None, s, LANES), lambda hp, bi: (bi, 0, hp)),
        out_shape=jax.ShapeDtypeStruct((b, s, W_BRANCH), BF16),
        scratch_shapes=[pltpu.VMEM((2, 2, N_REL_GROUPS, T_G, T_Q), F32)],
        compiler_params=pltpu.CompilerParams(
            dimension_semantics=("arbitrary", "arbitrary"), vmem_limit_bytes=VMEM_LIMIT),
        name="chunk_attn",
    )(qt, k, vt, gq, gk, tab)


def _interleave(streams):
    order = sorted(((i + 0.5) / n, s) for s, (_, n) in enumerate(streams) for i in range(n))
    for _, s in order:
        next(streams[s][0], None)
    for items, _ in streams:
        _exhaust(items)


def _mix_kernel(qa_ref, ka_ref, va_ref, qb_ref, kb_ref, vb_ref, qc_ref, kc_ref, vc_ref,
                alibi_ref, lam_ref, sg_ref, tab_ref, oa_ref, ob_ref, oc_ref,
                acc_ref, carry_ref, sc_ref, *, lam_init):
    sb = _StickBreaking(qa_ref, ka_ref, va_ref, oa_ref, acc_ref, carry_ref)
    diff_items, n_diff, _ = _diff_paths(qb_ref, kb_ref, vb_ref, alibi_ref, lam_ref, sg_ref, ob_ref,
                                        lam_init)
    chunk_items, n_chunk = _chunk_items(qc_ref, kc_ref, vc_ref, tab_ref, oc_ref, sc_ref)
    _interleave([(sb.main(), sb.n_main_items()), (diff_items(), n_diff),
                 (chunk_items(True), n_chunk)])
    sb.tails()


def _mix_attn(qt, k, vt, alibi, lam_qk, subln_g, tab, layer, lam_init):
    b, s, _ = k.shape
    n_hb = W_BRANCH // LANES
    index = lambda hb, bi: (bi, hb)
    lay = lambda hb, bi: (layer, 0, 0)
    out_spec = pl.BlockSpec((None, s, LANES), lambda hb, bi: (bi, 0, hb))
    out_shape = jax.ShapeDtypeStruct((b, s, W_BRANCH), BF16)
    return pl.pallas_call(
        functools.partial(_mix_kernel, lam_init=lam_init),
        grid=(n_hb, b),
        in_specs=[
            *_head_block_specs(0, s, index), *_head_block_specs(1, s, index),
            *_head_block_specs(2, s, index),
            pl.BlockSpec((None, s // T_Q, T_G, T_Q), lambda hb, bi: (hb, 0, 0, 0)),
            pl.BlockSpec((None, 4, HEAD_DIM), lay),
            pl.BlockSpec((None, 1, LANES), lay),
            pl.BlockSpec((None, 2, N_REL_GROUPS, T_G, T_Q), lambda hb, bi: (layer, hb, 0, 0, 0)),
        ],
        out_specs=[out_spec, out_spec, out_spec],
        out_shape=[out_shape, out_shape, out_shape],
        scratch_shapes=_sb_scratch(s) + [pltpu.VMEM((2, 2, N_REL_GROUPS, T_G, T_Q), F32)],
        compiler_params=pltpu.CompilerParams(
            dimension_semantics=("arbitrary", "arbitrary"), vmem_limit_bytes=VMEM_LIMIT),
        name="mix_attn",
    )(qt, k, vt, qt, k, vt, qt, k, vt, alibi, lam_qk, subln_g, tab)


def _rel_bias_tiles(rel_bias):
    lead = rel_bias.shape[:-1]
    span = (N_REL_GROUPS - 1) * T_G + T_Q
    period = span + T_G
    edge_lo = jnp.broadcast_to(rel_bias[..., :1], lead + (T_G - REL_CLIP,))
    edge_hi = jnp.broadcast_to(rel_bias[..., -1:], lead + (span - REL_CLIP - 1,))
    row = jnp.concatenate([rel_bias[..., REL_CLIP:], edge_hi, edge_lo, rel_bias[..., :REL_CLIP]], axis=-1)
    flat = jnp.tile(row, (1,) * len(lead) + (T_G,))[..., :T_G * (period - 1)]
    toep = flat.reshape(lead + (T_G, period - 1))
    tiles = jnp.stack([toep[..., d * T_G:d * T_G + T_Q] for d in range(N_REL_GROUPS)], axis=-3)
    kchunk = np.arange(T_G)[:, None] // CHUNK
    qchunk = np.arange(T_Q)[None, :] // CHUNK
    dd = np.stack([qchunk - kchunk + d * (T_G // CHUNK) for d in range(N_REL_GROUPS)])
    return jnp.where((dd >= 0) & (dd <= N_PAST_CHUNKS), tiles * LOG2E, NEG)


def kernel(x, norm_mix_g, w_in, b_gate, qk_g_diff, lambda_qk, subln_g, qk_g_ch, rel_bias,
           w_branch_sb, w_branch_diff, w_branch_ch, w_out, norm_ffn_g, w_gu, w_down):
    b, s, d = x.shape
    m = b * s
    w_qkv = w_in[:, :, :QKV_W].astype(BF16)
    w_gate = w_in[:, :, QKV_W:].astype(BF16)
    w_br = jnp.concatenate([w_branch_sb, w_branch_diff, w_branch_ch], axis=1).astype(BF16)
    w_out_b = w_out.astype(BF16)
    w_gu_b = w_gu.astype(BF16)
    w_down_b = w_down.astype(BF16)
    g_mix = norm_mix_g.reshape(DEPTH, 1, d)
    g_ffn = norm_ffn_g.reshape(DEPTH, 1, d)
    gq_diff = jnp.tile(qk_g_diff[:, 0:1, :], (1, 1, 2))
    gk_diff = jnp.tile(qk_g_diff[:, 1:2, :], (1, 1, 2))
    gq_ch = jnp.tile(qk_g_ch[:, 0:1, :], (1, 1, 2))
    gk_ch = jnp.tile(qk_g_ch[:, 1:2, :], (1, 1, 2))
    sg = subln_g.reshape(DEPTH, 1, 2 * HEAD_DIM)
    tab = _rel_bias_tiles(rel_bias)
    alibi = _alibi_tiles(s // T_Q)
    ones = jnp.ones((DEPTH, 1, W_BRANCH), F32)
    widen = lambda g: jnp.tile(g, (1, 1, W_BRANCH // LANES))
    q_scale = QK_SCALE * LOG2E
    qk_gain = jnp.concatenate([ones, ones, ones, widen(gq_diff) * q_scale, widen(gk_diff), ones,
                               widen(gq_ch) * q_scale, widen(gk_ch), ones], axis=-1)

    xf = x.reshape(m, d)
    for layer in range(DEPTH):
        lam_init = 0.8 - 0.6 * math.exp(-0.3 * layer)
        k, qt, vt = _qkv_proj(xf, g_mix, w_qkv, qk_gain, layer)
        k = k.reshape(b, s, N_BRANCH * W_BRANCH)
        qt = qt.reshape(b, s // T_Q, N_BRANCH * W_BRANCH, T_Q)
        vt = vt.reshape(b, s // T_Q, N_BRANCH * W_BRANCH, T_Q)
        def mixers_fused():
            return tuple(_mix_attn(qt, k, vt, alibi, lambda_qk, sg, tab, layer, lam_init))

        def mixers_separate():
            return (_sb_attn(qt, k, vt),
                    _diff_attn(qt, k, vt, alibi, gq_diff, gk_diff, lambda_qk, sg, layer, lam_init),
                    _chunk_attn(qt, k, vt, gq_ch, gk_ch, tab, layer))

        bounded = ((_qk_logit_bound(gq_diff[layer], gk_diff[layer]) <= SAFE_LOG2)
                   & (_chunk_logit_bound(gq_ch[layer], gk_ch[layer], tab[layer]) <= SAFE_LOG2))
        o_a, o_b, o_c = lax.cond(bounded, mixers_fused, mixers_separate)
        xf = _merge_out(xf, g_mix, o_a.reshape(m, W_BRANCH), o_b.reshape(m, W_BRANCH),
                        o_c.reshape(m, W_BRANCH), w_gate, b_gate, w_br, w_out_b, layer)
        xf = _ffn(xf, g_ffn, w_gu_b, w_down_b, layer)
    return xf.reshape(b, s, d)
```

```python
import functools
import math

import jax
import jax.numpy as jnp
import numpy as np
from jax import lax
from jax.experimental import pallas as pl
from jax.experimental.pallas import tpu as pltpu

F32 = jnp.float32
BF16 = jnp.bfloat16

D_MODEL = 1024
DEPTH = 4
CHUNK = 64
HEAD_DIM = 64
H_DIFF = 4
N_PAST_CHUNKS = 8
REL_CLIP = 128
W_BRANCH = 512
QKV_W = 9 * W_BRANCH
N_BRANCH = 3
D_FF = int(math.ceil(8 * D_MODEL / 3 / 256)) * 256
RMS_EPS = 1e-6
QK_SCALE = HEAD_DIM ** -0.5

LANES = 128
T_Q = 256
T_K = 128
T_G = 256
NEG = -1e30
SB_DEAD = -150.0
ONES_ROWS = 16
SB_LOOKAHEAD = 2
LOG2E = 1.4426950408889634
SAFE_LOG2 = 60.0
VMEM_LIMIT = 56 * 1024 * 1024

ROW_TILE = 1024
FF_CHUNK = 256


def _rms(x, g):
    return x * lax.rsqrt(jnp.mean(x * x, axis=-1, keepdims=True) + RMS_EPS) * g


def _rms_halves(x, g):
    lane = lax.broadcasted_iota(jnp.int32, (1, LANES), 1)
    first = lane < HEAD_DIM
    x2 = x * x
    s0 = jnp.sum(jnp.where(first, x2, 0.0), axis=-1, keepdims=True)
    s1 = jnp.sum(jnp.where(first, 0.0, x2), axis=-1, keepdims=True)
    ms = jnp.where(first, s0, s1) * (1.0 / HEAD_DIM)
    return x * lax.rsqrt(ms + RMS_EPS) * g


def _qk_logit_bound(gq, gk):
    return (1.02 * LOG2E * QK_SCALE * HEAD_DIM) * jnp.max(jnp.abs(gq)) * jnp.max(jnp.abs(gk))


def _head_row_masks():
    row = lax.broadcasted_iota(jnp.int32, (LANES, 1), 0)
    return row < HEAD_DIM, row >= HEAD_DIM


def _split_heads(qt, first, second):
    zero = jnp.zeros_like(qt)
    return jnp.where(first, qt, zero), jnp.where(second, qt, zero)


N_SECTIONS = 9


def _qkv_kernel(x_ref, g_ref, w_ref, qkg_ref, k_out, qt_out, vt_out):
    h = _rms(x_ref[...], g_ref[...]).astype(BF16)

    def project(sec):
        return jnp.dot(h, w_ref[:, sec * W_BRANCH:(sec + 1) * W_BRANCH], preferred_element_type=F32)

    z_next = project(0)
    for sec in range(N_SECTIONS):
        z = z_next
        if sec + 1 < N_SECTIONS:
            z_next = project(sec + 1)
        branch, role = divmod(sec, 3)
        for cb in range(W_BRANCH // LANES):
            blk = z[:, cb * LANES:(cb + 1) * LANES]
            col = sec * W_BRANCH + cb * LANES
            if branch > 0 and role < 2:
                blk = _rms_halves(blk, qkg_ref[:, col:col + LANES])
            elif role == 0:
                blk = blk * (QK_SCALE * LOG2E)
            out_col = branch * W_BRANCH + cb * LANES
            if role == 1:
                k_out[:, out_col:out_col + LANES] = blk.astype(BF16)
            else:
                out = qt_out if role == 0 else vt_out
                for r in range(ROW_TILE // T_Q):
                    out[r, out_col:out_col + LANES, :] = blk[r * T_Q:(r + 1) * T_Q, :].T.astype(BF16)


def _qkv_proj(x, g, w, qk_gain, layer):
    m = x.shape[0]
    width = N_BRANCH * W_BRANCH
    tiles = ROW_TILE // T_Q
    lay = lambda i: (layer, 0, 0)
    transposed = jax.ShapeDtypeStruct((m // T_Q, width, T_Q), BF16)
    return pl.pallas_call(
        _qkv_kernel,
        grid=(m // ROW_TILE,),
        in_specs=[
            pl.BlockSpec((ROW_TILE, D_MODEL), lambda i: (i, 0)),
            pl.BlockSpec((None, 1, D_MODEL), lay),
            pl.BlockSpec((None, D_MODEL, QKV_W), lay, pipeline_mode=pl.Buffered(1)),
            pl.BlockSpec((None, 1, QKV_W), lay),
        ],
        out_specs=[
            pl.BlockSpec((ROW_TILE, width), lambda i: (i, 0)),
            pl.BlockSpec((tiles, width, T_Q), lambda i: (i, 0, 0)),
            pl.BlockSpec((tiles, width, T_Q), lambda i: (i, 0, 0)),
        ],
        out_shape=[jax.ShapeDtypeStruct((m, width), BF16), transposed, transposed],
        compiler_params=pltpu.CompilerParams(
            dimension_semantics=("arbitrary",), vmem_limit_bytes=VMEM_LIMIT),
        name="qkv_proj",
    )(x, g, w, qk_gain)


def _merge_kernel(x_ref, g_ref, oa_ref, ob_ref, oc_ref, wg_ref, bg_ref, wbr_ref, wo_ref, out_ref):
    x = x_ref[...]
    h = _rms(x, g_ref[...]).astype(BF16)
    merged = None
    for br, o_ref in enumerate((oa_ref, ob_ref, oc_ref)):
        g_lin = jnp.dot(h, wg_ref[:, br * D_MODEL:(br + 1) * D_MODEL], preferred_element_type=F32)
        gate = 1.0 / (1.0 + jnp.exp(-(g_lin + bg_ref[br:br + 1, :])))
        proj = jnp.dot(o_ref[...], wbr_ref[br * W_BRANCH:(br + 1) * W_BRANCH, :],
                       preferred_element_type=F32)
        term = gate * proj
        merged = term if merged is None else merged + term
    out_ref[...] = x + jnp.dot(merged.astype(BF16), wo_ref[...], preferred_element_type=F32)


def _merge_out(x, g, o_a, o_b, o_c, w_gate, b_gate, w_br, w_out, layer):
    m = x.shape[0]
    row = lambda i: (i, 0)
    lay = lambda i: (layer, 0, 0)
    return pl.pallas_call(
        _merge_kernel,
        grid=(m // ROW_TILE,),
        in_specs=[
            pl.BlockSpec((ROW_TILE, D_MODEL), row),
            pl.BlockSpec((None, 1, D_MODEL), lay),
            pl.BlockSpec((ROW_TILE, W_BRANCH), row),
            pl.BlockSpec((ROW_TILE, W_BRANCH), row),
            pl.BlockSpec((ROW_TILE, W_BRANCH), row),
            pl.BlockSpec((None, D_MODEL, N_BRANCH * D_MODEL), lay, pipeline_mode=pl.Buffered(1)),
            pl.BlockSpec((None, N_BRANCH, D_MODEL), lay),
            pl.BlockSpec((None, N_BRANCH * W_BRANCH, D_MODEL), lay, pipeline_mode=pl.Buffered(1)),
            pl.BlockSpec((None, D_MODEL, D_MODEL), lay, pipeline_mode=pl.Buffered(1)),
        ],
        out_specs=pl.BlockSpec((ROW_TILE, D_MODEL), row),
        out_shape=jax.ShapeDtypeStruct((m, D_MODEL), F32),
        compiler_params=pltpu.CompilerParams(
            dimension_semantics=("arbitrary",), vmem_limit_bytes=VMEM_LIMIT),
        name="merge_out",
    )(x, g, o_a, o_b, o_c, w_gate, b_gate, w_br, w_out)


def _ffn_kernel(x_ref, g_ref, wgu_ref, wd_ref, out_ref, act_ref):
    x = x_ref[...]
    h = _rms(x, g_ref[...]).astype(BF16)
    for c in range(0, D_FF, FF_CHUNK):
        gate = jnp.dot(h, wgu_ref[:, c:c + FF_CHUNK], preferred_element_type=F32)
        up = jnp.dot(h, wgu_ref[:, D_FF + c:D_FF + c + FF_CHUNK], preferred_element_type=F32)
        silu = gate / (1.0 + jnp.exp(-gate))
        act_ref[:, c:c + FF_CHUNK] = (silu * up).astype(BF16)
    out_ref[...] = x + jnp.dot(act_ref[...], wd_ref[...], preferred_element_type=F32)


def _ffn(x, g, w_gu, w_down, layer):
    m = x.shape[0]
    row = lambda i: (i, 0)
    lay = lambda i: (layer, 0, 0)
    return pl.pallas_call(
        _ffn_kernel,
        grid=(m // ROW_TILE,),
        in_specs=[
            pl.BlockSpec((ROW_TILE, D_MODEL), row),
            pl.BlockSpec((None, 1, D_MODEL), lay),
            pl.BlockSpec((None, D_MODEL, 2 * D_FF), lay, pipeline_mode=pl.Buffered(1)),
            pl.BlockSpec((None, D_FF, D_MODEL), lay, pipeline_mode=pl.Buffered(1)),
        ],
        out_specs=pl.BlockSpec((ROW_TILE, D_MODEL), row),
        out_shape=jax.ShapeDtypeStruct((m, D_MODEL), F32),
        scratch_shapes=[pltpu.VMEM((ROW_TILE, D_FF), BF16)],
        compiler_params=pltpu.CompilerParams(
            dimension_semantics=("arbitrary",), vmem_limit_bytes=VMEM_LIMIT),
        name="ffn",
    )(x, g, w_gu, w_down)


def _sb_weights(z, cum, carry, mask):
    neg_abs = lax.bitcast_convert_type(
        lax.bitcast_convert_type(z, jnp.uint32) | jnp.uint32(0x80000000), F32)
    sp = jnp.log2(1.0 + jnp.exp2(neg_abs))
    log_beta = jnp.minimum(z, 0.0) - sp
    log_1m = log_beta - z
    if mask is not None:
        log_1m = jnp.where(mask, log_1m, 0.0)
    hi = log_1m.astype(BF16)
    lo = (log_1m - hi.astype(F32)).astype(BF16)
    afters = []
    for u in reversed(range(z.shape[0] // T_K)):
        r0, r1 = u * T_K, (u + 1) * T_K
        within = jnp.dot(cum, jnp.concatenate([hi[r0:r1], lo[r0:r1]], axis=0),
                         preferred_element_type=F32)
        afters.append(within + carry)
        carry = carry + within[0:1, :] + log_1m[r0:r0 + 1, :]
    after = afters[0] if len(afters) == 1 else jnp.concatenate(afters[::-1], axis=0)
    w = jnp.exp2(log_beta + after)
    if mask is not None:
        w = jnp.where(mask, w, 0.0)
    return w.astype(BF16), carry


class _StickBreaking:
    HEADS = range(2)
    GROUPS_IN_MAIN = 2

    def __init__(self, qt_ref, k_ref, vt_ref, o_ref, acc_ref, carry_ref):
        self.qt_ref, self.k_ref, self.vt_ref, self.o_ref = qt_ref, k_ref, vt_ref, o_ref
        self.acc_ref, self.carry_ref = acc_ref, carry_ref
        self.n_q = k_ref.shape[0] // T_Q
        self.masks = _head_row_masks()
        kk = lax.broadcasted_iota(jnp.int32, (T_K, T_K), 0)
        kk2 = lax.broadcasted_iota(jnp.int32, (T_K, T_K), 1)
        later = jnp.where(kk2 > kk, 1.0, 0.0).astype(BF16)
        self.cum = jnp.concatenate([later, later], axis=1)
        self.strict = (lax.broadcasted_iota(jnp.int32, (T_G, T_Q), 0)
                       < lax.broadcasted_iota(jnp.int32, (T_G, T_Q), 1))

    def main_groups(self, qi):
        return [g for g in range(qi, qi - self.GROUPS_IN_MAIN, -1) if g >= 0]

    def n_main_items(self):
        return sum(len(self.main_groups(qi)) for qi in range(self.n_q)) * len(self.HEADS)

    def load_q(self, qi):
        return _split_heads(self.qt_ref[qi], *self.masks)

    def logits(self, g, qts):
        k2 = self.k_ref[pl.ds(pl.multiple_of(g * T_G, T_G), T_G), :]
        return [jnp.dot(k2, qts[h], preferred_element_type=F32) for h in self.HEADS]

    def add_group(self, g, h, z, mask, carry, acc):
        w, carry = _sb_weights(z, self.cum, carry, mask)
        vth = self.vt_ref[g][h * HEAD_DIM:(h + 1) * HEAD_DIM, :]
        return carry, acc + jnp.dot(vth, w, preferred_element_type=F32)

    def store(self, qi, accs):
        q0 = pl.multiple_of(qi * T_Q, T_Q)
        self.o_ref[pl.ds(q0, T_Q), :] = jnp.concatenate(accs, axis=0).T.astype(BF16)

    def main(self):
        pairs = [(qi, g) for qi in range(self.n_q) for g in self.main_groups(qi)]
        zs = {}

        def issue(j):
            for qi, g in pairs[j:j + 1]:
                zs[qi, g] = self.logits(g, self.load_q(qi))

        for j in range(SB_LOOKAHEAD):
            issue(j)
        for j, (qi, g) in enumerate(pairs):
            issue(j + SB_LOOKAHEAD)
            if g == qi:
                carries = [jnp.zeros((1, T_Q), F32) for _ in self.HEADS]
                accs = [jnp.zeros((HEAD_DIM, T_Q), F32) for _ in self.HEADS]
            z = zs.pop((qi, g))
            for h in self.HEADS:
                carries[h], accs[h] = self.add_group(
                    g, h, z[h], self.strict if g == qi else None, carries[h], accs[h])
                yield
            if g == self.main_groups(qi)[-1]:
                self.store(qi, accs)
                for h in self.HEADS:
                    self.acc_ref[qi, h] = accs[h]
                    self.carry_ref[qi, h] = carries[h]

    def _live(self, carries):
        return jnp.max(functools.reduce(jnp.maximum, carries)) >= SB_DEAD

    def tails(self):
        tiles = [qi for qi in range(self.n_q) if qi - self.GROUPS_IN_MAIN >= 0]

        def tile_tail(qi):
            qts = self.load_q(qi)

            def live(st):
                g, carries, _ = st
                return (g >= 0) & self._live(carries)

            def body(st):
                g, carries, accs = st
                carries, accs = list(carries), list(accs)
                zs = self.logits(g, qts)
                for h in self.HEADS:
                    carries[h], accs[h] = self.add_group(g, h, zs[h], None, carries[h], accs[h])
                return g - 1, tuple(carries), tuple(accs)

            init = (qi - self.GROUPS_IN_MAIN,
                    tuple(self.carry_ref[qi, h] for h in self.HEADS),
                    tuple(self.acc_ref[qi, h] for h in self.HEADS))
            _, _, accs = lax.while_loop(live, body, init)
            self.store(qi, list(accs))

        def all_tails():
            for qi in tiles:
                tile_tail(qi)

        any_live = self._live([self.carry_ref[qi, h] for qi in tiles for h in self.HEADS])
        lax.cond(any_live, all_tails, lambda: None)


def _exhaust(items):
    for _ in items:
        pass


def _sb_kernel(qt_ref, k_ref, vt_ref, o_ref, acc_ref, carry_ref):
    sb = _StickBreaking(qt_ref, k_ref, vt_ref, o_ref, acc_ref, carry_ref)
    _exhaust(sb.main())
    sb.tails()


def _sb_scratch(seq):
    n_q = seq // T_Q
    return [pltpu.VMEM((n_q, 2, HEAD_DIM, T_Q), F32), pltpu.VMEM((n_q, 2, 1, T_Q), F32)]


def _head_block_specs(branch, seq, index):
    def transposed(*ids):
        bi, hb = index(*ids)
        return bi, 0, branch * (W_BRANCH // LANES) + hb, 0

    def rows(*ids):
        bi, hb = index(*ids)
        return bi, 0, branch * (W_BRANCH // LANES) + hb

    t_spec = pl.BlockSpec((None, seq // T_Q, LANES, T_Q), transposed)
    return [t_spec, pl.BlockSpec((None, seq, LANES), rows), t_spec]


def _sb_attn(qt, k, vt):
    b, s, _ = k.shape
    n_hp = W_BRANCH // LANES
    return pl.pallas_call(
        _sb_kernel,
        grid=(b, n_hp),
        in_specs=_head_block_specs(0, s, lambda bi, hp: (bi, hp)),
        out_specs=pl.BlockSpec((None, s, LANES), lambda bi, hp: (bi, 0, hp)),
        out_shape=jax.ShapeDtypeStruct((b, s, W_BRANCH), BF16),
        scratch_shapes=_sb_scratch(s),
        compiler_params=pltpu.CompilerParams(
            dimension_semantics=("arbitrary", "arbitrary"), vmem_limit_bytes=VMEM_LIMIT),
        name="sb_attn",
    )(qt, k, vt)


def _softmax_update(scores, vts, state):
    m, l, acc = state
    m_new = m
    for s in scores:
        m_new = jnp.maximum(m_new, jnp.max(s, axis=0, keepdims=True))
    alpha = jnp.exp2(m - m_new)
    l = alpha * l
    acc = alpha * acc
    for s, vt in zip(scores, vts):
        p = jnp.exp2(s - m_new)
        l = l + jnp.sum(p, axis=0, keepdims=True)
        acc = acc + jnp.dot(vt, p.astype(BF16), preferred_element_type=F32)
    return m_new, l, acc


def _diff_paths(qt_ref, kn_ref, vt_ref, bias_ref, lam_ref, sg_ref, o_ref, lam_init):
    seq = kn_ref.shape[0]
    n_q = seq // T_Q
    first, second = _head_row_masks()
    lq = lam_ref[...]
    lam = (jnp.exp(jnp.sum(lq[0:1] * lq[1:2], axis=-1, keepdims=True))
           - jnp.exp(jnp.sum(lq[2:3] * lq[3:4], axis=-1, keepdims=True)) + lam_init)

    def keys(g):
        return kn_ref[pl.ds(pl.multiple_of(g * T_G, T_G), T_G), :]

    def scores(qi, g):
        kn2, bias = keys(g), bias_ref[qi - g]
        return [jnp.dot(kn2, qtm, preferred_element_type=F32) + bias
                for qtm in _split_heads(qt_ref[qi], first, second)]

    def finish(qi, accs, sums):
        ob = accs[0] * (1.0 / sums[0]) - lam * (accs[1] * (1.0 / sums[1]))
        y = ob * lax.rsqrt(jnp.mean(ob * ob, axis=0, keepdims=True) + RMS_EPS)
        q0 = pl.multiple_of(qi * T_Q, T_Q)
        o_ref[pl.ds(q0, T_Q), :] = (y.T * sg_ref[...] * (1.0 - lam_init)).astype(BF16)

    def q_pair(qp, _):
        tiles = (2 * qp, 2 * qp + 1)

        def trip(it, states):
            gs = (2 * it, 2 * it + 1)
            vts = [vt_ref[g] for g in gs]
            scs = [[scores(qi, g) for g in gs] for qi in tiles]
            states = [list(st) for st in states]
            for i in range(2):
                for mp in range(2):
                    states[i][mp] = _softmax_update([sc[mp] for sc in scs[i]], vts, states[i][mp])
            return tuple(tuple(st) for st in states)

        init = (jnp.full((1, T_Q), NEG, F32), jnp.zeros((1, T_Q), F32), jnp.zeros((LANES, T_Q), F32))
        states = lax.fori_loop(0, qp, trip, ((init, init), (init, init)))
        gs = (2 * qp, 2 * qp + 1)
        vts = [vt_ref[g] for g in gs]
        scs = [[scores(tiles[0], gs[0])], [scores(tiles[1], g) for g in gs]]
        for i in range(2):
            done = [_softmax_update([sc[mp] for sc in scs[i]], vts, states[i][mp]) for mp in range(2)]
            finish(tiles[i], [st[2] for st in done], [st[1] for st in done])
        return 0

    work = [(qi, g) for qi in range(n_q) for g in range(qi + 1)]
    ones_rows = jnp.ones((ONES_ROWS, T_G), BF16)

    def bounded():
        z_next = scores(*work[0])
        for step, (qi, g) in enumerate(work):
            z = z_next
            if step + 1 < len(work):
                z_next = scores(*work[step + 1])
            vt1 = jnp.concatenate([vt_ref[g], ones_rows], axis=0)
            prods = [jnp.dot(vt1, jnp.exp2(z[mp]).astype(BF16), preferred_element_type=F32)
                     for mp in range(2)]
            pvs = [r[:LANES] for r in prods]
            psums = [r[LANES:LANES + 1] for r in prods]
            if g == 0:
                accs, sums = pvs, psums
            else:
                accs = [a + pv for a, pv in zip(accs, pvs)]
                sums = [s + ps_ for s, ps_ in zip(sums, psums)]
            if g == qi:
                finish(qi, accs, sums)
            yield

    def general():
        lax.fori_loop(0, n_q // 2, q_pair, 0)

    return bounded, len(work), general


def _diff_kernel(qt_ref, kn_ref, vt_ref, bias_ref, gq_ref, gk_ref, lam_ref, sg_ref, o_ref, *, lam_init):
    bounded, _, general = _diff_paths(qt_ref, kn_ref, vt_ref, bias_ref, lam_ref, sg_ref, o_ref, lam_init)
    lax.cond(_qk_logit_bound(gq_ref[...], gk_ref[...]) <= SAFE_LOG2,
             lambda: _exhaust(bounded()), general)


def _alibi_tiles(n_q):
    slopes = np.asarray([2.0 ** (-8.0 * (i + 1) / H_DIFF) for i in range(H_DIFF)], np.float32)
    slope2 = jnp.asarray(slopes * LOG2E, F32)[:, None, None, None]
    kpos = lax.broadcasted_iota(jnp.int32, (T_G, T_Q), 0)
    qpos = lax.broadcasted_iota(jnp.int32, (T_G, T_Q), 1)
    tiles_back = lax.broadcasted_iota(jnp.int32, (n_q, 1, 1), 0)
    dist = (qpos - kpos)[None] + T_G * tiles_back
    bias = -slope2 * jnp.abs(dist).astype(F32)[None]
    visible = (tiles_back > 0) | ((kpos // CHUNK) <= (qpos // CHUNK))[None]
    return jnp.where(visible[None], bias, NEG)


def _diff_attn(qt, k, vt, alibi, gq, gk, lam_qk, subln_g, layer, lam_init):
    b, s, _ = k.shape
    lay = lambda h, bi: (layer, 0, 0)
    return pl.pallas_call(
        functools.partial(_diff_kernel, lam_init=lam_init),
        grid=(H_DIFF, b),
        in_specs=[
            *_head_block_specs(1, s, lambda h, bi: (bi, h)),
            pl.BlockSpec((None, s // T_Q, T_G, T_Q), lambda h, bi: (h, 0, 0, 0)),
            pl.BlockSpec((None, 1, LANES), lay),
            pl.BlockSpec((None, 1, LANES), lay),
            pl.BlockSpec((None, 4, HEAD_DIM), lay),
            pl.BlockSpec((None, 1, LANES), lay),
        ],
        out_specs=pl.BlockSpec((None, s, LANES), lambda h, bi: (bi, 0, h)),
        out_shape=jax.ShapeDtypeStruct((b, s, W_BRANCH), BF16),
        compiler_params=pltpu.CompilerParams(
            dimension_semantics=("arbitrary", "arbitrary"), vmem_limit_bytes=VMEM_LIMIT),
        name="diff_attn",
    )(qt, k, vt, alibi, gq, gk, lam_qk, subln_g)


N_REL_GROUPS = (N_PAST_CHUNKS * CHUNK + T_Q - 1) // T_G + 1


def _chunk_items(qt_ref, kn_ref, vt_ref, tab_ref, o_ref, sc_ref):
    seq = kn_ref.shape[0]
    first, second = _head_row_masks()
    deltas = tuple(range(N_REL_GROUPS - 1, -1, -1))
    ones_rows = jnp.ones((ONES_ROWS, T_G), BF16)

    def score_stage(qi, tile_deltas, slot):
        qts = _split_heads(qt_ref[qi], first, second)
        for dl in tile_deltas:
            k2 = kn_ref[pl.ds(pl.multiple_of((qi - dl) * T_G, T_G), T_G), :]
            for h in range(2):
                sc_ref[slot, h, dl] = (jnp.dot(k2, qts[h], preferred_element_type=F32)
                                       + tab_ref[h, dl])

    def softmax_stage(qi, tile_deltas, slot, fixed_shift):
        outs = []
        for h in range(2):
            scores = [sc_ref[slot, h, dl] for dl in tile_deltas]
            if fixed_shift:
                ps = [jnp.exp2(s) for s in scores]
            else:
                m = functools.reduce(jnp.maximum, [jnp.max(s, axis=0, keepdims=True) for s in scores])
                ps = [jnp.exp2(s - m) for s in scores]
            acc = None
            for dl, p in zip(tile_deltas, ps):
                vth = jnp.concatenate(
                    [vt_ref[qi - dl][h * HEAD_DIM:(h + 1) * HEAD_DIM, :], ones_rows], axis=0)
                pv = jnp.dot(vth, p.astype(BF16), preferred_element_type=F32)
                acc = pv if acc is None else acc + pv
            outs.append(acc[:HEAD_DIM] * (1.0 / acc[HEAD_DIM:HEAD_DIM + 1]))
        q0 = pl.multiple_of(qi * T_Q, T_Q)
        o_ref[pl.ds(q0, T_Q), :] = jnp.concatenate(outs, axis=0).T.astype(BF16)

    def tile_deltas(qi):
        return tuple(dl for dl in deltas if qi - dl >= 0)

    n_q = seq // T_Q

    def run(fixed_shift):
        score_stage(0, tile_deltas(0), 0)
        for qi in range(n_q):
            if qi + 1 < n_q:
                score_stage(qi + 1, tile_deltas(qi + 1), (qi + 1) % 2)
            yield
            softmax_stage(qi, tile_deltas(qi), qi % 2, fixed_shift)
            yield

    return run, 2 * n_q


def _chunk_logit_bound(gq, gk, tabs):
    return _qk_logit_bound(gq, gk) + jnp.max(jnp.where(tabs > 0.5 * NEG, jnp.abs(tabs), 0.0))


def _chunk_kernel(qt_ref, kn_ref, vt_ref, gq_ref, gk_ref, tab_ref, o_ref, sc_ref):
    run, _ = _chunk_items(qt_ref, kn_ref, vt_ref, tab_ref, o_ref, sc_ref)
    bound = _chunk_logit_bound(gq_ref[...], gk_ref[...], tab_ref[...])
    lax.cond(bound <= SAFE_LOG2, lambda: _exhaust(run(True)), lambda: _exhaust(run(False)))


def _chunk_attn(qt, k, vt, gq, gk, tab, layer):
    b, s, _ = k.shape
    n_hp = W_BRANCH // LANES
    lay = lambda hp, bi: (layer, 0, 0)
    return pl.pallas_call(
        _chunk_kernel,
        grid=(n_hp, b),
        in_specs=[
            *_head_block_specs(2, s, lambda hp, bi: (bi, hp)),
            pl.BlockSpec((None, 1, LANES), lay),
            pl.BlockSpec((None, 1, LANES), lay),
            pl.BlockSpec((None, 2, N_REL_GROUPS, T_G, T_Q), lambda hp, bi: (layer, hp, 0, 0, 0)),
        ],
        out_specs=pl.BlockSpec((None, s, LANES), lambda hp, bi: (bi, 0, hp)),
        out_shape=jax.ShapeDtypeStruct((b, s, W_BRANCH), BF16),
        scratch_shapes=[pltpu.VMEM((2, 2, N_REL_GROUPS, T_G, T_Q), F32)],
        compiler_params=pltpu.CompilerParams(
            dimension_semantics=("arbitrary", "arbitrary"), vmem_limit_bytes=VMEM_LIMIT),
        name="chunk_attn",
    )(qt, k, vt, gq, gk, tab)


def _interleave(streams):
    order = sorted(((i + 0.5) / n, s) for s, (_, n) in enumerate(streams) for i in range(n))
    for _, s in order:
        next(streams[s][0], None)
    for items, _ in streams:
        _exhaust(items)


def _mix_kernel(qa_ref, ka_ref, va_ref, qb_ref, kb_ref, vb_ref, qc_ref, kc_ref, vc_ref,
                alibi_ref, lam_ref, sg_ref, tab_ref, oa_ref, ob_ref, oc_ref,
                acc_ref, carry_ref, sc_ref, *, lam_init):
    sb = _StickBreaking(qa_ref, ka_ref, va_ref, oa_ref, acc_ref, carry_ref)
    diff_items, n_diff, _ = _diff_paths(qb_ref, kb_ref, vb_ref, alibi_ref, lam_ref, sg_ref, ob_ref,
                                        lam_init)
    chunk_items, n_chunk = _chunk_items(qc_ref, kc_ref, vc_ref, tab_ref, oc_ref, sc_ref)
    _interleave([(sb.main(), sb.n_main_items()), (diff_items(), n_diff),
                 (chunk_items(True), n_chunk)])
    sb.tails()


def _mix_attn(qt, k, vt, alibi, lam_qk, subln_g, tab, layer, lam_init):
    b, s, _ = k.shape
    n_hb = W_BRANCH // LANES
    index = lambda hb, bi: (bi, hb)
    lay = lambda hb, bi: (layer, 0, 0)
    out_spec = pl.BlockSpec((None, s, LANES), lambda hb, bi: (bi, 0, hb))
    out_shape = jax.ShapeDtypeStruct((b, s, W_BRANCH), BF16)
    return pl.pallas_call(
        functools.partial(_mix_kernel, lam_init=lam_init),
        grid=(n_hb, b),
        in_specs=[
            *_head_block_specs(0, s, index), *_head_block_specs(1, s, index),
            *_head_block_specs(2, s, index),
            pl.BlockSpec((None, s // T_Q, T_G, T_Q), lambda hb, bi: (hb, 0, 0, 0)),
            pl.BlockSpec((None, 4, HEAD_DIM), lay),
            pl.BlockSpec((None, 1, LANES), lay),
            pl.BlockSpec((None, 2, N_REL_GROUPS, T_G, T_Q), lambda hb, bi: (layer, hb, 0, 0, 0)),
        ],
        out_specs=[out_spec, out_spec, out_spec],
        out_shape=[out_shape, out_shape, out_shape],
        scratch_shapes=_sb_scratch(s) + [pltpu.VMEM((2, 2, N_REL_GROUPS, T_G, T_Q), F32)],
        compiler_params=pltpu.CompilerParams(
            dimension_semantics=("arbitrary", "arbitrary"), vmem_limit_bytes=VMEM_LIMIT),
        name="mix_attn",
    )(qt, k, vt, qt, k, vt, qt, k, vt, alibi, lam_qk, subln_g, tab)


def _rel_bias_tiles(rel_bias):
    lead = rel_bias.shape[:-1]
    span = (N_REL_GROUPS - 1) * T_G + T_Q
    period = span + T_G
    edge_lo = jnp.broadcast_to(rel_bias[..., :1], lead + (T_G - REL_CLIP,))
    edge_hi = jnp.broadcast_to(rel_bias[..., -1:], lead + (span - REL_CLIP - 1,))
    row = jnp.concatenate([rel_bias[..., REL_CLIP:], edge_hi, edge_lo, rel_bias[..., :REL_CLIP]], axis=-1)
    flat = jnp.tile(row, (1,) * len(lead) + (T_G,))[..., :T_G * (period - 1)]
    toep = flat.reshape(lead + (T_G, period - 1))
    tiles = jnp.stack([toep[..., d * T_G:d * T_G + T_Q] for d in range(N_REL_GROUPS)], axis=-3)
    kchunk = np.arange(T_G)[:, None] // CHUNK
    qchunk = np.arange(T_Q)[None, :] // CHUNK
    dd = np.stack([qchunk - kchunk + d * (T_G // CHUNK) for d in range(N_REL_GROUPS)])
    return jnp.where((dd >= 0) & (dd <= N_PAST_CHUNKS), tiles * LOG2E, NEG)


def kernel(x, norm_mix_g, w_in, b_gate, qk_g_diff, lambda_qk, subln_g, qk_g_ch, rel_bias,
           w_branch_sb, w_branch_diff, w_branch_ch, w_out, norm_ffn_g, w_gu, w_down):
    b, s, d = x.shape
    m = b * s
    w_qkv = w_in[:, :, :QKV_W].astype(BF16)
    w_gate = w_in[:, :, QKV_W:].astype(BF16)
    w_br = jnp.concatenate([w_branch_sb, w_branch_diff, w_branch_ch], axis=1).astype(BF16)
    w_out_b = w_out.astype(BF16)
    w_gu_b = w_gu.astype(BF16)
    w_down_b = w_down.astype(BF16)
    g_mix = norm_mix_g.reshape(DEPTH, 1, d)
    g_ffn = norm_ffn_g.reshape(DEPTH, 1, d)
    gq_diff = jnp.tile(qk_g_diff[:, 0:1, :], (1, 1, 2))
    gk_diff = jnp.tile(qk_g_diff[:, 1:2, :], (1, 1, 2))
    gq_ch = jnp.tile(qk_g_ch[:, 0:1, :], (1, 1, 2))
    gk_ch = jnp.tile(qk_g_ch[:, 1:2, :], (1, 1, 2))
    sg = subln_g.reshape(DEPTH, 1, 2 * HEAD_DIM)
    tab = _rel_bias_tiles(rel_bias)
    alibi = _alibi_tiles(s // T_Q)
    ones = jnp.ones((DEPTH, 1, W_BRANCH), F32)
    widen = lambda g: jnp.tile(g, (1, 1, W_BRANCH // LANES))
    q_scale = QK_SCALE * LOG2E
    qk_gain = jnp.concatenate([ones, ones, ones, widen(gq_diff) * q_scale, widen(gk_diff), ones,
                               widen(gq_ch) * q_scale, widen(gk_ch), ones], axis=-1)

    xf = x.reshape(m, d)
    for layer in range(DEPTH):
        lam_init = 0.8 - 0.6 * math.exp(-0.3 * layer)
        k, qt, vt = _qkv_proj(xf, g_mix, w_qkv, qk_gain, layer)
        k = k.reshape(b, s, N_BRANCH * W_BRANCH)
        qt = qt.reshape(b, s // T_Q, N_BRANCH * W_BRANCH, T_Q)
        vt = vt.reshape(b, s // T_Q, N_BRANCH * W_BRANCH, T_Q)
        def mixers_fused():
            return tuple(_mix_attn(qt, k, vt, alibi, lambda_qk, sg, tab, layer, lam_init))

        def mixers_separate():
            return (_sb_attn(qt, k, vt),
                    _diff_attn(qt, k, vt, alibi, gq_diff, gk_diff, lambda_qk, sg, layer, lam_init),
                    _chunk_attn(qt, k, vt, gq_ch, gk_ch, tab, layer))

        bounded = ((_qk_logit_bound(gq_diff[layer], gk_diff[layer]) <= SAFE_LOG2)
                   & (_chunk_logit_bound(gq_ch[layer], gk_ch[layer], tab[layer]) <= SAFE_LOG2))
        o_a, o_b, o_c = lax.cond(bounded, mixers_fused, mixers_separate)
        xf = _merge_out(xf, g_mix, o_a.reshape(m, W_BRANCH), o_b.reshape(m, W_BRANCH),
                        o_c.reshape(m, W_BRANCH), w_gate, b_gate, w_br, w_out_b, layer)
        xf = _ffn(xf, g_ffn, w_gu_b, w_down_b, layer)
    return xf.reshape(b, s, d)
```

```python
import functools
import math

import jax
import jax.numpy as jnp
import numpy as np
from jax import lax
from jax.experimental import pallas as pl
from jax.experimental.pallas import tpu as pltpu

F32 = jnp.float32
BF16 = jnp.bfloat16

D_MODEL = 1024
DEPTH = 4
CHUNK = 64
HEAD_DIM = 64
H_DIFF = 4
N_PAST_CHUNKS = 8
REL_CLIP = 128
W_BRANCH = 512
QKV_W = 9 * W_BRANCH
N_BRANCH = 3
D_FF = int(math.ceil(8 * D_MODEL / 3 / 256)) * 256
RMS_EPS = 1e-6
QK_SCALE = HEAD_DIM ** -0.5

LANES = 128
T_Q = 256
T_K = 128
T_G = 256
NEG = -1e30
SB_DEAD = -150.0
ONES_ROWS = 16
SB_LOOKAHEAD = 2
LOG2E = 1.4426950408889634
SAFE_LOG2 = 60.0
VMEM_LIMIT = 56 * 1024 * 1024

ROW_TILE = 1024
FF_CHUNK = 256


def _rms(x, g):
    return x * lax.rsqrt(jnp.mean(x * x, axis=-1, keepdims=True) + RMS_EPS) * g


def _rms_halves(x, g):
    lane = lax.broadcasted_iota(jnp.int32, (1, LANES), 1)
    first = lane < HEAD_DIM
    x2 = x * x
    s0 = jnp.sum(jnp.where(first, x2, 0.0), axis=-1, keepdims=True)
    s1 = jnp.sum(jnp.where(first, 0.0, x2), axis=-1, keepdims=True)
    ms = jnp.where(first, s0, s1) * (1.0 / HEAD_DIM)
    return x * lax.rsqrt(ms + RMS_EPS) * g


def _qk_logit_bound(gq, gk):
    return (1.02 * LOG2E * QK_SCALE * HEAD_DIM) * jnp.max(jnp.abs(gq)) * jnp.max(jnp.abs(gk))


def _head_row_masks():
    row = lax.broadcasted_iota(jnp.int32, (LANES, 1), 0)
    return row < HEAD_DIM, row >= HEAD_DIM


def _split_heads(qt, first, second):
    zero = jnp.zeros_like(qt)
    return jnp.where(first, qt, zero), jnp.where(second, qt, zero)


N_SECTIONS = 9


def _qkv_kernel(x_ref, g_ref, w_ref, qkg_ref, k_out, qt_out, vt_out):
    h = _rms(x_ref[...], g_ref[...]).astype(BF16)

    def project(sec):
        return jnp.dot(h, w_ref[:, sec * W_BRANCH:(sec + 1) * W_BRANCH], preferred_element_type=F32)

    z_next = project(0)
    for sec in range(N_SECTIONS):
        z = z_next
        if sec + 1 < N_SECTIONS:
            z_next = project(sec + 1)
        branch, role = divmod(sec, 3)
        for cb in range(W_BRANCH // LANES):
            blk = z[:, cb * LANES:(cb + 1) * LANES]
            col = sec * W_BRANCH + cb * LANES
            if branch > 0 and role < 2:
                blk = _rms_halves(blk, qkg_ref[:, col:col + LANES])
            elif role == 0:
                blk = blk * (QK_SCALE * LOG2E)
            out_col = branch * W_BRANCH + cb * LANES
            if role == 1:
                k_out[:, out_col:out_col + LANES] = blk.astype(BF16)
            else:
                out = qt_out if role == 0 else vt_out
                for r in range(ROW_TILE // T_Q):
                    out[r, out_col:out_col + LANES, :] = blk[r * T_Q:(r + 1) * T_Q, :].T.astype(BF16)


def _qkv_proj(x, g, w, qk_gain, layer):
    m = x.shape[0]
    width = N_BRANCH * W_BRANCH
    tiles = ROW_TILE // T_Q
    lay = lambda i: (layer, 0, 0)
    transposed = jax.ShapeDtypeStruct((m // T_Q, width, T_Q), BF16)
    return pl.pallas_call(
        _qkv_kernel,
        grid=(m // ROW_TILE,),
        in_specs=[
            pl.BlockSpec((ROW_TILE, D_MODEL), lambda i: (i, 0)),
            pl.BlockSpec((None, 1, D_MODEL), lay),
            pl.BlockSpec((None, D_MODEL, QKV_W), lay, pipeline_mode=pl.Buffered(1)),
            pl.BlockSpec((None, 1, QKV_W), lay),
        ],
        out_specs=[
            pl.BlockSpec((ROW_TILE, width), lambda i: (i, 0)),
            pl.BlockSpec((tiles, width, T_Q), lambda i: (i, 0, 0)),
            pl.BlockSpec((tiles, width, T_Q), lambda i: (i, 0, 0)),
        ],
        out_shape=[jax.ShapeDtypeStruct((m, width), BF16), transposed, transposed],
        compiler_params=pltpu.CompilerParams(
            dimension_semantics=("arbitrary",), vmem_limit_bytes=VMEM_LIMIT),
        name="qkv_proj",
    )(x, g, w, qk_gain)


def _merge_kernel(x_ref, g_ref, oa_ref, ob_ref, oc_ref, wg_ref, bg_ref, wbr_ref, wo_ref, out_ref):
    x = x_ref[...]
    h = _rms(x, g_ref[...]).astype(BF16)
    merged = None
    for br, o_ref in enumerate((oa_ref, ob_ref, oc_ref)):
        g_lin = jnp.dot(h, wg_ref[:, br * D_MODEL:(br + 1) * D_MODEL], preferred_element_type=F32)
        gate = 1.0 / (1.0 + jnp.exp(-(g_lin + bg_ref[br:br + 1, :])))
        proj = jnp.dot(o_ref[...], wbr_ref[br * W_BRANCH:(br + 1) * W_BRANCH, :],
                       preferred_element_type=F32)
        term = gate * proj
        merged = term if merged is None else merged + term
    out_ref[...] = x + jnp.dot(merged.astype(BF16), wo_ref[...], preferred_element_type=F32)


def _merge_out(x, g, o_a, o_b, o_c, w_gate, b_gate, w_br, w_out, layer):
    m = x.shape[0]
    row = lambda i: (i, 0)
    lay = lambda i: (layer, 0, 0)
    return pl.pallas_call(
        _merge_kernel,
        grid=(m // ROW_TILE,),
        in_specs=[
            pl.BlockSpec((ROW_TILE, D_MODEL), row),
            pl.BlockSpec((None, 1, D_MODEL), lay),
            pl.BlockSpec((ROW_TILE, W_BRANCH), row),
            pl.BlockSpec((ROW_TILE, W_BRANCH), row),
            pl.BlockSpec((ROW_TILE, W_BRANCH), row),
            pl.BlockSpec((None, D_MODEL, N_BRANCH * D_MODEL), lay, pipeline_mode=pl.Buffered(1)),
            pl.BlockSpec((None, N_BRANCH, D_MODEL), lay),
            pl.BlockSpec((None, N_BRANCH * W_BRANCH, D_MODEL), lay, pipeline_mode=pl.Buffered(1)),
            pl.BlockSpec((None, D_MODEL, D_MODEL), lay, pipeline_mode=pl.Buffered(1)),
        ],
        out_specs=pl.BlockSpec((ROW_TILE, D_MODEL), row),
        out_shape=jax.ShapeDtypeStruct((m, D_MODEL), F32),
        compiler_params=pltpu.CompilerParams(
            dimension_semantics=("arbitrary",), vmem_limit_bytes=VMEM_LIMIT),
        name="merge_out",
    )(x, g, o_a, o_b, o_c, w_gate, b_gate, w_br, w_out)


def _ffn_kernel(x_ref, g_ref, wgu_ref, wd_ref, out_ref, act_ref):
    x = x_ref[...]
    h = _rms(x, g_ref[...]).astype(BF16)
    for c in range(0, D_FF, FF_CHUNK):
        gate = jnp.dot(h, wgu_ref[:, c:c + FF_CHUNK], preferred_element_type=F32)
        up = jnp.dot(h, wgu_ref[:, D_FF + c:D_FF + c + FF_CHUNK], preferred_element_type=F32)
        silu = gate / (1.0 + jnp.exp(-gate))
        act_ref[:, c:c + FF_CHUNK] = (silu * up).astype(BF16)
    out_ref[...] = x + jnp.dot(act_ref[...], wd_ref[...], preferred_element_type=F32)


def _ffn(x, g, w_gu, w_down, layer):
    m = x.shape[0]
    row = lambda i: (i, 0)
    lay = lambda i: (layer, 0, 0)
    return pl.pallas_call(
        _ffn_kernel,
        grid=(m // ROW_TILE,),
        in_specs=[
            pl.BlockSpec((ROW_TILE, D_MODEL), row),
            pl.BlockSpec((None, 1, D_MODEL), lay),
            pl.BlockSpec((None, D_MODEL, 2 * D_FF), lay, pipeline_mode=pl.Buffered(1)),
            pl.BlockSpec((None, D_FF, D_MODEL), lay, pipeline_mode=pl.Buffered(1)),
        ],
        out_specs=pl.BlockSpec((ROW_TILE, D_MODEL), row),
        out_shape=jax.ShapeDtypeStruct((m, D_MODEL), F32),
        scratch_shapes=[pltpu.VMEM((ROW_TILE, D_FF), BF16)],
        compiler_params=pltpu.CompilerParams(
            dimension_semantics=("arbitrary",), vmem_limit_bytes=VMEM_LIMIT),
        name="ffn",
    )(x, g, w_gu, w_down)


def _sb_weights(z, cum, carry, mask):
    neg_abs = lax.bitcast_convert_type(
        lax.bitcast_convert_type(z, jnp.uint32) | jnp.uint32(0x80000000), F32)
    sp = jnp.log2(1.0 + jnp.exp2(neg_abs))
    log_beta = jnp.minimum(z, 0.0) - sp
    log_1m = log_beta - z
    if mask is not None:
        log_1m = jnp.where(mask, log_1m, 0.0)
    hi = log_1m.astype(BF16)
    lo = (log_1m - hi.astype(F32)).astype(BF16)
    afters = []
    for u in reversed(range(z.shape[0] // T_K)):
        r0, r1 = u * T_K, (u + 1) * T_K
        within = jnp.dot(cum, jnp.concatenate([hi[r0:r1], lo[r0:r1]], axis=0),
                         preferred_element_type=F32)
        afters.append(within + carry)
        carry = carry + within[0:1, :] + log_1m[r0:r0 + 1, :]
    after = afters[0] if len(afters) == 1 else jnp.concatenate(afters[::-1], axis=0)
    w = jnp.exp2(log_beta + after)
    if mask is not None:
        w = jnp.where(mask, w, 0.0)
    return w.astype(BF16), carry


class _StickBreaking:
    HEADS = range(2)
    GROUPS_IN_MAIN = 2

    def __init__(self, qt_ref, k_ref, vt_ref, o_ref, acc_ref, carry_ref):
        self.qt_ref, self.k_ref, self.vt_ref, self.o_ref = qt_ref, k_ref, vt_ref, o_ref
        self.acc_ref, self.carry_ref = acc_ref, carry_ref
        self.n_q = k_ref.shape[0] // T_Q
        self.masks = _head_row_masks()
        kk = lax.broadcasted_iota(jnp.int32, (T_K, T_K), 0)
        kk2 = lax.broadcasted_iota(jnp.int32, (T_K, T_K), 1)
        later = jnp.where(kk2 > kk, 1.0, 0.0).astype(BF16)
        self.cum = jnp.concatenate([later, later], axis=1)
        self.strict = (lax.broadcasted_iota(jnp.int32, (T_G, T_Q), 0)
                       < lax.broadcasted_iota(jnp.int32, (T_G, T_Q), 1))

    def main_groups(self, qi):
        return [g for g in range(qi, qi - self.GROUPS_IN_MAIN, -1) if g >= 0]

    def n_main_items(self):
        return sum(len(self.main_groups(qi)) for qi in range(self.n_q)) * len(self.HEADS)

    def load_q(self, qi):
        return _split_heads(self.qt_ref[qi], *self.masks)

    def logits(self, g, qts):
        k2 = self.k_ref[pl.ds(pl.multiple_of(g * T_G, T_G), T_G), :]
        return [jnp.dot(k2, qts[h], preferred_element_type=F32) for h in self.HEADS]

    def add_group(self, g, h, z, mask, carry, acc):
        w, carry = _sb_weights(z, self.cum, carry, mask)
        vth = self.vt_ref[g][h * HEAD_DIM:(h + 1) * HEAD_DIM, :]
        return carry, acc + jnp.dot(vth, w, preferred_element_type=F32)

    def store(self, qi, accs):
        q0 = pl.multiple_of(qi * T_Q, T_Q)
        self.o_ref[pl.ds(q0, T_Q), :] = jnp.concatenate(accs, axis=0).T.astype(BF16)

    def main(self):
        pairs = [(qi, g) for qi in range(self.n_q) for g in self.main_groups(qi)]
        zs = {}

        def issue(j):
            for qi, g in pairs[j:j + 1]:
                zs[qi, g] = self.logits(g, self.load_q(qi))

        for j in range(SB_LOOKAHEAD):
            issue(j)
        for j, (qi, g) in enumerate(pairs):
            issue(j + SB_LOOKAHEAD)
            if g == qi:
                carries = [jnp.zeros((1, T_Q), F32) for _ in self.HEADS]
                accs = [jnp.zeros((HEAD_DIM, T_Q), F32) for _ in self.HEADS]
            z = zs.pop((qi, g))
            for h in self.HEADS:
                carries[h], accs[h] = self.add_group(
                    g, h, z[h], self.strict if g == qi else None, carries[h], accs[h])
                yield
            if g == self.main_groups(qi)[-1]:
                self.store(qi, accs)
                for h in self.HEADS:
                    self.acc_ref[qi, h] = accs[h]
                    self.carry_ref[qi, h] = carries[h]

    def _live(self, carries):
        return jnp.max(functools.reduce(jnp.maximum, carries)) >= SB_DEAD

    def tails(self):
        tiles = [qi for qi in range(self.n_q) if qi - self.GROUPS_IN_MAIN >= 0]

        def tile_tail(qi):
            qts = self.load_q(qi)

            def live(st):
                g, carries, _ = st
                return (g >= 0) & self._live(carries)

            def body(st):
                g, carries, accs = st
                carries, accs = list(carries), list(accs)
                zs = self.logits(g, qts)
                for h in self.HEADS:
                    carries[h], accs[h] = self.add_group(g, h, zs[h], None, carries[h], accs[h])
                return g - 1, tuple(carries), tuple(accs)

            init = (qi - self.GROUPS_IN_MAIN,
                    tuple(self.carry_ref[qi, h] for h in self.HEADS),
                    tuple(self.acc_ref[qi, h] for h in self.HEADS))
            _, _, accs = lax.while_loop(live, body, init)
            self.store(qi, list(accs))

        def all_tails():
            for qi in tiles:
                tile_tail(qi)

        any_live = self._live([self.carry_ref[qi, h] for qi in tiles for h in self.HEADS])
        lax.cond(any_live, all_tails, lambda: None)


def _exhaust(items):
    for _ in items:
        pass


def _sb_scratch(seq):
    n_q = seq // T_Q
    return [pltpu.VMEM((n_q, 2, HEAD_DIM, T_Q), F32), pltpu.VMEM((n_q, 2, 1, T_Q), F32)]


def _head_block_specs(branch, seq, index):
    def transposed(*ids):
        bi, hb = index(*ids)
        return bi, 0, branch * (W_BRANCH // LANES) + hb, 0

    def rows(*ids):
        bi, hb = index(*ids)
        return bi, 0, branch * (W_BRANCH // LANES) + hb

    t_spec = pl.BlockSpec((None, seq // T_Q, LANES, T_Q), transposed)
    return [t_spec, pl.BlockSpec((None, seq, LANES), rows), t_spec]


def _softmax_update(scores, vts, state):
    m, l, acc = state
    m_new = m
    for s in scores:
        m_new = jnp.maximum(m_new, jnp.max(s, axis=0, keepdims=True))
    alpha = jnp.exp2(m - m_new)
    l = alpha * l
    acc = alpha * acc
    for s, vt in zip(scores, vts):
        p = jnp.exp2(s - m_new)
        l = l + jnp.sum(p, axis=0, keepdims=True)
        acc = acc + jnp.dot(vt, p.astype(BF16), preferred_element_type=F32)
    return m_new, l, acc


def _diff_paths(qt_ref, kn_ref, vt_ref, bias_ref, lam_ref, sg_ref, o_ref, lam_init):
    seq = kn_ref.shape[0]
    n_q = seq // T_Q
    first, second = _head_row_masks()
    lq = lam_ref[...]
    lam = (jnp.exp(jnp.sum(lq[0:1] * lq[1:2], axis=-1, keepdims=True))
           - jnp.exp(jnp.sum(lq[2:3] * lq[3:4], axis=-1, keepdims=True)) + lam_init)

    def keys(g):
        return kn_ref[pl.ds(pl.multiple_of(g * T_G, T_G), T_G), :]

    def scores(qi, g):
        kn2, bias = keys(g), bias_ref[qi - g]
        return [jnp.dot(kn2, qtm, preferred_element_type=F32) + bias
                for qtm in _split_heads(qt_ref[qi], first, second)]

    def finish(qi, accs, sums):
        ob = accs[0] * (1.0 / sums[0]) - lam * (accs[1] * (1.0 / sums[1]))
        y = ob * lax.rsqrt(jnp.mean(ob * ob, axis=0, keepdims=True) + RMS_EPS)
        q0 = pl.multiple_of(qi * T_Q, T_Q)
        o_ref[pl.ds(q0, T_Q), :] = (y.T * sg_ref[...] * (1.0 - lam_init)).astype(BF16)

    def q_pair(qp, _):
        tiles = (2 * qp, 2 * qp + 1)

        def trip(it, states):
            gs = (2 * it, 2 * it + 1)
            vts = [vt_ref[g] for g in gs]
            scs = [[scores(qi, g) for g in gs] for qi in tiles]
            states = [list(st) for st in states]
            for i in range(2):
                for mp in range(2):
                    states[i][mp] = _softmax_update([sc[mp] for sc in scs[i]], vts, states[i][mp])
            return tuple(tuple(st) for st in states)

        init = (jnp.full((1, T_Q), NEG, F32), jnp.zeros((1, T_Q), F32), jnp.zeros((LANES, T_Q), F32))
        states = lax.fori_loop(0, qp, trip, ((init, init), (init, init)))
        gs = (2 * qp, 2 * qp + 1)
        vts = [vt_ref[g] for g in gs]
        scs = [[scores(tiles[0], gs[0])], [scores(tiles[1], g) for g in gs]]
        for i in range(2):
            done = [_softmax_update([sc[mp] for sc in scs[i]], vts, states[i][mp]) for mp in range(2)]
            finish(tiles[i], [st[2] for st in done], [st[1] for st in done])
        return 0

    work = [(qi, g) for qi in range(n_q) for g in range(qi + 1)]
    ones_rows = jnp.ones((ONES_ROWS, T_G), BF16)

    def bounded():
        z_next = scores(*work[0])
        for step, (qi, g) in enumerate(work):
            z = z_next
            if step + 1 < len(work):
                z_next = scores(*work[step + 1])
            vt1 = jnp.concatenate([vt_ref[g], ones_rows], axis=0)
            prods = [jnp.dot(vt1, jnp.exp2(z[mp]).astype(BF16), preferred_element_type=F32)
                     for mp in range(2)]
            pvs = [r[:LANES] for r in prods]
            psums = [r[LANES:LANES + 1] for r in prods]
            if g == 0:
                accs, sums = pvs, psums
            else:
                accs = [a + pv for a, pv in zip(accs, pvs)]
                sums = [s + ps_ for s, ps_ in zip(sums, psums)]
            if g == qi:
                finish(qi, accs, sums)
            yield

    def general():
        lax.fori_loop(0, n_q // 2, q_pair, 0)

    return bounded, len(work), general


def _alibi_tiles(n_q):
    slopes = np.asarray([2.0 ** (-8.0 * (i + 1) / H_DIFF) for i in range(H_DIFF)], np.float32)
    slope2 = jnp.asarray(slopes * LOG2E, F32)[:, None, None, None]
    kpos = lax.broadcasted_iota(jnp.int32, (T_G, T_Q), 0)
    qpos = lax.broadcasted_iota(jnp.int32, (T_G, T_Q), 1)
    tiles_back = lax.broadcasted_iota(jnp.int32, (n_q, 1, 1), 0)
    dist = (qpos - kpos)[None] + T_G * tiles_back
    bias = -slope2 * jnp.abs(dist).astype(F32)[None]
    visible = (tiles_back > 0) | ((kpos // CHUNK) <= (qpos // CHUNK))[None]
    return jnp.where(visible[None], bias, NEG)


N_REL_GROUPS = (N_PAST_CHUNKS * CHUNK + T_Q - 1) // T_G + 1


def _chunk_items(qt_ref, kn_ref, vt_ref, tab_ref, o_ref, sc_ref):
    seq = kn_ref.shape[0]
    first, second = _head_row_masks()
    deltas = tuple(range(N_REL_GROUPS - 1, -1, -1))
    ones_rows = jnp.ones((ONES_ROWS, T_G), BF16)

    def score_stage(qi, tile_deltas, slot):
        qts = _split_heads(qt_ref[qi], first, second)
        for dl in tile_deltas:
            k2 = kn_ref[pl.ds(pl.multiple_of((qi - dl) * T_G, T_G), T_G), :]
            for h in range(2):
                sc_ref[slot, h, dl] = (jnp.dot(k2, qts[h], preferred_element_type=F32)
                                       + tab_ref[h, dl])

    def softmax_stage(qi, tile_deltas, slot, fixed_shift):
        outs = []
        for h in range(2):
            scores = [sc_ref[slot, h, dl] for dl in tile_deltas]
            if fixed_shift:
                ps = [jnp.exp2(s) for s in scores]
            else:
                m = functools.reduce(jnp.maximum, [jnp.max(s, axis=0, keepdims=True) for s in scores])
                ps = [jnp.exp2(s - m) for s in scores]
            acc = None
            for dl, p in zip(tile_deltas, ps):
                vth = jnp.concatenate(
                    [vt_ref[qi - dl][h * HEAD_DIM:(h + 1) * HEAD_DIM, :], ones_rows], axis=0)
                pv = jnp.dot(vth, p.astype(BF16), preferred_element_type=F32)
                acc = pv if acc is None else acc + pv
            outs.append(acc[:HEAD_DIM] * (1.0 / acc[HEAD_DIM:HEAD_DIM + 1]))
        q0 = pl.multiple_of(qi * T_Q, T_Q)
        o_ref[pl.ds(q0, T_Q), :] = jnp.concatenate(outs, axis=0).T.astype(BF16)

    def tile_deltas(qi):
        return tuple(dl for dl in deltas if qi - dl >= 0)

    n_q = seq // T_Q

    def run(fixed_shift):
        score_stage(0, tile_deltas(0), 0)
        for qi in range(n_q):
            if qi + 1 < n_q:
                score_stage(qi + 1, tile_deltas(qi + 1), (qi + 1) % 2)
            yield
            softmax_stage(qi, tile_deltas(qi), qi % 2, fixed_shift)
            yield

    return run, 2 * n_q


def _chunk_logit_bound(gq, gk, tabs):
    return _qk_logit_bound(gq, gk) + jnp.max(jnp.where(tabs > 0.5 * NEG, jnp.abs(tabs), 0.0))


def _interleave(streams):
    order = sorted(((i + 0.5) / n, s) for s, (_, n) in enumerate(streams) for i in range(n))
    for _, s in order:
        next(streams[s][0], None)
    for items, _ in streams:
        _exhaust(items)


def _mix_kernel(qa_ref, ka_ref, va_ref, qb_ref, kb_ref, vb_ref, qc_ref, kc_ref, vc_ref,
                alibi_ref, lam_ref, sg_ref, tab_ref, gqb_ref, gkb_ref, gqc_ref, gkc_ref,
                oa_ref, ob_ref, oc_ref, acc_ref, carry_ref, sc_ref, *, lam_init):
    sb = _StickBreaking(qa_ref, ka_ref, va_ref, oa_ref, acc_ref, carry_ref)
    diff_items, n_diff, diff_online = _diff_paths(qb_ref, kb_ref, vb_ref, alibi_ref, lam_ref, sg_ref,
                                                  ob_ref, lam_init)
    chunk_items, n_chunk = _chunk_items(qc_ref, kc_ref, vc_ref, tab_ref, oc_ref, sc_ref)

    def interleaved():
        _interleave([(sb.main(), sb.n_main_items()), (diff_items(), n_diff),
                     (chunk_items(True), n_chunk)])

    def one_by_one():
        _exhaust(sb.main())
        diff_online()
        _exhaust(chunk_items(False))

    bounded = ((_qk_logit_bound(gqb_ref[...], gkb_ref[...]) <= SAFE_LOG2)
               & (_chunk_logit_bound(gqc_ref[...], gkc_ref[...], tab_ref[...]) <= SAFE_LOG2))
    lax.cond(bounded, interleaved, one_by_one)
    sb.tails()


def _mix_attn(qt, k, vt, alibi, lam_qk, subln_g, tab, gains, layer, lam_init):
    b, s, _ = k.shape
    n_hb = W_BRANCH // LANES
    index = lambda hb, bi: (bi, hb)
    lay = lambda hb, bi: (layer, 0, 0)
    out_spec = pl.BlockSpec((None, s, LANES), lambda hb, bi: (bi, 0, hb))
    out_shape = jax.ShapeDtypeStruct((b, s, W_BRANCH), BF16)
    return pl.pallas_call(
        functools.partial(_mix_kernel, lam_init=lam_init),
        grid=(n_hb, b),
        in_specs=[
            *_head_block_specs(0, s, index), *_head_block_specs(1, s, index),
            *_head_block_specs(2, s, index),
            pl.BlockSpec((None, s // T_Q, T_G, T_Q), lambda hb, bi: (hb, 0, 0, 0)),
            pl.BlockSpec((None, 4, HEAD_DIM), lay),
            pl.BlockSpec((None, 1, LANES), lay),
            pl.BlockSpec((None, 2, N_REL_GROUPS, T_G, T_Q), lambda hb, bi: (layer, hb, 0, 0, 0)),
            *[pl.BlockSpec((None, 1, LANES), lay) for _ in gains],
        ],
        out_specs=[out_spec, out_spec, out_spec],
        out_shape=[out_shape, out_shape, out_shape],
        scratch_shapes=_sb_scratch(s) + [pltpu.VMEM((2, 2, N_REL_GROUPS, T_G, T_Q), F32)],
        compiler_params=pltpu.CompilerParams(
            dimension_semantics=("arbitrary", "arbitrary"), vmem_limit_bytes=VMEM_LIMIT),
        name="mix_attn",
    )(qt, k, vt, qt, k, vt, qt, k, vt, alibi, lam_qk, subln_g, tab, *gains)


def _rel_bias_tiles(rel_bias):
    lead = rel_bias.shape[:-1]
    span = (N_REL_GROUPS - 1) * T_G + T_Q
    period = span + T_G
    edge_lo = jnp.broadcast_to(rel_bias[..., :1], lead + (T_G - REL_CLIP,))
    edge_hi = jnp.broadcast_to(rel_bias[..., -1:], lead + (span - REL_CLIP - 1,))
    row = jnp.concatenate([rel_bias[..., REL_CLIP:], edge_hi, edge_lo, rel_bias[..., :REL_CLIP]], axis=-1)
    flat = jnp.tile(row, (1,) * len(lead) + (T_G,))[..., :T_G * (period - 1)]
    toep = flat.reshape(lead + (T_G, period - 1))
    tiles = jnp.stack([toep[..., d * T_G:d * T_G + T_Q] for d in range(N_REL_GROUPS)], axis=-3)
    kchunk = np.arange(T_G)[:, None] // CHUNK
    qchunk = np.arange(T_Q)[None, :] // CHUNK
    dd = np.stack([qchunk - kchunk + d * (T_G // CHUNK) for d in range(N_REL_GROUPS)])
    return jnp.where((dd >= 0) & (dd <= N_PAST_CHUNKS), tiles * LOG2E, NEG)


def kernel(x, norm_mix_g, w_in, b_gate, qk_g_diff, lambda_qk, subln_g, qk_g_ch, rel_bias,
           w_branch_sb, w_branch_diff, w_branch_ch, w_out, norm_ffn_g, w_gu, w_down):
    b, s, d = x.shape
    m = b * s
    w_qkv = w_in[:, :, :QKV_W].astype(BF16)
    w_gate = w_in[:, :, QKV_W:].astype(BF16)
    w_br = jnp.concatenate([w_branch_sb, w_branch_diff, w_branch_ch], axis=1).astype(BF16)
    w_out_b = w_out.astype(BF16)
    w_gu_b = w_gu.astype(BF16)
    w_down_b = w_down.astype(BF16)
    g_mix = norm_mix_g.reshape(DEPTH, 1, d)
    g_ffn = norm_ffn_g.reshape(DEPTH, 1, d)
    gq_diff = jnp.tile(qk_g_diff[:, 0:1, :], (1, 1, 2))
    gk_diff = jnp.tile(qk_g_diff[:, 1:2, :], (1, 1, 2))
    gq_ch = jnp.tile(qk_g_ch[:, 0:1, :], (1, 1, 2))
    gk_ch = jnp.tile(qk_g_ch[:, 1:2, :], (1, 1, 2))
    sg = subln_g.reshape(DEPTH, 1, 2 * HEAD_DIM)
    tab = _rel_bias_tiles(rel_bias)
    alibi = _alibi_tiles(s // T_Q)
    ones = jnp.ones((DEPTH, 1, W_BRANCH), F32)
    widen = lambda g: jnp.tile(g, (1, 1, W_BRANCH // LANES))
    q_scale = QK_SCALE * LOG2E
    qk_gain = jnp.concatenate([ones, ones, ones, widen(gq_diff) * q_scale, widen(gk_diff), ones,
                               widen(gq_ch) * q_scale, widen(gk_ch), ones], axis=-1)

    xf = x.reshape(m, d)
    for layer in range(DEPTH):
        lam_init = 0.8 - 0.6 * math.exp(-0.3 * layer)
        k, qt, vt = _qkv_proj(xf, g_mix, w_qkv, qk_gain, layer)
        k = k.reshape(b, s, N_BRANCH * W_BRANCH)
        qt = qt.reshape(b, s // T_Q, N_BRANCH * W_BRANCH, T_Q)
        vt = vt.reshape(b, s // T_Q, N_BRANCH * W_BRANCH, T_Q)
        o_a, o_b, o_c = _mix_attn(qt, k, vt, alibi, lambda_qk, sg, tab,
                                  (gq_diff, gk_diff, gq_ch, gk_ch), layer, lam_init)
        xf = _merge_out(xf, g_mix, o_a.reshape(m, W_BRANCH), o_b.reshape(m, W_BRANCH),
                        o_c.reshape(m, W_BRANCH), w_gate, b_gate, w_br, w_out_b, layer)
        xf = _ffn(xf, g_ffn, w_gu_b, w_down_b, layer)
    return xf.reshape(b, s, d)
```

```python
import functools
import math

import jax
import jax.numpy as jnp
import numpy as np
from jax import lax
from jax.experimental import pallas as pl
from jax.experimental.pallas import tpu as pltpu

F32 = jnp.float32
BF16 = jnp.bfloat16

D_MODEL = 1024
DEPTH = 4
CHUNK = 64
HEAD_DIM = 64
H_DIFF = 4
N_PAST_CHUNKS = 8
REL_CLIP = 128
W_BRANCH = 512
QKV_W = 9 * W_BRANCH
N_BRANCH = 3
D_FF = int(math.ceil(8 * D_MODEL / 3 / 256)) * 256
RMS_EPS = 1e-6
QK_SCALE = HEAD_DIM ** -0.5

LANES = 128
T_Q = 256
T_K = 128
T_G = 256
NEG = -1e30
SB_DEAD = -150.0
ONES_ROWS = 16
SB_LOOKAHEAD = 2
LOG2E = 1.4426950408889634
SAFE_LOG2 = 60.0
VMEM_LIMIT = 56 * 1024 * 1024

ROW_TILE = 1024
FF_CHUNK = 256


def _rms(x, g):
    return x * lax.rsqrt(jnp.mean(x * x, axis=-1, keepdims=True) + RMS_EPS) * g


def _rms_halves(x, g):
    lane = lax.broadcasted_iota(jnp.int32, (1, LANES), 1)
    first = lane < HEAD_DIM
    x2 = x * x
    s0 = jnp.sum(jnp.where(first, x2, 0.0), axis=-1, keepdims=True)
    s1 = jnp.sum(jnp.where(first, 0.0, x2), axis=-1, keepdims=True)
    ms = jnp.where(first, s0, s1) * (1.0 / HEAD_DIM)
    return x * lax.rsqrt(ms + RMS_EPS) * g


def _qk_logit_bound(gq, gk):
    return (1.02 * LOG2E * QK_SCALE * HEAD_DIM) * jnp.max(jnp.abs(gq)) * jnp.max(jnp.abs(gk))


def _head_row_masks():
    row = lax.broadcasted_iota(jnp.int32, (LANES, 1), 0)
    return row < HEAD_DIM, row >= HEAD_DIM


def _split_heads(qt, first, second):
    zero = jnp.zeros_like(qt)
    return jnp.where(first, qt, zero), jnp.where(second, qt, zero)


N_SECTIONS = 9


def _qkv_kernel(x_ref, g_ref, w_ref, qkg_ref, k_out, qt_out, vt_out):
    h = _rms(x_ref[...], g_ref[...]).astype(BF16)

    def project(sec):
        return jnp.dot(h, w_ref[:, sec * W_BRANCH:(sec + 1) * W_BRANCH], preferred_element_type=F32)

    z_next = project(0)
    for sec in range(N_SECTIONS):
        z = z_next
        if sec + 1 < N_SECTIONS:
            z_next = project(sec + 1)
        branch, role = divmod(sec, 3)
        for cb in range(W_BRANCH // LANES):
            blk = z[:, cb * LANES:(cb + 1) * LANES]
            col = sec * W_BRANCH + cb * LANES
            if branch > 0 and role < 2:
                blk = _rms_halves(blk, qkg_ref[:, col:col + LANES])
            elif role == 0:
                blk = blk * (QK_SCALE * LOG2E)
            out_col = branch * W_BRANCH + cb * LANES
            if role == 1:
                k_out[:, out_col:out_col + LANES] = blk.astype(BF16)
            else:
                out = qt_out if role == 0 else vt_out
                for r in range(ROW_TILE // T_Q):
                    out[r, out_col:out_col + LANES, :] = blk[r * T_Q:(r + 1) * T_Q, :].T.astype(BF16)


def _qkv_proj(x, g, w, qk_gain, layer):
    m = x.shape[0]
    width = N_BRANCH * W_BRANCH
    tiles = ROW_TILE // T_Q
    lay = lambda i: (layer, 0, 0)
    transposed = jax.ShapeDtypeStruct((m // T_Q, width, T_Q), BF16)
    return pl.pallas_call(
        _qkv_kernel,
        grid=(m // ROW_TILE,),
        in_specs=[
            pl.BlockSpec((ROW_TILE, D_MODEL), lambda i: (i, 0)),
            pl.BlockSpec((None, 1, D_MODEL), lay),
            pl.BlockSpec((None, D_MODEL, QKV_W), lay, pipeline_mode=pl.Buffered(1)),
            pl.BlockSpec((None, 1, QKV_W), lay),
        ],
        out_specs=[
            pl.BlockSpec((ROW_TILE, width), lambda i: (i, 0)),
            pl.BlockSpec((tiles, width, T_Q), lambda i: (i, 0, 0)),
            pl.BlockSpec((tiles, width, T_Q), lambda i: (i, 0, 0)),
        ],
        out_shape=[jax.ShapeDtypeStruct((m, width), BF16), transposed, transposed],
        compiler_params=pltpu.CompilerParams(
            dimension_semantics=("arbitrary",), vmem_limit_bytes=VMEM_LIMIT),
        name="qkv_proj",
    )(x, g, w, qk_gain)


def _merge_kernel(x_ref, g_ref, oa_ref, ob_ref, oc_ref, wg_ref, bg_ref, wbr_ref, wo_ref, out_ref):
    x = x_ref[...]
    h = _rms(x, g_ref[...]).astype(BF16)
    merged = None
    for br, o_ref in enumerate((oa_ref, ob_ref, oc_ref)):
        g_lin = jnp.dot(h, wg_ref[:, br * D_MODEL:(br + 1) * D_MODEL], preferred_element_type=F32)
        gate = 1.0 / (1.0 + jnp.exp(-(g_lin + bg_ref[br:br + 1, :])))
        proj = jnp.dot(o_ref[...], wbr_ref[br * W_BRANCH:(br + 1) * W_BRANCH, :],
                       preferred_element_type=F32)
        term = gate * proj
        merged = term if merged is None else merged + term
    out_ref[...] = x + jnp.dot(merged.astype(BF16), wo_ref[...], preferred_element_type=F32)


def _merge_out(x, g, o_a, o_b, o_c, w_gate, b_gate, w_br, w_out, layer):
    m = x.shape[0]
    row = lambda i: (i, 0)
    lay = lambda i: (layer, 0, 0)
    return pl.pallas_call(
        _merge_kernel,
        grid=(m // ROW_TILE,),
        in_specs=[
            pl.BlockSpec((ROW_TILE, D_MODEL), row),
            pl.BlockSpec((None, 1, D_MODEL), lay),
            pl.BlockSpec((ROW_TILE, W_BRANCH), row),
            pl.BlockSpec((ROW_TILE, W_BRANCH), row),
            pl.BlockSpec((ROW_TILE, W_BRANCH), row),
            pl.BlockSpec((None, D_MODEL, N_BRANCH * D_MODEL), lay, pipeline_mode=pl.Buffered(1)),
            pl.BlockSpec((None, N_BRANCH, D_MODEL), lay),
            pl.BlockSpec((None, N_BRANCH * W_BRANCH, D_MODEL), lay, pipeline_mode=pl.Buffered(1)),
            pl.BlockSpec((None, D_MODEL, D_MODEL), lay, pipeline_mode=pl.Buffered(1)),
        ],
        out_specs=pl.BlockSpec((ROW_TILE, D_MODEL), row),
        out_shape=jax.ShapeDtypeStruct((m, D_MODEL), F32),
        compiler_params=pltpu.CompilerParams(
            dimension_semantics=("arbitrary",), vmem_limit_bytes=VMEM_LIMIT),
        name="merge_out",
    )(x, g, o_a, o_b, o_c, w_gate, b_gate, w_br, w_out)


def _ffn_kernel(x_ref, g_ref, wgu_ref, wd_ref, out_ref, act_ref):
    x = x_ref[...]
    h = _rms(x, g_ref[...]).astype(BF16)
    for c in range(0, D_FF, FF_CHUNK):
        gate = jnp.dot(h, wgu_ref[:, c:c + FF_CHUNK], preferred_element_type=F32)
        up = jnp.dot(h, wgu_ref[:, D_FF + c:D_FF + c + FF_CHUNK], preferred_element_type=F32)
        silu = gate / (1.0 + jnp.exp(-gate))
        act_ref[:, c:c + FF_CHUNK] = (silu * up).astype(BF16)
    out_ref[...] = x + jnp.dot(act_ref[...], wd_ref[...], preferred_element_type=F32)


def _ffn(x, g, w_gu, w_down, layer):
    m = x.shape[0]
    row = lambda i: (i, 0)
    lay = lambda i: (layer, 0, 0)
    return pl.pallas_call(
        _ffn_kernel,
        grid=(m // ROW_TILE,),
        in_specs=[
            pl.BlockSpec((ROW_TILE, D_MODEL), row),
            pl.BlockSpec((None, 1, D_MODEL), lay),
            pl.BlockSpec((None, D_MODEL, 2 * D_FF), lay, pipeline_mode=pl.Buffered(1)),
            pl.BlockSpec((None, D_FF, D_MODEL), lay, pipeline_mode=pl.Buffered(1)),
        ],
        out_specs=pl.BlockSpec((ROW_TILE, D_MODEL), row),
        out_shape=jax.ShapeDtypeStruct((m, D_MODEL), F32),
        scratch_shapes=[pltpu.VMEM((ROW_TILE, D_FF), BF16)],
        compiler_params=pltpu.CompilerParams(
            dimension_semantics=("arbitrary",), vmem_limit_bytes=VMEM_LIMIT),
        name="ffn",
    )(x, g, w_gu, w_down)


def _sb_weights(z, cum, carry, mask):
    neg_abs = lax.bitcast_convert_type(
        lax.bitcast_convert_type(z, jnp.uint32) | jnp.uint32(0x80000000), F32)
    sp = jnp.log2(1.0 + jnp.exp2(neg_abs))
    log_beta = jnp.minimum(z, 0.0) - sp
    log_1m = log_beta - z
    if mask is not None:
        log_1m = jnp.where(mask, log_1m, 0.0)
    hi = log_1m.astype(BF16)
    lo = (log_1m - hi.astype(F32)).astype(BF16)
    afters = []
    for u in reversed(range(z.shape[0] // T_K)):
        r0, r1 = u * T_K, (u + 1) * T_K
        within = jnp.dot(cum, jnp.concatenate([hi[r0:r1], lo[r0:r1]], axis=0),
                         preferred_element_type=F32)
        afters.append(within + carry)
        carry = carry + within[0:1, :] + log_1m[r0:r0 + 1, :]
    after = afters[0] if len(afters) == 1 else jnp.concatenate(afters[::-1], axis=0)
    w = jnp.exp2(log_beta + after)
    if mask is not None:
        w = jnp.where(mask, w, 0.0)
    return w.astype(BF16), carry


class _StickBreaking:
    HEADS = range(2)
    GROUPS_IN_MAIN = 2

    def __init__(self, qt_ref, k_ref, vt_ref, o_ref, acc_ref, carry_ref):
        self.qt_ref, self.k_ref, self.vt_ref, self.o_ref = qt_ref, k_ref, vt_ref, o_ref
        self.acc_ref, self.carry_ref = acc_ref, carry_ref
        self.n_q = k_ref.shape[0] // T_Q
        self.masks = _head_row_masks()
        kk = lax.broadcasted_iota(jnp.int32, (T_K, T_K), 0)
        kk2 = lax.broadcasted_iota(jnp.int32, (T_K, T_K), 1)
        later = jnp.where(kk2 > kk, 1.0, 0.0).astype(BF16)
        self.cum = jnp.concatenate([later, later], axis=1)
        self.strict = (lax.broadcasted_iota(jnp.int32, (T_G, T_Q), 0)
                       < lax.broadcasted_iota(jnp.int32, (T_G, T_Q), 1))

    def main_groups(self, qi):
        return [g for g in range(qi, qi - self.GROUPS_IN_MAIN, -1) if g >= 0]

    def n_main_items(self):
        return sum(len(self.main_groups(qi)) for qi in range(self.n_q)) * len(self.HEADS)

    def load_q(self, qi):
        return _split_heads(self.qt_ref[qi], *self.masks)

    def logits(self, g, qts):
        k2 = self.k_ref[pl.ds(pl.multiple_of(g * T_G, T_G), T_G), :]
        return [jnp.dot(k2, qts[h], preferred_element_type=F32) for h in self.HEADS]

    def add_group(self, g, h, z, mask, carry, acc):
        w, carry = _sb_weights(z, self.cum, carry, mask)
        vth = self.vt_ref[g][h * HEAD_DIM:(h + 1) * HEAD_DIM, :]
        return carry, acc + jnp.dot(vth, w, preferred_element_type=F32)

    def store(self, qi, accs):
        q0 = pl.multiple_of(qi * T_Q, T_Q)
        self.o_ref[pl.ds(q0, T_Q), :] = jnp.concatenate(accs, axis=0).T.astype(BF16)

    def main(self):
        pairs = [(qi, g) for qi in range(self.n_q) for g in self.main_groups(qi)]
        zs = {}

        def issue(j):
            for qi, g in pairs[j:j + 1]:
                zs[qi, g] = self.logits(g, self.load_q(qi))

        for j in range(SB_LOOKAHEAD):
            issue(j)
        for j, (qi, g) in enumerate(pairs):
            issue(j + SB_LOOKAHEAD)
            if g == qi:
                carries = [jnp.zeros((1, T_Q), F32) for _ in self.HEADS]
                accs = [jnp.zeros((HEAD_DIM, T_Q), F32) for _ in self.HEADS]
            z = zs.pop((qi, g))
            for h in self.HEADS:
                carries[h], accs[h] = self.add_group(
                    g, h, z[h], self.strict if g == qi else None, carries[h], accs[h])
                yield
            if g == self.main_groups(qi)[-1]:
                self.store(qi, accs)
                for h in self.HEADS:
                    self.acc_ref[qi, h] = accs[h]
                    self.carry_ref[qi, h] = carries[h]

    def _live(self, carries):
        return jnp.max(functools.reduce(jnp.maximum, carries)) >= SB_DEAD

    def tails(self):
        tiles = [qi for qi in range(self.n_q) if qi - self.GROUPS_IN_MAIN >= 0]

        def tile_tail(qi):
            qts = self.load_q(qi)

            def live(st):
                g, carries, _ = st
                return (g >= 0) & self._live(carries)

            def body(st):
                g, carries, accs = st
                carries, accs = list(carries), list(accs)
                zs = self.logits(g, qts)
                for h in self.HEADS:
                    carries[h], accs[h] = self.add_group(g, h, zs[h], None, carries[h], accs[h])
                return g - 1, tuple(carries), tuple(accs)

            init = (qi - self.GROUPS_IN_MAIN,
                    tuple(self.carry_ref[qi, h] for h in self.HEADS),
                    tuple(self.acc_ref[qi, h] for h in self.HEADS))
            _, _, accs = lax.while_loop(live, body, init)
            self.store(qi, list(accs))

        def all_tails():
            for qi in tiles:
                tile_tail(qi)

        any_live = self._live([self.carry_ref[qi, h] for qi in tiles for h in self.HEADS])
        lax.cond(any_live, all_tails, lambda: None)


def _exhaust(items):
    for _ in items:
        pass


def _sb_scratch(seq):
    n_q = seq // T_Q
    return [pltpu.VMEM((n_q, 2, HEAD_DIM, T_Q), F32), pltpu.VMEM((n_q, 2, 1, T_Q), F32)]


def _head_block_specs(branch, seq, index):
    def transposed(*ids):
        bi, hb = index(*ids)
        return bi, 0, branch * (W_BRANCH // LANES) + hb, 0

    def rows(*ids):
        bi, hb = index(*ids)
        return bi, 0, branch * (W_BRANCH // LANES) + hb

    t_spec = pl.BlockSpec((None, seq // T_Q, LANES, T_Q), transposed)
    return [t_spec, pl.BlockSpec((None, seq, LANES), rows), t_spec]


def _softmax_update(scores, vts, state):
    m, l, acc = state
    m_new = m
    for s in scores:
        m_new = jnp.maximum(m_new, jnp.max(s, axis=0, keepdims=True))
    alpha = jnp.exp2(m - m_new)
    l = alpha * l
    acc = alpha * acc
    for s, vt in zip(scores, vts):
        p = jnp.exp2(s - m_new)
        l = l + jnp.sum(p, axis=0, keepdims=True)
        acc = acc + jnp.dot(vt, p.astype(BF16), preferred_element_type=F32)
    return m_new, l, acc


def _diff_paths(qt_ref, kn_ref, vt_ref, bias_ref, lam_ref, sg_ref, o_ref, lam_init):
    seq = kn_ref.shape[0]
    n_q = seq // T_Q
    first, second = _head_row_masks()
    lq = lam_ref[...]
    lam = (jnp.exp(jnp.sum(lq[0:1] * lq[1:2], axis=-1, keepdims=True))
           - jnp.exp(jnp.sum(lq[2:3] * lq[3:4], axis=-1, keepdims=True)) + lam_init)

    def keys(g):
        return kn_ref[pl.ds(pl.multiple_of(g * T_G, T_G), T_G), :]

    def scores(qi, g):
        kn2, bias = keys(g), bias_ref[qi - g]
        return [jnp.dot(kn2, qtm, preferred_element_type=F32) + bias
                for qtm in _split_heads(qt_ref[qi], first, second)]

    def finish(qi, accs, sums):
        ob = accs[0] * (1.0 / sums[0]) - lam * (accs[1] * (1.0 / sums[1]))
        y = ob * lax.rsqrt(jnp.mean(ob * ob, axis=0, keepdims=True) + RMS_EPS)
        q0 = pl.multiple_of(qi * T_Q, T_Q)
        o_ref[pl.ds(q0, T_Q), :] = (y.T * sg_ref[...] * (1.0 - lam_init)).astype(BF16)

    def q_pair(qp, _):
        tiles = (2 * qp, 2 * qp + 1)

        def trip(it, states):
            gs = (2 * it, 2 * it + 1)
            vts = [vt_ref[g] for g in gs]
            scs = [[scores(qi, g) for g in gs] for qi in tiles]
            states = [list(st) for st in states]
            for i in range(2):
                for mp in range(2):
                    states[i][mp] = _softmax_update([sc[mp] for sc in scs[i]], vts, states[i][mp])
            return tuple(tuple(st) for st in states)

        init = (jnp.full((1, T_Q), NEG, F32), jnp.zeros((1, T_Q), F32), jnp.zeros((LANES, T_Q), F32))
        states = lax.fori_loop(0, qp, trip, ((init, init), (init, init)))
        gs = (2 * qp, 2 * qp + 1)
        vts = [vt_ref[g] for g in gs]
        scs = [[scores(tiles[0], gs[0])], [scores(tiles[1], g) for g in gs]]
        for i in range(2):
            done = [_softmax_update([sc[mp] for sc in scs[i]], vts, states[i][mp]) for mp in range(2)]
            finish(tiles[i], [st[2] for st in done], [st[1] for st in done])
        return 0

    work = [(qi, g) for qi in range(n_q) for g in range(qi + 1)]
    ones_rows = jnp.ones((ONES_ROWS, T_G), BF16)

    def bounded():
        z_next = scores(*work[0])
        for step, (qi, g) in enumerate(work):
            z = z_next
            if step + 1 < len(work):
                z_next = scores(*work[step + 1])
            vt1 = jnp.concatenate([vt_ref[g], ones_rows], axis=0)
            prods = [jnp.dot(vt1, jnp.exp2(z[mp]).astype(BF16), preferred_element_type=F32)
                     for mp in range(2)]
            pvs = [r[:LANES] for r in prods]
            psums = [r[LANES:LANES + 1] for r in prods]
            if g == 0:
                accs, sums = pvs, psums
            else:
                accs = [a + pv for a, pv in zip(accs, pvs)]
                sums = [s + ps_ for s, ps_ in zip(sums, psums)]
            if g == qi:
                finish(qi, accs, sums)
            yield

    def general():
        lax.fori_loop(0, n_q // 2, q_pair, 0)

    return bounded, len(work), general


def _alibi_tiles(n_q):
    slopes = np.asarray([2.0 ** (-8.0 * (i + 1) / H_DIFF) for i in range(H_DIFF)], np.float32)
    slope2 = jnp.asarray(slopes * LOG2E, F32)[:, None, None, None]
    kpos = lax.broadcasted_iota(jnp.int32, (T_G, T_Q), 0)
    qpos = lax.broadcasted_iota(jnp.int32, (T_G, T_Q), 1)
    tiles_back = lax.broadcasted_iota(jnp.int32, (n_q, 1, 1), 0)
    dist = (qpos - kpos)[None] + T_G * tiles_back
    bias = -slope2 * jnp.abs(dist).astype(F32)[None]
    visible = (tiles_back > 0) | ((kpos // CHUNK) <= (qpos // CHUNK))[None]
    return jnp.where(visible[None], bias, NEG)


N_REL_GROUPS = (N_PAST_CHUNKS * CHUNK + T_Q - 1) // T_G + 1


def _chunk_items(qt_ref, kn_ref, vt_ref, tab_ref, o_ref, sc_ref):
    seq = kn_ref.shape[0]
    first, second = _head_row_masks()
    deltas = tuple(range(N_REL_GROUPS - 1, -1, -1))
    ones_rows = jnp.ones((ONES_ROWS, T_G), BF16)

    def score_stage(qi, tile_deltas, slot):
        qts = _split_heads(qt_ref[qi], first, second)
        for dl in tile_deltas:
            k2 = kn_ref[pl.ds(pl.multiple_of((qi - dl) * T_G, T_G), T_G), :]
            for h in range(2):
                sc_ref[slot, h, dl] = (jnp.dot(k2, qts[h], preferred_element_type=F32)
                                       + tab_ref[h, dl])

    def softmax_stage(qi, tile_deltas, slot, fixed_shift):
        outs = []
        for h in range(2):
            scores = [sc_ref[slot, h, dl] for dl in tile_deltas]
            if fixed_shift:
                ps = [jnp.exp2(s) for s in scores]
            else:
                m = functools.reduce(jnp.maximum, [jnp.max(s, axis=0, keepdims=True) for s in scores])
                ps = [jnp.exp2(s - m) for s in scores]
            acc = None
            for dl, p in zip(tile_deltas, ps):
                vth = jnp.concatenate(
                    [vt_ref[qi - dl][h * HEAD_DIM:(h + 1) * HEAD_DIM, :], ones_rows], axis=0)
                pv = jnp.dot(vth, p.astype(BF16), preferred_element_type=F32)
                acc = pv if acc is None else acc + pv
            outs.append(acc[:HEAD_DIM] * (1.0 / acc[HEAD_DIM:HEAD_DIM + 1]))
        q0 = pl.multiple_of(qi * T_Q, T_Q)
        o_ref[pl.ds(q0, T_Q), :] = jnp.concatenate(outs, axis=0).T.astype(BF16)

    def tile_deltas(qi):
        return tuple(dl for dl in deltas if qi - dl >= 0)

    n_q = seq // T_Q

    def run(fixed_shift):
        score_stage(0, tile_deltas(0), 0)
        for qi in range(n_q):
            if qi + 1 < n_q:
                score_stage(qi + 1, tile_deltas(qi + 1), (qi + 1) % 2)
            yield
            softmax_stage(qi, tile_deltas(qi), qi % 2, fixed_shift)
            yield

    return run, 2 * n_q


def _chunk_logit_bound(gq, gk, tabs):
    return _qk_logit_bound(gq, gk) + jnp.max(jnp.where(tabs > 0.5 * NEG, jnp.abs(tabs), 0.0))


def _interleave(streams):
    order = sorted(((i + 0.5) / n, s) for s, (_, n) in enumerate(streams) for i in range(n))
    for _, s in order:
        next(streams[s][0], None)
    for items, _ in streams:
        _exhaust(items)


def _mix_kernel(bounded_ref, qa_ref, ka_ref, va_ref, qb_ref, kb_ref, vb_ref, qc_ref, kc_ref, vc_ref,
                alibi_ref, lam_ref, sg_ref, tab_ref,
                oa_ref, ob_ref, oc_ref, acc_ref, carry_ref, sc_ref, *, layer, lam_init):
    sb = _StickBreaking(qa_ref, ka_ref, va_ref, oa_ref, acc_ref, carry_ref)
    diff_items, n_diff, diff_online = _diff_paths(qb_ref, kb_ref, vb_ref, alibi_ref, lam_ref, sg_ref,
                                                  ob_ref, lam_init)
    chunk_items, n_chunk = _chunk_items(qc_ref, kc_ref, vc_ref, tab_ref, oc_ref, sc_ref)

    def interleaved():
        _interleave([(sb.main(), sb.n_main_items()), (diff_items(), n_diff),
                     (chunk_items(True), n_chunk)])

    def one_by_one():
        _exhaust(sb.main())
        diff_online()
        _exhaust(chunk_items(False))

    lax.cond(bounded_ref[layer] != 0, interleaved, one_by_one)
    sb.tails()


def _logits_bounded(gq_diff, gk_diff, gq_ch, gk_ch, tab):
    flags = [(_qk_logit_bound(gq_diff[l], gk_diff[l]) <= SAFE_LOG2)
             & (_chunk_logit_bound(gq_ch[l], gk_ch[l], tab[l]) <= SAFE_LOG2) for l in range(DEPTH)]
    return jnp.stack(flags).astype(jnp.int32)


def _mix_attn(qt, k, vt, alibi, lam_qk, subln_g, tab, bounded, layer, lam_init):
    b, s, _ = k.shape
    n_hb = W_BRANCH // LANES
    index = lambda hb, bi: (bi, hb)
    lay = lambda hb, bi: (layer, 0, 0)
    out_spec = pl.BlockSpec((None, s, LANES), lambda hb, bi: (bi, 0, hb))
    out_shape = jax.ShapeDtypeStruct((b, s, W_BRANCH), BF16)
    return pl.pallas_call(
        functools.partial(_mix_kernel, layer=layer, lam_init=lam_init),
        grid=(n_hb, b),
        in_specs=[
            pl.BlockSpec(memory_space=pltpu.SMEM),
            *_head_block_specs(0, s, index), *_head_block_specs(1, s, index),
            *_head_block_specs(2, s, index),
            pl.BlockSpec((None, s // T_Q, T_G, T_Q), lambda hb, bi: (hb, 0, 0, 0)),
            pl.BlockSpec((None, 4, HEAD_DIM), lay),
            pl.BlockSpec((None, 1, LANES), lay),
            pl.BlockSpec((None, 2, N_REL_GROUPS, T_G, T_Q), lambda hb, bi: (layer, hb, 0, 0, 0)),
        ],
        out_specs=[out_spec, out_spec, out_spec],
        out_shape=[out_shape, out_shape, out_shape],
        scratch_shapes=_sb_scratch(s) + [pltpu.VMEM((2, 2, N_REL_GROUPS, T_G, T_Q), F32)],
        compiler_params=pltpu.CompilerParams(
            dimension_semantics=("arbitrary", "arbitrary"), vmem_limit_bytes=VMEM_LIMIT),
        name="mix_attn",
    )(bounded, qt, k, vt, qt, k, vt, qt, k, vt, alibi, lam_qk, subln_g, tab)


def _rel_bias_tiles(rel_bias):
    lead = rel_bias.shape[:-1]
    span = (N_REL_GROUPS - 1) * T_G + T_Q
    period = span + T_G
    edge_lo = jnp.broadcast_to(rel_bias[..., :1], lead + (T_G - REL_CLIP,))
    edge_hi = jnp.broadcast_to(rel_bias[..., -1:], lead + (span - REL_CLIP - 1,))
    row = jnp.concatenate([rel_bias[..., REL_CLIP:], edge_hi, edge_lo, rel_bias[..., :REL_CLIP]], axis=-1)
    flat = jnp.tile(row, (1,) * len(lead) + (T_G,))[..., :T_G * (period - 1)]
    toep = flat.reshape(lead + (T_G, period - 1))
    tiles = jnp.stack([toep[..., d * T_G:d * T_G + T_Q] for d in range(N_REL_GROUPS)], axis=-3)
    kchunk = np.arange(T_G)[:, None] // CHUNK
    qchunk = np.arange(T_Q)[None, :] // CHUNK
    dd = np.stack([qchunk - kchunk + d * (T_G // CHUNK) for d in range(N_REL_GROUPS)])
    return jnp.where((dd >= 0) & (dd <= N_PAST_CHUNKS), tiles * LOG2E, NEG)


def kernel(x, norm_mix_g, w_in, b_gate, qk_g_diff, lambda_qk, subln_g, qk_g_ch, rel_bias,
           w_branch_sb, w_branch_diff, w_branch_ch, w_out, norm_ffn_g, w_gu, w_down):
    b, s, d = x.shape
    m = b * s
    w_qkv = w_in[:, :, :QKV_W].astype(BF16)
    w_gate = w_in[:, :, QKV_W:].astype(BF16)
    w_br = jnp.concatenate([w_branch_sb, w_branch_diff, w_branch_ch], axis=1).astype(BF16)
    w_out_b = w_out.astype(BF16)
    w_gu_b = w_gu.astype(BF16)
    w_down_b = w_down.astype(BF16)
    g_mix = norm_mix_g.reshape(DEPTH, 1, d)
    g_ffn = norm_ffn_g.reshape(DEPTH, 1, d)
    gq_diff = jnp.tile(qk_g_diff[:, 0:1, :], (1, 1, 2))
    gk_diff = jnp.tile(qk_g_diff[:, 1:2, :], (1, 1, 2))
    gq_ch = jnp.tile(qk_g_ch[:, 0:1, :], (1, 1, 2))
    gk_ch = jnp.tile(qk_g_ch[:, 1:2, :], (1, 1, 2))
    sg = subln_g.reshape(DEPTH, 1, 2 * HEAD_DIM)
    tab = _rel_bias_tiles(rel_bias)
    alibi = _alibi_tiles(s // T_Q)
    bounded = _logits_bounded(gq_diff, gk_diff, gq_ch, gk_ch, tab)
    ones = jnp.ones((DEPTH, 1, W_BRANCH), F32)
    widen = lambda g: jnp.tile(g, (1, 1, W_BRANCH // LANES))
    q_scale = QK_SCALE * LOG2E
    qk_gain = jnp.concatenate([ones, ones, ones, widen(gq_diff) * q_scale, widen(gk_diff), ones,
                               widen(gq_ch) * q_scale, widen(gk_ch), ones], axis=-1)

    xf = x.reshape(m, d)
    for layer in range(DEPTH):
        lam_init = 0.8 - 0.6 * math.exp(-0.3 * layer)
        k, qt, vt = _qkv_proj(xf, g_mix, w_qkv, qk_gain, layer)
        k = k.reshape(b, s, N_BRANCH * W_BRANCH)
        qt = qt.reshape(b, s // T_Q, N_BRANCH * W_BRANCH, T_Q)
        vt = vt.reshape(b, s // T_Q, N_BRANCH * W_BRANCH, T_Q)
        o_a, o_b, o_c = _mix_attn(qt, k, vt, alibi, lambda_qk, sg, tab, bounded, layer, lam_init)
        xf = _merge_out(xf, g_mix, o_a.reshape(m, W_BRANCH), o_b.reshape(m, W_BRANCH),
                        o_c.reshape(m, W_BRANCH), w_gate, b_gate, w_br, w_out_b, layer)
        xf = _ffn(xf, g_ffn, w_gu_b, w_down_b, layer)
    return xf.reshape(b, s, d)
```

```python
import functools
import math

import jax
import jax.numpy as jnp
import numpy as np
from jax import lax
from jax.experimental import pallas as pl
from jax.experimental.pallas import tpu as pltpu

F32 = jnp.float32
BF16 = jnp.bfloat16

D_MODEL = 1024
DEPTH = 4
CHUNK = 64
HEAD_DIM = 64
H_DIFF = 4
N_PAST_CHUNKS = 8
REL_CLIP = 128
W_BRANCH = 512
QKV_W = 9 * W_BRANCH
N_BRANCH = 3
D_FF = int(math.ceil(8 * D_MODEL / 3 / 256)) * 256
RMS_EPS = 1e-6
QK_SCALE = HEAD_DIM ** -0.5

LANES = 128
T_Q = 256
T_K = 128
T_G = 256
NEG = -1e30
SB_DEAD = -150.0
ONES_ROWS = 16
SB_LOOKAHEAD = 2
LOG2E = 1.4426950408889634
SAFE_LOG2 = 60.0
VMEM_LIMIT = 56 * 1024 * 1024

ROW_TILE = 1024
FF_CHUNK = 256


def _rms(x, g):
    return x * lax.rsqrt(jnp.mean(x * x, axis=-1, keepdims=True) + RMS_EPS) * g


def _rms_halves(x, g):
    lane = lax.broadcasted_iota(jnp.int32, (1, LANES), 1)
    first = lane < HEAD_DIM
    x2 = x * x
    s0 = jnp.sum(jnp.where(first, x2, 0.0), axis=-1, keepdims=True)
    s1 = jnp.sum(jnp.where(first, 0.0, x2), axis=-1, keepdims=True)
    ms = jnp.where(first, s0, s1) * (1.0 / HEAD_DIM)
    return x * lax.rsqrt(ms + RMS_EPS) * g


def _qk_logit_bound(gq, gk):
    return (1.02 * LOG2E * QK_SCALE * HEAD_DIM) * jnp.max(jnp.abs(gq)) * jnp.max(jnp.abs(gk))


def _head_row_masks():
    row = lax.broadcasted_iota(jnp.int32, (LANES, 1), 0)
    return row < HEAD_DIM, row >= HEAD_DIM


def _split_heads(qt, first, second):
    zero = jnp.zeros_like(qt)
    return jnp.where(first, qt, zero), jnp.where(second, qt, zero)


N_SECTIONS = 9


def _qkv_kernel(x_ref, g_ref, w_ref, qkg_ref, k_out, qt_out, vt_out):
    h = _rms(x_ref[...], g_ref[...]).astype(BF16)

    def project(sec):
        return jnp.dot(h, w_ref[:, sec * W_BRANCH:(sec + 1) * W_BRANCH], preferred_element_type=F32)

    z_next = project(0)
    for sec in range(N_SECTIONS):
        z = z_next
        if sec + 1 < N_SECTIONS:
            z_next = project(sec + 1)
        branch, role = divmod(sec, 3)
        for cb in range(W_BRANCH // LANES):
            blk = z[:, cb * LANES:(cb + 1) * LANES]
            col = sec * W_BRANCH + cb * LANES
            if branch > 0 and role < 2:
                blk = _rms_halves(blk, qkg_ref[:, col:col + LANES])
            elif role == 0:
                blk = blk * (QK_SCALE * LOG2E)
            out_col = branch * W_BRANCH + cb * LANES
            if role == 1:
                k_out[:, out_col:out_col + LANES] = blk.astype(BF16)
            else:
                out = qt_out if role == 0 else vt_out
                for r in range(ROW_TILE // T_Q):
                    out[r, out_col:out_col + LANES, :] = blk[r * T_Q:(r + 1) * T_Q, :].T.astype(BF16)


def _qkv_proj(x, g, w, qk_gain, layer):
    m = x.shape[0]
    width = N_BRANCH * W_BRANCH
    tiles = ROW_TILE // T_Q
    lay = lambda i: (layer, 0, 0)
    transposed = jax.ShapeDtypeStruct((m // T_Q, width, T_Q), BF16)
    return pl.pallas_call(
        _qkv_kernel,
        grid=(m // ROW_TILE,),
        in_specs=[
            pl.BlockSpec((ROW_TILE, D_MODEL), lambda i: (i, 0)),
            pl.BlockSpec((None, 1, D_MODEL), lay),
            pl.BlockSpec((None, D_MODEL, QKV_W), lay, pipeline_mode=pl.Buffered(1)),
            pl.BlockSpec((None, 1, QKV_W), lay),
        ],
        out_specs=[
            pl.BlockSpec((ROW_TILE, width), lambda i: (i, 0)),
            pl.BlockSpec((tiles, width, T_Q), lambda i: (i, 0, 0)),
            pl.BlockSpec((tiles, width, T_Q), lambda i: (i, 0, 0)),
        ],
        out_shape=[jax.ShapeDtypeStruct((m, width), BF16), transposed, transposed],
        compiler_params=pltpu.CompilerParams(
            dimension_semantics=("arbitrary",), vmem_limit_bytes=VMEM_LIMIT),
        name="qkv_proj",
    )(x, g, w, qk_gain)


def _merge_kernel(x_ref, g_ref, oa_ref, ob_ref, oc_ref, wg_ref, bg_ref, wbr_ref, wo_ref, out_ref):
    x = x_ref[...]
    h = _rms(x, g_ref[...]).astype(BF16)
    o_refs = (oa_ref, ob_ref, oc_ref)

    def branch_dots(br):
        return (jnp.dot(h, wg_ref[:, br * D_MODEL:(br + 1) * D_MODEL], preferred_element_type=F32),
                jnp.dot(o_refs[br][...], wbr_ref[br * W_BRANCH:(br + 1) * W_BRANCH, :],
                        preferred_element_type=F32))

    merged = None
    nxt = branch_dots(0)
    for br in range(N_BRANCH):
        g_lin, proj = nxt
        if br + 1 < N_BRANCH:
            nxt = branch_dots(br + 1)
        gate = 1.0 / (1.0 + jnp.exp(-(g_lin + bg_ref[br:br + 1, :])))
        term = gate * proj
        merged = term if merged is None else merged + term
    out_ref[...] = x + jnp.dot(merged.astype(BF16), wo_ref[...], preferred_element_type=F32)


def _merge_out(x, g, o_a, o_b, o_c, w_gate, b_gate, w_br, w_out, layer):
    m = x.shape[0]
    row = lambda i: (i, 0)
    lay = lambda i: (layer, 0, 0)
    return pl.pallas_call(
        _merge_kernel,
        grid=(m // ROW_TILE,),
        in_specs=[
            pl.BlockSpec((ROW_TILE, D_MODEL), row),
            pl.BlockSpec((None, 1, D_MODEL), lay),
            pl.BlockSpec((ROW_TILE, W_BRANCH), row),
            pl.BlockSpec((ROW_TILE, W_BRANCH), row),
            pl.BlockSpec((ROW_TILE, W_BRANCH), row),
            pl.BlockSpec((None, D_MODEL, N_BRANCH * D_MODEL), lay, pipeline_mode=pl.Buffered(1)),
            pl.BlockSpec((None, N_BRANCH, D_MODEL), lay),
            pl.BlockSpec((None, N_BRANCH * W_BRANCH, D_MODEL), lay, pipeline_mode=pl.Buffered(1)),
            pl.BlockSpec((None, D_MODEL, D_MODEL), lay, pipeline_mode=pl.Buffered(1)),
        ],
        out_specs=pl.BlockSpec((ROW_TILE, D_MODEL), row),
        out_shape=jax.ShapeDtypeStruct((m, D_MODEL), F32),
        compiler_params=pltpu.CompilerParams(
            dimension_semantics=("arbitrary",), vmem_limit_bytes=VMEM_LIMIT),
        name="merge_out",
    )(x, g, o_a, o_b, o_c, w_gate, b_gate, w_br, w_out)


def _ffn_kernel(x_ref, g_ref, wgu_ref, wd_ref, out_ref, act_ref):
    x = x_ref[...]
    h = _rms(x, g_ref[...]).astype(BF16)
    for c in range(0, D_FF, FF_CHUNK):
        gate = jnp.dot(h, wgu_ref[:, c:c + FF_CHUNK], preferred_element_type=F32)
        up = jnp.dot(h, wgu_ref[:, D_FF + c:D_FF + c + FF_CHUNK], preferred_element_type=F32)
        silu = gate / (1.0 + jnp.exp(-gate))
        act_ref[:, c:c + FF_CHUNK] = (silu * up).astype(BF16)
    out_ref[...] = x + jnp.dot(act_ref[...], wd_ref[...], preferred_element_type=F32)


def _ffn(x, g, w_gu, w_down, layer):
    m = x.shape[0]
    row = lambda i: (i, 0)
    lay = lambda i: (layer, 0, 0)
    return pl.pallas_call(
        _ffn_kernel,
        grid=(m // ROW_TILE,),
        in_specs=[
            pl.BlockSpec((ROW_TILE, D_MODEL), row),
            pl.BlockSpec((None, 1, D_MODEL), lay),
            pl.BlockSpec((None, D_MODEL, 2 * D_FF), lay, pipeline_mode=pl.Buffered(1)),
            pl.BlockSpec((None, D_FF, D_MODEL), lay, pipeline_mode=pl.Buffered(1)),
        ],
        out_specs=pl.BlockSpec((ROW_TILE, D_MODEL), row),
        out_shape=jax.ShapeDtypeStruct((m, D_MODEL), F32),
        scratch_shapes=[pltpu.VMEM((ROW_TILE, D_FF), BF16)],
        compiler_params=pltpu.CompilerParams(
            dimension_semantics=("arbitrary",), vmem_limit_bytes=VMEM_LIMIT),
        name="ffn",
    )(x, g, w_gu, w_down)


def _sb_weights(z, cum, carry, mask):
    neg_abs = lax.bitcast_convert_type(
        lax.bitcast_convert_type(z, jnp.uint32) | jnp.uint32(0x80000000), F32)
    sp = jnp.log2(1.0 + jnp.exp2(neg_abs))
    log_beta = jnp.minimum(z, 0.0) - sp
    log_1m = log_beta - z
    if mask is not None:
        log_1m = jnp.where(mask, log_1m, 0.0)
    hi = log_1m.astype(BF16)
    lo = (log_1m - hi.astype(F32)).astype(BF16)
    afters = []
    for u in reversed(range(z.shape[0] // T_K)):
        r0, r1 = u * T_K, (u + 1) * T_K
        within = jnp.dot(cum, jnp.concatenate([hi[r0:r1], lo[r0:r1]], axis=0),
                         preferred_element_type=F32)
        afters.append(within + carry)
        carry = carry + within[0:1, :] + log_1m[r0:r0 + 1, :]
    after = afters[0] if len(afters) == 1 else jnp.concatenate(afters[::-1], axis=0)
    w = jnp.exp2(log_beta + after)
    if mask is not None:
        w = jnp.where(mask, w, 0.0)
    return w.astype(BF16), carry


class _StickBreaking:
    HEADS = range(2)
    GROUPS_IN_MAIN = 2

    def __init__(self, qt_ref, k_ref, vt_ref, o_ref, acc_ref, carry_ref):
        self.qt_ref, self.k_ref, self.vt_ref, self.o_ref = qt_ref, k_ref, vt_ref, o_ref
        self.acc_ref, self.carry_ref = acc_ref, carry_ref
        self.n_q = k_ref.shape[0] // T_Q
        self.masks = _head_row_masks()
        kk = lax.broadcasted_iota(jnp.int32, (T_K, T_K), 0)
        kk2 = lax.broadcasted_iota(jnp.int32, (T_K, T_K), 1)
        later = jnp.where(kk2 > kk, 1.0, 0.0).astype(BF16)
        self.cum = jnp.concatenate([later, later], axis=1)
        self.strict = (lax.broadcasted_iota(jnp.int32, (T_G, T_Q), 0)
                       < lax.broadcasted_iota(jnp.int32, (T_G, T_Q), 1))

    def main_groups(self, qi):
        return [g for g in range(qi, qi - self.GROUPS_IN_MAIN, -1) if g >= 0]

    def n_main_items(self):
        return sum(len(self.main_groups(qi)) for qi in range(self.n_q)) * len(self.HEADS)

    def load_q(self, qi):
        return _split_heads(self.qt_ref[qi], *self.masks)

    def logits(self, g, qts):
        k2 = self.k_ref[pl.ds(pl.multiple_of(g * T_G, T_G), T_G), :]
        return [jnp.dot(k2, qts[h], preferred_element_type=F32) for h in self.HEADS]

    def add_group(self, g, h, z, mask, carry, acc):
        w, carry = _sb_weights(z, self.cum, carry, mask)
        vth = self.vt_ref[g][h * HEAD_DIM:(h + 1) * HEAD_DIM, :]
        return carry, acc + jnp.dot(vth, w, preferred_element_type=F32)

    def store(self, qi, accs):
        q0 = pl.multiple_of(qi * T_Q, T_Q)
        self.o_ref[pl.ds(q0, T_Q), :] = jnp.concatenate(accs, axis=0).T.astype(BF16)

    def main(self):
        pairs = [(qi, g) for qi in range(self.n_q) for g in self.main_groups(qi)]
        zs = {}

        def issue(j):
            for qi, g in pairs[j:j + 1]:
                zs[qi, g] = self.logits(g, self.load_q(qi))

        for j in range(SB_LOOKAHEAD):
            issue(j)
        for j, (qi, g) in enumerate(pairs):
            issue(j + SB_LOOKAHEAD)
            if g == qi:
                carries = [jnp.zeros((1, T_Q), F32) for _ in self.HEADS]
                accs = [jnp.zeros((HEAD_DIM, T_Q), F32) for _ in self.HEADS]
            z = zs.pop((qi, g))
            for h in self.HEADS:
                carries[h], accs[h] = self.add_group(
                    g, h, z[h], self.strict if g == qi else None, carries[h], accs[h])
                yield
            if g == self.main_groups(qi)[-1]:
                self.store(qi, accs)
                for h in self.HEADS:
                    self.acc_ref[qi, h] = accs[h]
                    self.carry_ref[qi, h] = carries[h]

    def _live(self, carries):
        return jnp.max(functools.reduce(jnp.maximum, carries)) >= SB_DEAD

    def tails(self):
        tiles = [qi for qi in range(self.n_q) if qi - self.GROUPS_IN_MAIN >= 0]

        def tile_tail(qi):
            qts = self.load_q(qi)

            def live(st):
                g, carries, _ = st
                return (g >= 0) & self._live(carries)

            def body(st):
                g, carries, accs = st
                carries, accs = list(carries), list(accs)
                zs = self.logits(g, qts)
                for h in self.HEADS:
                    carries[h], accs[h] = self.add_group(g, h, zs[h], None, carries[h], accs[h])
                return g - 1, tuple(carries), tuple(accs)

            init = (qi - self.GROUPS_IN_MAIN,
                    tuple(self.carry_ref[qi, h] for h in self.HEADS),
                    tuple(self.acc_ref[qi, h] for h in self.HEADS))
            _, _, accs = lax.while_loop(live, body, init)
            self.store(qi, list(accs))

        def all_tails():
            for qi in tiles:
                tile_tail(qi)

        any_live = self._live([self.carry_ref[qi, h] for qi in tiles for h in self.HEADS])
        lax.cond(any_live, all_tails, lambda: None)


def _exhaust(items):
    for _ in items:
        pass


def _sb_scratch(seq):
    n_q = seq // T_Q
    return [pltpu.VMEM((n_q, 2, HEAD_DIM, T_Q), F32), pltpu.VMEM((n_q, 2, 1, T_Q), F32)]


def _head_block_specs(branch, seq, index):
    def transposed(*ids):
        bi, hb = index(*ids)
        return bi, 0, branch * (W_BRANCH // LANES) + hb, 0

    def rows(*ids):
        bi, hb = index(*ids)
        return bi, 0, branch * (W_BRANCH // LANES) + hb

    t_spec = pl.BlockSpec((None, seq // T_Q, LANES, T_Q), transposed)
    return [t_spec, pl.BlockSpec((None, seq, LANES), rows), t_spec]


def _softmax_update(scores, vts, state):
    m, l, acc = state
    m_new = m
    for s in scores:
        m_new = jnp.maximum(m_new, jnp.max(s, axis=0, keepdims=True))
    alpha = jnp.exp2(m - m_new)
    l = alpha * l
    acc = alpha * acc
    for s, vt in zip(scores, vts):
        p = jnp.exp2(s - m_new)
        l = l + jnp.sum(p, axis=0, keepdims=True)
        acc = acc + jnp.dot(vt, p.astype(BF16), preferred_element_type=F32)
    return m_new, l, acc


def _diff_paths(qt_ref, kn_ref, vt_ref, bias_ref, lam_ref, sg_ref, o_ref, lam_init):
    seq = kn_ref.shape[0]
    n_q = seq // T_Q
    first, second = _head_row_masks()
    lq = lam_ref[...]
    lam = (jnp.exp(jnp.sum(lq[0:1] * lq[1:2], axis=-1, keepdims=True))
           - jnp.exp(jnp.sum(lq[2:3] * lq[3:4], axis=-1, keepdims=True)) + lam_init)

    def keys(g):
        return kn_ref[pl.ds(pl.multiple_of(g * T_G, T_G), T_G), :]

    def scores(qi, g):
        kn2, bias = keys(g), bias_ref[qi - g]
        return [jnp.dot(kn2, qtm, preferred_element_type=F32) + bias
                for qtm in _split_heads(qt_ref[qi], first, second)]

    def finish(qi, accs, sums):
        ob = accs[0] * (1.0 / sums[0]) - lam * (accs[1] * (1.0 / sums[1]))
        y = ob * lax.rsqrt(jnp.mean(ob * ob, axis=0, keepdims=True) + RMS_EPS)
        q0 = pl.multiple_of(qi * T_Q, T_Q)
        o_ref[pl.ds(q0, T_Q), :] = (y.T * sg_ref[...] * (1.0 - lam_init)).astype(BF16)

    def q_pair(qp, _):
        tiles = (2 * qp, 2 * qp + 1)

        def trip(it, states):
            gs = (2 * it, 2 * it + 1)
            vts = [vt_ref[g] for g in gs]
            scs = [[scores(qi, g) for g in gs] for qi in tiles]
            states = [list(st) for st in states]
            for i in range(2):
                for mp in range(2):
                    states[i][mp] = _softmax_update([sc[mp] for sc in scs[i]], vts, states[i][mp])
            return tuple(tuple(st) for st in states)

        init = (jnp.full((1, T_Q), NEG, F32), jnp.zeros((1, T_Q), F32), jnp.zeros((LANES, T_Q), F32))
        states = lax.fori_loop(0, qp, trip, ((init, init), (init, init)))
        gs = (2 * qp, 2 * qp + 1)
        vts = [vt_ref[g] for g in gs]
        scs = [[scores(tiles[0], gs[0])], [scores(tiles[1], g) for g in gs]]
        for i in range(2):
            done = [_softmax_update([sc[mp] for sc in scs[i]], vts, states[i][mp]) for mp in range(2)]
            finish(tiles[i], [st[2] for st in done], [st[1] for st in done])
        return 0

    work = [(qi, g) for qi in range(n_q) for g in range(qi + 1)]
    ones_rows = jnp.ones((ONES_ROWS, T_G), BF16)

    def bounded():
        z_next = scores(*work[0])
        for step, (qi, g) in enumerate(work):
            z = z_next
            if step + 1 < len(work):
                z_next = scores(*work[step + 1])
            vt1 = jnp.concatenate([vt_ref[g], ones_rows], axis=0)
            prods = [jnp.dot(vt1, jnp.exp2(z[mp]).astype(BF16), preferred_element_type=F32)
                     for mp in range(2)]
            pvs = [r[:LANES] for r in prods]
            psums = [r[LANES:LANES + 1] for r in prods]
            if g == 0:
                accs, sums = pvs, psums
            else:
                accs = [a + pv for a, pv in zip(accs, pvs)]
                sums = [s + ps_ for s, ps_ in zip(sums, psums)]
            if g == qi:
                finish(qi, accs, sums)
            yield

    def general():
        lax.fori_loop(0, n_q // 2, q_pair, 0)

    return bounded, len(work), general


def _alibi_tiles(n_q):
    slopes = np.asarray([2.0 ** (-8.0 * (i + 1) / H_DIFF) for i in range(H_DIFF)], np.float32)
    slope2 = jnp.asarray(slopes * LOG2E, F32)[:, None, None, None]
    kpos = lax.broadcasted_iota(jnp.int32, (T_G, T_Q), 0)
    qpos = lax.broadcasted_iota(jnp.int32, (T_G, T_Q), 1)
    tiles_back = lax.broadcasted_iota(jnp.int32, (n_q, 1, 1), 0)
    dist = (qpos - kpos)[None] + T_G * tiles_back
    bias = -slope2 * jnp.abs(dist).astype(F32)[None]
    visible = (tiles_back > 0) | ((kpos // CHUNK) <= (qpos // CHUNK))[None]
    return jnp.where(visible[None], bias, NEG)


N_REL_GROUPS = (N_PAST_CHUNKS * CHUNK + T_Q - 1) // T_G + 1


def _chunk_items(qt_ref, kn_ref, vt_ref, tab_ref, o_ref, sc_ref):
    seq = kn_ref.shape[0]
    first, second = _head_row_masks()
    deltas = tuple(range(N_REL_GROUPS - 1, -1, -1))
    ones_rows = jnp.ones((ONES_ROWS, T_G), BF16)

    def score_stage(qi, tile_deltas, slot):
        qts = _split_heads(qt_ref[qi], first, second)
        for dl in tile_deltas:
            k2 = kn_ref[pl.ds(pl.multiple_of((qi - dl) * T_G, T_G), T_G), :]
            for h in range(2):
                sc_ref[slot, h, dl] = (jnp.dot(k2, qts[h], preferred_element_type=F32)
                                       + tab_ref[h, dl])

    def softmax_stage(qi, tile_deltas, slot, fixed_shift):
        outs = []
        for h in range(2):
            scores = [sc_ref[slot, h, dl] for dl in tile_deltas]
            if fixed_shift:
                ps = [jnp.exp2(s) for s in scores]
            else:
                m = functools.reduce(jnp.maximum, [jnp.max(s, axis=0, keepdims=True) for s in scores])
                ps = [jnp.exp2(s - m) for s in scores]
            acc = None
            for dl, p in zip(tile_deltas, ps):
                vth = jnp.concatenate(
                    [vt_ref[qi - dl][h * HEAD_DIM:(h + 1) * HEAD_DIM, :], ones_rows], axis=0)
                pv = jnp.dot(vth, p.astype(BF16), preferred_element_type=F32)
                acc = pv if acc is None else acc + pv
            outs.append(acc[:HEAD_DIM] * (1.0 / acc[HEAD_DIM:HEAD_DIM + 1]))
        q0 = pl.multiple_of(qi * T_Q, T_Q)
        o_ref[pl.ds(q0, T_Q), :] = jnp.concatenate(outs, axis=0).T.astype(BF16)

    def tile_deltas(qi):
        return tuple(dl for dl in deltas if qi - dl >= 0)

    n_q = seq // T_Q

    def run(fixed_shift):
        score_stage(0, tile_deltas(0), 0)
        for qi in range(n_q):
            if qi + 1 < n_q:
                score_stage(qi + 1, tile_deltas(qi + 1), (qi + 1) % 2)
            yield
            softmax_stage(qi, tile_deltas(qi), qi % 2, fixed_shift)
            yield

    return run, 2 * n_q


def _chunk_logit_bound(gq, gk, tabs):
    return _qk_logit_bound(gq, gk) + jnp.max(jnp.where(tabs > 0.5 * NEG, jnp.abs(tabs), 0.0))


def _interleave(streams):
    order = sorted(((i + 0.5) / n, s) for s, (_, n) in enumerate(streams) for i in range(n))
    for _, s in order:
        next(streams[s][0], None)
    for items, _ in streams:
        _exhaust(items)


def _mix_kernel(bounded_ref, qa_ref, ka_ref, va_ref, qb_ref, kb_ref, vb_ref, qc_ref, kc_ref, vc_ref,
                alibi_ref, lam_ref, sg_ref, tab_ref,
                oa_ref, ob_ref, oc_ref, acc_ref, carry_ref, sc_ref, *, layer, lam_init):
    sb = _StickBreaking(qa_ref, ka_ref, va_ref, oa_ref, acc_ref, carry_ref)
    diff_items, n_diff, diff_online = _diff_paths(qb_ref, kb_ref, vb_ref, alibi_ref, lam_ref, sg_ref,
                                                  ob_ref, lam_init)
    chunk_items, n_chunk = _chunk_items(qc_ref, kc_ref, vc_ref, tab_ref, oc_ref, sc_ref)

    def interleaved():
        _interleave([(sb.main(), sb.n_main_items()), (diff_items(), n_diff),
                     (chunk_items(True), n_chunk)])

    def one_by_one():
        _exhaust(sb.main())
        diff_online()
        _exhaust(chunk_items(False))

    lax.cond(bounded_ref[layer] != 0, interleaved, one_by_one)
    sb.tails()


def _logits_bounded(gq_diff, gk_diff, gq_ch, gk_ch, tab):
    flags = [(_qk_logit_bound(gq_diff[l], gk_diff[l]) <= SAFE_LOG2)
             & (_chunk_logit_bound(gq_ch[l], gk_ch[l], tab[l]) <= SAFE_LOG2) for l in range(DEPTH)]
    return jnp.stack(flags).astype(jnp.int32)


def _mix_attn(qt, k, vt, alibi, lam_qk, subln_g, tab, bounded, layer, lam_init):
    b, s, _ = k.shape
    n_hb = W_BRANCH // LANES
    index = lambda hb, bi: (bi, hb)
    lay = lambda hb, bi: (layer, 0, 0)
    out_spec = pl.BlockSpec((None, s, LANES), lambda hb, bi: (bi, 0, hb))
    out_shape = jax.ShapeDtypeStruct((b, s, W_BRANCH), BF16)
    return pl.pallas_call(
        functools.partial(_mix_kernel, layer=layer, lam_init=lam_init),
        grid=(n_hb, b),
        in_specs=[
            pl.BlockSpec(memory_space=pltpu.SMEM),
            *_head_block_specs(0, s, index), *_head_block_specs(1, s, index),
            *_head_block_specs(2, s, index),
            pl.BlockSpec((None, s // T_Q, T_G, T_Q), lambda hb, bi: (hb, 0, 0, 0)),
            pl.BlockSpec((None, 4, HEAD_DIM), lay),
            pl.BlockSpec((None, 1, LANES), lay),
            pl.BlockSpec((None, 2, N_REL_GROUPS, T_G, T_Q), lambda hb, bi: (layer, hb, 0, 0, 0)),
        ],
        out_specs=[out_spec, out_spec, out_spec],
        out_shape=[out_shape, out_shape, out_shape],
        scratch_shapes=_sb_scratch(s) + [pltpu.VMEM((2, 2, N_REL_GROUPS, T_G, T_Q), F32)],
        compiler_params=pltpu.CompilerParams(
            dimension_semantics=("arbitrary", "arbitrary"), vmem_limit_bytes=VMEM_LIMIT),
        name="mix_attn",
    )(bounded, qt, k, vt, qt, k, vt, qt, k, vt, alibi, lam_qk, subln_g, tab)


def _rel_bias_tiles(rel_bias):
    lead = rel_bias.shape[:-1]
    span = (N_REL_GROUPS - 1) * T_G + T_Q
    period = span + T_G
    edge_lo = jnp.broadcast_to(rel_bias[..., :1], lead + (T_G - REL_CLIP,))
    edge_hi = jnp.broadcast_to(rel_bias[..., -1:], lead + (span - REL_CLIP - 1,))
    row = jnp.concatenate([rel_bias[..., REL_CLIP:], edge_hi, edge_lo, rel_bias[..., :REL_CLIP]], axis=-1)
    flat = jnp.tile(row, (1,) * len(lead) + (T_G,))[..., :T_G * (period - 1)]
    toep = flat.reshape(lead + (T_G, period - 1))
    tiles = jnp.stack([toep[..., d * T_G:d * T_G + T_Q] for d in range(N_REL_GROUPS)], axis=-3)
    kchunk = np.arange(T_G)[:, None] // CHUNK
    qchunk = np.arange(T_Q)[None, :] // CHUNK
    dd = np.stack([qchunk - kchunk + d * (T_G // CHUNK) for d in range(N_REL_GROUPS)])
    return jnp.where((dd >= 0) & (dd <= N_PAST_CHUNKS), tiles * LOG2E, NEG)


def kernel(x, norm_mix_g, w_in, b_gate, qk_g_diff, lambda_qk, subln_g, qk_g_ch, rel_bias,
           w_branch_sb, w_branch_diff, w_branch_ch, w_out, norm_ffn_g, w_gu, w_down):
    b, s, d = x.shape
    m = b * s
    w_qkv = w_in.astype(BF16)
    w_gate = w_qkv[:, :, QKV_W:]
    w_br = jnp.concatenate([w_branch_sb, w_branch_diff, w_branch_ch], axis=1).astype(BF16)
    w_out_b = w_out.astype(BF16)
    w_gu_b = w_gu.astype(BF16)
    w_down_b = w_down.astype(BF16)
    g_mix = norm_mix_g.reshape(DEPTH, 1, d)
    g_ffn = norm_ffn_g.reshape(DEPTH, 1, d)
    gq_diff = jnp.tile(qk_g_diff[:, 0:1, :], (1, 1, 2))
    gk_diff = jnp.tile(qk_g_diff[:, 1:2, :], (1, 1, 2))
    gq_ch = jnp.tile(qk_g_ch[:, 0:1, :], (1, 1, 2))
    gk_ch = jnp.tile(qk_g_ch[:, 1:2, :], (1, 1, 2))
    sg = subln_g.reshape(DEPTH, 1, 2 * HEAD_DIM)
    tab = _rel_bias_tiles(rel_bias)
    alibi = _alibi_tiles(s // T_Q)
    bounded = _logits_bounded(gq_diff, gk_diff, gq_ch, gk_ch, tab)
    ones = jnp.ones((DEPTH, 1, W_BRANCH), F32)
    widen = lambda g: jnp.tile(g, (1, 1, W_BRANCH // LANES))
    q_scale = QK_SCALE * LOG2E
    qk_gain = jnp.concatenate([ones, ones, ones, widen(gq_diff) * q_scale, widen(gk_diff), ones,
                               widen(gq_ch) * q_scale, widen(gk_ch), ones], axis=-1)

    xf = x.reshape(m, d)
    for layer in range(DEPTH):
        lam_init = 0.8 - 0.6 * math.exp(-0.3 * layer)
        k, qt, vt = _qkv_proj(xf, g_mix, w_qkv, qk_gain, layer)
        k = k.reshape(b, s, N_BRANCH * W_BRANCH)
        qt = qt.reshape(b, s // T_Q, N_BRANCH * W_BRANCH, T_Q)
        vt = vt.reshape(b, s // T_Q, N_BRANCH * W_BRANCH, T_Q)
        o_a, o_b, o_c = _mix_attn(qt, k, vt, alibi, lambda_qk, sg, tab, bounded, layer, lam_init)
        xf = _merge_out(xf, g_mix, o_a.reshape(m, W_BRANCH), o_b.reshape(m, W_BRANCH),
                        o_c.reshape(m, W_BRANCH), w_gate, b_gate, w_br, w_out_b, layer)
        xf = _ffn(xf, g_ffn, w_gu_b, w_down_b, layer)
    return xf.reshape(b, s, d)
```

```python
import functools
import math

import jax
import jax.numpy as jnp
import numpy as np
from jax import lax
from jax.experimental import pallas as pl
from jax.experimental.pallas import tpu as pltpu

F32 = jnp.float32
BF16 = jnp.bfloat16

D_MODEL = 1024
DEPTH = 4
CHUNK = 64
HEAD_DIM = 64
H_DIFF = 4
N_PAST_CHUNKS = 8
REL_CLIP = 128
W_BRANCH = 512
QKV_W = 9 * W_BRANCH
N_BRANCH = 3
D_FF = int(math.ceil(8 * D_MODEL / 3 / 256)) * 256
RMS_EPS = 1e-6
QK_SCALE = HEAD_DIM ** -0.5

LANES = 128
T_Q = 256
T_K = 128
T_G = 256
NEG = -1e30
SB_DEAD = -150.0
ONES_ROWS = 16
SB_LOOKAHEAD = 1
LOG2E = 1.4426950408889634
SAFE_LOG2 = 60.0
VMEM_LIMIT = 56 * 1024 * 1024

ROW_TILE = 1024
FF_CHUNK = 256


def _rms(x, g):
    return x * lax.rsqrt(jnp.mean(x * x, axis=-1, keepdims=True) + RMS_EPS) * g


def _rms_halves(x, g):
    lane = lax.broadcasted_iota(jnp.int32, (1, LANES), 1)
    first = lane < HEAD_DIM
    x2 = x * x
    s0 = jnp.sum(jnp.where(first, x2, 0.0), axis=-1, keepdims=True)
    s1 = jnp.sum(jnp.where(first, 0.0, x2), axis=-1, keepdims=True)
    ms = jnp.where(first, s0, s1) * (1.0 / HEAD_DIM)
    return x * lax.rsqrt(ms + RMS_EPS) * g


def _qk_logit_bound(gq, gk):
    return (1.02 * LOG2E * QK_SCALE * HEAD_DIM) * jnp.max(jnp.abs(gq)) * jnp.max(jnp.abs(gk))


def _head_row_masks():
    row = lax.broadcasted_iota(jnp.int32, (LANES, 1), 0)
    return row < HEAD_DIM, row >= HEAD_DIM


def _split_heads(qt, first, second):
    zero = jnp.zeros_like(qt)
    return jnp.where(first, qt, zero), jnp.where(second, qt, zero)


N_SECTIONS = 9


def _qkv_kernel(x_ref, g_ref, w_ref, qkg_ref, k_out, qt_out, vt_out):
    h = _rms(x_ref[...], g_ref[...]).astype(BF16)

    def project(sec):
        return jnp.dot(h, w_ref[:, sec * W_BRANCH:(sec + 1) * W_BRANCH], preferred_element_type=F32)

    z_next = project(0)
    for sec in range(N_SECTIONS):
        z = z_next
        if sec + 1 < N_SECTIONS:
            z_next = project(sec + 1)
        branch, role = divmod(sec, 3)
        for cb in range(W_BRANCH // LANES):
            blk = z[:, cb * LANES:(cb + 1) * LANES]
            col = sec * W_BRANCH + cb * LANES
            if branch > 0 and role < 2:
                blk = _rms_halves(blk, qkg_ref[:, col:col + LANES])
            elif role == 0:
                blk = blk * (QK_SCALE * LOG2E)
            out_col = branch * W_BRANCH + cb * LANES
            if role == 1:
                k_out[:, out_col:out_col + LANES] = blk.astype(BF16)
            else:
                out = qt_out if role == 0 else vt_out
                for r in range(ROW_TILE // T_Q):
                    out[r, out_col:out_col + LANES, :] = blk[r * T_Q:(r + 1) * T_Q, :].T.astype(BF16)


def _qkv_proj(x, g, w, qk_gain, layer):
    m = x.shape[0]
    width = N_BRANCH * W_BRANCH
    tiles = ROW_TILE // T_Q
    lay = lambda i: (layer, 0, 0)
    transposed = jax.ShapeDtypeStruct((m // T_Q, width, T_Q), BF16)
    return pl.pallas_call(
        _qkv_kernel,
        grid=(m // ROW_TILE,),
        in_specs=[
            pl.BlockSpec((ROW_TILE, D_MODEL), lambda i: (i, 0)),
            pl.BlockSpec((None, 1, D_MODEL), lay),
            pl.BlockSpec((None, D_MODEL, QKV_W), lay, pipeline_mode=pl.Buffered(1)),
            pl.BlockSpec((None, 1, QKV_W), lay),
        ],
        out_specs=[
            pl.BlockSpec((ROW_TILE, width), lambda i: (i, 0)),
            pl.BlockSpec((tiles, width, T_Q), lambda i: (i, 0, 0)),
            pl.BlockSpec((tiles, width, T_Q), lambda i: (i, 0, 0)),
        ],
        out_shape=[jax.ShapeDtypeStruct((m, width), BF16), transposed, transposed],
        compiler_params=pltpu.CompilerParams(
            dimension_semantics=("arbitrary",), vmem_limit_bytes=VMEM_LIMIT),
        name="qkv_proj",
    )(x, g, w, qk_gain)


def _merge_kernel(x_ref, g_ref, oa_ref, ob_ref, oc_ref, wg_ref, bg_ref, wbr_ref, wo_ref, out_ref):
    x = x_ref[...]
    h = _rms(x, g_ref[...]).astype(BF16)
    merged = None
    for br, o_ref in enumerate((oa_ref, ob_ref, oc_ref)):
        g_lin = jnp.dot(h, wg_ref[:, br * D_MODEL:(br + 1) * D_MODEL], preferred_element_type=F32)
        gate = 1.0 / (1.0 + jnp.exp(-(g_lin + bg_ref[br:br + 1, :])))
        proj = jnp.dot(o_ref[...], wbr_ref[br * W_BRANCH:(br + 1) * W_BRANCH, :],
                       preferred_element_type=F32)
        term = gate * proj
        merged = term if merged is None else merged + term
    out_ref[...] = x + jnp.dot(merged.astype(BF16), wo_ref[...], preferred_element_type=F32)


def _merge_out(x, g, o_a, o_b, o_c, w_gate, b_gate, w_br, w_out, layer):
    m = x.shape[0]
    row = lambda i: (i, 0)
    lay = lambda i: (layer, 0, 0)
    return pl.pallas_call(
        _merge_kernel,
        grid=(m // ROW_TILE,),
        in_specs=[
            pl.BlockSpec((ROW_TILE, D_MODEL), row),
            pl.BlockSpec((None, 1, D_MODEL), lay),
            pl.BlockSpec((ROW_TILE, W_BRANCH), row),
            pl.BlockSpec((ROW_TILE, W_BRANCH), row),
            pl.BlockSpec((ROW_TILE, W_BRANCH), row),
            pl.BlockSpec((None, D_MODEL, N_BRANCH * D_MODEL), lay, pipeline_mode=pl.Buffered(1)),
            pl.BlockSpec((None, N_BRANCH, D_MODEL), lay),
            pl.BlockSpec((None, N_BRANCH * W_BRANCH, D_MODEL), lay, pipeline_mode=pl.Buffered(1)),
            pl.BlockSpec((None, D_MODEL, D_MODEL), lay, pipeline_mode=pl.Buffered(1)),
        ],
        out_specs=pl.BlockSpec((ROW_TILE, D_MODEL), row),
        out_shape=jax.ShapeDtypeStruct((m, D_MODEL), F32),
        compiler_params=pltpu.CompilerParams(
            dimension_semantics=("arbitrary",), vmem_limit_bytes=VMEM_LIMIT),
        name="merge_out",
    )(x, g, o_a, o_b, o_c, w_gate, b_gate, w_br, w_out)


def _ffn_kernel(x_ref, g_ref, wgu_ref, wd_ref, out_ref, act_ref):
    x = x_ref[...]
    h = _rms(x, g_ref[...]).astype(BF16)
    for c in range(0, D_FF, FF_CHUNK):
        gate = jnp.dot(h, wgu_ref[:, c:c + FF_CHUNK], preferred_element_type=F32)
        up = jnp.dot(h, wgu_ref[:, D_FF + c:D_FF + c + FF_CHUNK], preferred_element_type=F32)
        silu = gate / (1.0 + jnp.exp(-gate))
        act_ref[:, c:c + FF_CHUNK] = (silu * up).astype(BF16)
    out_ref[...] = x + jnp.dot(act_ref[...], wd_ref[...], preferred_element_type=F32)


def _ffn(x, g, w_gu, w_down, layer):
    m = x.shape[0]
    row = lambda i: (i, 0)
    lay = lambda i: (layer, 0, 0)
    return pl.pallas_call(
        _ffn_kernel,
        grid=(m // ROW_TILE,),
        in_specs=[
            pl.BlockSpec((ROW_TILE, D_MODEL), row),
            pl.BlockSpec((None, 1, D_MODEL), lay),
            pl.BlockSpec((None, D_MODEL, 2 * D_FF), lay, pipeline_mode=pl.Buffered(1)),
            pl.BlockSpec((None, D_FF, D_MODEL), lay, pipeline_mode=pl.Buffered(1)),
        ],
        out_specs=pl.BlockSpec((ROW_TILE, D_MODEL), row),
        out_shape=jax.ShapeDtypeStruct((m, D_MODEL), F32),
        scratch_shapes=[pltpu.VMEM((ROW_TILE, D_FF), BF16)],
        compiler_params=pltpu.CompilerParams(
            dimension_semantics=("arbitrary",), vmem_limit_bytes=VMEM_LIMIT),
        name="ffn",
    )(x, g, w_gu, w_down)


def _sb_weights(z, cum, carry, mask):
    neg_abs = lax.bitcast_convert_type(
        lax.bitcast_convert_type(z, jnp.uint32) | jnp.uint32(0x80000000), F32)
    sp = jnp.log2(1.0 + jnp.exp2(neg_abs))
    log_beta = jnp.minimum(z, 0.0) - sp
    log_1m = log_beta - z
    if mask is not None:
        log_1m = jnp.where(mask, log_1m, 0.0)
    hi = log_1m.astype(BF16)
    lo = (log_1m - hi.astype(F32)).astype(BF16)
    afters = []
    for u in reversed(range(z.shape[0] // T_K)):
        r0, r1 = u * T_K, (u + 1) * T_K
        within = jnp.dot(cum, jnp.concatenate([hi[r0:r1], lo[r0:r1]], axis=0),
                         preferred_element_type=F32)
        afters.append(within + carry)
        carry = carry + within[0:1, :] + log_1m[r0:r0 + 1, :]
    after = afters[0] if len(afters) == 1 else jnp.concatenate(afters[::-1], axis=0)
    w = jnp.exp2(log_beta + after)
    if mask is not None:
        w = jnp.where(mask, w, 0.0)
    return w.astype(BF16), carry


class _StickBreaking:
    HEADS = range(2)
    GROUPS_IN_MAIN = 2

    def __init__(self, qt_ref, k_ref, vt_ref, o_ref, acc_ref, carry_ref):
        self.qt_ref, self.k_ref, self.vt_ref, self.o_ref = qt_ref, k_ref, vt_ref, o_ref
        self.acc_ref, self.carry_ref = acc_ref, carry_ref
        self.n_q = k_ref.shape[0] // T_Q
        self.masks = _head_row_masks()
        kk = lax.broadcasted_iota(jnp.int32, (T_K, T_K), 0)
        kk2 = lax.broadcasted_iota(jnp.int32, (T_K, T_K), 1)
        later = jnp.where(kk2 > kk, 1.0, 0.0).astype(BF16)
        self.cum = jnp.concatenate([later, later], axis=1)
        self.strict = (lax.broadcasted_iota(jnp.int32, (T_G, T_Q), 0)
                       < lax.broadcasted_iota(jnp.int32, (T_G, T_Q), 1))

    def main_groups(self, qi):
        return [g for g in range(qi, qi - self.GROUPS_IN_MAIN, -1) if g >= 0]

    def n_main_items(self):
        return sum(len(self.main_groups(qi)) for qi in range(self.n_q)) * len(self.HEADS)

    def load_q(self, qi):
        return _split_heads(self.qt_ref[qi], *self.masks)

    def logits(self, g, qts):
        k2 = self.k_ref[pl.ds(pl.multiple_of(g * T_G, T_G), T_G), :]
        return [jnp.dot(k2, qts[h], preferred_element_type=F32) for h in self.HEADS]

    def add_group(self, g, h, z, mask, carry, acc):
        w, carry = _sb_weights(z, self.cum, carry, mask)
        vth = self.vt_ref[g][h * HEAD_DIM:(h + 1) * HEAD_DIM, :]
        return carry, acc + jnp.dot(vth, w, preferred_element_type=F32)

    def store(self, qi, accs):
        q0 = pl.multiple_of(qi * T_Q, T_Q)
        self.o_ref[pl.ds(q0, T_Q), :] = jnp.concatenate(accs, axis=0).T.astype(BF16)

    def main(self):
        pairs = [(qi, g) for qi in range(self.n_q) for g in self.main_groups(qi)]
        zs = {}

        def issue(j):
            for qi, g in pairs[j:j + 1]:
                zs[qi, g] = self.logits(g, self.load_q(qi))

        for j in range(SB_LOOKAHEAD):
            issue(j)
        for j, (qi, g) in enumerate(pairs):
            issue(j + SB_LOOKAHEAD)
            if g == qi:
                carries = [jnp.zeros((1, T_Q), F32) for _ in self.HEADS]
                accs = [jnp.zeros((HEAD_DIM, T_Q), F32) for _ in self.HEADS]
            z = zs.pop((qi, g))
            for h in self.HEADS:
                carries[h], accs[h] = self.add_group(
                    g, h, z[h], self.strict if g == qi else None, carries[h], accs[h])
                yield
            if g == self.main_groups(qi)[-1]:
                self.store(qi, accs)
                for h in self.HEADS:
                    self.acc_ref[qi, h] = accs[h]
                    self.carry_ref[qi, h] = carries[h]

    def _live(self, carries):
        return jnp.max(functools.reduce(jnp.maximum, carries)) >= SB_DEAD

    def tails(self):
        tiles = [qi for qi in range(self.n_q) if qi - self.GROUPS_IN_MAIN >= 0]

        def tile_tail(qi):
            qts = self.load_q(qi)

            def live(st):
                g, carries, _ = st
                return (g >= 0) & self._live(carries)

            def body(st):
                g, carries, accs = st
                carries, accs = list(carries), list(accs)
                zs = self.logits(g, qts)
                for h in self.HEADS:
                    carries[h], accs[h] = self.add_group(g, h, zs[h], None, carries[h], accs[h])
                return g - 1, tuple(carries), tuple(accs)

            init = (qi - self.GROUPS_IN_MAIN,
                    tuple(self.carry_ref[qi, h] for h in self.HEADS),
                    tuple(self.acc_ref[qi, h] for h in self.HEADS))
            _, _, accs = lax.while_loop(live, body, init)
            self.store(qi, list(accs))

        def all_tails():
            for qi in tiles:
                tile_tail(qi)

        any_live = self._live([self.carry_ref[qi, h] for qi in tiles for h in self.HEADS])
        lax.cond(any_live, all_tails, lambda: None)


def _exhaust(items):
    for _ in items:
        pass


def _sb_scratch(seq):
    n_q = seq // T_Q
    return [pltpu.VMEM((n_q, 2, HEAD_DIM, T_Q), F32), pltpu.VMEM((n_q, 2, 1, T_Q), F32)]


def _head_block_specs(branch, seq, index):
    def transposed(*ids):
        bi, hb = index(*ids)
        return bi, 0, branch * (W_BRANCH // LANES) + hb, 0

    def rows(*ids):
        bi, hb = index(*ids)
        return bi, 0, branch * (W_BRANCH // LANES) + hb

    t_spec = pl.BlockSpec((None, seq // T_Q, LANES, T_Q), transposed)
    return [t_spec, pl.BlockSpec((None, seq, LANES), rows), t_spec]


def _softmax_update(scores, vts, state):
    m, l, acc = state
    m_new = m
    for s in scores:
        m_new = jnp.maximum(m_new, jnp.max(s, axis=0, keepdims=True))
    alpha = jnp.exp2(m - m_new)
    l = alpha * l
    acc = alpha * acc
    for s, vt in zip(scores, vts):
        p = jnp.exp2(s - m_new)
        l = l + jnp.sum(p, axis=0, keepdims=True)
        acc = acc + jnp.dot(vt, p.astype(BF16), preferred_element_type=F32)
    return m_new, l, acc


def _diff_paths(qt_ref, kn_ref, vt_ref, bias_ref, lam_ref, sg_ref, o_ref, lam_init):
    seq = kn_ref.shape[0]
    n_q = seq // T_Q
    first, second = _head_row_masks()
    lq = lam_ref[...]
    lam = (jnp.exp(jnp.sum(lq[0:1] * lq[1:2], axis=-1, keepdims=True))
           - jnp.exp(jnp.sum(lq[2:3] * lq[3:4], axis=-1, keepdims=True)) + lam_init)

    def keys(g):
        return kn_ref[pl.ds(pl.multiple_of(g * T_G, T_G), T_G), :]

    def scores(qi, g):
        kn2, bias = keys(g), bias_ref[qi - g]
        return [jnp.dot(kn2, qtm, preferred_element_type=F32) + bias
                for qtm in _split_heads(qt_ref[qi], first, second)]

    def finish(qi, accs, sums):
        ob = accs[0] * (1.0 / sums[0]) - lam * (accs[1] * (1.0 / sums[1]))
        y = ob * lax.rsqrt(jnp.mean(ob * ob, axis=0, keepdims=True) + RMS_EPS)
        q0 = pl.multiple_of(qi * T_Q, T_Q)
        o_ref[pl.ds(q0, T_Q), :] = (y.T * sg_ref[...] * (1.0 - lam_init)).astype(BF16)

    def q_pair(qp, _):
        tiles = (2 * qp, 2 * qp + 1)

        def trip(it, states):
            gs = (2 * it, 2 * it + 1)
            vts = [vt_ref[g] for g in gs]
            scs = [[scores(qi, g) for g in gs] for qi in tiles]
            states = [list(st) for st in states]
            for i in range(2):
                for mp in range(2):
                    states[i][mp] = _softmax_update([sc[mp] for sc in scs[i]], vts, states[i][mp])
            return tuple(tuple(st) for st in states)

        init = (jnp.full((1, T_Q), NEG, F32), jnp.zeros((1, T_Q), F32), jnp.zeros((LANES, T_Q), F32))
        states = lax.fori_loop(0, qp, trip, ((init, init), (init, init)))
        gs = (2 * qp, 2 * qp + 1)
        vts = [vt_ref[g] for g in gs]
        scs = [[scores(tiles[0], gs[0])], [scores(tiles[1], g) for g in gs]]
        for i in range(2):
            done = [_softmax_update([sc[mp] for sc in scs[i]], vts, states[i][mp]) for mp in range(2)]
            finish(tiles[i], [st[2] for st in done], [st[1] for st in done])
        return 0

    work = [(qi, g) for qi in range(n_q) for g in range(qi + 1)]
    ones_rows = jnp.ones((ONES_ROWS, T_G), BF16)

    def bounded():
        z_next = scores(*work[0])
        for step, (qi, g) in enumerate(work):
            z = z_next
            if step + 1 < len(work):
                z_next = scores(*work[step + 1])
            vt1 = jnp.concatenate([vt_ref[g], ones_rows], axis=0)
            prods = [jnp.dot(vt1, jnp.exp2(z[mp]).astype(BF16), preferred_element_type=F32)
                     for mp in range(2)]
            pvs = [r[:LANES] for r in prods]
            psums = [r[LANES:LANES + 1] for r in prods]
            if g == 0:
                accs, sums = pvs, psums
            else:
                accs = [a + pv for a, pv in zip(accs, pvs)]
                sums = [s + ps_ for s, ps_ in zip(sums, psums)]
            if g == qi:
                finish(qi, accs, sums)
            yield

    def general():
        lax.fori_loop(0, n_q // 2, q_pair, 0)

    return bounded, len(work), general


def _alibi_tiles(n_q):
    slopes = np.asarray([2.0 ** (-8.0 * (i + 1) / H_DIFF) for i in range(H_DIFF)], np.float32)
    slope2 = jnp.asarray(slopes * LOG2E, F32)[:, None, None, None]
    kpos = lax.broadcasted_iota(jnp.int32, (T_G, T_Q), 0)
    qpos = lax.broadcasted_iota(jnp.int32, (T_G, T_Q), 1)
    tiles_back = lax.broadcasted_iota(jnp.int32, (n_q, 1, 1), 0)
    dist = (qpos - kpos)[None] + T_G * tiles_back
    bias = -slope2 * jnp.abs(dist).astype(F32)[None]
    visible = (tiles_back > 0) | ((kpos // CHUNK) <= (qpos // CHUNK))[None]
    return jnp.where(visible[None], bias, NEG)


N_REL_GROUPS = (N_PAST_CHUNKS * CHUNK + T_Q - 1) // T_G + 1


def _chunk_items(qt_ref, kn_ref, vt_ref, tab_ref, o_ref, sc_ref):
    seq = kn_ref.shape[0]
    first, second = _head_row_masks()
    deltas = tuple(range(N_REL_GROUPS - 1, -1, -1))
    ones_rows = jnp.ones((ONES_ROWS, T_G), BF16)

    def score_stage(qi, tile_deltas, slot):
        qts = _split_heads(qt_ref[qi], first, second)
        for dl in tile_deltas:
            k2 = kn_ref[pl.ds(pl.multiple_of((qi - dl) * T_G, T_G), T_G), :]
            for h in range(2):
                sc_ref[slot, h, dl] = (jnp.dot(k2, qts[h], preferred_element_type=F32)
                                       + tab_ref[h, dl])

    def softmax_stage(qi, tile_deltas, slot, fixed_shift):
        outs = []
        for h in range(2):
            scores = [sc_ref[slot, h, dl] for dl in tile_deltas]
            if fixed_shift:
                ps = [jnp.exp2(s) for s in scores]
            else:
                m = functools.reduce(jnp.maximum, [jnp.max(s, axis=0, keepdims=True) for s in scores])
                ps = [jnp.exp2(s - m) for s in scores]
            acc = None
            for dl, p in zip(tile_deltas, ps):
                vth = jnp.concatenate(
                    [vt_ref[qi - dl][h * HEAD_DIM:(h + 1) * HEAD_DIM, :], ones_rows], axis=0)
                pv = jnp.dot(vth, p.astype(BF16), preferred_element_type=F32)
                acc = pv if acc is None else acc + pv
            outs.append(acc[:HEAD_DIM] * (1.0 / acc[HEAD_DIM:HEAD_DIM + 1]))
        q0 = pl.multiple_of(qi * T_Q, T_Q)
        o_ref[pl.ds(q0, T_Q), :] = jnp.concatenate(outs, axis=0).T.astype(BF16)

    def tile_deltas(qi):
        return tuple(dl for dl in deltas if qi - dl >= 0)

    n_q = seq // T_Q

    def run(fixed_shift):
        score_stage(0, tile_deltas(0), 0)
        for qi in range(n_q):
            if qi + 1 < n_q:
                score_stage(qi + 1, tile_deltas(qi + 1), (qi + 1) % 2)
            yield
            softmax_stage(qi, tile_deltas(qi), qi % 2, fixed_shift)
            yield

    return run, 2 * n_q


def _chunk_logit_bound(gq, gk, tabs):
    return _qk_logit_bound(gq, gk) + jnp.max(jnp.where(tabs > 0.5 * NEG, jnp.abs(tabs), 0.0))


def _interleave(streams):
    order = sorted(((i + 0.5) / n, s) for s, (_, n) in enumerate(streams) for i in range(n))
    for _, s in order:
        next(streams[s][0], None)
    for items, _ in streams:
        _exhaust(items)


def _mix_kernel(bounded_ref, qa_ref, ka_ref, va_ref, qb_ref, kb_ref, vb_ref, qc_ref, kc_ref, vc_ref,
                alibi_ref, lam_ref, sg_ref, tab_ref,
                oa_ref, ob_ref, oc_ref, acc_ref, carry_ref, sc_ref, *, layer, lam_init):
    sb = _StickBreaking(qa_ref, ka_ref, va_ref, oa_ref, acc_ref, carry_ref)
    diff_items, n_diff, diff_online = _diff_paths(qb_ref, kb_ref, vb_ref, alibi_ref, lam_ref, sg_ref,
                                                  ob_ref, lam_init)
    chunk_items, n_chunk = _chunk_items(qc_ref, kc_ref, vc_ref, tab_ref, oc_ref, sc_ref)

    def interleaved():
        _interleave([(sb.main(), sb.n_main_items()), (diff_items(), n_diff),
                     (chunk_items(True), n_chunk)])

    def one_by_one():
        _exhaust(sb.main())
        diff_online()
        _exhaust(chunk_items(False))

    lax.cond(bounded_ref[layer] != 0, interleaved, one_by_one)
    sb.tails()


def _logits_bounded(gq_diff, gk_diff, gq_ch, gk_ch, tab):
    flags = [(_qk_logit_bound(gq_diff[l], gk_diff[l]) <= SAFE_LOG2)
             & (_chunk_logit_bound(gq_ch[l], gk_ch[l], tab[l]) <= SAFE_LOG2) for l in range(DEPTH)]
    return jnp.stack(flags).astype(jnp.int32)


def _mix_attn(qt, k, vt, alibi, lam_qk, subln_g, tab, bounded, layer, lam_init):
    b, s, _ = k.shape
    n_hb = W_BRANCH // LANES
    index = lambda hb, bi: (bi, hb)
    lay = lambda hb, bi: (layer, 0, 0)
    out_spec = pl.BlockSpec((None, s, LANES), lambda hb, bi: (bi, 0, hb))
    out_shape = jax.ShapeDtypeStruct((b, s, W_BRANCH), BF16)
    return pl.pallas_call(
        functools.partial(_mix_kernel, layer=layer, lam_init=lam_init),
        grid=(n_hb, b),
        in_specs=[
            pl.BlockSpec(memory_space=pltpu.SMEM),
            *_head_block_specs(0, s, index), *_head_block_specs(1, s, index),
            *_head_block_specs(2, s, index),
            pl.BlockSpec((None, s // T_Q, T_G, T_Q), lambda hb, bi: (hb, 0, 0, 0)),
            pl.BlockSpec((None, 4, HEAD_DIM), lay),
            pl.BlockSpec((None, 1, LANES), lay),
            pl.BlockSpec((None, 2, N_REL_GROUPS, T_G, T_Q), lambda hb, bi: (layer, hb, 0, 0, 0)),
        ],
        out_specs=[out_spec, out_spec, out_spec],
        out_shape=[out_shape, out_shape, out_shape],
        scratch_shapes=_sb_scratch(s) + [pltpu.VMEM((2, 2, N_REL_GROUPS, T_G, T_Q), F32)],
        compiler_params=pltpu.CompilerParams(
            dimension_semantics=("arbitrary", "arbitrary"), vmem_limit_bytes=VMEM_LIMIT),
        name="mix_attn",
    )(bounded, qt, k, vt, qt, k, vt, qt, k, vt, alibi, lam_qk, subln_g, tab)


def _rel_bias_tiles(rel_bias):
    lead = rel_bias.shape[:-1]
    span = (N_REL_GROUPS - 1) * T_G + T_Q
    period = span + T_G
    edge_lo = jnp.broadcast_to(rel_bias[..., :1], lead + (T_G - REL_CLIP,))
    edge_hi = jnp.broadcast_to(rel_bias[..., -1:], lead + (span - REL_CLIP - 1,))
    row = jnp.concatenate([rel_bias[..., REL_CLIP:], edge_hi, edge_lo, rel_bias[..., :REL_CLIP]], axis=-1)
    flat = jnp.tile(row, (1,) * len(lead) + (T_G,))[..., :T_G * (period - 1)]
    toep = flat.reshape(lead + (T_G, period - 1))
    tiles = jnp.stack([toep[..., d * T_G:d * T_G + T_Q] for d in range(N_REL_GROUPS)], axis=-3)
    kchunk = np.arange(T_G)[:, None] // CHUNK
    qchunk = np.arange(T_Q)[None, :] // CHUNK
    dd = np.stack([qchunk - kchunk + d * (T_G // CHUNK) for d in range(N_REL_GROUPS)])
    return jnp.where((dd >= 0) & (dd <= N_PAST_CHUNKS), tiles * LOG2E, NEG)


def kernel(x, norm_mix_g, w_in, b_gate, qk_g_diff, lambda_qk, subln_g, qk_g_ch, rel_bias,
           w_branch_sb, w_branch_diff, w_branch_ch, w_out, norm_ffn_g, w_gu, w_down):
    b, s, d = x.shape
    m = b * s
    w_qkv = w_in.astype(BF16)
    w_gate = w_qkv[:, :, QKV_W:]
    w_br = jnp.concatenate([w_branch_sb, w_branch_diff, w_branch_ch], axis=1).astype(BF16)
    w_out_b = w_out.astype(BF16)
    w_gu_b = w_gu.astype(BF16)
    w_down_b = w_down.astype(BF16)
    g_mix = norm_mix_g.reshape(DEPTH, 1, d)
    g_ffn = norm_ffn_g.reshape(DEPTH, 1, d)
    gq_diff = jnp.tile(qk_g_diff[:, 0:1, :], (1, 1, 2))
    gk_diff = jnp.tile(qk_g_diff[:, 1:2, :], (1, 1, 2))
    gq_ch = jnp.tile(qk_g_ch[:, 0:1, :], (1, 1, 2))
    gk_ch = jnp.tile(qk_g_ch[:, 1:2, :], (1, 1, 2))
    sg = subln_g.reshape(DEPTH, 1, 2 * HEAD_DIM)
    tab = _rel_bias_tiles(rel_bias)
    alibi = _alibi_tiles(s // T_Q)
    bounded = _logits_bounded(gq_diff, gk_diff, gq_ch, gk_ch, tab)
    ones = jnp.ones((DEPTH, 1, W_BRANCH), F32)
    widen = lambda g: jnp.tile(g, (1, 1, W_BRANCH // LANES))
    q_scale = QK_SCALE * LOG2E
    qk_gain = jnp.concatenate([ones, ones, ones, widen(gq_diff) * q_scale, widen(gk_diff), ones,
                               widen(gq_ch) * q_scale, widen(gk_ch), ones], axis=-1)

    xf = x.reshape(m, d)
    for layer in range(DEPTH):
        lam_init = 0.8 - 0.6 * math.exp(-0.3 * layer)
        k, qt, vt = _qkv_proj(xf, g_mix, w_qkv, qk_gain, layer)
        k = k.reshape(b, s, N_BRANCH * W_BRANCH)
        qt = qt.reshape(b, s // T_Q, N_BRANCH * W_BRANCH, T_Q)
        vt = vt.reshape(b, s // T_Q, N_BRANCH * W_BRANCH, T_Q)
        o_a, o_b, o_c = _mix_attn(qt, k, vt, alibi, lambda_qk, sg, tab, bounded, layer, lam_init)
        xf = _merge_out(xf, g_mix, o_a.reshape(m, W_BRANCH), o_b.reshape(m, W_BRANCH),
                        o_c.reshape(m, W_BRANCH), w_gate, b_gate, w_br, w_out_b, layer)
        xf = _ffn(xf, g_ffn, w_gu_b, w_down_b, layer)
    return xf.reshape(b, s, d)
```

```python
import functools
import math

import jax
import jax.numpy as jnp
import numpy as np
from jax import lax
from jax.experimental import pallas as pl
from jax.experimental.pallas import tpu as pltpu

F32 = jnp.float32
BF16 = jnp.bfloat16

D_MODEL = 1024
DEPTH = 4
CHUNK = 64
HEAD_DIM = 64
H_DIFF = 4
N_PAST_CHUNKS = 8
REL_CLIP = 128
W_BRANCH = 512
QKV_W = 9 * W_BRANCH
N_BRANCH = 3
D_FF = int(math.ceil(8 * D_MODEL / 3 / 256)) * 256
RMS_EPS = 1e-6
QK_SCALE = HEAD_DIM ** -0.5

LANES = 128
T_Q = 256
T_K = 128
T_G = 256
NEG = -1e30
SB_DEAD = -150.0
ONES_ROWS = 16
SB_LOOKAHEAD = 1
LOG2E = 1.4426950408889634
SAFE_LOG2 = 60.0
VMEM_LIMIT = 56 * 1024 * 1024

ROW_TILE = 1024
FF_CHUNK = 256


def _rms(x, g):
    return x * lax.rsqrt(jnp.mean(x * x, axis=-1, keepdims=True) + RMS_EPS) * g


def _rms_halves(x, g):
    lane = lax.broadcasted_iota(jnp.int32, (1, LANES), 1)
    first = lane < HEAD_DIM
    x2 = x * x
    s0 = jnp.sum(jnp.where(first, x2, 0.0), axis=-1, keepdims=True)
    s1 = jnp.sum(jnp.where(first, 0.0, x2), axis=-1, keepdims=True)
    ms = jnp.where(first, s0, s1) * (1.0 / HEAD_DIM)
    return x * lax.rsqrt(ms + RMS_EPS) * g


def _qk_logit_bound(gq, gk):
    return (1.02 * LOG2E * QK_SCALE * HEAD_DIM) * jnp.max(jnp.abs(gq)) * jnp.max(jnp.abs(gk))


def _head_row_masks():
    row = lax.broadcasted_iota(jnp.int32, (LANES, 1), 0)
    return row < HEAD_DIM, row >= HEAD_DIM


def _split_heads(qt, first, second):
    zero = jnp.zeros_like(qt)
    return jnp.where(first, qt, zero), jnp.where(second, qt, zero)


N_SECTIONS = 9


def _qkv_kernel(x_ref, g_ref, w_ref, qkg_ref, k_out, qt_out, vt_out):
    h = _rms(x_ref[...], g_ref[...]).astype(BF16)

    def project(sec):
        return jnp.dot(h, w_ref[:, sec * W_BRANCH:(sec + 1) * W_BRANCH], preferred_element_type=F32)

    z_next = project(0)
    for sec in range(N_SECTIONS):
        z = z_next
        if sec + 1 < N_SECTIONS:
            z_next = project(sec + 1)
        branch, role = divmod(sec, 3)
        for cb in range(W_BRANCH // LANES):
            blk = z[:, cb * LANES:(cb + 1) * LANES]
            col = sec * W_BRANCH + cb * LANES
            if branch > 0 and role < 2:
                blk = _rms_halves(blk, qkg_ref[:, col:col + LANES])
            elif role == 0:
                blk = blk * (QK_SCALE * LOG2E)
            out_col = branch * W_BRANCH + cb * LANES
            if role == 1:
                k_out[:, out_col:out_col + LANES] = blk.astype(BF16)
            else:
                out = qt_out if role == 0 else vt_out
                for r in range(ROW_TILE // T_Q):
                    out[r, out_col:out_col + LANES, :] = blk[r * T_Q:(r + 1) * T_Q, :].T.astype(BF16)


def _qkv_proj(x, g, w, qk_gain, layer):
    m = x.shape[0]
    width = N_BRANCH * W_BRANCH
    tiles = ROW_TILE // T_Q
    lay = lambda i: (layer, 0, 0)
    transposed = jax.ShapeDtypeStruct((m // T_Q, width, T_Q), BF16)
    return pl.pallas_call(
        _qkv_kernel,
        grid=(m // ROW_TILE,),
        in_specs=[
            pl.BlockSpec((ROW_TILE, D_MODEL), lambda i: (i, 0)),
            pl.BlockSpec((None, 1, D_MODEL), lay),
            pl.BlockSpec((None, D_MODEL, QKV_W), lay, pipeline_mode=pl.Buffered(1)),
            pl.BlockSpec((None, 1, QKV_W), lay),
        ],
        out_specs=[
            pl.BlockSpec((ROW_TILE, width), lambda i: (i, 0)),
            pl.BlockSpec((tiles, width, T_Q), lambda i: (i, 0, 0)),
            pl.BlockSpec((tiles, width, T_Q), lambda i: (i, 0, 0)),
        ],
        out_shape=[jax.ShapeDtypeStruct((m, width), BF16), transposed, transposed],
        compiler_params=pltpu.CompilerParams(
            dimension_semantics=("arbitrary",), vmem_limit_bytes=VMEM_LIMIT),
        name="qkv_proj",
    )(x, g, w, qk_gain)


def _merge_kernel(x_ref, g_ref, oa_ref, ob_ref, oc_ref, wg_ref, bg_ref, wbr_ref, wo_ref, out_ref):
    x = x_ref[...]
    h = _rms(x, g_ref[...]).astype(BF16)
    merged = None
    for br, o_ref in enumerate((oa_ref, ob_ref, oc_ref)):
        g_lin = jnp.dot(h, wg_ref[:, br * D_MODEL:(br + 1) * D_MODEL], preferred_element_type=F32)
        gate = 1.0 / (1.0 + jnp.exp(-(g_lin + bg_ref[br:br + 1, :])))
        proj = jnp.dot(o_ref[...], wbr_ref[br * W_BRANCH:(br + 1) * W_BRANCH, :],
                       preferred_element_type=F32)
        term = gate * proj
        merged = term if merged is None else merged + term
    out_ref[...] = x + jnp.dot(merged.astype(BF16), wo_ref[...], preferred_element_type=F32)


def _merge_out(x, g, o_a, o_b, o_c, w_gate, b_gate, w_br, w_out, layer):
    m = x.shape[0]
    row = lambda i: (i, 0)
    lay = lambda i: (layer, 0, 0)
    return pl.pallas_call(
        _merge_kernel,
        grid=(m // ROW_TILE,),
        in_specs=[
            pl.BlockSpec((ROW_TILE, D_MODEL), row),
            pl.BlockSpec((None, 1, D_MODEL), lay),
            pl.BlockSpec((ROW_TILE, W_BRANCH), row),
            pl.BlockSpec((ROW_TILE, W_BRANCH), row),
            pl.BlockSpec((ROW_TILE, W_BRANCH), row),
            pl.BlockSpec((None, D_MODEL, N_BRANCH * D_MODEL), lay, pipeline_mode=pl.Buffered(1)),
            pl.BlockSpec((None, N_BRANCH, D_MODEL), lay),
            pl.BlockSpec((None, N_BRANCH * W_BRANCH, D_MODEL), lay, pipeline_mode=pl.Buffered(1)),
            pl.BlockSpec((None, D_MODEL, D_MODEL), lay, pipeline_mode=pl.Buffered(1)),
        ],
        out_specs=pl.BlockSpec((ROW_TILE, D_MODEL), row),
        out_shape=jax.ShapeDtypeStruct((m, D_MODEL), F32),
        compiler_params=pltpu.CompilerParams(
            dimension_semantics=("arbitrary",), vmem_limit_bytes=VMEM_LIMIT),
        name="merge_out",
    )(x, g, o_a, o_b, o_c, w_gate, b_gate, w_br, w_out)


def _ffn_kernel(x_ref, g_ref, wgu_ref, wd_ref, out_ref, act_ref):
    x = x_ref[...]
    h = _rms(x, g_ref[...]).astype(BF16)
    for c in range(0, D_FF, FF_CHUNK):
        gate = jnp.dot(h, wgu_ref[:, c:c + FF_CHUNK], preferred_element_type=F32)
        up = jnp.dot(h, wgu_ref[:, D_FF + c:D_FF + c + FF_CHUNK], preferred_element_type=F32)
        silu = gate / (1.0 + jnp.exp(-gate))
        act_ref[:, c:c + FF_CHUNK] = (silu * up).astype(BF16)
    out_ref[...] = x + jnp.dot(act_ref[...], wd_ref[...], preferred_element_type=F32)


def _ffn(x, g, w_gu, w_down, layer):
    m = x.shape[0]
    row = lambda i: (i, 0)
    lay = lambda i: (layer, 0, 0)
    return pl.pallas_call(
        _ffn_kernel,
        grid=(m // ROW_TILE,),
        in_specs=[
            pl.BlockSpec((ROW_TILE, D_MODEL), row),
            pl.BlockSpec((None, 1, D_MODEL), lay),
            pl.BlockSpec((None, D_MODEL, 2 * D_FF), lay, pipeline_mode=pl.Buffered(1)),
            pl.BlockSpec((None, D_FF, D_MODEL), lay, pipeline_mode=pl.Buffered(1)),
        ],
        out_specs=pl.BlockSpec((ROW_TILE, D_MODEL), row),
        out_shape=jax.ShapeDtypeStruct((m, D_MODEL), F32),
        scratch_shapes=[pltpu.VMEM((ROW_TILE, D_FF), BF16)],
        compiler_params=pltpu.CompilerParams(
            dimension_semantics=("arbitrary",), vmem_limit_bytes=VMEM_LIMIT),
        name="ffn",
    )(x, g, w_gu, w_down)


def _sb_weights(z, cum, carry, mask):
    sp = jnp.log2(1.0 + jnp.exp2(-jnp.abs(z)))
    log_beta = jnp.minimum(z, 0.0) - sp
    log_1m = log_beta - z
    if mask is not None:
        log_1m = jnp.where(mask, log_1m, 0.0)
    hi = log_1m.astype(BF16)
    lo = (log_1m - hi.astype(F32)).astype(BF16)
    afters = []
    for u in reversed(range(z.shape[0] // T_K)):
        r0, r1 = u * T_K, (u + 1) * T_K
        within = jnp.dot(cum, jnp.concatenate([hi[r0:r1], lo[r0:r1]], axis=0),
                         preferred_element_type=F32)
        afters.append(within + carry)
        carry = carry + within[0:1, :] + log_1m[r0:r0 + 1, :]
    after = afters[0] if len(afters) == 1 else jnp.concatenate(afters[::-1], axis=0)
    w = jnp.exp2(log_beta + after)
    if mask is not None:
        w = jnp.where(mask, w, 0.0)
    return w.astype(BF16), carry


class _StickBreaking:
    HEADS = range(2)
    GROUPS_IN_MAIN = 2

    def __init__(self, qt_ref, k_ref, vt_ref, o_ref, acc_ref, carry_ref):
        self.qt_ref, self.k_ref, self.vt_ref, self.o_ref = qt_ref, k_ref, vt_ref, o_ref
        self.acc_ref, self.carry_ref = acc_ref, carry_ref
        self.n_q = k_ref.shape[0] // T_Q
        self.masks = _head_row_masks()
        kk = lax.broadcasted_iota(jnp.int32, (T_K, T_K), 0)
        kk2 = lax.broadcasted_iota(jnp.int32, (T_K, T_K), 1)
        later = jnp.where(kk2 > kk, 1.0, 0.0).astype(BF16)
        self.cum = jnp.concatenate([later, later], axis=1)
        self.strict = (lax.broadcasted_iota(jnp.int32, (T_G, T_Q), 0)
                       < lax.broadcasted_iota(jnp.int32, (T_G, T_Q), 1))

    def main_groups(self, qi):
        return [g for g in range(qi, qi - self.GROUPS_IN_MAIN, -1) if g >= 0]

    def n_main_items(self):
        return sum(len(self.main_groups(qi)) for qi in range(self.n_q)) * len(self.HEADS)

    def load_q(self, qi):
        return _split_heads(self.qt_ref[qi], *self.masks)

    def logits(self, g, qts):
        k2 = self.k_ref[pl.ds(pl.multiple_of(g * T_G, T_G), T_G), :]
        return [jnp.dot(k2, qts[h], preferred_element_type=F32) for h in self.HEADS]

    def add_group(self, g, h, z, mask, carry, acc):
        w, carry = _sb_weights(z, self.cum, carry, mask)
        vth = self.vt_ref[g][h * HEAD_DIM:(h + 1) * HEAD_DIM, :]
        return carry, acc + jnp.dot(vth, w, preferred_element_type=F32)

    def store(self, qi, accs):
        q0 = pl.multiple_of(qi * T_Q, T_Q)
        self.o_ref[pl.ds(q0, T_Q), :] = jnp.concatenate(accs, axis=0).T.astype(BF16)

    def main(self):
        pairs = [(qi, g) for qi in range(self.n_q) for g in self.main_groups(qi)]
        zs = {}

        def issue(j):
            for qi, g in pairs[j:j + 1]:
                zs[qi, g] = self.logits(g, self.load_q(qi))

        for j in range(SB_LOOKAHEAD):
            issue(j)
        for j, (qi, g) in enumerate(pairs):
            issue(j + SB_LOOKAHEAD)
            if g == qi:
                carries = [jnp.zeros((1, T_Q), F32) for _ in self.HEADS]
                accs = [jnp.zeros((HEAD_DIM, T_Q), F32) for _ in self.HEADS]
            z = zs.pop((qi, g))
            for h in self.HEADS:
                carries[h], accs[h] = self.add_group(
                    g, h, z[h], self.strict if g == qi else None, carries[h], accs[h])
                yield
            if g == self.main_groups(qi)[-1]:
                self.store(qi, accs)
                for h in self.HEADS:
                    self.acc_ref[qi, h] = accs[h]
                    self.carry_ref[qi, h] = carries[h]

    def _live(self, carries):
        return jnp.max(functools.reduce(jnp.maximum, carries)) >= SB_DEAD

    def tails(self):
        tiles = [qi for qi in range(self.n_q) if qi - self.GROUPS_IN_MAIN >= 0]

        def tile_tail(qi):
            qts = self.load_q(qi)

            def live(st):
                g, carries, _ = st
                return (g >= 0) & self._live(carries)

            def body(st):
                g, carries, accs = st
                carries, accs = list(carries), list(accs)
                zs = self.logits(g, qts)
                for h in self.HEADS:
                    carries[h], accs[h] = self.add_group(g, h, zs[h], None, carries[h], accs[h])
                return g - 1, tuple(carries), tuple(accs)

            init = (qi - self.GROUPS_IN_MAIN,
                    tuple(self.carry_ref[qi, h] for h in self.HEADS),
                    tuple(self.acc_ref[qi, h] for h in self.HEADS))
            _, _, accs = lax.while_loop(live, body, init)
            self.store(qi, list(accs))

        def all_tails():
            for qi in tiles:
                tile_tail(qi)

        any_live = self._live([self.carry_ref[qi, h] for qi in tiles for h in self.HEADS])
        lax.cond(any_live, all_tails, lambda: None)


def _exhaust(items):
    for _ in items:
        pass


def _sb_scratch(seq):
    n_q = seq // T_Q
    return [pltpu.VMEM((n_q, 2, HEAD_DIM, T_Q), F32), pltpu.VMEM((n_q, 2, 1, T_Q), F32)]


def _head_block_specs(branch, seq, index):
    def transposed(*ids):
        bi, hb = index(*ids)
        return bi, 0, branch * (W_BRANCH // LANES) + hb, 0

    def rows(*ids):
        bi, hb = index(*ids)
        return bi, 0, branch * (W_BRANCH // LANES) + hb

    t_spec = pl.BlockSpec((None, seq // T_Q, LANES, T_Q), transposed)
    return [t_spec, pl.BlockSpec((None, seq, LANES), rows), t_spec]


def _softmax_update(scores, vts, state):
    m, l, acc = state
    m_new = m
    for s in scores:
        m_new = jnp.maximum(m_new, jnp.max(s, axis=0, keepdims=True))
    alpha = jnp.exp2(m - m_new)
    l = alpha * l
    acc = alpha * acc
    for s, vt in zip(scores, vts):
        p = jnp.exp2(s - m_new)
        l = l + jnp.sum(p, axis=0, keepdims=True)
        acc = acc + jnp.dot(vt, p.astype(BF16), preferred_element_type=F32)
    return m_new, l, acc


def _diff_paths(qt_ref, kn_ref, vt_ref, bias_ref, lam_ref, sg_ref, o_ref, lam_init):
    seq = kn_ref.shape[0]
    n_q = seq // T_Q
    first, second = _head_row_masks()
    lq = lam_ref[...]
    lam = (jnp.exp(jnp.sum(lq[0:1] * lq[1:2], axis=-1, keepdims=True))
           - jnp.exp(jnp.sum(lq[2:3] * lq[3:4], axis=-1, keepdims=True)) + lam_init)

    def keys(g):
        return kn_ref[pl.ds(pl.multiple_of(g * T_G, T_G), T_G), :]

    def scores(qi, g):
        kn2, bias = keys(g), bias_ref[qi - g]
        return [jnp.dot(kn2, qtm, preferred_element_type=F32) + bias
                for qtm in _split_heads(qt_ref[qi], first, second)]

    def finish(qi, accs, sums):
        ob = accs[0] * (1.0 / sums[0]) - lam * (accs[1] * (1.0 / sums[1]))
        y = ob * lax.rsqrt(jnp.mean(ob * ob, axis=0, keepdims=True) + RMS_EPS)
        q0 = pl.multiple_of(qi * T_Q, T_Q)
        o_ref[pl.ds(q0, T_Q), :] = (y.T * sg_ref[...] * (1.0 - lam_init)).astype(BF16)

    def q_pair(qp, _):
        tiles = (2 * qp, 2 * qp + 1)

        def trip(it, states):
            gs = (2 * it, 2 * it + 1)
            vts = [vt_ref[g] for g in gs]
            scs = [[scores(qi, g) for g in gs] for qi in tiles]
            states = [list(st) for st in states]
            for i in range(2):
                for mp in range(2):
                    states[i][mp] = _softmax_update([sc[mp] for sc in scs[i]], vts, states[i][mp])
            return tuple(tuple(st) for st in states)

        init = (jnp.full((1, T_Q), NEG, F32), jnp.zeros((1, T_Q), F32), jnp.zeros((LANES, T_Q), F32))
        states = lax.fori_loop(0, qp, trip, ((init, init), (init, init)))
        gs = (2 * qp, 2 * qp + 1)
        vts = [vt_ref[g] for g in gs]
        scs = [[scores(tiles[0], gs[0])], [scores(tiles[1], g) for g in gs]]
        for i in range(2):
            done = [_softmax_update([sc[mp] for sc in scs[i]], vts, states[i][mp]) for mp in range(2)]
            finish(tiles[i], [st[2] for st in done], [st[1] for st in done])
        return 0

    work = [(qi, g) for qi in range(n_q) for g in range(qi + 1)]
    ones_rows = jnp.ones((ONES_ROWS, T_G), BF16)

    def bounded():
        z_next = scores(*work[0])
        for step, (qi, g) in enumerate(work):
            z = z_next
            if step + 1 < len(work):
                z_next = scores(*work[step + 1])
            vt1 = jnp.concatenate([vt_ref[g], ones_rows], axis=0)
            prods = [jnp.dot(vt1, jnp.exp2(z[mp]).astype(BF16), preferred_element_type=F32)
                     for mp in range(2)]
            pvs = [r[:LANES] for r in prods]
            psums = [r[LANES:LANES + 1] for r in prods]
            if g == 0:
                accs, sums = pvs, psums
            else:
                accs = [a + pv for a, pv in zip(accs, pvs)]
                sums = [s + ps_ for s, ps_ in zip(sums, psums)]
            if g == qi:
                finish(qi, accs, sums)
            yield

    def general():
        lax.fori_loop(0, n_q // 2, q_pair, 0)

    return bounded, len(work), general


def _alibi_tiles(n_q):
    slopes = np.asarray([2.0 ** (-8.0 * (i + 1) / H_DIFF) for i in range(H_DIFF)], np.float32)
    slope2 = jnp.asarray(slopes * LOG2E, F32)[:, None, None, None]
    kpos = lax.broadcasted_iota(jnp.int32, (T_G, T_Q), 0)
    qpos = lax.broadcasted_iota(jnp.int32, (T_G, T_Q), 1)
    tiles_back = lax.broadcasted_iota(jnp.int32, (n_q, 1, 1), 0)
    dist = (qpos - kpos)[None] + T_G * tiles_back
    bias = -slope2 * jnp.abs(dist).astype(F32)[None]
    visible = (tiles_back > 0) | ((kpos // CHUNK) <= (qpos // CHUNK))[None]
    return jnp.where(visible[None], bias, NEG)


N_REL_GROUPS = (N_PAST_CHUNKS * CHUNK + T_Q - 1) // T_G + 1


def _chunk_items(qt_ref, kn_ref, vt_ref, tab_ref, o_ref, sc_ref):
    seq = kn_ref.shape[0]
    first, second = _head_row_masks()
    deltas = tuple(range(N_REL_GROUPS - 1, -1, -1))
    ones_rows = jnp.ones((ONES_ROWS, T_G), BF16)

    def score_stage(qi, tile_deltas, slot):
        qts = _split_heads(qt_ref[qi], first, second)
        for dl in tile_deltas:
            k2 = kn_ref[pl.ds(pl.multiple_of((qi - dl) * T_G, T_G), T_G), :]
            for h in range(2):
                sc_ref[slot, h, dl] = (jnp.dot(k2, qts[h], preferred_element_type=F32)
                                       + tab_ref[h, dl])

    def softmax_stage(qi, tile_deltas, slot, fixed_shift):
        outs = []
        for h in range(2):
            scores = [sc_ref[slot, h, dl] for dl in tile_deltas]
            if fixed_shift:
                ps = [jnp.exp2(s) for s in scores]
            else:
                m = functools.reduce(jnp.maximum, [jnp.max(s, axis=0, keepdims=True) for s in scores])
                ps = [jnp.exp2(s - m) for s in scores]
            acc = None
            for dl, p in zip(tile_deltas, ps):
                vth = jnp.concatenate(
                    [vt_ref[qi - dl][h * HEAD_DIM:(h + 1) * HEAD_DIM, :], ones_rows], axis=0)
                pv = jnp.dot(vth, p.astype(BF16), preferred_element_type=F32)
                acc = pv if acc is None else acc + pv
            outs.append(acc[:HEAD_DIM] * (1.0 / acc[HEAD_DIM:HEAD_DIM + 1]))
        q0 = pl.multiple_of(qi * T_Q, T_Q)
        o_ref[pl.ds(q0, T_Q), :] = jnp.concatenate(outs, axis=0).T.astype(BF16)

    def tile_deltas(qi):
        return tuple(dl for dl in deltas if qi - dl >= 0)

    n_q = seq // T_Q

    def run(fixed_shift):
        score_stage(0, tile_deltas(0), 0)
        for qi in range(n_q):
            if qi + 1 < n_q:
                score_stage(qi + 1, tile_deltas(qi + 1), (qi + 1) % 2)
            yield
            softmax_stage(qi, tile_deltas(qi), qi % 2, fixed_shift)
            yield

    return run, 2 * n_q


def _chunk_logit_bound(gq, gk, tabs):
    return _qk_logit_bound(gq, gk) + jnp.max(jnp.where(tabs > 0.5 * NEG, jnp.abs(tabs), 0.0))


def _interleave(streams):
    order = sorted(((i + 0.5) / n, s) for s, (_, n) in enumerate(streams) for i in range(n))
    for _, s in order:
        next(streams[s][0], None)
    for items, _ in streams:
        _exhaust(items)


def _mix_kernel(bounded_ref, qa_ref, ka_ref, va_ref, qb_ref, kb_ref, vb_ref, qc_ref, kc_ref, vc_ref,
                alibi_ref, lam_ref, sg_ref, tab_ref,
                oa_ref, ob_ref, oc_ref, acc_ref, carry_ref, sc_ref, *, layer, lam_init):
    sb = _StickBreaking(qa_ref, ka_ref, va_ref, oa_ref, acc_ref, carry_ref)
    diff_items, n_diff, diff_online = _diff_paths(qb_ref, kb_ref, vb_ref, alibi_ref, lam_ref, sg_ref,
                                                  ob_ref, lam_init)
    chunk_items, n_chunk = _chunk_items(qc_ref, kc_ref, vc_ref, tab_ref, oc_ref, sc_ref)

    def interleaved():
        _interleave([(sb.main(), sb.n_main_items()), (diff_items(), n_diff),
                     (chunk_items(True), n_chunk)])

    def one_by_one():
        _exhaust(sb.main())
        diff_online()
        _exhaust(chunk_items(False))

    lax.cond(bounded_ref[layer] != 0, interleaved, one_by_one)
    sb.tails()


def _logits_bounded(gq_diff, gk_diff, gq_ch, gk_ch, tab):
    flags = [(_qk_logit_bound(gq_diff[l], gk_diff[l]) <= SAFE_LOG2)
             & (_chunk_logit_bound(gq_ch[l], gk_ch[l], tab[l]) <= SAFE_LOG2) for l in range(DEPTH)]
    return jnp.stack(flags).astype(jnp.int32)


def _mix_attn(qt, k, vt, alibi, lam_qk, subln_g, tab, bounded, layer, lam_init):
    b, s, _ = k.shape
    n_hb = W_BRANCH // LANES
    index = lambda hb, bi: (bi, hb)
    lay = lambda hb, bi: (layer, 0, 0)
    out_spec = pl.BlockSpec((None, s, LANES), lambda hb, bi: (bi, 0, hb))
    out_shape = jax.ShapeDtypeStruct((b, s, W_BRANCH), BF16)
    return pl.pallas_call(
        functools.partial(_mix_kernel, layer=layer, lam_init=lam_init),
        grid=(n_hb, b),
        in_specs=[
            pl.BlockSpec(memory_space=pltpu.SMEM),
            *_head_block_specs(0, s, index), *_head_block_specs(1, s, index),
            *_head_block_specs(2, s, index),
            pl.BlockSpec((None, s // T_Q, T_G, T_Q), lambda hb, bi: (hb, 0, 0, 0)),
            pl.BlockSpec((None, 4, HEAD_DIM), lay),
            pl.BlockSpec((None, 1, LANES), lay),
            pl.BlockSpec((None, 2, N_REL_GROUPS, T_G, T_Q), lambda hb, bi: (layer, hb, 0, 0, 0)),
        ],
        out_specs=[out_spec, out_spec, out_spec],
        out_shape=[out_shape, out_shape, out_shape],
        scratch_shapes=_sb_scratch(s) + [pltpu.VMEM((2, 2, N_REL_GROUPS, T_G, T_Q), F32)],
        compiler_params=pltpu.CompilerParams(
            dimension_semantics=("arbitrary", "arbitrary"), vmem_limit_bytes=VMEM_LIMIT),
        name="mix_attn",
    )(bounded, qt, k, vt, qt, k, vt, qt, k, vt, alibi, lam_qk, subln_g, tab)


def _rel_bias_tiles(rel_bias):
    lead = rel_bias.shape[:-1]
    span = (N_REL_GROUPS - 1) * T_G + T_Q
    period = span + T_G
    edge_lo = jnp.broadcast_to(rel_bias[..., :1], lead + (T_G - REL_CLIP,))
    edge_hi = jnp.broadcast_to(rel_bias[..., -1:], lead + (span - REL_CLIP - 1,))
    row = jnp.concatenate([rel_bias[..., REL_CLIP:], edge_hi, edge_lo, rel_bias[..., :REL_CLIP]], axis=-1)
    flat = jnp.tile(row, (1,) * len(lead) + (T_G,))[..., :T_G * (period - 1)]
    toep = flat.reshape(lead + (T_G, period - 1))
    tiles = jnp.stack([toep[..., d * T_G:d * T_G + T_Q] for d in range(N_REL_GROUPS)], axis=-3)
    kchunk = np.arange(T_G)[:, None] // CHUNK
    qchunk = np.arange(T_Q)[None, :] // CHUNK
    dd = np.stack([qchunk - kchunk + d * (T_G // CHUNK) for d in range(N_REL_GROUPS)])
    return jnp.where((dd >= 0) & (dd <= N_PAST_CHUNKS), tiles * LOG2E, NEG)


def kernel(x, norm_mix_g, w_in, b_gate, qk_g_diff, lambda_qk, subln_g, qk_g_ch, rel_bias,
           w_branch_sb, w_branch_diff, w_branch_ch, w_out, norm_ffn_g, w_gu, w_down):
    b, s, d = x.shape
    m = b * s
    w_qkv = w_in.astype(BF16)
    w_gate = w_qkv[:, :, QKV_W:]
    w_br = jnp.concatenate([w_branch_sb, w_branch_diff, w_branch_ch], axis=1).astype(BF16)
    w_out_b = w_out.astype(BF16)
    w_gu_b = w_gu.astype(BF16)
    w_down_b = w_down.astype(BF16)
    g_mix = norm_mix_g.reshape(DEPTH, 1, d)
    g_ffn = norm_ffn_g.reshape(DEPTH, 1, d)
    gq_diff = jnp.tile(qk_g_diff[:, 0:1, :], (1, 1, 2))
    gk_diff = jnp.tile(qk_g_diff[:, 1:2, :], (1, 1, 2))
    gq_ch = jnp.tile(qk_g_ch[:, 0:1, :], (1, 1, 2))
    gk_ch = jnp.tile(qk_g_ch[:, 1:2, :], (1, 1, 2))
    sg = subln_g.reshape(DEPTH, 1, 2 * HEAD_DIM)
    tab = _rel_bias_tiles(rel_bias)
    alibi = _alibi_tiles(s // T_Q)
    bounded = _logits_bounded(gq_diff, gk_diff, gq_ch, gk_ch, tab)
    ones = jnp.ones((DEPTH, 1, W_BRANCH), F32)
    widen = lambda g: jnp.tile(g, (1, 1, W_BRANCH // LANES))
    q_scale = QK_SCALE * LOG2E
    qk_gain = jnp.concatenate([ones, ones, ones, widen(gq_diff) * q_scale, widen(gk_diff), ones,
                               widen(gq_ch) * q_scale, widen(gk_ch), ones], axis=-1)

    xf = x.reshape(m, d)
    for layer in range(DEPTH):
        lam_init = 0.8 - 0.6 * math.exp(-0.3 * layer)
        k, qt, vt = _qkv_proj(xf, g_mix, w_qkv, qk_gain, layer)
        k = k.reshape(b, s, N_BRANCH * W_BRANCH)
        qt = qt.reshape(b, s // T_Q, N_BRANCH * W_BRANCH, T_Q)
        vt = vt.reshape(b, s // T_Q, N_BRANCH * W_BRANCH, T_Q)
        o_a, o_b, o_c = _mix_attn(qt, k, vt, alibi, lambda_qk, sg, tab, bounded, layer, lam_init)
        xf = _merge_out(xf, g_mix, o_a.reshape(m, W_BRANCH), o_b.reshape(m, W_BRANCH),
                        o_c.reshape(m, W_BRANCH), w_gate, b_gate, w_br, w_out_b, layer)
        xf = _ffn(xf, g_ffn, w_gu_b, w_down_b, layer)
    return xf.reshape(b, s, d)
```

```python
import functools
import math

import jax
import jax.numpy as jnp
import numpy as np
from jax import lax
from jax.experimental import pallas as pl
from jax.experimental.pallas import tpu as pltpu

F32 = jnp.float32
BF16 = jnp.bfloat16

D_MODEL = 1024
DEPTH = 4
CHUNK = 64
HEAD_DIM = 64
H_DIFF = 4
N_PAST_CHUNKS = 8
REL_CLIP = 128
W_BRANCH = 512
QKV_W = 9 * W_BRANCH
N_BRANCH = 3
D_FF = int(math.ceil(8 * D_MODEL / 3 / 256)) * 256
RMS_EPS = 1e-6
QK_SCALE = HEAD_DIM ** -0.5

LANES = 128
T_Q = 256
T_K = 128
T_G = 256
NEG = -1e30
SB_DEAD = -150.0
ONES_ROWS = 16
SB_LOOKAHEAD = 1
LOG2E = 1.4426950408889634
SAFE_LOG2 = 60.0
VMEM_LIMIT = 56 * 1024 * 1024

ROW_TILE = 1024
FF_CHUNK = 256


def _rms(x, g):
    return x * lax.rsqrt(jnp.mean(x * x, axis=-1, keepdims=True) + RMS_EPS) * g


def _rms_halves(x, g):
    lane = lax.broadcasted_iota(jnp.int32, (1, LANES), 1)
    first = lane < HEAD_DIM
    x2 = x * x
    s0 = jnp.sum(jnp.where(first, x2, 0.0), axis=-1, keepdims=True)
    s1 = jnp.sum(jnp.where(first, 0.0, x2), axis=-1, keepdims=True)
    ms = jnp.where(first, s0, s1) * (1.0 / HEAD_DIM)
    return x * lax.rsqrt(ms + RMS_EPS) * g


def _qk_logit_bound(gq, gk):
    return (1.02 * LOG2E * QK_SCALE * HEAD_DIM) * jnp.max(jnp.abs(gq)) * jnp.max(jnp.abs(gk))


def _head_row_masks():
    row = lax.broadcasted_iota(jnp.int32, (LANES, 1), 0)
    return row < HEAD_DIM, row >= HEAD_DIM


def _split_heads(qt, first, second):
    zero = jnp.zeros_like(qt)
    return jnp.where(first, qt, zero), jnp.where(second, qt, zero)


N_SECTIONS = 9


def _qkv_kernel(x_ref, g_ref, w_ref, qkg_ref, k_out, qt_out, vt_out):
    h = _rms(x_ref[...], g_ref[...]).astype(BF16)

    def project(sec):
        return jnp.dot(h, w_ref[:, sec * W_BRANCH:(sec + 1) * W_BRANCH], preferred_element_type=F32)

    z_next = project(0)
    for sec in range(N_SECTIONS):
        z = z_next
        if sec + 1 < N_SECTIONS:
            z_next = project(sec + 1)
        branch, role = divmod(sec, 3)
        for cb in range(W_BRANCH // LANES):
            blk = z[:, cb * LANES:(cb + 1) * LANES]
            col = sec * W_BRANCH + cb * LANES
            if branch > 0 and role < 2:
                blk = _rms_halves(blk, qkg_ref[:, col:col + LANES])
            elif role == 0:
                blk = blk * (QK_SCALE * LOG2E)
            out_col = branch * W_BRANCH + cb * LANES
            if role == 1:
                k_out[:, out_col:out_col + LANES] = blk.astype(BF16)
            else:
                out = qt_out if role == 0 else vt_out
                for r in range(ROW_TILE // T_Q):
                    out[r, out_col:out_col + LANES, :] = blk[r * T_Q:(r + 1) * T_Q, :].T.astype(BF16)


def _qkv_proj(x, g, w, qk_gain, layer):
    m = x.shape[0]
    width = N_BRANCH * W_BRANCH
    tiles = ROW_TILE // T_Q
    lay = lambda i: (layer, 0, 0)
    transposed = jax.ShapeDtypeStruct((m // T_Q, width, T_Q), BF16)
    return pl.pallas_call(
        _qkv_kernel,
        grid=(m // ROW_TILE,),
        in_specs=[
            pl.BlockSpec((ROW_TILE, D_MODEL), lambda i: (i, 0)),
            pl.BlockSpec((None, 1, D_MODEL), lay),
            pl.BlockSpec((None, D_MODEL, QKV_W), lay, pipeline_mode=pl.Buffered(1)),
            pl.BlockSpec((None, 1, QKV_W), lay),
        ],
        out_specs=[
            pl.BlockSpec((ROW_TILE, width), lambda i: (i, 0)),
            pl.BlockSpec((tiles, width, T_Q), lambda i: (i, 0, 0)),
            pl.BlockSpec((tiles, width, T_Q), lambda i: (i, 0, 0)),
        ],
        out_shape=[jax.ShapeDtypeStruct((m, width), BF16), transposed, transposed],
        compiler_params=pltpu.CompilerParams(
            dimension_semantics=("arbitrary",), vmem_limit_bytes=VMEM_LIMIT),
        name="qkv_proj",
    )(x, g, w, qk_gain)


def _merge_kernel(x_ref, g_ref, oa_ref, ob_ref, oc_ref, wg_ref, bg_ref, wbr_ref, wo_ref, out_ref):
    x = x_ref[...]
    h = _rms(x, g_ref[...]).astype(BF16)
    merged = None
    for br, o_ref in enumerate((oa_ref, ob_ref, oc_ref)):
        g_lin = jnp.dot(h, wg_ref[:, br * D_MODEL:(br + 1) * D_MODEL], preferred_element_type=F32)
        gate = 1.0 / (1.0 + jnp.exp(-(g_lin + bg_ref[br:br + 1, :])))
        proj = jnp.dot(o_ref[...], wbr_ref[br * W_BRANCH:(br + 1) * W_BRANCH, :],
                       preferred_element_type=F32)
        term = gate * proj
        merged = term if merged is None else merged + term
    out_ref[...] = x + jnp.dot(merged.astype(BF16), wo_ref[...], preferred_element_type=F32)


def _merge_out(x, g, o_a, o_b, o_c, w_gate, b_gate, w_br, w_out, layer):
    m = x.shape[0]
    row = lambda i: (i, 0)
    lay = lambda i: (layer, 0, 0)
    return pl.pallas_call(
        _merge_kernel,
        grid=(m // ROW_TILE,),
        in_specs=[
            pl.BlockSpec((ROW_TILE, D_MODEL), row),
            pl.BlockSpec((None, 1, D_MODEL), lay),
            pl.BlockSpec((ROW_TILE, W_BRANCH), row),
            pl.BlockSpec((ROW_TILE, W_BRANCH), row),
            pl.BlockSpec((ROW_TILE, W_BRANCH), row),
            pl.BlockSpec((None, D_MODEL, N_BRANCH * D_MODEL), lay, pipeline_mode=pl.Buffered(1)),
            pl.BlockSpec((None, N_BRANCH, D_MODEL), lay),
            pl.BlockSpec((None, N_BRANCH * W_BRANCH, D_MODEL), lay, pipeline_mode=pl.Buffered(1)),
            pl.BlockSpec((None, D_MODEL, D_MODEL), lay, pipeline_mode=pl.Buffered(1)),
        ],
        out_specs=pl.BlockSpec((ROW_TILE, D_MODEL), row),
        out_shape=jax.ShapeDtypeStruct((m, D_MODEL), F32),
        compiler_params=pltpu.CompilerParams(
            dimension_semantics=("arbitrary",), vmem_limit_bytes=VMEM_LIMIT),
        name="merge_out",
    )(x, g, o_a, o_b, o_c, w_gate, b_gate, w_br, w_out)


def _ffn_kernel(x_ref, g_ref, wgu_ref, wd_ref, out_ref, act_ref):
    x = x_ref[...]
    h = _rms(x, g_ref[...]).astype(BF16)
    for c in range(0, D_FF, FF_CHUNK):
        gate = jnp.dot(h, wgu_ref[:, c:c + FF_CHUNK], preferred_element_type=F32)
        up = jnp.dot(h, wgu_ref[:, D_FF + c:D_FF + c + FF_CHUNK], preferred_element_type=F32)
        silu = gate / (1.0 + jnp.exp(-gate))
        act_ref[:, c:c + FF_CHUNK] = (silu * up).astype(BF16)
    out_ref[...] = x + jnp.dot(act_ref[...], wd_ref[...], preferred_element_type=F32)


def _ffn(x, g, w_gu, w_down, layer):
    m = x.shape[0]
    row = lambda i: (i, 0)
    lay = lambda i: (layer, 0, 0)
    return pl.pallas_call(
        _ffn_kernel,
        grid=(m // ROW_TILE,),
        in_specs=[
            pl.BlockSpec((ROW_TILE, D_MODEL), row),
            pl.BlockSpec((None, 1, D_MODEL), lay),
            pl.BlockSpec((None, D_MODEL, 2 * D_FF), lay, pipeline_mode=pl.Buffered(1)),
            pl.BlockSpec((None, D_FF, D_MODEL), lay, pipeline_mode=pl.Buffered(1)),
        ],
        out_specs=pl.BlockSpec((ROW_TILE, D_MODEL), row),
        out_shape=jax.ShapeDtypeStruct((m, D_MODEL), F32),
        scratch_shapes=[pltpu.VMEM((ROW_TILE, D_FF), BF16)],
        compiler_params=pltpu.CompilerParams(
            dimension_semantics=("arbitrary",), vmem_limit_bytes=VMEM_LIMIT),
        name="ffn",
    )(x, g, w_gu, w_down)


def _sb_weights(z, cum, carry, mask):
    sp = jnp.log2(1.0 + jnp.exp2(-jnp.abs(z)))
    log_beta = jnp.minimum(z, 0.0) - sp
    log_1m = log_beta - z
    if mask is not None:
        log_1m = jnp.where(mask, log_1m, 0.0)
    hi = log_1m.astype(BF16)
    lo = (log_1m - hi.astype(F32)).astype(BF16)
    afters = []
    for u in reversed(range(z.shape[0] // T_K)):
        r0, r1 = u * T_K, (u + 1) * T_K
        within = jnp.dot(cum, jnp.concatenate([hi[r0:r1], lo[r0:r1]], axis=0),
                         preferred_element_type=F32)
        afters.append(within + carry)
        carry = carry + within[0:1, :] + log_1m[r0:r0 + 1, :]
    after = afters[0] if len(afters) == 1 else jnp.concatenate(afters[::-1], axis=0)
    w = jnp.exp2(log_beta + after)
    if mask is not None:
        w = jnp.where(mask, w, 0.0)
    return w.astype(BF16), carry


class _StickBreaking:
    HEADS = range(2)
    GROUPS_IN_MAIN = 2

    def __init__(self, qt_ref, k_ref, vt_ref, o_ref, acc_ref, carry_ref, z_ref):
        self.qt_ref, self.k_ref, self.vt_ref, self.o_ref = qt_ref, k_ref, vt_ref, o_ref
        self.acc_ref, self.carry_ref, self.z_ref = acc_ref, carry_ref, z_ref
        self.n_q = k_ref.shape[0] // T_Q
        self.masks = _head_row_masks()
        kk = lax.broadcasted_iota(jnp.int32, (T_K, T_K), 0)
        kk2 = lax.broadcasted_iota(jnp.int32, (T_K, T_K), 1)
        later = jnp.where(kk2 > kk, 1.0, 0.0).astype(BF16)
        self.cum = jnp.concatenate([later, later], axis=1)
        self.strict = (lax.broadcasted_iota(jnp.int32, (T_G, T_Q), 0)
                       < lax.broadcasted_iota(jnp.int32, (T_G, T_Q), 1))

    def main_groups(self, qi):
        return [g for g in range(qi, qi - self.GROUPS_IN_MAIN, -1) if g >= 0]

    def n_main_items(self):
        return sum(len(self.main_groups(qi)) for qi in range(self.n_q)) * len(self.HEADS)

    def load_q(self, qi):
        return _split_heads(self.qt_ref[qi], *self.masks)

    def logits(self, g, qts):
        k2 = self.k_ref[pl.ds(pl.multiple_of(g * T_G, T_G), T_G), :]
        return [jnp.dot(k2, qts[h], preferred_element_type=F32) for h in self.HEADS]

    def add_group(self, g, h, z, mask, carry, acc):
        w, carry = _sb_weights(z, self.cum, carry, mask)
        vth = self.vt_ref[g][h * HEAD_DIM:(h + 1) * HEAD_DIM, :]
        return carry, acc + jnp.dot(vth, w, preferred_element_type=F32)

    def store(self, qi, accs):
        q0 = pl.multiple_of(qi * T_Q, T_Q)
        self.o_ref[pl.ds(q0, T_Q), :] = jnp.concatenate(accs, axis=0).T.astype(BF16)

    def main(self):
        pairs = [(qi, g) for qi in range(self.n_q) for g in self.main_groups(qi)]
        n_slots = SB_LOOKAHEAD + 1

        def issue(j):
            for qi, g in pairs[j:j + 1]:
                for h, z in enumerate(self.logits(g, self.load_q(qi))):
                    self.z_ref[j % n_slots, h] = z

        for j in range(SB_LOOKAHEAD):
            issue(j)
        for j, (qi, g) in enumerate(pairs):
            issue(j + SB_LOOKAHEAD)
            if g == qi:
                carries = [jnp.zeros((1, T_Q), F32) for _ in self.HEADS]
                accs = [jnp.zeros((HEAD_DIM, T_Q), F32) for _ in self.HEADS]
            for h in self.HEADS:
                carries[h], accs[h] = self.add_group(
                    g, h, self.z_ref[j % n_slots, h], self.strict if g == qi else None,
                    carries[h], accs[h])
                yield
            if g == self.main_groups(qi)[-1]:
                self.store(qi, accs)
                for h in self.HEADS:
                    self.acc_ref[qi, h] = accs[h]
                    self.carry_ref[qi, h] = carries[h]

    def _live(self, carries):
        return jnp.max(functools.reduce(jnp.maximum, carries)) >= SB_DEAD

    def tails(self):
        tiles = [qi for qi in range(self.n_q) if qi - self.GROUPS_IN_MAIN >= 0]

        def tile_tail(qi):
            qts = self.load_q(qi)

            def live(st):
                g, carries, _ = st
                return (g >= 0) & self._live(carries)

            def body(st):
                g, carries, accs = st
                carries, accs = list(carries), list(accs)
                zs = self.logits(g, qts)
                for h in self.HEADS:
                    carries[h], accs[h] = self.add_group(g, h, zs[h], None, carries[h], accs[h])
                return g - 1, tuple(carries), tuple(accs)

            init = (qi - self.GROUPS_IN_MAIN,
                    tuple(self.carry_ref[qi, h] for h in self.HEADS),
                    tuple(self.acc_ref[qi, h] for h in self.HEADS))
            _, _, accs = lax.while_loop(live, body, init)
            self.store(qi, list(accs))

        def all_tails():
            for qi in tiles:
                tile_tail(qi)

        any_live = self._live([self.carry_ref[qi, h] for qi in tiles for h in self.HEADS])
        lax.cond(any_live, all_tails, lambda: None)


def _exhaust(items):
    for _ in items:
        pass


def _sb_scratch(seq):
    n_q = seq // T_Q
    return [pltpu.VMEM((n_q, 2, HEAD_DIM, T_Q), F32), pltpu.VMEM((n_q, 2, 1, T_Q), F32),
            pltpu.VMEM((SB_LOOKAHEAD + 1, 2, T_G, T_Q), F32)]


def _head_block_specs(branch, seq, index):
    def transposed(*ids):
        bi, hb = index(*ids)
        return bi, 0, branch * (W_BRANCH // LANES) + hb, 0

    def rows(*ids):
        bi, hb = index(*ids)
        return bi, 0, branch * (W_BRANCH // LANES) + hb

    t_spec = pl.BlockSpec((None, seq // T_Q, LANES, T_Q), transposed)
    return [t_spec, pl.BlockSpec((None, seq, LANES), rows), t_spec]


def _softmax_update(scores, vts, state):
    m, l, acc = state
    m_new = m
    for s in scores:
        m_new = jnp.maximum(m_new, jnp.max(s, axis=0, keepdims=True))
    alpha = jnp.exp2(m - m_new)
    l = alpha * l
    acc = alpha * acc
    for s, vt in zip(scores, vts):
        p = jnp.exp2(s - m_new)
        l = l + jnp.sum(p, axis=0, keepdims=True)
        acc = acc + jnp.dot(vt, p.astype(BF16), preferred_element_type=F32)
    return m_new, l, acc


def _diff_paths(qt_ref, kn_ref, vt_ref, bias_ref, lam_ref, sg_ref, o_ref, lam_init):
    seq = kn_ref.shape[0]
    n_q = seq // T_Q
    first, second = _head_row_masks()
    lq = lam_ref[...]
    lam = (jnp.exp(jnp.sum(lq[0:1] * lq[1:2], axis=-1, keepdims=True))
           - jnp.exp(jnp.sum(lq[2:3] * lq[3:4], axis=-1, keepdims=True)) + lam_init)

    def keys(g):
        return kn_ref[pl.ds(pl.multiple_of(g * T_G, T_G), T_G), :]

    def scores(qi, g):
        kn2, bias = keys(g), bias_ref[qi - g]
        return [jnp.dot(kn2, qtm, preferred_element_type=F32) + bias
                for qtm in _split_heads(qt_ref[qi], first, second)]

    def finish(qi, accs, sums):
        ob = accs[0] * (1.0 / sums[0]) - lam * (accs[1] * (1.0 / sums[1]))
        y = ob * lax.rsqrt(jnp.mean(ob * ob, axis=0, keepdims=True) + RMS_EPS)
        q0 = pl.multiple_of(qi * T_Q, T_Q)
        o_ref[pl.ds(q0, T_Q), :] = (y.T * sg_ref[...] * (1.0 - lam_init)).astype(BF16)

    def q_pair(qp, _):
        tiles = (2 * qp, 2 * qp + 1)

        def trip(it, states):
            gs = (2 * it, 2 * it + 1)
            vts = [vt_ref[g] for g in gs]
            scs = [[scores(qi, g) for g in gs] for qi in tiles]
            states = [list(st) for st in states]
            for i in range(2):
                for mp in range(2):
                    states[i][mp] = _softmax_update([sc[mp] for sc in scs[i]], vts, states[i][mp])
            return tuple(tuple(st) for st in states)

        init = (jnp.full((1, T_Q), NEG, F32), jnp.zeros((1, T_Q), F32), jnp.zeros((LANES, T_Q), F32))
        states = lax.fori_loop(0, qp, trip, ((init, init), (init, init)))
        gs = (2 * qp, 2 * qp + 1)
        vts = [vt_ref[g] for g in gs]
        scs = [[scores(tiles[0], gs[0])], [scores(tiles[1], g) for g in gs]]
        for i in range(2):
            done = [_softmax_update([sc[mp] for sc in scs[i]], vts, states[i][mp]) for mp in range(2)]
            finish(tiles[i], [st[2] for st in done], [st[1] for st in done])
        return 0

    work = [(qi, g) for qi in range(n_q) for g in range(qi + 1)]
    ones_rows = jnp.ones((ONES_ROWS, T_G), BF16)

    def bounded():
        z_next = scores(*work[0])
        for step, (qi, g) in enumerate(work):
            z = z_next
            if step + 1 < len(work):
                z_next = scores(*work[step + 1])
            vt1 = jnp.concatenate([vt_ref[g], ones_rows], axis=0)
            prods = [jnp.dot(vt1, jnp.exp2(z[mp]).astype(BF16), preferred_element_type=F32)
                     for mp in range(2)]
            pvs = [r[:LANES] for r in prods]
            psums = [r[LANES:LANES + 1] for r in prods]
            if g == 0:
                accs, sums = pvs, psums
            else:
                accs = [a + pv for a, pv in zip(accs, pvs)]
                sums = [s + ps_ for s, ps_ in zip(sums, psums)]
            if g == qi:
                finish(qi, accs, sums)
            yield

    def general():
        lax.fori_loop(0, n_q // 2, q_pair, 0)

    return bounded, len(work), general


def _alibi_tiles(n_q):
    slopes = np.asarray([2.0 ** (-8.0 * (i + 1) / H_DIFF) for i in range(H_DIFF)], np.float32)
    slope2 = jnp.asarray(slopes * LOG2E, F32)[:, None, None, None]
    kpos = lax.broadcasted_iota(jnp.int32, (T_G, T_Q), 0)
    qpos = lax.broadcasted_iota(jnp.int32, (T_G, T_Q), 1)
    tiles_back = lax.broadcasted_iota(jnp.int32, (n_q, 1, 1), 0)
    dist = (qpos - kpos)[None] + T_G * tiles_back
    bias = -slope2 * jnp.abs(dist).astype(F32)[None]
    visible = (tiles_back > 0) | ((kpos // CHUNK) <= (qpos // CHUNK))[None]
    return jnp.where(visible[None], bias, NEG)


N_REL_GROUPS = (N_PAST_CHUNKS * CHUNK + T_Q - 1) // T_G + 1


def _chunk_items(qt_ref, kn_ref, vt_ref, tab_ref, o_ref, sc_ref):
    seq = kn_ref.shape[0]
    first, second = _head_row_masks()
    deltas = tuple(range(N_REL_GROUPS - 1, -1, -1))
    ones_rows = jnp.ones((ONES_ROWS, T_G), BF16)

    def score_stage(qi, tile_deltas, slot):
        qts = _split_heads(qt_ref[qi], first, second)
        for dl in tile_deltas:
            k2 = kn_ref[pl.ds(pl.multiple_of((qi - dl) * T_G, T_G), T_G), :]
            for h in range(2):
                sc_ref[slot, h, dl] = (jnp.dot(k2, qts[h], preferred_element_type=F32)
                                       + tab_ref[h, dl])

    def softmax_stage(qi, tile_deltas, slot, fixed_shift):
        outs = []
        for h in range(2):
            scores = [sc_ref[slot, h, dl] for dl in tile_deltas]
            if fixed_shift:
                ps = [jnp.exp2(s) for s in scores]
            else:
                m = functools.reduce(jnp.maximum, [jnp.max(s, axis=0, keepdims=True) for s in scores])
                ps = [jnp.exp2(s - m) for s in scores]
            acc = None
            for dl, p in zip(tile_deltas, ps):
                vth = jnp.concatenate(
                    [vt_ref[qi - dl][h * HEAD_DIM:(h + 1) * HEAD_DIM, :], ones_rows], axis=0)
                pv = jnp.dot(vth, p.astype(BF16), preferred_element_type=F32)
                acc = pv if acc is None else acc + pv
            outs.append(acc[:HEAD_DIM] * (1.0 / acc[HEAD_DIM:HEAD_DIM + 1]))
        q0 = pl.multiple_of(qi * T_Q, T_Q)
        o_ref[pl.ds(q0, T_Q), :] = jnp.concatenate(outs, axis=0).T.astype(BF16)

    def tile_deltas(qi):
        return tuple(dl for dl in deltas if qi - dl >= 0)

    n_q = seq // T_Q

    def run(fixed_shift):
        score_stage(0, tile_deltas(0), 0)
        for qi in range(n_q):
            if qi + 1 < n_q:
                score_stage(qi + 1, tile_deltas(qi + 1), (qi + 1) % 2)
            yield
            softmax_stage(qi, tile_deltas(qi), qi % 2, fixed_shift)
            yield

    return run, 2 * n_q


def _chunk_logit_bound(gq, gk, tabs):
    return _qk_logit_bound(gq, gk) + jnp.max(jnp.where(tabs > 0.5 * NEG, jnp.abs(tabs), 0.0))


def _interleave(streams):
    order = sorted(((i + 0.5) / n, s) for s, (_, n) in enumerate(streams) for i in range(n))
    for _, s in order:
        next(streams[s][0], None)
    for items, _ in streams:
        _exhaust(items)


def _mix_kernel(bounded_ref, qa_ref, ka_ref, va_ref, qb_ref, kb_ref, vb_ref, qc_ref, kc_ref, vc_ref,
                alibi_ref, lam_ref, sg_ref, tab_ref,
                oa_ref, ob_ref, oc_ref, acc_ref, carry_ref, z_ref, sc_ref, *, layer, lam_init):
    sb = _StickBreaking(qa_ref, ka_ref, va_ref, oa_ref, acc_ref, carry_ref, z_ref)
    diff_items, n_diff, diff_online = _diff_paths(qb_ref, kb_ref, vb_ref, alibi_ref, lam_ref, sg_ref,
                                                  ob_ref, lam_init)
    chunk_items, n_chunk = _chunk_items(qc_ref, kc_ref, vc_ref, tab_ref, oc_ref, sc_ref)

    def interleaved():
        _interleave([(sb.main(), sb.n_main_items()), (diff_items(), n_diff),
                     (chunk_items(True), n_chunk)])

    def one_by_one():
        _exhaust(sb.main())
        diff_online()
        _exhaust(chunk_items(False))

    lax.cond(bounded_ref[layer] != 0, interleaved, one_by_one)
    sb.tails()


def _logits_bounded(gq_diff, gk_diff, gq_ch, gk_ch, tab):
    flags = [(_qk_logit_bound(gq_diff[l], gk_diff[l]) <= SAFE_LOG2)
             & (_chunk_logit_bound(gq_ch[l], gk_ch[l], tab[l]) <= SAFE_LOG2) for l in range(DEPTH)]
    return jnp.stack(flags).astype(jnp.int32)


def _mix_attn(qt, k, vt, alibi, lam_qk, subln_g, tab, bounded, layer, lam_init):
    b, s, _ = k.shape
    n_hb = W_BRANCH // LANES
    index = lambda hb, bi: (bi, hb)
    lay = lambda hb, bi: (layer, 0, 0)
    out_spec = pl.BlockSpec((None, s, LANES), lambda hb, bi: (bi, 0, hb))
    out_shape = jax.ShapeDtypeStruct((b, s, W_BRANCH), BF16)
    return pl.pallas_call(
        functools.partial(_mix_kernel, layer=layer, lam_init=lam_init),
        grid=(n_hb, b),
        in_specs=[
            pl.BlockSpec(memory_space=pltpu.SMEM),
            *_head_block_specs(0, s, index), *_head_block_specs(1, s, index),
            *_head_block_specs(2, s, index),
            pl.BlockSpec((None, s // T_Q, T_G, T_Q), lambda hb, bi: (hb, 0, 0, 0)),
            pl.BlockSpec((None, 4, HEAD_DIM), lay),
            pl.BlockSpec((None, 1, LANES), lay),
            pl.BlockSpec((None, 2, N_REL_GROUPS, T_G, T_Q), lambda hb, bi: (layer, hb, 0, 0, 0)),
        ],
        out_specs=[out_spec, out_spec, out_spec],
        out_shape=[out_shape, out_shape, out_shape],
        scratch_shapes=_sb_scratch(s) + [pltpu.VMEM((2, 2, N_REL_GROUPS, T_G, T_Q), F32)],
        compiler_params=pltpu.CompilerParams(
            dimension_semantics=("arbitrary", "arbitrary"), vmem_limit_bytes=VMEM_LIMIT),
        name="mix_attn",
    )(bounded, qt, k, vt, qt, k, vt, qt, k, vt, alibi, lam_qk, subln_g, tab)


def _rel_bias_tiles(rel_bias):
    lead = rel_bias.shape[:-1]
    span = (N_REL_GROUPS - 1) * T_G + T_Q
    period = span + T_G
    edge_lo = jnp.broadcast_to(rel_bias[..., :1], lead + (T_G - REL_CLIP,))
    edge_hi = jnp.broadcast_to(rel_bias[..., -1:], lead + (span - REL_CLIP - 1,))
    row = jnp.concatenate([rel_bias[..., REL_CLIP:], edge_hi, edge_lo, rel_bias[..., :REL_CLIP]], axis=-1)
    flat = jnp.tile(row, (1,) * len(lead) + (T_G,))[..., :T_G * (period - 1)]
    toep = flat.reshape(lead + (T_G, period - 1))
    tiles = jnp.stack([toep[..., d * T_G:d * T_G + T_Q] for d in range(N_REL_GROUPS)], axis=-3)
    kchunk = np.arange(T_G)[:, None] // CHUNK
    qchunk = np.arange(T_Q)[None, :] // CHUNK
    dd = np.stack([qchunk - kchunk + d * (T_G // CHUNK) for d in range(N_REL_GROUPS)])
    return jnp.where((dd >= 0) & (dd <= N_PAST_CHUNKS), tiles * LOG2E, NEG)


def kernel(x, norm_mix_g, w_in, b_gate, qk_g_diff, lambda_qk, subln_g, qk_g_ch, rel_bias,
           w_branch_sb, w_branch_diff, w_branch_ch, w_out, norm_ffn_g, w_gu, w_down):
    b, s, d = x.shape
    m = b * s
    w_qkv = w_in.astype(BF16)
    w_gate = w_qkv[:, :, QKV_W:]
    w_br = jnp.concatenate([w_branch_sb, w_branch_diff, w_branch_ch], axis=1).astype(BF16)
    w_out_b = w_out.astype(BF16)
    w_gu_b = w_gu.astype(BF16)
    w_down_b = w_down.astype(BF16)
    g_mix = norm_mix_g.reshape(DEPTH, 1, d)
    g_ffn = norm_ffn_g.reshape(DEPTH, 1, d)
    gq_diff = jnp.tile(qk_g_diff[:, 0:1, :], (1, 1, 2))
    gk_diff = jnp.tile(qk_g_diff[:, 1:2, :], (1, 1, 2))
    gq_ch = jnp.tile(qk_g_ch[:, 0:1, :], (1, 1, 2))
    gk_ch = jnp.tile(qk_g_ch[:, 1:2, :], (1, 1, 2))
    sg = subln_g.reshape(DEPTH, 1, 2 * HEAD_DIM)
    tab = _rel_bias_tiles(rel_bias)
    alibi = _alibi_tiles(s // T_Q)
    bounded = _logits_bounded(gq_diff, gk_diff, gq_ch, gk_ch, tab)
    ones = jnp.ones((DEPTH, 1, W_BRANCH), F32)
    widen = lambda g: jnp.tile(g, (1, 1, W_BRANCH // LANES))
    q_scale = QK_SCALE * LOG2E
    qk_gain = jnp.concatenate([ones, ones, ones, widen(gq_diff) * q_scale, widen(gk_diff), ones,
                               widen(gq_ch) * q_scale, widen(gk_ch), ones], axis=-1)

    xf = x.reshape(m, d)
    for layer in range(DEPTH):
        lam_init = 0.8 - 0.6 * math.exp(-0.3 * layer)
        k, qt, vt = _qkv_proj(xf, g_mix, w_qkv, qk_gain, layer)
        k = k.reshape(b, s, N_BRANCH * W_BRANCH)
        qt = qt.reshape(b, s // T_Q, N_BRANCH * W_BRANCH, T_Q)
        vt = vt.reshape(b, s // T_Q, N_BRANCH * W_BRANCH, T_Q)
        o_a, o_b, o_c = _mix_attn(qt, k, vt, alibi, lambda_qk, sg, tab, bounded, layer, lam_init)
        xf = _merge_out(xf, g_mix, o_a.reshape(m, W_BRANCH), o_b.reshape(m, W_BRANCH),
                        o_c.reshape(m, W_BRANCH), w_gate, b_gate, w_br, w_out_b, layer)
        xf = _ffn(xf, g_ffn, w_gu_b, w_down_b, layer)
    return xf.reshape(b, s, d)
```

```python
import functools
import math

import jax
import jax.numpy as jnp
import numpy as np
from jax import lax
from jax.experimental import pallas as pl
from jax.experimental.pallas import tpu as pltpu

F32 = jnp.float32
BF16 = jnp.bfloat16

D_MODEL = 1024
DEPTH = 4
CHUNK = 64
HEAD_DIM = 64
H_DIFF = 4
N_PAST_CHUNKS = 8
REL_CLIP = 128
W_BRANCH = 512
QKV_W = 9 * W_BRANCH
N_BRANCH = 3
D_FF = int(math.ceil(8 * D_MODEL / 3 / 256)) * 256
RMS_EPS = 1e-6
QK_SCALE = HEAD_DIM ** -0.5

LANES = 128
T_Q = 256
T_K = 128
T_G = 256
NEG = -1e30
SB_DEAD = -150.0
ONES_ROWS = 16
SB_LOOKAHEAD = 3
LOG2E = 1.4426950408889634
SAFE_LOG2 = 60.0
VMEM_LIMIT = 56 * 1024 * 1024

ROW_TILE = 1024
FF_CHUNK = 256


def _rms(x, g):
    return x * lax.rsqrt(jnp.mean(x * x, axis=-1, keepdims=True) + RMS_EPS) * g


def _rms_halves(x, g):
    lane = lax.broadcasted_iota(jnp.int32, (1, LANES), 1)
    first = lane < HEAD_DIM
    x2 = x * x
    s0 = jnp.sum(jnp.where(first, x2, 0.0), axis=-1, keepdims=True)
    s1 = jnp.sum(jnp.where(first, 0.0, x2), axis=-1, keepdims=True)
    ms = jnp.where(first, s0, s1) * (1.0 / HEAD_DIM)
    return x * lax.rsqrt(ms + RMS_EPS) * g


def _qk_logit_bound(gq, gk):
    return (1.02 * LOG2E * QK_SCALE * HEAD_DIM) * jnp.max(jnp.abs(gq)) * jnp.max(jnp.abs(gk))


def _head_row_masks():
    row = lax.broadcasted_iota(jnp.int32, (LANES, 1), 0)
    return row < HEAD_DIM, row >= HEAD_DIM


def _split_heads(qt, first, second):
    zero = jnp.zeros_like(qt)
    return jnp.where(first, qt, zero), jnp.where(second, qt, zero)


N_SECTIONS = 9


def _qkv_kernel(x_ref, g_ref, w_ref, qkg_ref, k_out, qt_out, vt_out):
    h = _rms(x_ref[...], g_ref[...]).astype(BF16)

    def project(sec):
        return jnp.dot(h, w_ref[:, sec * W_BRANCH:(sec + 1) * W_BRANCH], preferred_element_type=F32)

    z_next = project(0)
    for sec in range(N_SECTIONS):
        z = z_next
        if sec + 1 < N_SECTIONS:
            z_next = project(sec + 1)
        branch, role = divmod(sec, 3)
        for cb in range(W_BRANCH // LANES):
            blk = z[:, cb * LANES:(cb + 1) * LANES]
            col = sec * W_BRANCH + cb * LANES
            if branch > 0 and role < 2:
                blk = _rms_halves(blk, qkg_ref[:, col:col + LANES])
            elif role == 0:
                blk = blk * (QK_SCALE * LOG2E)
            out_col = branch * W_BRANCH + cb * LANES
            if role == 1:
                k_out[:, out_col:out_col + LANES] = blk.astype(BF16)
            else:
                out = qt_out if role == 0 else vt_out
                for r in range(ROW_TILE // T_Q):
                    out[r, out_col:out_col + LANES, :] = blk[r * T_Q:(r + 1) * T_Q, :].T.astype(BF16)


def _qkv_proj(x, g, w, qk_gain, layer):
    m = x.shape[0]
    width = N_BRANCH * W_BRANCH
    tiles = ROW_TILE // T_Q
    lay = lambda i: (layer, 0, 0)
    transposed = jax.ShapeDtypeStruct((m // T_Q, width, T_Q), BF16)
    return pl.pallas_call(
        _qkv_kernel,
        grid=(m // ROW_TILE,),
        in_specs=[
            pl.BlockSpec((ROW_TILE, D_MODEL), lambda i: (i, 0)),
            pl.BlockSpec((None, 1, D_MODEL), lay),
            pl.BlockSpec((None, D_MODEL, QKV_W), lay, pipeline_mode=pl.Buffered(1)),
            pl.BlockSpec((None, 1, QKV_W), lay),
        ],
        out_specs=[
            pl.BlockSpec((ROW_TILE, width), lambda i: (i, 0)),
            pl.BlockSpec((tiles, width, T_Q), lambda i: (i, 0, 0)),
            pl.BlockSpec((tiles, width, T_Q), lambda i: (i, 0, 0)),
        ],
        out_shape=[jax.ShapeDtypeStruct((m, width), BF16), transposed, transposed],
        compiler_params=pltpu.CompilerParams(
            dimension_semantics=("arbitrary",), vmem_limit_bytes=VMEM_LIMIT),
        name="qkv_proj",
    )(x, g, w, qk_gain)


def _merge_kernel(x_ref, g_ref, oa_ref, ob_ref, oc_ref, wg_ref, bg_ref, wbr_ref, wo_ref, out_ref):
    x = x_ref[...]
    h = _rms(x, g_ref[...]).astype(BF16)
    merged = None
    for br, o_ref in enumerate((oa_ref, ob_ref, oc_ref)):
        g_lin = jnp.dot(h, wg_ref[:, br * D_MODEL:(br + 1) * D_MODEL], preferred_element_type=F32)
        gate = 1.0 / (1.0 + jnp.exp(-(g_lin + bg_ref[br:br + 1, :])))
        proj = jnp.dot(o_ref[...], wbr_ref[br * W_BRANCH:(br + 1) * W_BRANCH, :],
                       preferred_element_type=F32)
        term = gate * proj
        merged = term if merged is None else merged + term
    out_ref[...] = x + jnp.dot(merged.astype(BF16), wo_ref[...], preferred_element_type=F32)


def _merge_out(x, g, o_a, o_b, o_c, w_gate, b_gate, w_br, w_out, layer):
    m = x.shape[0]
    row = lambda i: (i, 0)
    lay = lambda i: (layer, 0, 0)
    return pl.pallas_call(
        _merge_kernel,
        grid=(m // ROW_TILE,),
        in_specs=[
            pl.BlockSpec((ROW_TILE, D_MODEL), row),
            pl.BlockSpec((None, 1, D_MODEL), lay),
            pl.BlockSpec((ROW_TILE, W_BRANCH), row),
            pl.BlockSpec((ROW_TILE, W_BRANCH), row),
            pl.BlockSpec((ROW_TILE, W_BRANCH), row),
            pl.BlockSpec((None, D_MODEL, N_BRANCH * D_MODEL), lay, pipeline_mode=pl.Buffered(1)),
            pl.BlockSpec((None, N_BRANCH, D_MODEL), lay),
            pl.BlockSpec((None, N_BRANCH * W_BRANCH, D_MODEL), lay, pipeline_mode=pl.Buffered(1)),
            pl.BlockSpec((None, D_MODEL, D_MODEL), lay, pipeline_mode=pl.Buffered(1)),
        ],
        out_specs=pl.BlockSpec((ROW_TILE, D_MODEL), row),
        out_shape=jax.ShapeDtypeStruct((m, D_MODEL), F32),
        compiler_params=pltpu.CompilerParams(
            dimension_semantics=("arbitrary",), vmem_limit_bytes=VMEM_LIMIT),
        name="merge_out",
    )(x, g, o_a, o_b, o_c, w_gate, b_gate, w_br, w_out)


def _ffn_kernel(x_ref, g_ref, wgu_ref, wd_ref, out_ref, act_ref):
    x = x_ref[...]
    h = _rms(x, g_ref[...]).astype(BF16)
    for c in range(0, D_FF, FF_CHUNK):
        gate = jnp.dot(h, wgu_ref[:, c:c + FF_CHUNK], preferred_element_type=F32)
        up = jnp.dot(h, wgu_ref[:, D_FF + c:D_FF + c + FF_CHUNK], preferred_element_type=F32)
        silu = gate / (1.0 + jnp.exp(-gate))
        act_ref[:, c:c + FF_CHUNK] = (silu * up).astype(BF16)
    out_ref[...] = x + jnp.dot(act_ref[...], wd_ref[...], preferred_element_type=F32)


def _ffn(x, g, w_gu, w_down, layer):
    m = x.shape[0]
    row = lambda i: (i, 0)
    lay = lambda i: (layer, 0, 0)
    return pl.pallas_call(
        _ffn_kernel,
        grid=(m // ROW_TILE,),
        in_specs=[
            pl.BlockSpec((ROW_TILE, D_MODEL), row),
            pl.BlockSpec((None, 1, D_MODEL), lay),
            pl.BlockSpec((None, D_MODEL, 2 * D_FF), lay, pipeline_mode=pl.Buffered(1)),
            pl.BlockSpec((None, D_FF, D_MODEL), lay, pipeline_mode=pl.Buffered(1)),
        ],
        out_specs=pl.BlockSpec((ROW_TILE, D_MODEL), row),
        out_shape=jax.ShapeDtypeStruct((m, D_MODEL), F32),
        scratch_shapes=[pltpu.VMEM((ROW_TILE, D_FF), BF16)],
        compiler_params=pltpu.CompilerParams(
            dimension_semantics=("arbitrary",), vmem_limit_bytes=VMEM_LIMIT),
        name="ffn",
    )(x, g, w_gu, w_down)


def _sb_weights(z, cum, carry, mask):
    sp = jnp.log2(1.0 + jnp.exp2(-jnp.abs(z)))
    log_beta = jnp.minimum(z, 0.0) - sp
    log_1m = log_beta - z
    if mask is not None:
        log_1m = jnp.where(mask, log_1m, 0.0)
    hi = log_1m.astype(BF16)
    lo = (log_1m - hi.astype(F32)).astype(BF16)
    afters = []
    for u in reversed(range(z.shape[0] // T_K)):
        r0, r1 = u * T_K, (u + 1) * T_K
        within = jnp.dot(cum, jnp.concatenate([hi[r0:r1], lo[r0:r1]], axis=0),
                         preferred_element_type=F32)
        afters.append(within + carry)
        carry = carry + within[0:1, :] + log_1m[r0:r0 + 1, :]
    after = afters[0] if len(afters) == 1 else jnp.concatenate(afters[::-1], axis=0)
    w = jnp.exp2(log_beta + after)
    if mask is not None:
        w = jnp.where(mask, w, 0.0)
    return w.astype(BF16), carry


class _StickBreaking:
    HEADS = range(2)
    GROUPS_IN_MAIN = 2

    def __init__(self, qt_ref, k_ref, vt_ref, o_ref, acc_ref, carry_ref, z_ref):
        self.qt_ref, self.k_ref, self.vt_ref, self.o_ref = qt_ref, k_ref, vt_ref, o_ref
        self.acc_ref, self.carry_ref, self.z_ref = acc_ref, carry_ref, z_ref
        self.n_q = k_ref.shape[0] // T_Q
        self.masks = _head_row_masks()
        kk = lax.broadcasted_iota(jnp.int32, (T_K, T_K), 0)
        kk2 = lax.broadcasted_iota(jnp.int32, (T_K, T_K), 1)
        later = jnp.where(kk2 > kk, 1.0, 0.0).astype(BF16)
        self.cum = jnp.concatenate([later, later], axis=1)
        self.strict = (lax.broadcasted_iota(jnp.int32, (T_G, T_Q), 0)
                       < lax.broadcasted_iota(jnp.int32, (T_G, T_Q), 1))

    def main_groups(self, qi):
        return [g for g in range(qi, qi - self.GROUPS_IN_MAIN, -1) if g >= 0]

    def n_main_items(self):
        return sum(len(self.main_groups(qi)) for qi in range(self.n_q)) * len(self.HEADS)

    def load_q(self, qi):
        return _split_heads(self.qt_ref[qi], *self.masks)

    def logits(self, g, qts):
        k2 = self.k_ref[pl.ds(pl.multiple_of(g * T_G, T_G), T_G), :]
        return [jnp.dot(k2, qts[h], preferred_element_type=F32) for h in self.HEADS]

    def add_group(self, g, h, z, mask, carry, acc):
        w, carry = _sb_weights(z, self.cum, carry, mask)
        vth = self.vt_ref[g][h * HEAD_DIM:(h + 1) * HEAD_DIM, :]
        return carry, acc + jnp.dot(vth, w, preferred_element_type=F32)

    def store(self, qi, accs):
        q0 = pl.multiple_of(qi * T_Q, T_Q)
        self.o_ref[pl.ds(q0, T_Q), :] = jnp.concatenate(accs, axis=0).T.astype(BF16)

    def main(self):
        pairs = [(qi, g) for qi in range(self.n_q) for g in self.main_groups(qi)]
        n_slots = SB_LOOKAHEAD + 1

        def issue(j):
            for qi, g in pairs[j:j + 1]:
                for h, z in enumerate(self.logits(g, self.load_q(qi))):
                    self.z_ref[j % n_slots, h] = z

        for j in range(SB_LOOKAHEAD):
            issue(j)
        for j, (qi, g) in enumerate(pairs):
            issue(j + SB_LOOKAHEAD)
            if g == qi:
                carries = [jnp.zeros((1, T_Q), F32) for _ in self.HEADS]
                accs = [jnp.zeros((HEAD_DIM, T_Q), F32) for _ in self.HEADS]
            for h in self.HEADS:
                carries[h], accs[h] = self.add_group(
                    g, h, self.z_ref[j % n_slots, h], self.strict if g == qi else None,
                    carries[h], accs[h])
                yield
            if g == self.main_groups(qi)[-1]:
                self.store(qi, accs)
                for h in self.HEADS:
                    self.acc_ref[qi, h] = accs[h]
                    self.carry_ref[qi, h] = carries[h]

    def _live(self, carries):
        return jnp.max(functools.reduce(jnp.maximum, carries)) >= SB_DEAD

    def tails(self):
        tiles = [qi for qi in range(self.n_q) if qi - self.GROUPS_IN_MAIN >= 0]

        def tile_tail(qi):
            qts = self.load_q(qi)

            def live(st):
                g, carries, _ = st
                return (g >= 0) & self._live(carries)

            def body(st):
                g, carries, accs = st
                carries, accs = list(carries), list(accs)
                zs = self.logits(g, qts)
                for h in self.HEADS:
                    carries[h], accs[h] = self.add_group(g, h, zs[h], None, carries[h], accs[h])
                return g - 1, tuple(carries), tuple(accs)

            init = (qi - self.GROUPS_IN_MAIN,
                    tuple(self.carry_ref[qi, h] for h in self.HEADS),
                    tuple(self.acc_ref[qi, h] for h in self.HEADS))
            _, _, accs = lax.while_loop(live, body, init)
            self.store(qi, list(accs))

        def all_tails():
            for qi in tiles:
                tile_tail(qi)

        any_live = self._live([self.carry_ref[qi, h] for qi in tiles for h in self.HEADS])
        lax.cond(any_live, all_tails, lambda: None)


def _exhaust(items):
    for _ in items:
        pass


def _sb_scratch(seq):
    n_q = seq // T_Q
    return [pltpu.VMEM((n_q, 2, HEAD_DIM, T_Q), F32), pltpu.VMEM((n_q, 2, 1, T_Q), F32),
            pltpu.VMEM((SB_LOOKAHEAD + 1, 2, T_G, T_Q), F32)]


def _head_block_specs(branch, seq, index):
    def transposed(*ids):
        bi, hb = index(*ids)
        return bi, 0, branch * (W_BRANCH // LANES) + hb, 0

    def rows(*ids):
        bi, hb = index(*ids)
        return bi, 0, branch * (W_BRANCH // LANES) + hb

    t_spec = pl.BlockSpec((None, seq // T_Q, LANES, T_Q), transposed)
    return [t_spec, pl.BlockSpec((None, seq, LANES), rows), t_spec]


def _softmax_update(scores, vts, state):
    m, l, acc = state
    m_new = m
    for s in scores:
        m_new = jnp.maximum(m_new, jnp.max(s, axis=0, keepdims=True))
    alpha = jnp.exp2(m - m_new)
    l = alpha * l
    acc = alpha * acc
    for s, vt in zip(scores, vts):
        p = jnp.exp2(s - m_new)
        l = l + jnp.sum(p, axis=0, keepdims=True)
        acc = acc + jnp.dot(vt, p.astype(BF16), preferred_element_type=F32)
    return m_new, l, acc


def _diff_paths(qt_ref, kn_ref, vt_ref, bias_ref, lam_ref, sg_ref, o_ref, lam_init):
    seq = kn_ref.shape[0]
    n_q = seq // T_Q
    first, second = _head_row_masks()
    lq = lam_ref[...]
    lam = (jnp.exp(jnp.sum(lq[0:1] * lq[1:2], axis=-1, keepdims=True))
           - jnp.exp(jnp.sum(lq[2:3] * lq[3:4], axis=-1, keepdims=True)) + lam_init)

    def keys(g):
        return kn_ref[pl.ds(pl.multiple_of(g * T_G, T_G), T_G), :]

    def scores(qi, g):
        kn2, bias = keys(g), bias_ref[qi - g]
        return [jnp.dot(kn2, qtm, preferred_element_type=F32) + bias
                for qtm in _split_heads(qt_ref[qi], first, second)]

    def finish(qi, accs, sums):
        ob = accs[0] * (1.0 / sums[0]) - lam * (accs[1] * (1.0 / sums[1]))
        y = ob * lax.rsqrt(jnp.mean(ob * ob, axis=0, keepdims=True) + RMS_EPS)
        q0 = pl.multiple_of(qi * T_Q, T_Q)
        o_ref[pl.ds(q0, T_Q), :] = (y.T * sg_ref[...] * (1.0 - lam_init)).astype(BF16)

    def q_pair(qp, _):
        tiles = (2 * qp, 2 * qp + 1)

        def trip(it, states):
            gs = (2 * it, 2 * it + 1)
            vts = [vt_ref[g] for g in gs]
            scs = [[scores(qi, g) for g in gs] for qi in tiles]
            states = [list(st) for st in states]
            for i in range(2):
                for mp in range(2):
                    states[i][mp] = _softmax_update([sc[mp] for sc in scs[i]], vts, states[i][mp])
            return tuple(tuple(st) for st in states)

        init = (jnp.full((1, T_Q), NEG, F32), jnp.zeros((1, T_Q), F32), jnp.zeros((LANES, T_Q), F32))
        states = lax.fori_loop(0, qp, trip, ((init, init), (init, init)))
        gs = (2 * qp, 2 * qp + 1)
        vts = [vt_ref[g] for g in gs]
        scs = [[scores(tiles[0], gs[0])], [scores(tiles[1], g) for g in gs]]
        for i in range(2):
            done = [_softmax_update([sc[mp] for sc in scs[i]], vts, states[i][mp]) for mp in range(2)]
            finish(tiles[i], [st[2] for st in done], [st[1] for st in done])
        return 0

    work = [(qi, g) for qi in range(n_q) for g in range(qi + 1)]
    ones_rows = jnp.ones((ONES_ROWS, T_G), BF16)

    def bounded():
        z_next = scores(*work[0])
        for step, (qi, g) in enumerate(work):
            z = z_next
            if step + 1 < len(work):
                z_next = scores(*work[step + 1])
            vt1 = jnp.concatenate([vt_ref[g], ones_rows], axis=0)
            prods = [jnp.dot(vt1, jnp.exp2(z[mp]).astype(BF16), preferred_element_type=F32)
                     for mp in range(2)]
            pvs = [r[:LANES] for r in prods]
            psums = [r[LANES:LANES + 1] for r in prods]
            if g == 0:
                accs, sums = pvs, psums
            else:
                accs = [a + pv for a, pv in zip(accs, pvs)]
                sums = [s + ps_ for s, ps_ in zip(sums, psums)]
            if g == qi:
                finish(qi, accs, sums)
            yield

    def general():
        lax.fori_loop(0, n_q // 2, q_pair, 0)

    return bounded, len(work), general


def _alibi_tiles(n_q):
    slopes = np.asarray([2.0 ** (-8.0 * (i + 1) / H_DIFF) for i in range(H_DIFF)], np.float32)
    slope2 = jnp.asarray(slopes * LOG2E, F32)[:, None, None, None]
    kpos = lax.broadcasted_iota(jnp.int32, (T_G, T_Q), 0)
    qpos = lax.broadcasted_iota(jnp.int32, (T_G, T_Q), 1)
    tiles_back = lax.broadcasted_iota(jnp.int32, (n_q, 1, 1), 0)
    dist = (qpos - kpos)[None] + T_G * tiles_back
    bias = -slope2 * jnp.abs(dist).astype(F32)[None]
    visible = (tiles_back > 0) | ((kpos // CHUNK) <= (qpos // CHUNK))[None]
    return jnp.where(visible[None], bias, NEG)


N_REL_GROUPS = (N_PAST_CHUNKS * CHUNK + T_Q - 1) // T_G + 1


def _chunk_items(qt_ref, kn_ref, vt_ref, tab_ref, o_ref, sc_ref):
    seq = kn_ref.shape[0]
    first, second = _head_row_masks()
    deltas = tuple(range(N_REL_GROUPS - 1, -1, -1))
    ones_rows = jnp.ones((ONES_ROWS, T_G), BF16)

    def score_stage(qi, tile_deltas, slot):
        qts = _split_heads(qt_ref[qi], first, second)
        for dl in tile_deltas:
            k2 = kn_ref[pl.ds(pl.multiple_of((qi - dl) * T_G, T_G), T_G), :]
            for h in range(2):
                sc_ref[slot, h, dl] = (jnp.dot(k2, qts[h], preferred_element_type=F32)
                                       + tab_ref[h, dl])

    def softmax_stage(qi, tile_deltas, slot, fixed_shift):
        outs = []
        for h in range(2):
            scores = [sc_ref[slot, h, dl] for dl in tile_deltas]
            if fixed_shift:
                ps = [jnp.exp2(s) for s in scores]
            else:
                m = functools.reduce(jnp.maximum, [jnp.max(s, axis=0, keepdims=True) for s in scores])
                ps = [jnp.exp2(s - m) for s in scores]
            acc = None
            for dl, p in zip(tile_deltas, ps):
                vth = jnp.concatenate(
                    [vt_ref[qi - dl][h * HEAD_DIM:(h + 1) * HEAD_DIM, :], ones_rows], axis=0)
                pv = jnp.dot(vth, p.astype(BF16), preferred_element_type=F32)
                acc = pv if acc is None else acc + pv
            outs.append(acc[:HEAD_DIM] * (1.0 / acc[HEAD_DIM:HEAD_DIM + 1]))
        q0 = pl.multiple_of(qi * T_Q, T_Q)
        o_ref[pl.ds(q0, T_Q), :] = jnp.concatenate(outs, axis=0).T.astype(BF16)

    def tile_deltas(qi):
        return tuple(dl for dl in deltas if qi - dl >= 0)

    n_q = seq // T_Q

    def run(fixed_shift):
        score_stage(0, tile_deltas(0), 0)
        for qi in range(n_q):
            if qi + 1 < n_q:
                score_stage(qi + 1, tile_deltas(qi + 1), (qi + 1) % 2)
            yield
            softmax_stage(qi, tile_deltas(qi), qi % 2, fixed_shift)
            yield

    return run, 2 * n_q


def _chunk_logit_bound(gq, gk, tabs):
    return _qk_logit_bound(gq, gk) + jnp.max(jnp.where(tabs > 0.5 * NEG, jnp.abs(tabs), 0.0))


def _interleave(streams):
    order = sorted(((i + 0.5) / n, s) for s, (_, n) in enumerate(streams) for i in range(n))
    for _, s in order:
        next(streams[s][0], None)
    for items, _ in streams:
        _exhaust(items)


def _mix_kernel(bounded_ref, qa_ref, ka_ref, va_ref, qb_ref, kb_ref, vb_ref, qc_ref, kc_ref, vc_ref,
                alibi_ref, lam_ref, sg_ref, tab_ref,
                oa_ref, ob_ref, oc_ref, acc_ref, carry_ref, z_ref, sc_ref, *, layer, lam_init):
    sb = _StickBreaking(qa_ref, ka_ref, va_ref, oa_ref, acc_ref, carry_ref, z_ref)
    diff_items, n_diff, diff_online = _diff_paths(qb_ref, kb_ref, vb_ref, alibi_ref, lam_ref, sg_ref,
                                                  ob_ref, lam_init)
    chunk_items, n_chunk = _chunk_items(qc_ref, kc_ref, vc_ref, tab_ref, oc_ref, sc_ref)

    def interleaved():
        _interleave([(sb.main(), sb.n_main_items()), (diff_items(), n_diff),
                     (chunk_items(True), n_chunk)])

    def one_by_one():
        _exhaust(sb.main())
        diff_online()
        _exhaust(chunk_items(False))

    lax.cond(bounded_ref[layer] != 0, interleaved, one_by_one)
    sb.tails()


def _logits_bounded(gq_diff, gk_diff, gq_ch, gk_ch, tab):
    flags = [(_qk_logit_bound(gq_diff[l], gk_diff[l]) <= SAFE_LOG2)
             & (_chunk_logit_bound(gq_ch[l], gk_ch[l], tab[l]) <= SAFE_LOG2) for l in range(DEPTH)]
    return jnp.stack(flags).astype(jnp.int32)


def _mix_attn(qt, k, vt, alibi, lam_qk, subln_g, tab, bounded, layer, lam_init):
    b, s, _ = k.shape
    n_hb = W_BRANCH // LANES
    index = lambda hb, bi: (bi, hb)
    lay = lambda hb, bi: (layer, 0, 0)
    out_spec = pl.BlockSpec((None, s, LANES), lambda hb, bi: (bi, 0, hb))
    out_shape = jax.ShapeDtypeStruct((b, s, W_BRANCH), BF16)
    return pl.pallas_call(
        functools.partial(_mix_kernel, layer=layer, lam_init=lam_init),
        grid=(n_hb, b),
        in_specs=[
            pl.BlockSpec(memory_space=pltpu.SMEM),
            *_head_block_specs(0, s, index), *_head_block_specs(1, s, index),
            *_head_block_specs(2, s, index),
            pl.BlockSpec((None, s // T_Q, T_G, T_Q), lambda hb, bi: (hb, 0, 0, 0)),
            pl.BlockSpec((None, 4, HEAD_DIM), lay),
            pl.BlockSpec((None, 1, LANES), lay),
            pl.BlockSpec((None, 2, N_REL_GROUPS, T_G, T_Q), lambda hb, bi: (layer, hb, 0, 0, 0)),
        ],
        out_specs=[out_spec, out_spec, out_spec],
        out_shape=[out_shape, out_shape, out_shape],
        scratch_shapes=_sb_scratch(s) + [pltpu.VMEM((2, 2, N_REL_GROUPS, T_G, T_Q), F32)],
        compiler_params=pltpu.CompilerParams(
            dimension_semantics=("arbitrary", "arbitrary"), vmem_limit_bytes=VMEM_LIMIT),
        name="mix_attn",
    )(bounded, qt, k, vt, qt, k, vt, qt, k, vt, alibi, lam_qk, subln_g, tab)


def _rel_bias_tiles(rel_bias):
    lead = rel_bias.shape[:-1]
    span = (N_REL_GROUPS - 1) * T_G + T_Q
    period = span + T_G
    edge_lo = jnp.broadcast_to(rel_bias[..., :1], lead + (T_G - REL_CLIP,))
    edge_hi = jnp.broadcast_to(rel_bias[..., -1:], lead + (span - REL_CLIP - 1,))
    row = jnp.concatenate([rel_bias[..., REL_CLIP:], edge_hi, edge_lo, rel_bias[..., :REL_CLIP]], axis=-1)
    flat = jnp.tile(row, (1,) * len(lead) + (T_G,))[..., :T_G * (period - 1)]
    toep = flat.reshape(lead + (T_G, period - 1))
    tiles = jnp.stack([toep[..., d * T_G:d * T_G + T_Q] for d in range(N_REL_GROUPS)], axis=-3)
    kchunk = np.arange(T_G)[:, None] // CHUNK
    qchunk = np.arange(T_Q)[None, :] // CHUNK
    dd = np.stack([qchunk - kchunk + d * (T_G // CHUNK) for d in range(N_REL_GROUPS)])
    return jnp.where((dd >= 0) & (dd <= N_PAST_CHUNKS), tiles * LOG2E, NEG)


def kernel(x, norm_mix_g, w_in, b_gate, qk_g_diff, lambda_qk, subln_g, qk_g_ch, rel_bias,
           w_branch_sb, w_branch_diff, w_branch_ch, w_out, norm_ffn_g, w_gu, w_down):
    b, s, d = x.shape
    m = b * s
    w_qkv = w_in.astype(BF16)
    w_gate = w_qkv[:, :, QKV_W:]
    w_br = jnp.concatenate([w_branch_sb, w_branch_diff, w_branch_ch], axis=1).astype(BF16)
    w_out_b = w_out.astype(BF16)
    w_gu_b = w_gu.astype(BF16)
    w_down_b = w_down.astype(BF16)
    g_mix = norm_mix_g.reshape(DEPTH, 1, d)
    g_ffn = norm_ffn_g.reshape(DEPTH, 1, d)
    gq_diff = jnp.tile(qk_g_diff[:, 0:1, :], (1, 1, 2))
    gk_diff = jnp.tile(qk_g_diff[:, 1:2, :], (1, 1, 2))
    gq_ch = jnp.tile(qk_g_ch[:, 0:1, :], (1, 1, 2))
    gk_ch = jnp.tile(qk_g_ch[:, 1:2, :], (1, 1, 2))
    sg = subln_g.reshape(DEPTH, 1, 2 * HEAD_DIM)
    tab = _rel_bias_tiles(rel_bias)
    alibi = _alibi_tiles(s // T_Q)
    bounded = _logits_bounded(gq_diff, gk_diff, gq_ch, gk_ch, tab)
    ones = jnp.ones((DEPTH, 1, W_BRANCH), F32)
    widen = lambda g: jnp.tile(g, (1, 1, W_BRANCH // LANES))
    q_scale = QK_SCALE * LOG2E
    qk_gain = jnp.concatenate([ones, ones, ones, widen(gq_diff) * q_scale, widen(gk_diff), ones,
                               widen(gq_ch) * q_scale, widen(gk_ch), ones], axis=-1)

    xf = x.reshape(m, d)
    for layer in range(DEPTH):
        lam_init = 0.8 - 0.6 * math.exp(-0.3 * layer)
        k, qt, vt = _qkv_proj(xf, g_mix, w_qkv, qk_gain, layer)
        k = k.reshape(b, s, N_BRANCH * W_BRANCH)
        qt = qt.reshape(b, s // T_Q, N_BRANCH * W_BRANCH, T_Q)
        vt = vt.reshape(b, s // T_Q, N_BRANCH * W_BRANCH, T_Q)
        o_a, o_b, o_c = _mix_attn(qt, k, vt, alibi, lambda_qk, sg, tab, bounded, layer, lam_init)
        xf = _merge_out(xf, g_mix, o_a.reshape(m, W_BRANCH), o_b.reshape(m, W_BRANCH),
                        o_c.reshape(m, W_BRANCH), w_gate, b_gate, w_br, w_out_b, layer)
        xf = _ffn(xf, g_ffn, w_gu_b, w_down_b, layer)
    return xf.reshape(b, s, d)
```

```python
import functools
import math

import jax
import jax.numpy as jnp
import numpy as np
from jax import lax
from jax.experimental import pallas as pl
from jax.experimental.pallas import tpu as pltpu

F32 = jnp.float32
BF16 = jnp.bfloat16

D_MODEL = 1024
DEPTH = 4
CHUNK = 64
HEAD_DIM = 64
H_DIFF = 4
N_PAST_CHUNKS = 8
REL_CLIP = 128
W_BRANCH = 512
QKV_W = 9 * W_BRANCH
N_BRANCH = 3
D_FF = int(math.ceil(8 * D_MODEL / 3 / 256)) * 256
RMS_EPS = 1e-6
QK_SCALE = HEAD_DIM ** -0.5

LANES = 128
T_Q = 256
T_K = 128
T_G = 256
NEG = -1e30
SB_DEAD = -150.0
ONES_ROWS = 16
DIFF_LOOKAHEAD = 2
SB_LOOKAHEAD = 1
LOG2E = 1.4426950408889634
SAFE_LOG2 = 60.0
VMEM_LIMIT = 56 * 1024 * 1024

ROW_TILE = 1024
FF_CHUNK = 256


def _rms(x, g):
    return x * lax.rsqrt(jnp.mean(x * x, axis=-1, keepdims=True) + RMS_EPS) * g


def _rms_halves(x, g):
    lane = lax.broadcasted_iota(jnp.int32, (1, LANES), 1)
    first = lane < HEAD_DIM
    x2 = x * x
    s0 = jnp.sum(jnp.where(first, x2, 0.0), axis=-1, keepdims=True)
    s1 = jnp.sum(jnp.where(first, 0.0, x2), axis=-1, keepdims=True)
    ms = jnp.where(first, s0, s1) * (1.0 / HEAD_DIM)
    return x * lax.rsqrt(ms + RMS_EPS) * g


def _qk_logit_bound(gq, gk):
    return (1.02 * LOG2E * QK_SCALE * HEAD_DIM) * jnp.max(jnp.abs(gq)) * jnp.max(jnp.abs(gk))


def _head_row_masks():
    row = lax.broadcasted_iota(jnp.int32, (LANES, 1), 0)
    return row < HEAD_DIM, row >= HEAD_DIM


def _split_heads(qt, first, second):
    zero = jnp.zeros_like(qt)
    return jnp.where(first, qt, zero), jnp.where(second, qt, zero)


N_SECTIONS = 9


def _qkv_kernel(x_ref, g_ref, w_ref, qkg_ref, k_out, qt_out, vt_out):
    h = _rms(x_ref[...], g_ref[...]).astype(BF16)

    def project(sec):
        return jnp.dot(h, w_ref[:, sec * W_BRANCH:(sec + 1) * W_BRANCH], preferred_element_type=F32)

    z_next = project(0)
    for sec in range(N_SECTIONS):
        z = z_next
        if sec + 1 < N_SECTIONS:
            z_next = project(sec + 1)
        branch, role = divmod(sec, 3)
        for cb in range(W_BRANCH // LANES):
            blk = z[:, cb * LANES:(cb + 1) * LANES]
            col = sec * W_BRANCH + cb * LANES
            if branch > 0 and role < 2:
                blk = _rms_halves(blk, qkg_ref[:, col:col + LANES])
            elif role == 0:
                blk = blk * (QK_SCALE * LOG2E)
            out_col = branch * W_BRANCH + cb * LANES
            if role == 1:
                k_out[:, out_col:out_col + LANES] = blk.astype(BF16)
            else:
                out = qt_out if role == 0 else vt_out
                for r in range(ROW_TILE // T_Q):
                    out[r, out_col:out_col + LANES, :] = blk[r * T_Q:(r + 1) * T_Q, :].T.astype(BF16)


def _qkv_proj(x, g, w, qk_gain, layer):
    m = x.shape[0]
    width = N_BRANCH * W_BRANCH
    tiles = ROW_TILE // T_Q
    lay = lambda i: (layer, 0, 0)
    transposed = jax.ShapeDtypeStruct((m // T_Q, width, T_Q), BF16)
    return pl.pallas_call(
        _qkv_kernel,
        grid=(m // ROW_TILE,),
        in_specs=[
            pl.BlockSpec((ROW_TILE, D_MODEL), lambda i: (i, 0)),
            pl.BlockSpec((None, 1, D_MODEL), lay),
            pl.BlockSpec((None, D_MODEL, QKV_W), lay, pipeline_mode=pl.Buffered(1)),
            pl.BlockSpec((None, 1, QKV_W), lay),
        ],
        out_specs=[
            pl.BlockSpec((ROW_TILE, width), lambda i: (i, 0)),
            pl.BlockSpec((tiles, width, T_Q), lambda i: (i, 0, 0)),
            pl.BlockSpec((tiles, width, T_Q), lambda i: (i, 0, 0)),
        ],
        out_shape=[jax.ShapeDtypeStruct((m, width), BF16), transposed, transposed],
        compiler_params=pltpu.CompilerParams(
            dimension_semantics=("arbitrary",), vmem_limit_bytes=VMEM_LIMIT),
        name="qkv_proj",
    )(x, g, w, qk_gain)


def _merge_kernel(x_ref, g_ref, oa_ref, ob_ref, oc_ref, wg_ref, bg_ref, wbr_ref, wo_ref, out_ref):
    x = x_ref[...]
    h = _rms(x, g_ref[...]).astype(BF16)
    merged = None
    for br, o_ref in enumerate((oa_ref, ob_ref, oc_ref)):
        g_lin = jnp.dot(h, wg_ref[:, br * D_MODEL:(br + 1) * D_MODEL], preferred_element_type=F32)
        gate = 1.0 / (1.0 + jnp.exp(-(g_lin + bg_ref[br:br + 1, :])))
        proj = jnp.dot(o_ref[...], wbr_ref[br * W_BRANCH:(br + 1) * W_BRANCH, :],
                       preferred_element_type=F32)
        term = gate * proj
        merged = term if merged is None else merged + term
    out_ref[...] = x + jnp.dot(merged.astype(BF16), wo_ref[...], preferred_element_type=F32)


def _merge_out(x, g, o_a, o_b, o_c, w_gate, b_gate, w_br, w_out, layer):
    m = x.shape[0]
    row = lambda i: (i, 0)
    lay = lambda i: (layer, 0, 0)
    return pl.pallas_call(
        _merge_kernel,
        grid=(m // ROW_TILE,),
        in_specs=[
            pl.BlockSpec((ROW_TILE, D_MODEL), row),
            pl.BlockSpec((None, 1, D_MODEL), lay),
            pl.BlockSpec((ROW_TILE, W_BRANCH), row),
            pl.BlockSpec((ROW_TILE, W_BRANCH), row),
            pl.BlockSpec((ROW_TILE, W_BRANCH), row),
            pl.BlockSpec((None, D_MODEL, N_BRANCH * D_MODEL), lay, pipeline_mode=pl.Buffered(1)),
            pl.BlockSpec((None, N_BRANCH, D_MODEL), lay),
            pl.BlockSpec((None, N_BRANCH * W_BRANCH, D_MODEL), lay, pipeline_mode=pl.Buffered(1)),
            pl.BlockSpec((None, D_MODEL, D_MODEL), lay, pipeline_mode=pl.Buffered(1)),
        ],
        out_specs=pl.BlockSpec((ROW_TILE, D_MODEL), row),
        out_shape=jax.ShapeDtypeStruct((m, D_MODEL), F32),
        compiler_params=pltpu.CompilerParams(
            dimension_semantics=("arbitrary",), vmem_limit_bytes=VMEM_LIMIT),
        name="merge_out",
    )(x, g, o_a, o_b, o_c, w_gate, b_gate, w_br, w_out)


def _ffn_kernel(x_ref, g_ref, wgu_ref, wd_ref, out_ref, act_ref):
    x = x_ref[...]
    h = _rms(x, g_ref[...]).astype(BF16)
    for c in range(0, D_FF, FF_CHUNK):
        gate = jnp.dot(h, wgu_ref[:, c:c + FF_CHUNK], preferred_element_type=F32)
        up = jnp.dot(h, wgu_ref[:, D_FF + c:D_FF + c + FF_CHUNK], preferred_element_type=F32)
        silu = gate / (1.0 + jnp.exp(-gate))
        act_ref[:, c:c + FF_CHUNK] = (silu * up).astype(BF16)
    out_ref[...] = x + jnp.dot(act_ref[...], wd_ref[...], preferred_element_type=F32)


def _ffn(x, g, w_gu, w_down, layer):
    m = x.shape[0]
    row = lambda i: (i, 0)
    lay = lambda i: (layer, 0, 0)
    return pl.pallas_call(
        _ffn_kernel,
        grid=(m // ROW_TILE,),
        in_specs=[
            pl.BlockSpec((ROW_TILE, D_MODEL), row),
            pl.BlockSpec((None, 1, D_MODEL), lay),
            pl.BlockSpec((None, D_MODEL, 2 * D_FF), lay, pipeline_mode=pl.Buffered(1)),
            pl.BlockSpec((None, D_FF, D_MODEL), lay, pipeline_mode=pl.Buffered(1)),
        ],
        out_specs=pl.BlockSpec((ROW_TILE, D_MODEL), row),
        out_shape=jax.ShapeDtypeStruct((m, D_MODEL), F32),
        scratch_shapes=[pltpu.VMEM((ROW_TILE, D_FF), BF16)],
        compiler_params=pltpu.CompilerParams(
            dimension_semantics=("arbitrary",), vmem_limit_bytes=VMEM_LIMIT),
        name="ffn",
    )(x, g, w_gu, w_down)


def _sb_weights(z, cum, carry, mask):
    sp = jnp.log2(1.0 + jnp.exp2(-jnp.abs(z)))
    log_beta = jnp.minimum(z, 0.0) - sp
    log_1m = log_beta - z
    if mask is not None:
        log_1m = jnp.where(mask, log_1m, 0.0)
    hi = log_1m.astype(BF16)
    lo = (log_1m - hi.astype(F32)).astype(BF16)
    afters = []
    for u in reversed(range(z.shape[0] // T_K)):
        r0, r1 = u * T_K, (u + 1) * T_K
        within = jnp.dot(cum, jnp.concatenate([hi[r0:r1], lo[r0:r1]], axis=0),
                         preferred_element_type=F32)
        afters.append(within + carry)
        carry = carry + within[0:1, :] + log_1m[r0:r0 + 1, :]
    after = afters[0] if len(afters) == 1 else jnp.concatenate(afters[::-1], axis=0)
    w = jnp.exp2(log_beta + after)
    if mask is not None:
        w = jnp.where(mask, w, 0.0)
    return w.astype(BF16), carry


class _StickBreaking:
    HEADS = range(2)
    GROUPS_IN_MAIN = 2

    def __init__(self, qt_ref, k_ref, vt_ref, o_ref, acc_ref, carry_ref, z_ref):
        self.qt_ref, self.k_ref, self.vt_ref, self.o_ref = qt_ref, k_ref, vt_ref, o_ref
        self.acc_ref, self.carry_ref, self.z_ref = acc_ref, carry_ref, z_ref
        self.n_q = k_ref.shape[0] // T_Q
        self.masks = _head_row_masks()
        kk = lax.broadcasted_iota(jnp.int32, (T_K, T_K), 0)
        kk2 = lax.broadcasted_iota(jnp.int32, (T_K, T_K), 1)
        later = jnp.where(kk2 > kk, 1.0, 0.0).astype(BF16)
        self.cum = jnp.concatenate([later, later], axis=1)
        self.strict = (lax.broadcasted_iota(jnp.int32, (T_G, T_Q), 0)
                       < lax.broadcasted_iota(jnp.int32, (T_G, T_Q), 1))

    def main_groups(self, qi):
        return [g for g in range(qi, qi - self.GROUPS_IN_MAIN, -1) if g >= 0]

    def n_main_items(self):
        return sum(len(self.main_groups(qi)) for qi in range(self.n_q)) * len(self.HEADS)

    def load_q(self, qi):
        return _split_heads(self.qt_ref[qi], *self.masks)

    def logits(self, g, qts):
        k2 = self.k_ref[pl.ds(pl.multiple_of(g * T_G, T_G), T_G), :]
        return [jnp.dot(k2, qts[h], preferred_element_type=F32) for h in self.HEADS]

    def add_group(self, g, h, z, mask, carry, acc):
        w, carry = _sb_weights(z, self.cum, carry, mask)
        vth = self.vt_ref[g][h * HEAD_DIM:(h + 1) * HEAD_DIM, :]
        return carry, acc + jnp.dot(vth, w, preferred_element_type=F32)

    def store(self, qi, accs):
        q0 = pl.multiple_of(qi * T_Q, T_Q)
        self.o_ref[pl.ds(q0, T_Q), :] = jnp.concatenate(accs, axis=0).T.astype(BF16)

    def main(self):
        pairs = [(qi, g) for qi in range(self.n_q) for g in self.main_groups(qi)]
        n_slots = SB_LOOKAHEAD + 1

        def issue(j):
            for qi, g in pairs[j:j + 1]:
                for h, z in enumerate(self.logits(g, self.load_q(qi))):
                    self.z_ref[j % n_slots, h] = z

        for j in range(SB_LOOKAHEAD):
            issue(j)
        for j, (qi, g) in enumerate(pairs):
            issue(j + SB_LOOKAHEAD)
            if g == qi:
                carries = [jnp.zeros((1, T_Q), F32) for _ in self.HEADS]
                accs = [jnp.zeros((HEAD_DIM, T_Q), F32) for _ in self.HEADS]
            for h in self.HEADS:
                carries[h], accs[h] = self.add_group(
                    g, h, self.z_ref[j % n_slots, h], self.strict if g == qi else None,
                    carries[h], accs[h])
                yield
            if g == self.main_groups(qi)[-1]:
                self.store(qi, accs)
                for h in self.HEADS:
                    self.acc_ref[qi, h] = accs[h]
                    self.carry_ref[qi, h] = carries[h]

    def _live(self, carries):
        return jnp.max(functools.reduce(jnp.maximum, carries)) >= SB_DEAD

    def tails(self):
        tiles = [qi for qi in range(self.n_q) if qi - self.GROUPS_IN_MAIN >= 0]

        def tile_tail(qi):
            qts = self.load_q(qi)

            def live(st):
                g, carries, _ = st
                return (g >= 0) & self._live(carries)

            def body(st):
                g, carries, accs = st
                carries, accs = list(carries), list(accs)
                zs = self.logits(g, qts)
                for h in self.HEADS:
                    carries[h], accs[h] = self.add_group(g, h, zs[h], None, carries[h], accs[h])
                return g - 1, tuple(carries), tuple(accs)

            init = (qi - self.GROUPS_IN_MAIN,
                    tuple(self.carry_ref[qi, h] for h in self.HEADS),
                    tuple(self.acc_ref[qi, h] for h in self.HEADS))
            _, _, accs = lax.while_loop(live, body, init)
            self.store(qi, list(accs))

        def all_tails():
            for qi in tiles:
                tile_tail(qi)

        any_live = self._live([self.carry_ref[qi, h] for qi in tiles for h in self.HEADS])
        lax.cond(any_live, all_tails, lambda: None)


def _exhaust(items):
    for _ in items:
        pass


def _sb_scratch(seq):
    n_q = seq // T_Q
    return [pltpu.VMEM((n_q, 2, HEAD_DIM, T_Q), F32), pltpu.VMEM((n_q, 2, 1, T_Q), F32),
            pltpu.VMEM((SB_LOOKAHEAD + 1, 2, T_G, T_Q), F32)]


def _head_block_specs(branch, seq, index):
    def transposed(*ids):
        bi, hb = index(*ids)
        return bi, 0, branch * (W_BRANCH // LANES) + hb, 0

    def rows(*ids):
        bi, hb = index(*ids)
        return bi, 0, branch * (W_BRANCH // LANES) + hb

    t_spec = pl.BlockSpec((None, seq // T_Q, LANES, T_Q), transposed)
    return [t_spec, pl.BlockSpec((None, seq, LANES), rows), t_spec]


def _softmax_update(scores, vts, state):
    m, l, acc = state
    m_new = m
    for s in scores:
        m_new = jnp.maximum(m_new, jnp.max(s, axis=0, keepdims=True))
    alpha = jnp.exp2(m - m_new)
    l = alpha * l
    acc = alpha * acc
    for s, vt in zip(scores, vts):
        p = jnp.exp2(s - m_new)
        l = l + jnp.sum(p, axis=0, keepdims=True)
        acc = acc + jnp.dot(vt, p.astype(BF16), preferred_element_type=F32)
    return m_new, l, acc


def _diff_paths(qt_ref, kn_ref, vt_ref, bias_ref, lam_ref, sg_ref, o_ref, lam_init):
    seq = kn_ref.shape[0]
    n_q = seq // T_Q
    first, second = _head_row_masks()
    lq = lam_ref[...]
    lam = (jnp.exp(jnp.sum(lq[0:1] * lq[1:2], axis=-1, keepdims=True))
           - jnp.exp(jnp.sum(lq[2:3] * lq[3:4], axis=-1, keepdims=True)) + lam_init)

    def keys(g):
        return kn_ref[pl.ds(pl.multiple_of(g * T_G, T_G), T_G), :]

    def scores(qi, g):
        kn2, bias = keys(g), bias_ref[qi - g]
        return [jnp.dot(kn2, qtm, preferred_element_type=F32) + bias
                for qtm in _split_heads(qt_ref[qi], first, second)]

    def finish(qi, accs, sums):
        ob = accs[0] * (1.0 / sums[0]) - lam * (accs[1] * (1.0 / sums[1]))
        y = ob * lax.rsqrt(jnp.mean(ob * ob, axis=0, keepdims=True) + RMS_EPS)
        q0 = pl.multiple_of(qi * T_Q, T_Q)
        o_ref[pl.ds(q0, T_Q), :] = (y.T * sg_ref[...] * (1.0 - lam_init)).astype(BF16)

    def q_pair(qp, _):
        tiles = (2 * qp, 2 * qp + 1)

        def trip(it, states):
            gs = (2 * it, 2 * it + 1)
            vts = [vt_ref[g] for g in gs]
            scs = [[scores(qi, g) for g in gs] for qi in tiles]
            states = [list(st) for st in states]
            for i in range(2):
                for mp in range(2):
                    states[i][mp] = _softmax_update([sc[mp] for sc in scs[i]], vts, states[i][mp])
            return tuple(tuple(st) for st in states)

        init = (jnp.full((1, T_Q), NEG, F32), jnp.zeros((1, T_Q), F32), jnp.zeros((LANES, T_Q), F32))
        states = lax.fori_loop(0, qp, trip, ((init, init), (init, init)))
        gs = (2 * qp, 2 * qp + 1)
        vts = [vt_ref[g] for g in gs]
        scs = [[scores(tiles[0], gs[0])], [scores(tiles[1], g) for g in gs]]
        for i in range(2):
            done = [_softmax_update([sc[mp] for sc in scs[i]], vts, states[i][mp]) for mp in range(2)]
            finish(tiles[i], [st[2] for st in done], [st[1] for st in done])
        return 0

    work = [(qi, g) for qi in range(n_q) for g in range(qi + 1)]
    ones_rows = jnp.ones((ONES_ROWS, T_G), BF16)

    def bounded():
        ahead = [scores(*item) for item in work[:DIFF_LOOKAHEAD]]
        for step, (qi, g) in enumerate(work):
            z = ahead.pop(0)
            if step + DIFF_LOOKAHEAD < len(work):
                ahead.append(scores(*work[step + DIFF_LOOKAHEAD]))
            vt1 = jnp.concatenate([vt_ref[g], ones_rows], axis=0)
            prods = [jnp.dot(vt1, jnp.exp2(z[mp]).astype(BF16), preferred_element_type=F32)
                     for mp in range(2)]
            pvs = [r[:LANES] for r in prods]
            psums = [r[LANES:LANES + 1] for r in prods]
            if g == 0:
                accs, sums = pvs, psums
            else:
                accs = [a + pv for a, pv in zip(accs, pvs)]
                sums = [s + ps_ for s, ps_ in zip(sums, psums)]
            if g == qi:
                finish(qi, accs, sums)
            yield

    def general():
        lax.fori_loop(0, n_q // 2, q_pair, 0)

    return bounded, len(work), general


def _alibi_tiles(n_q):
    slopes = np.asarray([2.0 ** (-8.0 * (i + 1) / H_DIFF) for i in range(H_DIFF)], np.float32)
    slope2 = jnp.asarray(slopes * LOG2E, F32)[:, None, None, None]
    kpos = lax.broadcasted_iota(jnp.int32, (T_G, T_Q), 0)
    qpos = lax.broadcasted_iota(jnp.int32, (T_G, T_Q), 1)
    tiles_back = lax.broadcasted_iota(jnp.int32, (n_q, 1, 1), 0)
    dist = (qpos - kpos)[None] + T_G * tiles_back
    bias = -slope2 * jnp.abs(dist).astype(F32)[None]
    visible = (tiles_back > 0) | ((kpos // CHUNK) <= (qpos // CHUNK))[None]
    return jnp.where(visible[None], bias, NEG)


N_REL_GROUPS = (N_PAST_CHUNKS * CHUNK + T_Q - 1) // T_G + 1


def _chunk_items(qt_ref, kn_ref, vt_ref, tab_ref, o_ref, sc_ref):
    seq = kn_ref.shape[0]
    first, second = _head_row_masks()
    deltas = tuple(range(N_REL_GROUPS - 1, -1, -1))
    ones_rows = jnp.ones((ONES_ROWS, T_G), BF16)

    def score_stage(qi, tile_deltas, slot):
        qts = _split_heads(qt_ref[qi], first, second)
        for dl in tile_deltas:
            k2 = kn_ref[pl.ds(pl.multiple_of((qi - dl) * T_G, T_G), T_G), :]
            for h in range(2):
                sc_ref[slot, h, dl] = (jnp.dot(k2, qts[h], preferred_element_type=F32)
                                       + tab_ref[h, dl])

    def softmax_stage(qi, tile_deltas, slot, fixed_shift):
        outs = []
        for h in range(2):
            scores = [sc_ref[slot, h, dl] for dl in tile_deltas]
            if fixed_shift:
                ps = [jnp.exp2(s) for s in scores]
            else:
                m = functools.reduce(jnp.maximum, [jnp.max(s, axis=0, keepdims=True) for s in scores])
                ps = [jnp.exp2(s - m) for s in scores]
            acc = None
            for dl, p in zip(tile_deltas, ps):
                vth = jnp.concatenate(
                    [vt_ref[qi - dl][h * HEAD_DIM:(h + 1) * HEAD_DIM, :], ones_rows], axis=0)
                pv = jnp.dot(vth, p.astype(BF16), preferred_element_type=F32)
                acc = pv if acc is None else acc + pv
            outs.append(acc[:HEAD_DIM] * (1.0 / acc[HEAD_DIM:HEAD_DIM + 1]))
        q0 = pl.multiple_of(qi * T_Q, T_Q)
        o_ref[pl.ds(q0, T_Q), :] = jnp.concatenate(outs, axis=0).T.astype(BF16)

    def tile_deltas(qi):
        return tuple(dl for dl in deltas if qi - dl >= 0)

    n_q = seq // T_Q

    def run(fixed_shift):
        score_stage(0, tile_deltas(0), 0)
        for qi in range(n_q):
            if qi + 1 < n_q:
                score_stage(qi + 1, tile_deltas(qi + 1), (qi + 1) % 2)
            yield
            softmax_stage(qi, tile_deltas(qi), qi % 2, fixed_shift)
            yield

    return run, 2 * n_q


def _chunk_logit_bound(gq, gk, tabs):
    return _qk_logit_bound(gq, gk) + jnp.max(jnp.where(tabs > 0.5 * NEG, jnp.abs(tabs), 0.0))


def _interleave(streams):
    order = sorted(((i + 0.5) / n, s) for s, (_, n) in enumerate(streams) for i in range(n))
    for _, s in order:
        next(streams[s][0], None)
    for items, _ in streams:
        _exhaust(items)


def _mix_kernel(bounded_ref, qa_ref, ka_ref, va_ref, qb_ref, kb_ref, vb_ref, qc_ref, kc_ref, vc_ref,
                alibi_ref, lam_ref, sg_ref, tab_ref,
                oa_ref, ob_ref, oc_ref, acc_ref, carry_ref, z_ref, sc_ref, *, layer, lam_init):
    sb = _StickBreaking(qa_ref, ka_ref, va_ref, oa_ref, acc_ref, carry_ref, z_ref)
    diff_items, n_diff, diff_online = _diff_paths(qb_ref, kb_ref, vb_ref, alibi_ref, lam_ref, sg_ref,
                                                  ob_ref, lam_init)
    chunk_items, n_chunk = _chunk_items(qc_ref, kc_ref, vc_ref, tab_ref, oc_ref, sc_ref)

    def interleaved():
        _interleave([(sb.main(), sb.n_main_items()), (diff_items(), n_diff),
                     (chunk_items(True), n_chunk)])

    def one_by_one():
        _exhaust(sb.main())
        diff_online()
        _exhaust(chunk_items(False))

    lax.cond(bounded_ref[layer] != 0, interleaved, one_by_one)
    sb.tails()


def _logits_bounded(gq_diff, gk_diff, gq_ch, gk_ch, tab):
    flags = [(_qk_logit_bound(gq_diff[l], gk_diff[l]) <= SAFE_LOG2)
             & (_chunk_logit_bound(gq_ch[l], gk_ch[l], tab[l]) <= SAFE_LOG2) for l in range(DEPTH)]
    return jnp.stack(flags).astype(jnp.int32)


def _mix_attn(qt, k, vt, alibi, lam_qk, subln_g, tab, bounded, layer, lam_init):
    b, s, _ = k.shape
    n_hb = W_BRANCH // LANES
    index = lambda hb, bi: (bi, hb)
    lay = lambda hb, bi: (layer, 0, 0)
    out_spec = pl.BlockSpec((None, s, LANES), lambda hb, bi: (bi, 0, hb))
    out_shape = jax.ShapeDtypeStruct((b, s, W_BRANCH), BF16)
    return pl.pallas_call(
        functools.partial(_mix_kernel, layer=layer, lam_init=lam_init),
        grid=(n_hb, b),
        in_specs=[
            pl.BlockSpec(memory_space=pltpu.SMEM),
            *_head_block_specs(0, s, index), *_head_block_specs(1, s, index),
            *_head_block_specs(2, s, index),
            pl.BlockSpec((None, s // T_Q, T_G, T_Q), lambda hb, bi: (hb, 0, 0, 0)),
            pl.BlockSpec((None, 4, HEAD_DIM), lay),
            pl.BlockSpec((None, 1, LANES), lay),
            pl.BlockSpec((None, 2, N_REL_GROUPS, T_G, T_Q), lambda hb, bi: (layer, hb, 0, 0, 0)),
        ],
        out_specs=[out_spec, out_spec, out_spec],
        out_shape=[out_shape, out_shape, out_shape],
        scratch_shapes=_sb_scratch(s) + [pltpu.VMEM((2, 2, N_REL_GROUPS, T_G, T_Q), F32)],
        compiler_params=pltpu.CompilerParams(
            dimension_semantics=("arbitrary", "arbitrary"), vmem_limit_bytes=VMEM_LIMIT),
        name="mix_attn",
    )(bounded, qt, k, vt, qt, k, vt, qt, k, vt, alibi, lam_qk, subln_g, tab)


def _rel_bias_tiles(rel_bias):
    lead = rel_bias.shape[:-1]
    span = (N_REL_GROUPS - 1) * T_G + T_Q
    period = span + T_G + 1
    edge_lo = jnp.broadcast_to(rel_bias[..., :1], lead + (period - span - REL_CLIP,))
    edge_hi = jnp.broadcast_to(rel_bias[..., -1:], lead + (span - REL_CLIP - 1,))
    row = jnp.concatenate([rel_bias[..., REL_CLIP:], edge_hi, edge_lo, rel_bias[..., :REL_CLIP]], axis=-1)
    flat = jnp.tile(row, (1,) * len(lead) + (T_G,))[..., :T_G * (period - 1)]
    toep = flat.reshape(lead + (T_G, period - 1))
    tiles = jnp.stack([toep[..., d * T_G:d * T_G + T_Q] for d in range(N_REL_GROUPS)], axis=-3)
    kchunk = np.arange(T_G)[:, None] // CHUNK
    qchunk = np.arange(T_Q)[None, :] // CHUNK
    dd = np.stack([qchunk - kchunk + d * (T_G // CHUNK) for d in range(N_REL_GROUPS)])
    return jnp.where((dd >= 0) & (dd <= N_PAST_CHUNKS), tiles * LOG2E, NEG)


def kernel(x, norm_mix_g, w_in, b_gate, qk_g_diff, lambda_qk, subln_g, qk_g_ch, rel_bias,
           w_branch_sb, w_branch_diff, w_branch_ch, w_out, norm_ffn_g, w_gu, w_down):
    b, s, d = x.shape
    m = b * s
    w_qkv = w_in.astype(BF16)
    w_gate = w_qkv[:, :, QKV_W:]
    w_br = jnp.concatenate([w_branch_sb, w_branch_diff, w_branch_ch], axis=1).astype(BF16)
    w_out_b = w_out.astype(BF16)
    w_gu_b = w_gu.astype(BF16)
    w_down_b = w_down.astype(BF16)
    g_mix = norm_mix_g.reshape(DEPTH, 1, d)
    g_ffn = norm_ffn_g.reshape(DEPTH, 1, d)
    gq_diff = jnp.tile(qk_g_diff[:, 0:1, :], (1, 1, 2))
    gk_diff = jnp.tile(qk_g_diff[:, 1:2, :], (1, 1, 2))
    gq_ch = jnp.tile(qk_g_ch[:, 0:1, :], (1, 1, 2))
    gk_ch = jnp.tile(qk_g_ch[:, 1:2, :], (1, 1, 2))
    sg = subln_g.reshape(DEPTH, 1, 2 * HEAD_DIM)
    tab = _rel_bias_tiles(rel_bias)
    alibi = _alibi_tiles(s // T_Q)
    bounded = _logits_bounded(gq_diff, gk_diff, gq_ch, gk_ch, tab)
    ones = jnp.ones((DEPTH, 1, W_BRANCH), F32)
    widen = lambda g: jnp.tile(g, (1, 1, W_BRANCH // LANES))
    q_scale = QK_SCALE * LOG2E
    qk_gain = jnp.concatenate([ones, ones, ones, widen(gq_diff) * q_scale, widen(gk_diff), ones,
                               widen(gq_ch) * q_scale, widen(gk_ch), ones], axis=-1)

    xf = x.reshape(m, d)
    for layer in range(DEPTH):
        lam_init = 0.8 - 0.6 * math.exp(-0.3 * layer)
        k, qt, vt = _qkv_proj(xf, g_mix, w_qkv, qk_gain, layer)
        k = k.reshape(b, s, N_BRANCH * W_BRANCH)
        qt = qt.reshape(b, s // T_Q, N_BRANCH * W_BRANCH, T_Q)
        vt = vt.reshape(b, s // T_Q, N_BRANCH * W_BRANCH, T_Q)
        o_a, o_b, o_c = _mix_attn(qt, k, vt, alibi, lambda_qk, sg, tab, bounded, layer, lam_init)
        xf = _merge_out(xf, g_mix, o_a.reshape(m, W_BRANCH), o_b.reshape(m, W_BRANCH),
                        o_c.reshape(m, W_BRANCH), w_gate, b_gate, w_br, w_out_b, layer)
        xf = _ffn(xf, g_ffn, w_gu_b, w_down_b, layer)
    return xf.reshape(b, s, d)
```

```python
import functools
import math

import jax
import jax.numpy as jnp
import numpy as np
from jax import lax
from jax.experimental import pallas as pl
from jax.experimental.pallas import tpu as pltpu

F32 = jnp.float32
BF16 = jnp.bfloat16

D_MODEL = 1024
DEPTH = 4
CHUNK = 64
HEAD_DIM = 64
H_DIFF = 4
N_PAST_CHUNKS = 8
REL_CLIP = 128
W_BRANCH = 512
QKV_W = 9 * W_BRANCH
N_BRANCH = 3
D_FF = int(math.ceil(8 * D_MODEL / 3 / 256)) * 256
RMS_EPS = 1e-6
QK_SCALE = HEAD_DIM ** -0.5

LANES = 128
T_Q = 256
T_K = 128
T_G = 256
NEG = -1e30
SB_DEAD = -150.0
ONES_ROWS = 16
DIFF_LOOKAHEAD = 4
SB_LOOKAHEAD = 1
LOG2E = 1.4426950408889634
SAFE_LOG2 = 60.0
VMEM_LIMIT = 56 * 1024 * 1024

ROW_TILE = 1024
FF_CHUNK = 256


def _rms(x, g):
    return x * lax.rsqrt(jnp.mean(x * x, axis=-1, keepdims=True) + RMS_EPS) * g


def _rms_halves(x, g):
    lane = lax.broadcasted_iota(jnp.int32, (1, LANES), 1)
    first = lane < HEAD_DIM
    x2 = x * x
    s0 = jnp.sum(jnp.where(first, x2, 0.0), axis=-1, keepdims=True)
    s1 = jnp.sum(jnp.where(first, 0.0, x2), axis=-1, keepdims=True)
    ms = jnp.where(first, s0, s1) * (1.0 / HEAD_DIM)
    return x * lax.rsqrt(ms + RMS_EPS) * g


def _qk_logit_bound(gq, gk):
    return (1.02 * LOG2E * QK_SCALE * HEAD_DIM) * jnp.max(jnp.abs(gq)) * jnp.max(jnp.abs(gk))


def _head_row_masks():
    row = lax.broadcasted_iota(jnp.int32, (LANES, 1), 0)
    return row < HEAD_DIM, row >= HEAD_DIM


def _split_heads(qt, first, second):
    zero = jnp.zeros_like(qt)
    return jnp.where(first, qt, zero), jnp.where(second, qt, zero)


N_SECTIONS = 9


def _qkv_kernel(x_ref, g_ref, w_ref, qkg_ref, k_out, qt_out, vt_out):
    h = _rms(x_ref[...], g_ref[...]).astype(BF16)

    def project(sec):
        return jnp.dot(h, w_ref[:, sec * W_BRANCH:(sec + 1) * W_BRANCH], preferred_element_type=F32)

    z_next = project(0)
    for sec in range(N_SECTIONS):
        z = z_next
        if sec + 1 < N_SECTIONS:
            z_next = project(sec + 1)
        branch, role = divmod(sec, 3)
        for cb in range(W_BRANCH // LANES):
            blk = z[:, cb * LANES:(cb + 1) * LANES]
            col = sec * W_BRANCH + cb * LANES
            if branch > 0 and role < 2:
                blk = _rms_halves(blk, qkg_ref[:, col:col + LANES])
            elif role == 0:
                blk = blk * (QK_SCALE * LOG2E)
            out_col = branch * W_BRANCH + cb * LANES
            if role == 1:
                k_out[:, out_col:out_col + LANES] = blk.astype(BF16)
            else:
                out = qt_out if role == 0 else vt_out
                for r in range(ROW_TILE // T_Q):
                    out[r, out_col:out_col + LANES, :] = blk[r * T_Q:(r + 1) * T_Q, :].T.astype(BF16)


def _qkv_proj(x, g, w, qk_gain, layer):
    m = x.shape[0]
    width = N_BRANCH * W_BRANCH
    tiles = ROW_TILE // T_Q
    lay = lambda i: (layer, 0, 0)
    transposed = jax.ShapeDtypeStruct((m // T_Q, width, T_Q), BF16)
    return pl.pallas_call(
        _qkv_kernel,
        grid=(m // ROW_TILE,),
        in_specs=[
            pl.BlockSpec((ROW_TILE, D_MODEL), lambda i: (i, 0)),
            pl.BlockSpec((None, 1, D_MODEL), lay),
            pl.BlockSpec((None, D_MODEL, QKV_W), lay, pipeline_mode=pl.Buffered(1)),
            pl.BlockSpec((None, 1, QKV_W), lay),
        ],
        out_specs=[
            pl.BlockSpec((ROW_TILE, width), lambda i: (i, 0)),
            pl.BlockSpec((tiles, width, T_Q), lambda i: (i, 0, 0)),
            pl.BlockSpec((tiles, width, T_Q), lambda i: (i, 0, 0)),
        ],
        out_shape=[jax.ShapeDtypeStruct((m, width), BF16), transposed, transposed],
        compiler_params=pltpu.CompilerParams(
            dimension_semantics=("arbitrary",), vmem_limit_bytes=VMEM_LIMIT),
        name="qkv_proj",
    )(x, g, w, qk_gain)


def _merge_kernel(x_ref, g_ref, oa_ref, ob_ref, oc_ref, wg_ref, bg_ref, wbr_ref, wo_ref, out_ref):
    x = x_ref[...]
    h = _rms(x, g_ref[...]).astype(BF16)
    merged = None
    for br, o_ref in enumerate((oa_ref, ob_ref, oc_ref)):
        g_lin = jnp.dot(h, wg_ref[:, br * D_MODEL:(br + 1) * D_MODEL], preferred_element_type=F32)
        gate = 1.0 / (1.0 + jnp.exp(-(g_lin + bg_ref[br:br + 1, :])))
        proj = jnp.dot(o_ref[...], wbr_ref[br * W_BRANCH:(br + 1) * W_BRANCH, :],
                       preferred_element_type=F32)
        term = gate * proj
        merged = term if merged is None else merged + term
    out_ref[...] = x + jnp.dot(merged.astype(BF16), wo_ref[...], preferred_element_type=F32)


def _merge_out(x, g, o_a, o_b, o_c, w_gate, b_gate, w_br, w_out, layer):
    m = x.shape[0]
    row = lambda i: (i, 0)
    lay = lambda i: (layer, 0, 0)
    return pl.pallas_call(
        _merge_kernel,
        grid=(m // ROW_TILE,),
        in_specs=[
            pl.BlockSpec((ROW_TILE, D_MODEL), row),
            pl.BlockSpec((None, 1, D_MODEL), lay),
            pl.BlockSpec((ROW_TILE, W_BRANCH), row),
            pl.BlockSpec((ROW_TILE, W_BRANCH), row),
            pl.BlockSpec((ROW_TILE, W_BRANCH), row),
            pl.BlockSpec((None, D_MODEL, N_BRANCH * D_MODEL), lay, pipeline_mode=pl.Buffered(1)),
            pl.BlockSpec((None, N_BRANCH, D_MODEL), lay),
            pl.BlockSpec((None, N_BRANCH * W_BRANCH, D_MODEL), lay, pipeline_mode=pl.Buffered(1)),
            pl.BlockSpec((None, D_MODEL, D_MODEL), lay, pipeline_mode=pl.Buffered(1)),
        ],
        out_specs=pl.BlockSpec((ROW_TILE, D_MODEL), row),
        out_shape=jax.ShapeDtypeStruct((m, D_MODEL), F32),
        compiler_params=pltpu.CompilerParams(
            dimension_semantics=("arbitrary",), vmem_limit_bytes=VMEM_LIMIT),
        name="merge_out",
    )(x, g, o_a, o_b, o_c, w_gate, b_gate, w_br, w_out)


def _ffn_kernel(x_ref, g_ref, wgu_ref, wd_ref, out_ref, act_ref):
    x = x_ref[...]
    h = _rms(x, g_ref[...]).astype(BF16)
    for c in range(0, D_FF, FF_CHUNK):
        gate = jnp.dot(h, wgu_ref[:, c:c + FF_CHUNK], preferred_element_type=F32)
        up = jnp.dot(h, wgu_ref[:, D_FF + c:D_FF + c + FF_CHUNK], preferred_element_type=F32)
        silu = gate / (1.0 + jnp.exp(-gate))
        act_ref[:, c:c + FF_CHUNK] = (silu * up).astype(BF16)
    out_ref[...] = x + jnp.dot(act_ref[...], wd_ref[...], preferred_element_type=F32)


def _ffn(x, g, w_gu, w_down, layer):
    m = x.shape[0]
    row = lambda i: (i, 0)
    lay = lambda i: (layer, 0, 0)
    return pl.pallas_call(
        _ffn_kernel,
        grid=(m // ROW_TILE,),
        in_specs=[
            pl.BlockSpec((ROW_TILE, D_MODEL), row),
            pl.BlockSpec((None, 1, D_MODEL), lay),
            pl.BlockSpec((None, D_MODEL, 2 * D_FF), lay, pipeline_mode=pl.Buffered(1)),
            pl.BlockSpec((None, D_FF, D_MODEL), lay, pipeline_mode=pl.Buffered(1)),
        ],
        out_specs=pl.BlockSpec((ROW_TILE, D_MODEL), row),
        out_shape=jax.ShapeDtypeStruct((m, D_MODEL), F32),
        scratch_shapes=[pltpu.VMEM((ROW_TILE, D_FF), BF16)],
        compiler_params=pltpu.CompilerParams(
            dimension_semantics=("arbitrary",), vmem_limit_bytes=VMEM_LIMIT),
        name="ffn",
    )(x, g, w_gu, w_down)


def _sb_weights(z, cum, carry, mask):
    sp = jnp.log2(1.0 + jnp.exp2(-jnp.abs(z)))
    log_beta = jnp.minimum(z, 0.0) - sp
    log_1m = log_beta - z
    if mask is not None:
        log_1m = jnp.where(mask, log_1m, 0.0)
    hi = log_1m.astype(BF16)
    lo = (log_1m - hi.astype(F32)).astype(BF16)
    afters = []
    for u in reversed(range(z.shape[0] // T_K)):
        r0, r1 = u * T_K, (u + 1) * T_K
        within = jnp.dot(cum, jnp.concatenate([hi[r0:r1], lo[r0:r1]], axis=0),
                         preferred_element_type=F32)
        afters.append(within + carry)
        carry = carry + within[0:1, :] + log_1m[r0:r0 + 1, :]
    after = afters[0] if len(afters) == 1 else jnp.concatenate(afters[::-1], axis=0)
    w = jnp.exp2(log_beta + after)
    if mask is not None:
        w = jnp.where(mask, w, 0.0)
    return w.astype(BF16), carry


class _StickBreaking:
    HEADS = range(2)
    GROUPS_IN_MAIN = 2

    def __init__(self, qt_ref, k_ref, vt_ref, o_ref, acc_ref, carry_ref, z_ref):
        self.qt_ref, self.k_ref, self.vt_ref, self.o_ref = qt_ref, k_ref, vt_ref, o_ref
        self.acc_ref, self.carry_ref, self.z_ref = acc_ref, carry_ref, z_ref
        self.n_q = k_ref.shape[0] // T_Q
        self.masks = _head_row_masks()
        kk = lax.broadcasted_iota(jnp.int32, (T_K, T_K), 0)
        kk2 = lax.broadcasted_iota(jnp.int32, (T_K, T_K), 1)
        later = jnp.where(kk2 > kk, 1.0, 0.0).astype(BF16)
        self.cum = jnp.concatenate([later, later], axis=1)
        self.strict = (lax.broadcasted_iota(jnp.int32, (T_G, T_Q), 0)
                       < lax.broadcasted_iota(jnp.int32, (T_G, T_Q), 1))

    def main_groups(self, qi):
        return [g for g in range(qi, qi - self.GROUPS_IN_MAIN, -1) if g >= 0]

    def n_main_items(self):
        return sum(len(self.main_groups(qi)) for qi in range(self.n_q)) * len(self.HEADS)

    def load_q(self, qi):
        return _split_heads(self.qt_ref[qi], *self.masks)

    def logits(self, g, qts):
        k2 = self.k_ref[pl.ds(pl.multiple_of(g * T_G, T_G), T_G), :]
        return [jnp.dot(k2, qts[h], preferred_element_type=F32) for h in self.HEADS]

    def add_group(self, g, h, z, mask, carry, acc):
        w, carry = _sb_weights(z, self.cum, carry, mask)
        vth = self.vt_ref[g][h * HEAD_DIM:(h + 1) * HEAD_DIM, :]
        return carry, acc + jnp.dot(vth, w, preferred_element_type=F32)

    def store(self, qi, accs):
        q0 = pl.multiple_of(qi * T_Q, T_Q)
        self.o_ref[pl.ds(q0, T_Q), :] = jnp.concatenate(accs, axis=0).T.astype(BF16)

    def main(self):
        pairs = [(qi, g) for qi in range(self.n_q) for g in self.main_groups(qi)]
        n_slots = SB_LOOKAHEAD + 1

        def issue(j):
            for qi, g in pairs[j:j + 1]:
                for h, z in enumerate(self.logits(g, self.load_q(qi))):
                    self.z_ref[j % n_slots, h] = z

        for j in range(SB_LOOKAHEAD):
            issue(j)
        for j, (qi, g) in enumerate(pairs):
            issue(j + SB_LOOKAHEAD)
            if g == qi:
                carries = [jnp.zeros((1, T_Q), F32) for _ in self.HEADS]
                accs = [jnp.zeros((HEAD_DIM, T_Q), F32) for _ in self.HEADS]
            for h in self.HEADS:
                carries[h], accs[h] = self.add_group(
                    g, h, self.z_ref[j % n_slots, h], self.strict if g == qi else None,
                    carries[h], accs[h])
                yield
            if g == self.main_groups(qi)[-1]:
                self.store(qi, accs)
                for h in self.HEADS:
                    self.acc_ref[qi, h] = accs[h]
                    self.carry_ref[qi, h] = carries[h]

    def _live(self, carries):
        return jnp.max(functools.reduce(jnp.maximum, carries)) >= SB_DEAD

    def tails(self):
        tiles = [qi for qi in range(self.n_q) if qi - self.GROUPS_IN_MAIN >= 0]

        def tile_tail(qi):
            qts = self.load_q(qi)

            def live(st):
                g, carries, _ = st
                return (g >= 0) & self._live(carries)

            def body(st):
                g, carries, accs = st
                carries, accs = list(carries), list(accs)
                zs = self.logits(g, qts)
                for h in self.HEADS:
                    carries[h], accs[h] = self.add_group(g, h, zs[h], None, carries[h], accs[h])
                return g - 1, tuple(carries), tuple(accs)

            init = (qi - self.GROUPS_IN_MAIN,
                    tuple(self.carry_ref[qi, h] for h in self.HEADS),
                    tuple(self.acc_ref[qi, h] for h in self.HEADS))
            _, _, accs = lax.while_loop(live, body, init)
            self.store(qi, list(accs))

        def all_tails():
            for qi in tiles:
                tile_tail(qi)

        any_live = self._live([self.carry_ref[qi, h] for qi in tiles for h in self.HEADS])
        lax.cond(any_live, all_tails, lambda: None)


def _exhaust(items):
    for _ in items:
        pass


def _sb_scratch(seq):
    n_q = seq // T_Q
    return [pltpu.VMEM((n_q, 2, HEAD_DIM, T_Q), F32), pltpu.VMEM((n_q, 2, 1, T_Q), F32),
            pltpu.VMEM((SB_LOOKAHEAD + 1, 2, T_G, T_Q), F32)]


def _head_block_specs(branch, seq, index):
    def transposed(*ids):
        bi, hb = index(*ids)
        return bi, 0, branch * (W_BRANCH // LANES) + hb, 0

    def rows(*ids):
        bi, hb = index(*ids)
        return bi, 0, branch * (W_BRANCH // LANES) + hb

    t_spec = pl.BlockSpec((None, seq // T_Q, LANES, T_Q), transposed)
    return [t_spec, pl.BlockSpec((None, seq, LANES), rows), t_spec]


def _softmax_update(scores, vts, state):
    m, l, acc = state
    m_new = m
    for s in scores:
        m_new = jnp.maximum(m_new, jnp.max(s, axis=0, keepdims=True))
    alpha = jnp.exp2(m - m_new)
    l = alpha * l
    acc = alpha * acc
    for s, vt in zip(scores, vts):
        p = jnp.exp2(s - m_new)
        l = l + jnp.sum(p, axis=0, keepdims=True)
        acc = acc + jnp.dot(vt, p.astype(BF16), preferred_element_type=F32)
    return m_new, l, acc


def _diff_paths(qt_ref, kn_ref, vt_ref, bias_ref, lam_ref, sg_ref, o_ref, lam_init):
    seq = kn_ref.shape[0]
    n_q = seq // T_Q
    first, second = _head_row_masks()
    lq = lam_ref[...]
    lam = (jnp.exp(jnp.sum(lq[0:1] * lq[1:2], axis=-1, keepdims=True))
           - jnp.exp(jnp.sum(lq[2:3] * lq[3:4], axis=-1, keepdims=True)) + lam_init)

    def keys(g):
        return kn_ref[pl.ds(pl.multiple_of(g * T_G, T_G), T_G), :]

    def scores(qi, g):
        kn2, bias = keys(g), bias_ref[qi - g]
        return [jnp.dot(kn2, qtm, preferred_element_type=F32) + bias
                for qtm in _split_heads(qt_ref[qi], first, second)]

    def finish(qi, accs, sums):
        ob = accs[0] * (1.0 / sums[0]) - lam * (accs[1] * (1.0 / sums[1]))
        y = ob * lax.rsqrt(jnp.mean(ob * ob, axis=0, keepdims=True) + RMS_EPS)
        q0 = pl.multiple_of(qi * T_Q, T_Q)
        o_ref[pl.ds(q0, T_Q), :] = (y.T * sg_ref[...] * (1.0 - lam_init)).astype(BF16)

    def q_pair(qp, _):
        tiles = (2 * qp, 2 * qp + 1)

        def trip(it, states):
            gs = (2 * it, 2 * it + 1)
            vts = [vt_ref[g] for g in gs]
            scs = [[scores(qi, g) for g in gs] for qi in tiles]
            states = [list(st) for st in states]
            for i in range(2):
                for mp in range(2):
                    states[i][mp] = _softmax_update([sc[mp] for sc in scs[i]], vts, states[i][mp])
            return tuple(tuple(st) for st in states)

        init = (jnp.full((1, T_Q), NEG, F32), jnp.zeros((1, T_Q), F32), jnp.zeros((LANES, T_Q), F32))
        states = lax.fori_loop(0, qp, trip, ((init, init), (init, init)))
        gs = (2 * qp, 2 * qp + 1)
        vts = [vt_ref[g] for g in gs]
        scs = [[scores(tiles[0], gs[0])], [scores(tiles[1], g) for g in gs]]
        for i in range(2):
            done = [_softmax_update([sc[mp] for sc in scs[i]], vts, states[i][mp]) for mp in range(2)]
            finish(tiles[i], [st[2] for st in done], [st[1] for st in done])
        return 0

    work = [(qi, g) for qi in range(n_q) for g in range(qi + 1)]
    ones_rows = jnp.ones((ONES_ROWS, T_G), BF16)

    def bounded():
        ahead = [scores(*item) for item in work[:DIFF_LOOKAHEAD]]
        for step, (qi, g) in enumerate(work):
            z = ahead.pop(0)
            if step + DIFF_LOOKAHEAD < len(work):
                ahead.append(scores(*work[step + DIFF_LOOKAHEAD]))
            vt1 = jnp.concatenate([vt_ref[g], ones_rows], axis=0)
            prods = [jnp.dot(vt1, jnp.exp2(z[mp]).astype(BF16), preferred_element_type=F32)
                     for mp in range(2)]
            pvs = [r[:LANES] for r in prods]
            psums = [r[LANES:LANES + 1] for r in prods]
            if g == 0:
                accs, sums = pvs, psums
            else:
                accs = [a + pv for a, pv in zip(accs, pvs)]
                sums = [s + ps_ for s, ps_ in zip(sums, psums)]
            if g == qi:
                finish(qi, accs, sums)
            yield

    def general():
        lax.fori_loop(0, n_q // 2, q_pair, 0)

    return bounded, len(work), general


def _alibi_tiles(n_q):
    slopes = np.asarray([2.0 ** (-8.0 * (i + 1) / H_DIFF) for i in range(H_DIFF)], np.float32)
    slope2 = jnp.asarray(slopes * LOG2E, F32)[:, None, None, None]
    kpos = lax.broadcasted_iota(jnp.int32, (T_G, T_Q), 0)
    qpos = lax.broadcasted_iota(jnp.int32, (T_G, T_Q), 1)
    tiles_back = lax.broadcasted_iota(jnp.int32, (n_q, 1, 1), 0)
    dist = (qpos - kpos)[None] + T_G * tiles_back
    bias = -slope2 * jnp.abs(dist).astype(F32)[None]
    visible = (tiles_back > 0) | ((kpos // CHUNK) <= (qpos // CHUNK))[None]
    return jnp.where(visible[None], bias, NEG)


N_REL_GROUPS = (N_PAST_CHUNKS * CHUNK + T_Q - 1) // T_G + 1


def _chunk_items(qt_ref, kn_ref, vt_ref, tab_ref, o_ref, sc_ref):
    seq = kn_ref.shape[0]
    first, second = _head_row_masks()
    deltas = tuple(range(N_REL_GROUPS - 1, -1, -1))
    ones_rows = jnp.ones((ONES_ROWS, T_G), BF16)

    def score_stage(qi, tile_deltas, slot):
        qts = _split_heads(qt_ref[qi], first, second)
        for dl in tile_deltas:
            k2 = kn_ref[pl.ds(pl.multiple_of((qi - dl) * T_G, T_G), T_G), :]
            for h in range(2):
                sc_ref[slot, h, dl] = (jnp.dot(k2, qts[h], preferred_element_type=F32)
                                       + tab_ref[h, dl])

    def softmax_stage(qi, tile_deltas, slot, fixed_shift):
        outs = []
        for h in range(2):
            scores = [sc_ref[slot, h, dl] for dl in tile_deltas]
            if fixed_shift:
                ps = [jnp.exp2(s) for s in scores]
            else:
                m = functools.reduce(jnp.maximum, [jnp.max(s, axis=0, keepdims=True) for s in scores])
                ps = [jnp.exp2(s - m) for s in scores]
            acc = None
            for dl, p in zip(tile_deltas, ps):
                vth = jnp.concatenate(
                    [vt_ref[qi - dl][h * HEAD_DIM:(h + 1) * HEAD_DIM, :], ones_rows], axis=0)
                pv = jnp.dot(vth, p.astype(BF16), preferred_element_type=F32)
                acc = pv if acc is None else acc + pv
            outs.append(acc[:HEAD_DIM] * (1.0 / acc[HEAD_DIM:HEAD_DIM + 1]))
        q0 = pl.multiple_of(qi * T_Q, T_Q)
        o_ref[pl.ds(q0, T_Q), :] = jnp.concatenate(outs, axis=0).T.astype(BF16)

    def tile_deltas(qi):
        return tuple(dl for dl in deltas if qi - dl >= 0)

    n_q = seq // T_Q

    def run(fixed_shift):
        score_stage(0, tile_deltas(0), 0)
        for qi in range(n_q):
            if qi + 1 < n_q:
                score_stage(qi + 1, tile_deltas(qi + 1), (qi + 1) % 2)
            yield
            softmax_stage(qi, tile_deltas(qi), qi % 2, fixed_shift)
            yield

    return run, 2 * n_q


def _chunk_logit_bound(gq, gk, tabs):
    return _qk_logit_bound(gq, gk) + jnp.max(jnp.where(tabs > 0.5 * NEG, jnp.abs(tabs), 0.0))


def _interleave(streams):
    order = sorted(((i + 0.5) / n, s) for s, (_, n) in enumerate(streams) for i in range(n))
    for _, s in order:
        next(streams[s][0], None)
    for items, _ in streams:
        _exhaust(items)


def _mix_kernel(bounded_ref, qa_ref, ka_ref, va_ref, qb_ref, kb_ref, vb_ref, qc_ref, kc_ref, vc_ref,
                alibi_ref, lam_ref, sg_ref, tab_ref,
                oa_ref, ob_ref, oc_ref, acc_ref, carry_ref, z_ref, sc_ref, *, layer, lam_init):
    sb = _StickBreaking(qa_ref, ka_ref, va_ref, oa_ref, acc_ref, carry_ref, z_ref)
    diff_items, n_diff, diff_online = _diff_paths(qb_ref, kb_ref, vb_ref, alibi_ref, lam_ref, sg_ref,
                                                  ob_ref, lam_init)
    chunk_items, n_chunk = _chunk_items(qc_ref, kc_ref, vc_ref, tab_ref, oc_ref, sc_ref)

    def interleaved():
        _interleave([(sb.main(), sb.n_main_items()), (diff_items(), n_diff),
                     (chunk_items(True), n_chunk)])

    def one_by_one():
        _exhaust(sb.main())
        diff_online()
        _exhaust(chunk_items(False))

    lax.cond(bounded_ref[layer] != 0, interleaved, one_by_one)
    sb.tails()


def _logits_bounded(gq_diff, gk_diff, gq_ch, gk_ch, tab):
    flags = [(_qk_logit_bound(gq_diff[l], gk_diff[l]) <= SAFE_LOG2)
             & (_chunk_logit_bound(gq_ch[l], gk_ch[l], tab[l]) <= SAFE_LOG2) for l in range(DEPTH)]
    return jnp.stack(flags).astype(jnp.int32)


def _mix_attn(qt, k, vt, alibi, lam_qk, subln_g, tab, bounded, layer, lam_init):
    b, s, _ = k.shape
    n_hb = W_BRANCH // LANES
    index = lambda hb, bi: (bi, hb)
    lay = lambda hb, bi: (layer, 0, 0)
    out_spec = pl.BlockSpec((None, s, LANES), lambda hb, bi: (bi, 0, hb))
    out_shape = jax.ShapeDtypeStruct((b, s, W_BRANCH), BF16)
    return pl.pallas_call(
        functools.partial(_mix_kernel, layer=layer, lam_init=lam_init),
        grid=(n_hb, b),
        in_specs=[
            pl.BlockSpec(memory_space=pltpu.SMEM),
            *_head_block_specs(0, s, index), *_head_block_specs(1, s, index),
            *_head_block_specs(2, s, index),
            pl.BlockSpec((None, s // T_Q, T_G, T_Q), lambda hb, bi: (hb, 0, 0, 0)),
            pl.BlockSpec((None, 4, HEAD_DIM), lay),
            pl.BlockSpec((None, 1, LANES), lay),
            pl.BlockSpec((None, 2, N_REL_GROUPS, T_G, T_Q), lambda hb, bi: (layer, hb, 0, 0, 0)),
        ],
        out_specs=[out_spec, out_spec, out_spec],
        out_shape=[out_shape, out_shape, out_shape],
        scratch_shapes=_sb_scratch(s) + [pltpu.VMEM((2, 2, N_REL_GROUPS, T_G, T_Q), F32)],
        compiler_params=pltpu.CompilerParams(
            dimension_semantics=("arbitrary", "arbitrary"), vmem_limit_bytes=VMEM_LIMIT),
        name="mix_attn",
    )(bounded, qt, k, vt, qt, k, vt, qt, k, vt, alibi, lam_qk, subln_g, tab)


def _rel_bias_tiles(rel_bias):
    lead = rel_bias.shape[:-1]
    span = (N_REL_GROUPS - 1) * T_G + T_Q
    period = span + T_G + 1
    edge_lo = jnp.broadcast_to(rel_bias[..., :1], lead + (period - span - REL_CLIP,))
    edge_hi = jnp.broadcast_to(rel_bias[..., -1:], lead + (span - REL_CLIP - 1,))
    row = jnp.concatenate([rel_bias[..., REL_CLIP:], edge_hi, edge_lo, rel_bias[..., :REL_CLIP]], axis=-1)
    flat = jnp.tile(row, (1,) * len(lead) + (T_G,))[..., :T_G * (period - 1)]
    toep = flat.reshape(lead + (T_G, period - 1))
    tiles = jnp.stack([toep[..., d * T_G:d * T_G + T_Q] for d in range(N_REL_GROUPS)], axis=-3)
    kchunk = np.arange(T_G)[:, None] // CHUNK
    qchunk = np.arange(T_Q)[None, :] // CHUNK
    dd = np.stack([qchunk - kchunk + d * (T_G // CHUNK) for d in range(N_REL_GROUPS)])
    return jnp.where((dd >= 0) & (dd <= N_PAST_CHUNKS), tiles * LOG2E, NEG)


def kernel(x, norm_mix_g, w_in, b_gate, qk_g_diff, lambda_qk, subln_g, qk_g_ch, rel_bias,
           w_branch_sb, w_branch_diff, w_branch_ch, w_out, norm_ffn_g, w_gu, w_down):
    b, s, d = x.shape
    m = b * s
    w_qkv = w_in.astype(BF16)
    w_gate = w_qkv[:, :, QKV_W:]
    w_br = jnp.concatenate([w_branch_sb, w_branch_diff, w_branch_ch], axis=1).astype(BF16)
    w_out_b = w_out.astype(BF16)
    w_gu_b = w_gu.astype(BF16)
    w_down_b = w_down.astype(BF16)
    g_mix = norm_mix_g.reshape(DEPTH, 1, d)
    g_ffn = norm_ffn_g.reshape(DEPTH, 1, d)
    gq_diff = jnp.tile(qk_g_diff[:, 0:1, :], (1, 1, 2))
    gk_diff = jnp.tile(qk_g_diff[:, 1:2, :], (1, 1, 2))
    gq_ch = jnp.tile(qk_g_ch[:, 0:1, :], (1, 1, 2))
    gk_ch = jnp.tile(qk_g_ch[:, 1:2, :], (1, 1, 2))
    sg = subln_g.reshape(DEPTH, 1, 2 * HEAD_DIM)
    tab = _rel_bias_tiles(rel_bias)
    alibi = _alibi_tiles(s // T_Q)
    bounded = _logits_bounded(gq_diff, gk_diff, gq_ch, gk_ch, tab)
    ones = jnp.ones((DEPTH, 1, W_BRANCH), F32)
    widen = lambda g: jnp.tile(g, (1, 1, W_BRANCH // LANES))
    q_scale = QK_SCALE * LOG2E
    qk_gain = jnp.concatenate([ones, ones, ones, widen(gq_diff) * q_scale, widen(gk_diff), ones,
                               widen(gq_ch) * q_scale, widen(gk_ch), ones], axis=-1)

    xf = x.reshape(m, d)
    for layer in range(DEPTH):
        lam_init = 0.8 - 0.6 * math.exp(-0.3 * layer)
        k, qt, vt = _qkv_proj(xf, g_mix, w_qkv, qk_gain, layer)
        k = k.reshape(b, s, N_BRANCH * W_BRANCH)
        qt = qt.reshape(b, s // T_Q, N_BRANCH * W_BRANCH, T_Q)
        vt = vt.reshape(b, s // T_Q, N_BRANCH * W_BRANCH, T_Q)
        o_a, o_b, o_c = _mix_attn(qt, k, vt, alibi, lambda_qk, sg, tab, bounded, layer, lam_init)
        xf = _merge_out(xf, g_mix, o_a.reshape(m, W_BRANCH), o_b.reshape(m, W_BRANCH),
                        o_c.reshape(m, W_BRANCH), w_gate, b_gate, w_br, w_out_b, layer)
        xf = _ffn(xf, g_ffn, w_gu_b, w_down_b, layer)
    return xf.reshape(b, s, d)
```

```python
import functools
import math

import jax
import jax.numpy as jnp
import numpy as np
from jax import lax
from jax.experimental import pallas as pl
from jax.experimental.pallas import tpu as pltpu

F32 = jnp.float32
BF16 = jnp.bfloat16

D_MODEL = 1024
DEPTH = 4
CHUNK = 64
HEAD_DIM = 64
H_DIFF = 4
N_PAST_CHUNKS = 8
REL_CLIP = 128
W_BRANCH = 512
QKV_W = 9 * W_BRANCH
N_BRANCH = 3
D_FF = int(math.ceil(8 * D_MODEL / 3 / 256)) * 256
RMS_EPS = 1e-6
QK_SCALE = HEAD_DIM ** -0.5

LANES = 128
T_Q = 256
T_K = 128
T_G = 256
NEG = -1e30
SB_DEAD = -150.0
ONES_ROWS = 16
DIFF_LOOKAHEAD = 3
SB_LOOKAHEAD = 1
LOG2E = 1.4426950408889634
SAFE_LOG2 = 60.0
VMEM_LIMIT = 56 * 1024 * 1024

ROW_TILE = 1024
FF_CHUNK = 256


def _rms(x, g):
    return x * lax.rsqrt(jnp.mean(x * x, axis=-1, keepdims=True) + RMS_EPS) * g


def _rms_halves(x, g):
    lane = lax.broadcasted_iota(jnp.int32, (1, LANES), 1)
    first = lane < HEAD_DIM
    x2 = x * x
    s0 = jnp.sum(jnp.where(first, x2, 0.0), axis=-1, keepdims=True)
    s1 = jnp.sum(jnp.where(first, 0.0, x2), axis=-1, keepdims=True)
    ms = jnp.where(first, s0, s1) * (1.0 / HEAD_DIM)
    return x * lax.rsqrt(ms + RMS_EPS) * g


def _qk_logit_bound(gq, gk):
    return (1.02 * LOG2E * QK_SCALE * HEAD_DIM) * jnp.max(jnp.abs(gq)) * jnp.max(jnp.abs(gk))


def _head_row_masks():
    row = lax.broadcasted_iota(jnp.int32, (LANES, 1), 0)
    return row < HEAD_DIM, row >= HEAD_DIM


def _split_heads(qt, first, second):
    zero = jnp.zeros_like(qt)
    return jnp.where(first, qt, zero), jnp.where(second, qt, zero)


N_SECTIONS = 9


def _qkv_kernel(x_ref, g_ref, w_ref, qkg_ref, k_out, qt_out, vt_out):
    h = _rms(x_ref[...], g_ref[...]).astype(BF16)

    def project(sec):
        return jnp.dot(h, w_ref[:, sec * W_BRANCH:(sec + 1) * W_BRANCH], preferred_element_type=F32)

    z_next = project(0)
    for sec in range(N_SECTIONS):
        z = z_next
        if sec + 1 < N_SECTIONS:
            z_next = project(sec + 1)
        branch, role = divmod(sec, 3)
        for cb in range(W_BRANCH // LANES):
            blk = z[:, cb * LANES:(cb + 1) * LANES]
            col = sec * W_BRANCH + cb * LANES
            if branch > 0 and role < 2:
                blk = _rms_halves(blk, qkg_ref[:, col:col + LANES])
            elif role == 0:
                blk = blk * (QK_SCALE * LOG2E)
            out_col = branch * W_BRANCH + cb * LANES
            if role == 1:
                k_out[:, out_col:out_col + LANES] = blk.astype(BF16)
            else:
                out = qt_out if role == 0 else vt_out
                for r in range(ROW_TILE // T_Q):
                    out[r, out_col:out_col + LANES, :] = blk[r * T_Q:(r + 1) * T_Q, :].T.astype(BF16)


def _qkv_proj(x, g, w, qk_gain, layer):
    m = x.shape[0]
    width = N_BRANCH * W_BRANCH
    tiles = ROW_TILE // T_Q
    lay = lambda i: (layer, 0, 0)
    transposed = jax.ShapeDtypeStruct((m // T_Q, width, T_Q), BF16)
    return pl.pallas_call(
        _qkv_kernel,
        grid=(m // ROW_TILE,),
        in_specs=[
            pl.BlockSpec((ROW_TILE, D_MODEL), lambda i: (i, 0)),
            pl.BlockSpec((None, 1, D_MODEL), lay),
            pl.BlockSpec((None, D_MODEL, QKV_W), lay, pipeline_mode=pl.Buffered(1)),
            pl.BlockSpec((None, 1, QKV_W), lay),
        ],
        out_specs=[
            pl.BlockSpec((ROW_TILE, width), lambda i: (i, 0)),
            pl.BlockSpec((tiles, width, T_Q), lambda i: (i, 0, 0)),
            pl.BlockSpec((tiles, width, T_Q), lambda i: (i, 0, 0)),
        ],
        out_shape=[jax.ShapeDtypeStruct((m, width), BF16), transposed, transposed],
        compiler_params=pltpu.CompilerParams(
            dimension_semantics=("arbitrary",), vmem_limit_bytes=VMEM_LIMIT),
        name="qkv_proj",
    )(x, g, w, qk_gain)


def _merge_kernel(x_ref, g_ref, oa_ref, ob_ref, oc_ref, wg_ref, bg_ref, wbr_ref, wo_ref, out_ref):
    x = x_ref[...]
    h = _rms(x, g_ref[...]).astype(BF16)
    merged = None
    for br, o_ref in enumerate((oa_ref, ob_ref, oc_ref)):
        g_lin = jnp.dot(h, wg_ref[:, br * D_MODEL:(br + 1) * D_MODEL], preferred_element_type=F32)
        gate = 1.0 / (1.0 + jnp.exp(-(g_lin + bg_ref[br:br + 1, :])))
        proj = jnp.dot(o_ref[...], wbr_ref[br * W_BRANCH:(br + 1) * W_BRANCH, :],
                       preferred_element_type=F32)
        term = gate * proj
        merged = term if merged is None else merged + term
    out_ref[...] = x + jnp.dot(merged.astype(BF16), wo_ref[...], preferred_element_type=F32)


def _merge_out(x, g, o_a, o_b, o_c, w_gate, b_gate, w_br, w_out, layer):
    m = x.shape[0]
    row = lambda i: (i, 0)
    lay = lambda i: (layer, 0, 0)
    return pl.pallas_call(
        _merge_kernel,
        grid=(m // ROW_TILE,),
        in_specs=[
            pl.BlockSpec((ROW_TILE, D_MODEL), row),
            pl.BlockSpec((None, 1, D_MODEL), lay),
            pl.BlockSpec((ROW_TILE, W_BRANCH), row),
            pl.BlockSpec((ROW_TILE, W_BRANCH), row),
            pl.BlockSpec((ROW_TILE, W_BRANCH), row),
            pl.BlockSpec((None, D_MODEL, N_BRANCH * D_MODEL), lay, pipeline_mode=pl.Buffered(1)),
            pl.BlockSpec((None, N_BRANCH, D_MODEL), lay),
            pl.BlockSpec((None, N_BRANCH * W_BRANCH, D_MODEL), lay, pipeline_mode=pl.Buffered(1)),
            pl.BlockSpec((None, D_MODEL, D_MODEL), lay, pipeline_mode=pl.Buffered(1)),
        ],
        out_specs=pl.BlockSpec((ROW_TILE, D_MODEL), row),
        out_shape=jax.ShapeDtypeStruct((m, D_MODEL), F32),
        compiler_params=pltpu.CompilerParams(
            dimension_semantics=("arbitrary",), vmem_limit_bytes=VMEM_LIMIT),
        name="merge_out",
    )(x, g, o_a, o_b, o_c, w_gate, b_gate, w_br, w_out)


def _ffn_kernel(x_ref, g_ref, wgu_ref, wd_ref, out_ref, act_ref):
    x = x_ref[...]
    h = _rms(x, g_ref[...]).astype(BF16)
    for c in range(0, D_FF, FF_CHUNK):
        gate = jnp.dot(h, wgu_ref[:, c:c + FF_CHUNK], preferred_element_type=F32)
        up = jnp.dot(h, wgu_ref[:, D_FF + c:D_FF + c + FF_CHUNK], preferred_element_type=F32)
        silu = gate / (1.0 + jnp.exp(-gate))
        act_ref[:, c:c + FF_CHUNK] = (silu * up).astype(BF16)
    out_ref[...] = x + jnp.dot(act_ref[...], wd_ref[...], preferred_element_type=F32)


def _ffn(x, g, w_gu, w_down, layer):
    m = x.shape[0]
    row = lambda i: (i, 0)
    lay = lambda i: (layer, 0, 0)
    return pl.pallas_call(
        _ffn_kernel,
        grid=(m // ROW_TILE,),
        in_specs=[
            pl.BlockSpec((ROW_TILE, D_MODEL), row),
            pl.BlockSpec((None, 1, D_MODEL), lay),
            pl.BlockSpec((None, D_MODEL, 2 * D_FF), lay, pipeline_mode=pl.Buffered(1)),
            pl.BlockSpec((None, D_FF, D_MODEL), lay, pipeline_mode=pl.Buffered(1)),
        ],
        out_specs=pl.BlockSpec((ROW_TILE, D_MODEL), row),
        out_shape=jax.ShapeDtypeStruct((m, D_MODEL), F32),
        scratch_shapes=[pltpu.VMEM((ROW_TILE, D_FF), BF16)],
        compiler_params=pltpu.CompilerParams(
            dimension_semantics=("arbitrary",), vmem_limit_bytes=VMEM_LIMIT),
        name="ffn",
    )(x, g, w_gu, w_down)


def _sb_weights(z, cum, carry, mask):
    sp = jnp.log2(1.0 + jnp.exp2(-jnp.abs(z)))
    log_beta = jnp.minimum(z, 0.0) - sp
    log_1m = log_beta - z
    if mask is not None:
        log_1m = jnp.where(mask, log_1m, 0.0)
    hi = log_1m.astype(BF16)
    lo = (log_1m - hi.astype(F32)).astype(BF16)
    afters = []
    for u in reversed(range(z.shape[0] // T_K)):
        r0, r1 = u * T_K, (u + 1) * T_K
        within = jnp.dot(cum, jnp.concatenate([hi[r0:r1], lo[r0:r1]], axis=0),
                         preferred_element_type=F32)
        afters.append(within + carry)
        carry = carry + within[0:1, :] + log_1m[r0:r0 + 1, :]
    after = afters[0] if len(afters) == 1 else jnp.concatenate(afters[::-1], axis=0)
    w = jnp.exp2(log_beta + after)
    if mask is not None:
        w = jnp.where(mask, w, 0.0)
    return w.astype(BF16), carry


class _StickBreaking:
    HEADS = range(2)
    GROUPS_IN_MAIN = 2

    def __init__(self, qt_ref, k_ref, vt_ref, o_ref, acc_ref, carry_ref, z_ref):
        self.qt_ref, self.k_ref, self.vt_ref, self.o_ref = qt_ref, k_ref, vt_ref, o_ref
        self.acc_ref, self.carry_ref, self.z_ref = acc_ref, carry_ref, z_ref
        self.n_q = k_ref.shape[0] // T_Q
        self.masks = _head_row_masks()
        kk = lax.broadcasted_iota(jnp.int32, (T_K, T_K), 0)
        kk2 = lax.broadcasted_iota(jnp.int32, (T_K, T_K), 1)
        later = jnp.where(kk2 > kk, 1.0, 0.0).astype(BF16)
        self.cum = jnp.concatenate([later, later], axis=1)
        self.strict = (lax.broadcasted_iota(jnp.int32, (T_G, T_Q), 0)
                       < lax.broadcasted_iota(jnp.int32, (T_G, T_Q), 1))

    def main_groups(self, qi):
        return [g for g in range(qi, qi - self.GROUPS_IN_MAIN, -1) if g >= 0]

    def n_main_items(self):
        return sum(len(self.main_groups(qi)) for qi in range(self.n_q)) * len(self.HEADS)

    def load_q(self, qi):
        return _split_heads(self.qt_ref[qi], *self.masks)

    def logits(self, g, qts):
        k2 = self.k_ref[pl.ds(pl.multiple_of(g * T_G, T_G), T_G), :]
        return [jnp.dot(k2, qts[h], preferred_element_type=F32) for h in self.HEADS]

    def add_group(self, g, h, z, mask, carry, acc):
        w, carry = _sb_weights(z, self.cum, carry, mask)
        vth = self.vt_ref[g][h * HEAD_DIM:(h + 1) * HEAD_DIM, :]
        return carry, acc + jnp.dot(vth, w, preferred_element_type=F32)

    def store(self, qi, accs):
        q0 = pl.multiple_of(qi * T_Q, T_Q)
        self.o_ref[pl.ds(q0, T_Q), :] = jnp.concatenate(accs, axis=0).T.astype(BF16)

    def main(self):
        pairs = [(qi, g) for qi in range(self.n_q) for g in self.main_groups(qi)]
        n_slots = SB_LOOKAHEAD + 1

        def issue(j):
            for qi, g in pairs[j:j + 1]:
                for h, z in enumerate(self.logits(g, self.load_q(qi))):
                    self.z_ref[j % n_slots, h] = z

        for j in range(SB_LOOKAHEAD):
            issue(j)
        for j, (qi, g) in enumerate(pairs):
            issue(j + SB_LOOKAHEAD)
            if g == qi:
                carries = [jnp.zeros((1, T_Q), F32) for _ in self.HEADS]
                accs = [jnp.zeros((HEAD_DIM, T_Q), F32) for _ in self.HEADS]
            for h in self.HEADS:
                carries[h], accs[h] = self.add_group(
                    g, h, self.z_ref[j % n_slots, h], self.strict if g == qi else None,
                    carries[h], accs[h])
                yield
            if g == self.main_groups(qi)[-1]:
                self.store(qi, accs)
                for h in self.HEADS:
                    self.acc_ref[qi, h] = accs[h]
                    self.carry_ref[qi, h] = carries[h]

    def _live(self, carries):
        return jnp.max(functools.reduce(jnp.maximum, carries)) >= SB_DEAD

    def tails(self):
        tiles = [qi for qi in range(self.n_q) if qi - self.GROUPS_IN_MAIN >= 0]

        def tile_tail(qi):
            qts = self.load_q(qi)

            def live(st):
                g, carries, _ = st
                return (g >= 0) & self._live(carries)

            def body(st):
                g, carries, accs = st
                carries, accs = list(carries), list(accs)
                zs = self.logits(g, qts)
                for h in self.HEADS:
                    carries[h], accs[h] = self.add_group(g, h, zs[h], None, carries[h], accs[h])
                return g - 1, tuple(carries), tuple(accs)

            init = (qi - self.GROUPS_IN_MAIN,
                    tuple(self.carry_ref[qi, h] for h in self.HEADS),
                    tuple(self.acc_ref[qi, h] for h in self.HEADS))
            _, _, accs = lax.while_loop(live, body, init)
            self.store(qi, list(accs))

        def all_tails():
            for qi in tiles:
                tile_tail(qi)

        any_live = self._live([self.carry_ref[qi, h] for qi in tiles for h in self.HEADS])
        lax.cond(any_live, all_tails, lambda: None)


def _exhaust(items):
    for _ in items:
        pass


def _sb_scratch(seq):
    n_q = seq // T_Q
    return [pltpu.VMEM((n_q, 2, HEAD_DIM, T_Q), F32), pltpu.VMEM((n_q, 2, 1, T_Q), F32),
            pltpu.VMEM((SB_LOOKAHEAD + 1, 2, T_G, T_Q), F32)]


def _head_block_specs(branch, seq, index):
    def transposed(*ids):
        bi, hb = index(*ids)
        return bi, 0, branch * (W_BRANCH // LANES) + hb, 0

    def rows(*ids):
        bi, hb = index(*ids)
        return bi, 0, branch * (W_BRANCH // LANES) + hb

    t_spec = pl.BlockSpec((None, seq // T_Q, LANES, T_Q), transposed)
    return [t_spec, pl.BlockSpec((None, seq, LANES), rows), t_spec]


def _softmax_update(scores, vts, state):
    m, l, acc = state
    m_new = m
    for s in scores:
        m_new = jnp.maximum(m_new, jnp.max(s, axis=0, keepdims=True))
    alpha = jnp.exp2(m - m_new)
    l = alpha * l
    acc = alpha * acc
    for s, vt in zip(scores, vts):
        p = jnp.exp2(s - m_new)
        l = l + jnp.sum(p, axis=0, keepdims=True)
        acc = acc + jnp.dot(vt, p.astype(BF16), preferred_element_type=F32)
    return m_new, l, acc


def _diff_paths(qt_ref, kn_ref, vt_ref, bias_ref, lam_ref, sg_ref, o_ref, lam_init):
    seq = kn_ref.shape[0]
    n_q = seq // T_Q
    first, second = _head_row_masks()
    lq = lam_ref[...]
    lam = (jnp.exp(jnp.sum(lq[0:1] * lq[1:2], axis=-1, keepdims=True))
           - jnp.exp(jnp.sum(lq[2:3] * lq[3:4], axis=-1, keepdims=True)) + lam_init)

    def keys(g):
        return kn_ref[pl.ds(pl.multiple_of(g * T_G, T_G), T_G), :]

    def scores(qi, g):
        kn2, bias = keys(g), bias_ref[qi - g]
        return [jnp.dot(kn2, qtm, preferred_element_type=F32) + bias
                for qtm in _split_heads(qt_ref[qi], first, second)]

    def finish(qi, accs, sums):
        ob = accs[0] * (1.0 / sums[0]) - lam * (accs[1] * (1.0 / sums[1]))
        y = ob * lax.rsqrt(jnp.mean(ob * ob, axis=0, keepdims=True) + RMS_EPS)
        q0 = pl.multiple_of(qi * T_Q, T_Q)
        o_ref[pl.ds(q0, T_Q), :] = (y.T * sg_ref[...] * (1.0 - lam_init)).astype(BF16)

    def q_pair(qp, _):
        tiles = (2 * qp, 2 * qp + 1)

        def trip(it, states):
            gs = (2 * it, 2 * it + 1)
            vts = [vt_ref[g] for g in gs]
            scs = [[scores(qi, g) for g in gs] for qi in tiles]
            states = [list(st) for st in states]
            for i in range(2):
                for mp in range(2):
                    states[i][mp] = _softmax_update([sc[mp] for sc in scs[i]], vts, states[i][mp])
            return tuple(tuple(st) for st in states)

        init = (jnp.full((1, T_Q), NEG, F32), jnp.zeros((1, T_Q), F32), jnp.zeros((LANES, T_Q), F32))
        states = lax.fori_loop(0, qp, trip, ((init, init), (init, init)))
        gs = (2 * qp, 2 * qp + 1)
        vts = [vt_ref[g] for g in gs]
        scs = [[scores(tiles[0], gs[0])], [scores(tiles[1], g) for g in gs]]
        for i in range(2):
            done = [_softmax_update([sc[mp] for sc in scs[i]], vts, states[i][mp]) for mp in range(2)]
            finish(tiles[i], [st[2] for st in done], [st[1] for st in done])
        return 0

    work = [(qi, g) for qi in range(n_q) for g in range(qi + 1)]
    ones_rows = jnp.ones((ONES_ROWS, T_G), BF16)

    def bounded():
        ahead = [scores(*item) for item in work[:DIFF_LOOKAHEAD]]
        for step, (qi, g) in enumerate(work):
            z = ahead.pop(0)
            if step + DIFF_LOOKAHEAD < len(work):
                ahead.append(scores(*work[step + DIFF_LOOKAHEAD]))
            vt1 = jnp.concatenate([vt_ref[g], ones_rows], axis=0)
            prods = [jnp.dot(vt1, jnp.exp2(z[mp]).astype(BF16), preferred_element_type=F32)
                     for mp in range(2)]
            pvs = [r[:LANES] for r in prods]
            psums = [r[LANES:LANES + 1] for r in prods]
            if g == 0:
                accs, sums = pvs, psums
            else:
                accs = [a + pv for a, pv in zip(accs, pvs)]
                sums = [s + ps_ for s, ps_ in zip(sums, psums)]
            if g == qi:
                finish(qi, accs, sums)
            yield

    def general():
        lax.fori_loop(0, n_q // 2, q_pair, 0)

    return bounded, len(work), general


def _alibi_tiles(n_q):
    slopes = np.asarray([2.0 ** (-8.0 * (i + 1) / H_DIFF) for i in range(H_DIFF)], np.float32)
    slope2 = jnp.asarray(slopes * LOG2E, F32)[:, None, None, None]
    kpos = lax.broadcasted_iota(jnp.int32, (T_G, T_Q), 0)
    qpos = lax.broadcasted_iota(jnp.int32, (T_G, T_Q), 1)
    tiles_back = lax.broadcasted_iota(jnp.int32, (n_q, 1, 1), 0)
    dist = (qpos - kpos)[None] + T_G * tiles_back
    bias = -slope2 * jnp.abs(dist).astype(F32)[None]
    visible = (tiles_back > 0) | ((kpos // CHUNK) <= (qpos // CHUNK))[None]
    return jnp.where(visible[None], bias, NEG)


N_REL_GROUPS = (N_PAST_CHUNKS * CHUNK + T_Q - 1) // T_G + 1


def _chunk_items(qt_ref, kn_ref, vt_ref, tab_ref, o_ref, sc_ref):
    seq = kn_ref.shape[0]
    first, second = _head_row_masks()
    deltas = tuple(range(N_REL_GROUPS - 1, -1, -1))
    ones_rows = jnp.ones((ONES_ROWS, T_G), BF16)

    def score_stage(qi, tile_deltas, slot):
        qts = _split_heads(qt_ref[qi], first, second)
        for dl in tile_deltas:
            k2 = kn_ref[pl.ds(pl.multiple_of((qi - dl) * T_G, T_G), T_G), :]
            for h in range(2):
                sc_ref[slot, h, dl] = (jnp.dot(k2, qts[h], preferred_element_type=F32)
                                       + tab_ref[h, dl])

    def softmax_stage(qi, tile_deltas, slot, fixed_shift):
        outs = []
        for h in range(2):
            scores = [sc_ref[slot, h, dl] for dl in tile_deltas]
            if fixed_shift:
                ps = [jnp.exp2(s) for s in scores]
            else:
                m = functools.reduce(jnp.maximum, [jnp.max(s, axis=0, keepdims=True) for s in scores])
                ps = [jnp.exp2(s - m) for s in scores]
            acc = None
            for dl, p in zip(tile_deltas, ps):
                vth = jnp.concatenate(
                    [vt_ref[qi - dl][h * HEAD_DIM:(h + 1) * HEAD_DIM, :], ones_rows], axis=0)
                pv = jnp.dot(vth, p.astype(BF16), preferred_element_type=F32)
                acc = pv if acc is None else acc + pv
            outs.append(acc[:HEAD_DIM] * (1.0 / acc[HEAD_DIM:HEAD_DIM + 1]))
        q0 = pl.multiple_of(qi * T_Q, T_Q)
        o_ref[pl.ds(q0, T_Q), :] = jnp.concatenate(outs, axis=0).T.astype(BF16)

    def tile_deltas(qi):
        return tuple(dl for dl in deltas if qi - dl >= 0)

    n_q = seq // T_Q

    def run(fixed_shift):
        score_stage(0, tile_deltas(0), 0)
        for qi in range(n_q):
            if qi + 1 < n_q:
                score_stage(qi + 1, tile_deltas(qi + 1), (qi + 1) % 2)
            yield
            softmax_stage(qi, tile_deltas(qi), qi % 2, fixed_shift)
            yield

    return run, 2 * n_q


def _chunk_logit_bound(gq, gk, tabs):
    return _qk_logit_bound(gq, gk) + jnp.max(jnp.where(tabs > 0.5 * NEG, jnp.abs(tabs), 0.0))


def _interleave(streams):
    order = sorted(((i + 0.5) / n, s) for s, (_, n) in enumerate(streams) for i in range(n))
    for _, s in order:
        next(streams[s][0], None)
    for items, _ in streams:
        _exhaust(items)


def _mix_kernel(bounded_ref, qa_ref, ka_ref, va_ref, qb_ref, kb_ref, vb_ref, qc_ref, kc_ref, vc_ref,
                alibi_ref, lam_ref, sg_ref, tab_ref,
                oa_ref, ob_ref, oc_ref, acc_ref, carry_ref, z_ref, sc_ref, *, layer, lam_init):
    sb = _StickBreaking(qa_ref, ka_ref, va_ref, oa_ref, acc_ref, carry_ref, z_ref)
    diff_items, n_diff, diff_online = _diff_paths(qb_ref, kb_ref, vb_ref, alibi_ref, lam_ref, sg_ref,
                                                  ob_ref, lam_init)
    chunk_items, n_chunk = _chunk_items(qc_ref, kc_ref, vc_ref, tab_ref, oc_ref, sc_ref)

    def interleaved():
        _interleave([(sb.main(), sb.n_main_items()), (diff_items(), n_diff),
                     (chunk_items(True), n_chunk)])

    def one_by_one():
        _exhaust(sb.main())
        diff_online()
        _exhaust(chunk_items(False))

    lax.cond(bounded_ref[layer] != 0, interleaved, one_by_one)
    sb.tails()


def _logits_bounded(gq_diff, gk_diff, gq_ch, gk_ch, tab):
    flags = [(_qk_logit_bound(gq_diff[l], gk_diff[l]) <= SAFE_LOG2)
             & (_chunk_logit_bound(gq_ch[l], gk_ch[l], tab[l]) <= SAFE_LOG2) for l in range(DEPTH)]
    return jnp.stack(flags).astype(jnp.int32)


def _mix_attn(qt, k, vt, alibi, lam_qk, subln_g, tab, bounded, layer, lam_init):
    b, s, _ = k.shape
    n_hb = W_BRANCH // LANES
    index = lambda hb, bi: (bi, hb)
    lay = lambda hb, bi: (layer, 0, 0)
    out_spec = pl.BlockSpec((None, s, LANES), lambda hb, bi: (bi, 0, hb))
    out_shape = jax.ShapeDtypeStruct((b, s, W_BRANCH), BF16)
    return pl.pallas_call(
        functools.partial(_mix_kernel, layer=layer, lam_init=lam_init),
        grid=(n_hb, b),
        in_specs=[
            pl.BlockSpec(memory_space=pltpu.SMEM),
            *_head_block_specs(0, s, index), *_head_block_specs(1, s, index),
            *_head_block_specs(2, s, index),
            pl.BlockSpec((None, s // T_Q, T_G, T_Q), lambda hb, bi: (hb, 0, 0, 0)),
            pl.BlockSpec((None, 4, HEAD_DIM), lay),
            pl.BlockSpec((None, 1, LANES), lay),
            pl.BlockSpec((None, 2, N_REL_GROUPS, T_G, T_Q), lambda hb, bi: (layer, hb, 0, 0, 0)),
        ],
        out_specs=[out_spec, out_spec, out_spec],
        out_shape=[out_shape, out_shape, out_shape],
        scratch_shapes=_sb_scratch(s) + [pltpu.VMEM((2, 2, N_REL_GROUPS, T_G, T_Q), F32)],
        compiler_params=pltpu.CompilerParams(
            dimension_semantics=("arbitrary", "arbitrary"), vmem_limit_bytes=VMEM_LIMIT),
        name="mix_attn",
    )(bounded, qt, k, vt, qt, k, vt, qt, k, vt, alibi, lam_qk, subln_g, tab)


def _rel_bias_tiles(rel_bias):
    lead = rel_bias.shape[:-1]
    span = (N_REL_GROUPS - 1) * T_G + T_Q
    period = span + T_G + 1
    edge_lo = jnp.broadcast_to(rel_bias[..., :1], lead + (period - span - REL_CLIP,))
    edge_hi = jnp.broadcast_to(rel_bias[..., -1:], lead + (span - REL_CLIP - 1,))
    row = jnp.concatenate([rel_bias[..., REL_CLIP:], edge_hi, edge_lo, rel_bias[..., :REL_CLIP]], axis=-1)
    flat = jnp.tile(row, (1,) * len(lead) + (T_G,))[..., :T_G * (period - 1)]
    toep = flat.reshape(lead + (T_G, period - 1))
    tiles = jnp.stack([toep[..., d * T_G:d * T_G + T_Q] for d in range(N_REL_GROUPS)], axis=-3)
    kchunk = np.arange(T_G)[:, None] // CHUNK
    qchunk = np.arange(T_Q)[None, :] // CHUNK
    dd = np.stack([qchunk - kchunk + d * (T_G // CHUNK) for d in range(N_REL_GROUPS)])
    return jnp.where((dd >= 0) & (dd <= N_PAST_CHUNKS), tiles * LOG2E, NEG)


def kernel(x, norm_mix_g, w_in, b_gate, qk_g_diff, lambda_qk, subln_g, qk_g_ch, rel_bias,
           w_branch_sb, w_branch_diff, w_branch_ch, w_out, norm_ffn_g, w_gu, w_down):
    b, s, d = x.shape
    m = b * s
    w_qkv = w_in.astype(BF16)
    w_gate = w_qkv[:, :, QKV_W:]
    w_br = jnp.concatenate([w_branch_sb, w_branch_diff, w_branch_ch], axis=1).astype(BF16)
    w_out_b = w_out.astype(BF16)
    w_gu_b = w_gu.astype(BF16)
    w_down_b = w_down.astype(BF16)
    g_mix = norm_mix_g.reshape(DEPTH, 1, d)
    g_ffn = norm_ffn_g.reshape(DEPTH, 1, d)
    gq_diff = jnp.tile(qk_g_diff[:, 0:1, :], (1, 1, 2))
    gk_diff = jnp.tile(qk_g_diff[:, 1:2, :], (1, 1, 2))
    gq_ch = jnp.tile(qk_g_ch[:, 0:1, :], (1, 1, 2))
    gk_ch = jnp.tile(qk_g_ch[:, 1:2, :], (1, 1, 2))
    sg = subln_g.reshape(DEPTH, 1, 2 * HEAD_DIM)
    tab = _rel_bias_tiles(rel_bias)
    alibi = _alibi_tiles(s // T_Q)
    bounded = _logits_bounded(gq_diff, gk_diff, gq_ch, gk_ch, tab)
    ones = jnp.ones((DEPTH, 1, W_BRANCH), F32)
    widen = lambda g: jnp.tile(g, (1, 1, W_BRANCH // LANES))
    q_scale = QK_SCALE * LOG2E
    qk_gain = jnp.concatenate([ones, ones, ones, widen(gq_diff) * q_scale, widen(gk_diff), ones,
                               widen(gq_ch) * q_scale, widen(gk_ch), ones], axis=-1)

    xf = x.reshape(m, d)
    for layer in range(DEPTH):
        lam_init = 0.8 - 0.6 * math.exp(-0.3 * layer)
        k, qt, vt = _qkv_proj(xf, g_mix, w_qkv, qk_gain, layer)
        k = k.reshape(b, s, N_BRANCH * W_BRANCH)
        qt = qt.reshape(b, s // T_Q, N_BRANCH * W_BRANCH, T_Q)
        vt = vt.reshape(b, s // T_Q, N_BRANCH * W_BRANCH, T_Q)
        o_a, o_b, o_c = _mix_attn(qt, k, vt, alibi, lambda_qk, sg, tab, bounded, layer, lam_init)
        xf = _merge_out(xf, g_mix, o_a.reshape(m, W_BRANCH), o_b.reshape(m, W_BRANCH),
                        o_c.reshape(m, W_BRANCH), w_gate, b_gate, w_br, w_out_b, layer)
        xf = _ffn(xf, g_ffn, w_gu_b, w_down_b, layer)
    return xf.reshape(b, s, d)
```
